```python
import math
import jax, jax.numpy as jnp
from jax import lax
import numpy as np

D_MODEL = 1024
BATCH = 8
SEQ = 2048
DEPTH = 4

N_MIXERS = 2
N_RWKV_LAYERS = (DEPTH + 1) // 2
N_ATTN_LAYERS = DEPTH // 2
RMS_EPS = 1e-6
N_MOD = 6

RWKV_HEAD = 64
RWKV_HEADS = D_MODEL // RWKV_HEAD
D_DECAY_LORA = 64
D_AAA_LORA = 64
D_GATE_LORA = 128
RWKV_GN_EPS = 64e-5
N_DIRS = 2
N_SHIFT_MIX = 6

ATTN_GROUPS = ((128, 1), (512, 4), (2048, 16))
N_GROUPS = len(ATTN_GROUPS)
ATTN_HEADS = 8
ATTN_HEAD_DIM = D_MODEL // ATTN_HEADS
QKV_WIDTH = N_GROUPS * 3 * ATTN_HEADS * ATTN_HEAD_DIM
Q_BLOCK = 64
NEG_INF = -1e30

N_EXPERT_GROUPS = 4
EXPERTS_PER_GROUP = 8
TOP_K_INNER = 2
D_EXPERT = D_MODEL // 4

kernel_name = "hybrid_rwkv7_dilated_alibi_hmoe_encoder"


def rms_norm(x, g):
    x32 = x.astype(jnp.float32)
    y = x32 * lax.rsqrt(jnp.mean(x32 * x32, axis=-1, keepdims=True) + RMS_EPS)
    return (y * g.astype(jnp.float32)).astype(x.dtype)


def modulate(h, shift, scale):
    return h * (1.0 + scale[:, None, :]) + shift[:, None, :]


def _wkv7_scan(r, w, k, v, a, b, reverse):
    bsz, _, h, n = r.shape

    def step(state, inp):
        r_t, w_t, k_t, v_t, a_t, b_t = inp
        sa = jnp.einsum('bhvk,bhk->bhv', state, a_t)
        state = (state * w_t[:, :, None, :]
                 + sa[:, :, :, None] * b_t[:, :, None, :]
                 + v_t[:, :, :, None] * k_t[:, :, None, :])
        y_t = jnp.einsum('bhvk,bhk->bhv', state, r_t)
        return state, y_t

    s0 = jnp.zeros((bsz, h, n, n), jnp.float32)
    xs = tuple(jnp.moveaxis(t, 1, 0) for t in (r, w, k, v, a, b))
    _, ys = lax.scan(step, s0, xs, reverse=reverse)
    return jnp.moveaxis(ys, 0, 1)


def rwkv7_mix(h, mu, w_rkv, w0, w1, w2, a0, a1, a2, g1, g2, k_k, k_a, r_k, ln_w, ln_b, w_o):
    bsz, s, d = h.shape
    f32 = jnp.float32
    x_prev = jnp.pad(h, ((0, 0), (1, 0), (0, 0)))[:, :-1]
    x_next = jnp.pad(h, ((0, 0), (0, 1), (0, 0)))[:, 1:]
    xx = 0.5 * (x_prev + x_next) - h
    xs = h[:, :, None, :] + xx[:, :, None, :] * mu
    rkv = jnp.einsum('bsjd,jde->bsje', xs[:, :, :3], w_rkv)
    r, k, v = rkv[:, :, 0], rkv[:, :, 1], rkv[:, :, 2]
    xw, xa, xg = xs[:, :, 3], xs[:, :, 4], xs[:, :, 5]

    def lora(xin, A, Bm, act):
        return jnp.einsum('zbsr,zre->zbse', act(jnp.einsum('bsd,zdr->zbsr', xin, A)), Bm)

    w_log = -jax.nn.softplus(-(w0[:, None, None, :] + lora(xw, w1, w2, jnp.tanh)).astype(f32)) - 0.5
    decay = jnp.exp(-jnp.exp(w_log))
    a = jax.nn.sigmoid((a0[:, None, None, :] + lora(xa, a1, a2, lambda t: t)).astype(f32))
    g = lora(xg, g1, g2, jax.nn.sigmoid).astype(f32)

    def heads(t):
        return t.reshape(t.shape[:-1] + (RWKV_HEADS, RWKV_HEAD))

    r32, k32, v32 = (heads(t.astype(f32)) for t in (r, k, v))
    kk = k32 * heads(k_k.astype(f32))
    kk = kk / jnp.maximum(jnp.sqrt(jnp.sum(kk * kk, axis=-1, keepdims=True)), 1e-12)
    a_h, decay_h = heads(a), heads(decay)
    k_dir = k32[None] * (1.0 + (a_h - 1.0) * heads(k_a.astype(f32)))
    b_dir = kk[None] * a_h
    y_f = _wkv7_scan(r32, decay_h[0], k_dir[0], v32, -kk, b_dir[0], reverse=False)
    y_b = _wkv7_scan(r32, decay_h[1], k_dir[1], v32, -kk, b_dir[1], reverse=True)
    y = jnp.stack([y_f, y_b])
    mean = jnp.mean(y, axis=-1, keepdims=True)
    var = jnp.mean(jnp.square(y - mean), axis=-1, keepdims=True)
    y = (y - mean) * lax.rsqrt(var + RWKV_GN_EPS) * heads(ln_w.astype(f32)) + heads(ln_b.astype(f32))
    bonus = jnp.sum(r32[None] * k_dir * r_k.astype(f32), axis=-1, keepdims=True) * v32[None]
    o = jnp.sum(heads(g) * (y + bonus), axis=0).reshape(bsz, s, d)
    return jnp.einsum('bsd,de->bse', o.astype(h.dtype), w_o)


def alibi_slopes(n):
    return 2.0 ** (-8.0 * jnp.arange(1, n + 1, dtype=jnp.float32) / n)


def _dilated_group(q, k, v, slopes, window, dilation):
    bsz, s, h, dh = q.shape
    half = window // (2 * dilation)
    L = s // dilation
    nb = -(-L // Q_BLOCK)
    lp = nb * Q_BLOCK
    kb_len = Q_BLOCK + 2 * half

    def strided(t):
        return t.reshape(bsz, L, dilation, h, dh).transpose(0, 2, 3, 1, 4)

    qd = jnp.pad(strided(q), ((0, 0), (0, 0), (0, 0), (0, lp - L), (0, 0)))
    qd = qd.reshape(bsz, dilation, h, nb, Q_BLOCK, dh)
    pad_kv = ((0, 0), (0, 0), (0, 0), (half, lp - L + half), (0, 0))
    kd = jnp.pad(strided(k), pad_kv)
    vd = jnp.pad(strided(v), pad_kv)
    key_idx = jnp.arange(nb)[:, None] * Q_BLOCK + jnp.arange(kb_len)[None, :]
    kblk = kd[:, :, :, key_idx]
    vblk = vd[:, :, :, key_idx]
    scores = jnp.einsum('brhnqd,brhnkd->brhnqk', qd, kblk) * (dh ** -0.5)
    q_pos = jnp.arange(lp).reshape(nb, Q_BLOCK)
    k_pos = key_idx - half
    rel = jnp.abs(k_pos[:, None, :] - q_pos[:, :, None])
    valid = (rel <= half) & (k_pos[:, None, :] >= 0) & (k_pos[:, None, :] < L)
    alibi = -slopes[:, None, None, None] * (rel * dilation).astype(jnp.float32)
    scores = jnp.where(valid, scores + alibi[None, None], NEG_INF)
    m = jnp.max(scores, axis=-1, keepdims=True)
    e = jnp.exp(scores - m)
    den = jnp.sum(e, axis=-1, keepdims=True)
    o = jnp.einsum('brhnqk,brhnkd->brhnqd', e, vblk) / den
    lse = (m + jnp.log(den))[..., 0]

    def unstride(t):
        t = t.reshape((bsz, dilation, h, lp) + t.shape[5:])[:, :, :, :L]
        t = jnp.moveaxis(t, 3, 1)
        return t.reshape((bsz, s, h) + t.shape[4:])

    return unstride(o), unstride(lse)


def dilated_attention_mix(h, w_qkv, w_o):
    bsz, s, d = h.shape
    qkv = jnp.einsum('bsd,de->bse', h, w_qkv).astype(jnp.float32)
    qkv = qkv.reshape(bsz, s, N_GROUPS, 3, ATTN_HEADS, ATTN_HEAD_DIM)
    slopes = alibi_slopes(N_GROUPS * ATTN_HEADS).reshape(N_GROUPS, ATTN_HEADS)
    outs, lses = [], []
    for gi, (window, dil) in enumerate(ATTN_GROUPS):
        o, lse = _dilated_group(qkv[:, :, gi, 0], qkv[:, :, gi, 1], qkv[:, :, gi, 2], slopes[gi], window, dil)
        outs.append(o)
        lses.append(lse)
    o = jnp.stack(outs)
    alpha = jax.nn.softmax(jnp.stack(lses), axis=0)
    merged = jnp.sum(alpha[..., None] * o, axis=0).reshape(bsz, s, d)
    return jnp.einsum('bsd,de->bse', merged.astype(h.dtype), w_o)


def hier_moe(h, router_g, router_g_b, router_e, router_e_b, w_gate, w_up, w_down):
    bsz, s, d = h.shape
    xt = h.reshape(-1, d)
    f32 = jnp.float32
    g_logits = (xt @ router_g + router_g_b).astype(f32)
    g_prob = jax.nn.softmax(g_logits, axis=-1)
    _, g_top = lax.top_k(g_logits, 1)
    p_group = jnp.take_along_axis(g_prob, g_top, axis=-1)
    e_logits = (jnp.einsum('td,gde->tge', xt, router_e) + router_e_b).astype(f32)
    e_sel = jnp.take_along_axis(e_logits, g_top[:, :, None], axis=1)[:, 0]
    top_v, top_i = lax.top_k(e_sel, TOP_K_INNER)
    top_w = jax.nn.softmax(top_v, axis=-1) * p_group
    inner = jnp.sum(jax.nn.one_hot(top_i, EXPERTS_PER_GROUP, dtype=f32) * top_w[..., None], axis=1)
    gates = jax.nn.one_hot(g_top[:, 0], N_EXPERT_GROUPS, dtype=f32)[:, :, None] * inner[:, None, :]
    out = jnp.zeros((xt.shape[0], d), f32)
    for gi in range(N_EXPERT_GROUPS):
        hid = jax.nn.silu(jnp.einsum('td,edf->tef', xt, w_gate[gi])) * jnp.einsum('td,edf->tef', xt, w_up[gi])
        hid = hid * gates[:, gi, :, None].astype(hid.dtype)
        out = out + jnp.einsum('tef,efd->td', hid, w_down[gi]).astype(f32)
    return out.reshape(bsz, s, d).astype(h.dtype)


def setup_inputs(seed: int = 0) -> dict:
    key = jax.random.key(seed)
    ks = iter(jax.random.split(key, 48))
    f32 = jnp.float32
    D, NR, NA = D_MODEL, N_RWKV_LAYERS, N_ATTN_LAYERS
    G, E, F = N_EXPERT_GROUPS, EXPERTS_PER_GROUP, D_EXPERT

    def nrm(shape, scale):
        return jax.random.normal(next(ks), shape, f32) * scale

    def uni(shape, lo, hi):
        return jax.random.uniform(next(ks), shape, f32, minval=lo, maxval=hi)

    return {
        "x": nrm((BATCH, SEQ, D), 1.0),
        "c": nrm((BATCH, D), 1.0),
        "ada_w": nrm((DEPTH, D, N_MOD * D), 0.5 * D ** -0.5),
        "ada_b": nrm((DEPTH, N_MOD * D), 0.01),
        "norm_tm_g": 1.0 + nrm((DEPTH, D), 0.02),
        "norm_cm_g": 1.0 + nrm((DEPTH, D), 0.02),
        "rw_mu": uni((NR, N_SHIFT_MIX, D), 0.0, 1.0),
        "rw_w_rkv": nrm((NR, 3, D, D), D ** -0.5),
        "rw_w0": uni((NR, N_DIRS, D), -4.5, -0.5),
        "rw_w1": nrm((NR, N_DIRS, D, D_DECAY_LORA), D ** -0.5),
        "rw_w2": nrm((NR, N_DIRS, D_DECAY_LORA, D), 0.5 * D_DECAY_LORA ** -0.5),
        "rw_a0": nrm((NR, N_DIRS, D), 0.5),
        "rw_a1": nrm((NR, N_DIRS, D, D_AAA_LORA), D ** -0.5),
        "rw_a2": nrm((NR, N_DIRS, D_AAA_LORA, D), 0.5 * D_AAA_LORA ** -0.5),
        "rw_g1": nrm((NR, N_DIRS, D, D_GATE_LORA), D ** -0.5),
        "rw_g2": nrm((NR, N_DIRS, D_GATE_LORA, D), D_GATE_LORA ** -0.5),
        "rw_k_k": 0.85 + nrm((NR, D), 0.05),
        "rw_k_a": 1.0 + nrm((NR, D), 0.05),
        "rw_r_k": nrm((NR, RWKV_HEADS, RWKV_HEAD), 0.1),
        "rw_ln_w": 1.0 + nrm((NR, D), 0.02),
        "rw_ln_b": nrm((NR, D), 0.01),
        "rw_w_o": nrm((NR, D, D), D ** -0.5),
        "at_w_qkv": nrm((NA, D, QKV_WIDTH), D ** -0.5),
        "at_w_o": nrm((NA, D, D), D ** -0.5),
        "moe_router_g": nrm((DEPTH, D, G), D ** -0.5),
        "moe_router_g_b": nrm((DEPTH, G), 0.01),
        "moe_router_e": nrm((DEPTH, G, D, E), D ** -0.5),
        "moe_router_e_b": nrm((DEPTH, G, E), 0.01),
        "moe_w_gate": nrm((DEPTH, G, E, D, F), D ** -0.5),
        "moe_w_up": nrm((DEPTH, G, E, D, F), D ** -0.5),
        "moe_w_down": nrm((DEPTH, G, E, F, D), F ** -0.5),
        "final_g": 1.0 + nrm((D,), 0.02),
    }


def reference(x, c, ada_w, ada_b, norm_tm_g, norm_cm_g,
              rw_mu, rw_w_rkv, rw_w0, rw_w1, rw_w2, rw_a0, rw_a1, rw_a2, rw_g1, rw_g2,
              rw_k_k, rw_k_a, rw_r_k, rw_ln_w, rw_ln_b, rw_w_o,
              at_w_qkv, at_w_o,
              moe_router_g, moe_router_g_b, moe_router_e, moe_router_e_b,
              moe_w_gate, moe_w_up, moe_w_down, final_g):
    silu_c = jax.nn.silu(c)
    for i in range(DEPTH):
        mod = silu_c @ ada_w[i] + ada_b[i]
        sh_t, sc_t, ga_t, sh_c, sc_c, ga_c = jnp.split(mod, N_MOD, axis=-1)
        hn = modulate(rms_norm(x, norm_tm_g[i]), sh_t, sc_t)
        j = i // N_MIXERS
        if i % N_MIXERS == 0:
            tm = rwkv7_mix(hn, rw_mu[j], rw_w_rkv[j], rw_w0[j], rw_w1[j], rw_w2[j],
                           rw_a0[j], rw_a1[j], rw_a2[j], rw_g1[j], rw_g2[j],
                           rw_k_k[j], rw_k_a[j], rw_r_k[j], rw_ln_w[j], rw_ln_b[j], rw_w_o[j])
        else:
            tm = dilated_attention_mix(hn, at_w_qkv[j], at_w_o[j])
        x = x + ga_t[:, None, :] * tm
        hn = modulate(rms_norm(x, norm_cm_g[i]), sh_c, sc_c)
        cm = hier_moe(hn, moe_router_g[i], moe_router_g_b[i], moe_router_e[i], moe_router_e_b[i],
                      moe_w_gate[i], moe_w_up[i], moe_w_down[i])
        x = x + ga_c[:, None, :] * cm
    return rms_norm(x, final_g)
```

```python
import functools
import math

import numpy as np
import jax
import jax.numpy as jnp
from jax import lax
from jax.experimental import pallas as pl
from jax.experimental.pallas import tpu as pltpu

F32 = jnp.float32
BF16 = jnp.bfloat16

RMS_EPS = 1e-6
N_MOD = 6
RWKV_HEAD = 64
RWKV_GN_EPS = 64e-5
ATTN_GROUPS = ((128, 1), (512, 4), (2048, 16))
ATTN_HEADS = 8
ATTN_HEAD_DIM = 128
NEG_INF = -1e30
N_EXPERT_GROUPS = 4
EXPERTS_PER_GROUP = 8
N_EXPERTS = N_EXPERT_GROUPS * EXPERTS_PER_GROUP
ROUTER_LANES = 128

SCAN_CHUNK = 64
V7X_VMEM_LIMIT = 56 * 1024 * 1024
SCAN_PREC = lax.Precision.HIGHEST


def _params(sem):
    return pltpu.CompilerParams(dimension_semantics=sem, vmem_limit_bytes=V7X_VMEM_LIMIT)


def _dot(a, b):
    return jnp.dot(a, b, preferred_element_type=F32)


def _sigmoid(x):
    return 1.0 / (1.0 + jnp.exp(-x))


def _split2(x):
    hi = x.astype(BF16)
    lo = (x - hi.astype(F32)).astype(BF16)
    return hi, lo


def _split3(x):
    h1 = x.astype(BF16)
    r1 = x - h1.astype(F32)
    h2 = r1.astype(BF16)
    h3 = (r1 - h2.astype(F32)).astype(BF16)
    return h1, h2, h3


def _ada_kernel(c_ref, w_ref, b_ref, o_ref):
    c = c_ref[...]
    sc = c * _sigmoid(c)
    o_ref[...] = _dot(sc.astype(BF16), w_ref[...].astype(BF16)) + b_ref[...]


def _ada_mod(c, ada_w, ada_b):
    depth, d, n = ada_w.shape
    bsz = c.shape[0]
    tn = n // 4
    out = pl.pallas_call(
        _ada_kernel,
        grid=(depth, n // tn),
        in_specs=[
            pl.BlockSpec((bsz, d), lambda i, j: (0, 0)),
            pl.BlockSpec((None, d, tn), lambda i, j: (i, 0, j)),
            pl.BlockSpec((None, 1, tn), lambda i, j: (i, 0, j)),
        ],
        out_specs=pl.BlockSpec((None, bsz, tn), lambda i, j: (i, 0, j)),
        out_shape=jax.ShapeDtypeStruct((depth, bsz, n), F32),
        compiler_params=_params(("parallel", "parallel")),
        name="ada_mod",
    )(c, ada_w, ada_b.reshape(depth, 1, n))
    return out.reshape(depth, bsz, N_MOD, 1, d).transpose(0, 2, 1, 3, 4)


def _norm_kernel(*refs, modulated, n_out):
    if modulated:
        x_ref, g_ref, sh_ref, sc_ref = refs[:4]
    else:
        x_ref, g_ref = refs[:2]
    outs = refs[-n_out:]
    x = x_ref[...]
    ms = jnp.mean(x * x, axis=-1, keepdims=True)
    y = x * lax.rsqrt(ms + RMS_EPS) * g_ref[...]
    if modulated:
        y = y * (1.0 + sc_ref[...]) + sh_ref[...]
    for o in outs:
        o[...] = y.astype(o.dtype)


def _norm_mod(x2, g, shift, scale, bsz, out_dtypes, ts=512):
    t, d = x2.shape
    s = t // bsz
    ts = min(ts, s)
    ns = s // ts
    modulated = shift is not None
    row_spec = pl.BlockSpec((ts, d), lambda b, i: (b * ns + i, 0))
    in_specs = [row_spec, pl.BlockSpec((1, d), lambda b, i: (0, 0))]
    args = [x2, g.reshape(1, d)]
    if modulated:
        vec = pl.BlockSpec((None, 1, d), lambda b, i: (b, 0, 0))
        in_specs += [vec, vec]
        args += [shift, scale]
    outs = pl.pallas_call(
        functools.partial(_norm_kernel, modulated=modulated, n_out=len(out_dtypes)),
        grid=(bsz, ns),
        in_specs=in_specs,
        out_specs=[row_spec] * len(out_dtypes),
        out_shape=[jax.ShapeDtypeStruct((t, d), dt) for dt in out_dtypes],
        compiler_params=_params(("parallel", "parallel")),
        name="norm_mod",
    )(*args)
    return outs


def _proj_res_kernel(a_ref, w_ref, x_ref, ga_ref, o_ref):
    o_ref[...] = x_ref[...] + ga_ref[...] * _dot(a_ref[...], w_ref[...])


def _proj_residual(a, w, x2, gate, bsz, tm=512):
    t, d = x2.shape
    s = t // bsz
    tm = min(tm, s)
    ns = s // tm
    row = lambda b, i: (b * ns + i, 0)
    return pl.pallas_call(
        _proj_res_kernel,
        grid=(bsz, ns),
        in_specs=[
            pl.BlockSpec((tm, a.shape[1]), row),
            pl.BlockSpec(w.shape, lambda b, i: (0, 0)),
            pl.BlockSpec((tm, d), row),
            pl.BlockSpec((None, 1, d), lambda b, i: (b, 0, 0)),
        ],
        out_specs=pl.BlockSpec((tm, d), row),
        out_shape=jax.ShapeDtypeStruct((t, d), F32),
        compiler_params=_params(("parallel", "parallel")),
        name="proj_residual",
    )(a, w, x2, gate)


def _mm_kernel(a_ref, b_ref, o_ref):
    o_ref[...] = _dot(a_ref[...], b_ref[...]).astype(o_ref.dtype)


def _matmul(a, b, out_dtype, tm=512, tn=1024):
    m, k = a.shape
    n = b.shape[1]
    tm, tn = min(tm, m), min(tn, n)
    return pl.pallas_call(
        _mm_kernel,
        grid=(n // tn, m // tm),
        in_specs=[
            pl.BlockSpec((tm, k), lambda j, i: (i, 0)),
            pl.BlockSpec((k, tn), lambda j, i: (0, j)),
        ],
        out_specs=pl.BlockSpec((tm, tn), lambda j, i: (i, j)),
        out_shape=jax.ShapeDtypeStruct((m, n), out_dtype),
        compiler_params=_params(("parallel", "parallel")),
        name="matmul",
    )(a, b)


def _softplus(z):
    return jnp.maximum(z, 0.0) + jnp.log(1.0 + jnp.exp(-jnp.abs(z)))


def _seg_sum(x, seg_ref, segt_ref):
    xh, xl = _split2(x)
    s = _dot(xh, seg_ref[...]) + _dot(xl, seg_ref[...])
    sh, sl = _split2(s)
    return _dot(sh, segt_ref[...]) + _dot(sl, segt_ref[...])


def _rwkv_prep_kernel(hn_ref, prev_ref, next_ref, mu_ref, wrkv_ref, w1_ref, a1_ref, g1_ref,
                      w2_ref, a2_ref, g2_ref, vec_ref, cm_ref, edm_ref, seg_ref, segt_ref,
                      at_ref, rt_ref, bt_ref, kt_ref, v_ref, ed_ref, g_ref, gb_ref, *, ts, ns):
    i = pl.program_id(1)
    cur = hn_ref[...]
    row = lax.broadcasted_iota(jnp.int32, (ts, 1), 0)
    prev_row = jnp.where(i > 0, prev_ref[7:8, :], 0.0)
    next_row = jnp.where(i < ns - 1, next_ref[0:1, :], 0.0)
    x_prev = jnp.where(row == 0, prev_row, pltpu.roll(cur, 1, 0))
    x_next = jnp.where(row == ts - 1, next_row, pltpu.roll(cur, ts - 1, 0))
    xx = 0.5 * (x_prev + x_next) - cur

    def mix(j):
        return (cur + xx * mu_ref[j:j + 1, :]).astype(BF16)

    r = _dot(mix(0), wrkv_ref[0])
    k = _dot(mix(1), wrkv_ref[1])
    v = _dot(mix(2), wrkv_ref[2])
    tw = jnp.tanh(_dot(mix(3), w1_ref[...]))
    ta = _dot(mix(4), a1_ref[...])
    tg = _sigmoid(_dot(mix(5), g1_ref[...]))
    lane = lax.broadcasted_iota(jnp.int32, (1, tw.shape[1]), 1)
    half = tw.shape[1] // 2
    hg = tg.shape[1] // 2

    k_k = vec_ref[4:5, :]
    k_a = vec_ref[5:6, :]
    r_k = vec_ref[6:7, :]
    kk = k * k_k
    ss = _seg_sum(kk * kk, seg_ref, segt_ref)
    kkn = kk / jnp.maximum(jnp.sqrt(ss), 1e-12)
    v_ref[...] = v.astype(v_ref.dtype)

    gb = jnp.zeros_like(cur)
    for z in range(2):
        sel = (lane < half) if z == 0 else (lane >= half)
        lw = _dot(jnp.where(sel, tw, 0.0).astype(BF16), w2_ref[...])
        la = _dot(jnp.where(sel, ta, 0.0).astype(BF16), a2_ref[...])
        g = _dot(tg[:, z * hg:(z + 1) * hg].astype(BF16), g2_ref[z])
        w_log = -_softplus(-(vec_ref[z:z + 1, :] + lw)) - 0.5
        ld = -jnp.exp(w_log)
        a = _sigmoid(vec_ref[2 + z:3 + z, :] + la)
        kdir = k * (1.0 + (a - 1.0) * k_a)
        b = kkn * a
        parts = _split3(ld)
        cmz = cm_ref[z]
        lm = _dot(cmz, parts[0]) + _dot(cmz, parts[1]) + _dot(cmz, parts[2])
        e_in = jnp.exp(lm)
        e_inv = jnp.exp(-lm)
        e_ex = jnp.exp(lm - ld)
        rt_ref[z] = (r * e_in).astype(rt_ref.dtype)
        at_ref[z] = (-kkn * e_ex).astype(at_ref.dtype)
        bt_ref[z] = (b * e_inv).astype(bt_ref.dtype)
        kt_ref[z] = (kdir * e_inv).astype(kt_ref.dtype)
        edz = edm_ref[z]
        ed = jnp.exp(_dot(edz, parts[0]) + _dot(edz, parts[1]) + _dot(edz, parts[2]))
        for ck in range(ed.shape[0] // 8):
            ed_ref[z, ck] = ed[ck * 8:(ck + 1) * 8]
        g_ref[z] = g.astype(g_ref.dtype)
        bonus = _seg_sum(r * kdir * r_k, seg_ref, segt_ref) * v
        gb = gb + g * bonus
    gb_ref[...] = gb


def _chunk_matrices(ts, c):
    t = np.arange(ts)
    same = (t[:, None] // c) == (t[None, :] // c)
    pos = t % c
    cm = np.zeros((2, ts, ts), np.float32)
    nck = ts // c
    edm = np.zeros((2, 8 * nck, ts), np.float32)
    tri_f = same & (t[None, :] <= t[:, None])
    sel_f = same & (pos[None, :] <= c // 2 - 1)
    cm[0] = tri_f.astype(np.float32) - sel_f.astype(np.float32)
    tri_b = same & (t[None, :] >= t[:, None])
    sel_b = same & (pos[None, :] >= c // 2)
    cm[1] = tri_b.astype(np.float32) - sel_b.astype(np.float32)
    for ck in range(nck):
        inck = (t // c) == ck
        edm[0, 8 * ck] = inck & (pos <= c // 2 - 1)
        edm[0, 8 * ck + 1] = inck & (pos > c // 2 - 1)
        edm[1, 8 * ck] = inck & (pos >= c // 2)
        edm[1, 8 * ck + 1] = inck & (pos < c // 2)
    return jnp.asarray(cm, BF16), jnp.asarray(edm, BF16)


def _seg_matrices(d, head):
    seg = np.zeros((d, 128), np.float32)
    seg[np.arange(d), np.arange(d) // head] = 1.0
    return jnp.asarray(seg, BF16), jnp.asarray(seg.T.copy(), BF16)


def _rwkv_prep(hn, bsz, p, scan_dtype, ts=128):
    t, d = hn.shape
    s = t // bsz
    ts = min(ts, s)
    ns = s // ts
    c = SCAN_CHUNK
    assert ts % c == 0 and ts % 8 == 0
    nck = ts // c
    cm, edm = _chunk_matrices(ts, c)
    seg, segt = _seg_matrices(d, RWKV_HEAD)
    r8 = ts // 8
    nblk8 = t // 8
    row = lambda b, i: (b * ns + i, 0)
    full = lambda a: pl.BlockSpec(a.shape, lambda b, i, _n=a.ndim: (0,) * _n)
    weights = [p["mu8"], p["w_rkv"], p["w1c"], p["a1c"], p["g1c"], p["w2s"], p["a2s"], p["g2"],
               p["vec8"], cm, edm, seg, segt]
    dir_spec = pl.BlockSpec((2, ts, d), lambda b, i: (0, b * ns + i, 0))
    outs = pl.pallas_call(
        functools.partial(_rwkv_prep_kernel, ts=ts, ns=ns),
        grid=(bsz, ns),
        in_specs=[
            pl.BlockSpec((ts, d), row),
            pl.BlockSpec((8, d), lambda b, i: (jnp.maximum((b * ns + i) * r8 - 1, 0), 0)),
            pl.BlockSpec((8, d), lambda b, i: (jnp.minimum((b * ns + i + 1) * r8, nblk8 - 1), 0)),
        ] + [full(a) for a in weights],
        out_specs=[dir_spec, dir_spec, dir_spec, dir_spec,
                   pl.BlockSpec((ts, d), row),
                   pl.BlockSpec((2, nck, 8, d), lambda b, i: (0, b * ns + i, 0, 0)),
                   dir_spec,
                   pl.BlockSpec((ts, d), row)],
        out_shape=[jax.ShapeDtypeStruct((2, t, d), scan_dtype)] * 4 + [
            jax.ShapeDtypeStruct((t, d), scan_dtype),
            jax.ShapeDtypeStruct((2, t // c, 8, d), F32),
            jax.ShapeDtypeStruct((2, t, d), F32),
            jax.ShapeDtypeStruct((t, d), F32)],
        compiler_params=_params(("parallel", "parallel")),
        name="rwkv_prep",
    )(hn, hn, hn, *weights)
    return outs


def _sdot(a, b, dims):
    return lax.dot_general(a, b, (dims, ((), ())), precision=SCAN_PREC, preferred_element_type=F32)


def _scan_kernel(at_ref, rt_ref, bt_ref, kt_ref, v_ref, ed_ref, y_ref, s_ref, *, c, n_heads):
    z = pl.program_id(1)
    ci = pl.program_id(2)

    @pl.when(ci == 0)
    def _():
        s_ref[...] = jnp.zeros_like(s_ref)

    c2 = 2 * c
    pw_lanes = 2 * RWKV_HEAD
    row = lax.broadcasted_iota(jnp.int32, (c2, c2), 0)
    col = lax.broadcasted_iota(jnp.int32, (c2, c2), 1)
    same_blk = (row // c) == (col // c)
    diff = jnp.where(z == 0, row - col, col - row)
    strict = same_blk & (diff > 0)
    incl = same_blk & (diff >= 0)
    incl2 = jnp.concatenate([incl, incl], axis=1)
    lane_a = lax.broadcasted_iota(jnp.int32, (c, pw_lanes), 1) < RWKV_HEAD
    same_head = ((lax.broadcasted_iota(jnp.int32, (pw_lanes, pw_lanes), 0) // RWKV_HEAD)
                 == (lax.broadcasted_iota(jnp.int32, (pw_lanes, pw_lanes), 1) // RWKV_HEAD))
    n_steps = int(math.log2(c))
    nt = (((1,), (1,)))
    nn = (((1,), (0,)))
    tn = (((0,), (0,)))
    for hp in range(n_heads // 2):
        sl = pl.ds(hp * pw_lanes, pw_lanes)
        a2 = at_ref[:, sl].astype(F32)
        r2 = rt_ref[:, sl].astype(F32)
        b2 = bt_ref[:, sl].astype(F32)
        k2 = kt_ref[:, sl].astype(F32)
        v2 = v_ref[:, sl].astype(F32)
        e = ed_ref[0:1, sl]
        dd = ed_ref[1:2, sl]
        st = s_ref[hp] * e
        lhs = jnp.concatenate([jnp.where(lane_a, a2, 0.0), jnp.where(lane_a, 0.0, a2),
                               jnp.where(lane_a, r2, 0.0), jnp.where(lane_a, 0.0, r2)], axis=0)
        rhs = jnp.concatenate([b2, b2, k2, k2], axis=0)
        vv = jnp.concatenate([v2, v2], axis=0)
        gm = _sdot(lhs, rhs, nt)
        q = _sdot(lhs, st, nt)
        x = q[:c2] + _sdot(jnp.where(strict, gm[:c2, c2:], 0.0), vv, nn)
        pw = jnp.where(strict, gm[:c2, :c2], 0.0)
        for step in range(n_steps):
            x = x + _sdot(pw, x, nn)
            if step + 1 < n_steps:
                pw = _sdot(pw, pw, nn)
        uvs = jnp.concatenate([x, vv], axis=0)
        ys = q[c2:] + _sdot(jnp.where(incl2, gm[c2:, :], 0.0), uvs, nn)
        y_ref[:, sl] = jnp.where(lane_a, ys[:c], ys[c:])
        u2 = jnp.where(lane_a, x[:c], x[c:])
        upd = _sdot(jnp.concatenate([u2, v2], axis=0), jnp.concatenate([b2, k2], axis=0), tn)
        s_ref[hp] = (st + jnp.where(same_head, upd, 0.0)) * dd


def _rwkv_scan(at, rt, bt, kt, v, ed, bsz):
    _, t, d = at.shape
    s = t // bsz
    c = SCAN_CHUNK
    nc = s // c
    n_heads = d // RWKV_HEAD

    def cidx(b, z, ci):
        return b * nc + ci + z * (nc - 1 - 2 * ci)

    dspec = pl.BlockSpec((None, c, d), lambda b, z, ci: (z, cidx(b, z, ci), 0))
    return pl.pallas_call(
        functools.partial(_scan_kernel, c=c, n_heads=n_heads),
        grid=(bsz, 2, nc),
        in_specs=[dspec, dspec, dspec, dspec,
                  pl.BlockSpec((c, d), lambda b, z, ci: (cidx(b, z, ci), 0)),
                  pl.BlockSpec((None, None, 8, d), lambda b, z, ci: (z, cidx(b, z, ci), 0, 0))],
        out_specs=dspec,
        out_shape=jax.ShapeDtypeStruct((2, t, d), F32),
        scratch_shapes=[pltpu.VMEM((n_heads // 2, 2 * RWKV_HEAD, 2 * RWKV_HEAD), F32)],
        compiler_params=_params(("parallel", "parallel", "arbitrary")),
        name="rwkv_scan",
    )(at, rt, bt, kt, v, ed)


def _rwkv_out_kernel(y_ref, g_ref, gb_ref, x_ref, ga_ref, ln_ref, wo_ref, seg_ref, segt_ref, o_ref):
    o = gb_ref[...]
    inv_n = 1.0 / RWKV_HEAD
    for z in range(2):
        y = y_ref[z]
        mean = _seg_sum(y, seg_ref, segt_ref) * inv_n
        yc = y - mean
        var = _seg_sum(yc * yc, seg_ref, segt_ref) * inv_n
        yn = yc * lax.rsqrt(var + RWKV_GN_EPS) * ln_ref[0:1, :] + ln_ref[1:2, :]
        o = o + g_ref[z] * yn
    o_ref[...] = x_ref[...] + ga_ref[...] * _dot(o.astype(BF16), wo_ref[...])


def _rwkv_out(y, g, gb, x2, gate, ln8, w_o, bsz, tm=256):
    t, d = x2.shape
    s = t // bsz
    tm = min(tm, s)
    ns = s // tm
    seg, segt = _seg_matrices(d, RWKV_HEAD)
    row = lambda b, i: (b * ns + i, 0)
    dir_spec = pl.BlockSpec((2, tm, d), lambda b, i: (0, b * ns + i, 0))
    full = lambda a: pl.BlockSpec(a.shape, lambda b, i, _n=a.ndim: (0,) * _n)
    return pl.pallas_call(
        _rwkv_out_kernel,
        grid=(bsz, ns),
        in_specs=[dir_spec, dir_spec, pl.BlockSpec((tm, d), row), pl.BlockSpec((tm, d), row),
                  pl.BlockSpec((None, 1, d), lambda b, i: (b, 0, 0)),
                  full(ln8), full(w_o), full(seg), full(segt)],
        out_specs=pl.BlockSpec((tm, d), row),
        out_shape=jax.ShapeDtypeStruct((t, d), F32),
        compiler_params=_params(("parallel", "parallel")),
        name="rwkv_out",
    )(y, g, gb, x2, gate, ln8, w_o, seg, segt)


def _rwkv_layer(x2, bsz, hn, gate, p):
    at, rt, bt, kt, v, ed, g, gb = _rwkv_prep(hn, bsz, p, F32)
    y = _rwkv_scan(at, rt, bt, kt, v, ed, bsz)
    return _rwkv_out(y, g, gb, x2, gate, p["ln8"], p["w_o"], bsz)


def _attn_kernel(slope_ref, q1, q2, q3, k1, k2, k3, v1, v2, v3, o_ref, *, tq, s, wins):
    h = pl.program_id(1)
    t0 = pl.program_id(2) * tq
    qs = (q1, q2, q3)
    ks = (k1, k2, k3)
    vs = (v1, v2, v3)
    scale = ATTN_HEAD_DIM ** -0.5
    scores = []
    starts = []
    for gi, (window, dil) in enumerate(ATTN_GROUPS):
        lim = window // 2
        w = wins[gi]
        if w == s:
            ws = 0
            kblk = ks[gi][...]
        else:
            ws = pl.multiple_of(jnp.clip(t0 - lim, 0, s - w), 16)
            kblk = ks[gi][pl.ds(ws, w), :]
        sc = lax.dot_general(qs[gi][...], kblk, ((((1,), (1,))), ((), ())), preferred_element_type=F32)
        rel = (lax.broadcasted_iota(jnp.int32, (tq, w), 1) + ws) - (lax.broadcasted_iota(jnp.int32, (tq, w), 0) + t0)
        dist = jnp.abs(rel)
        valid = (dist <= lim) & ((rel & (dil - 1)) == 0)
        slope = slope_ref[gi, h]
        sc = jnp.where(valid, sc * scale - slope * dist.astype(F32), NEG_INF)
        scores.append(sc)
        starts.append(ws)
    m = scores[0].max(axis=-1, keepdims=True)
    for sc in scores[1:]:
        m = jnp.maximum(m, sc.max(axis=-1, keepdims=True))
    den = jnp.zeros_like(m)
    acc = jnp.zeros((tq, ATTN_HEAD_DIM), F32)
    for gi in range(3):
        p = jnp.exp(scores[gi] - m)
        den = den + p.sum(axis=-1, keepdims=True)
        w = wins[gi]
        vblk = vs[gi][...] if w == s else vs[gi][pl.ds(starts[gi], w), :]
        acc = acc + _dot(p.astype(BF16), vblk)
    o_ref[...] = (acc / den).astype(o_ref.dtype)


def _alibi_slopes(n):
    return 2.0 ** (-8.0 * jnp.arange(1, n + 1, dtype=F32) / n)


def _attention(qkv, bsz, tq=256):
    t = qkv.shape[0]
    s = t // bsz
    tq = min(tq, s)
    nq = s // tq
    dh = ATTN_HEAD_DIM
    nh = ATTN_HEADS
    wins = tuple(min(s, tq + window) for window, _ in ATTN_GROUPS)
    slopes = _alibi_slopes(len(ATTN_GROUPS) * nh).reshape(len(ATTN_GROUPS), nh)

    def qspec(gi):
        return pl.BlockSpec((tq, dh), lambda b, h, i: (b * nq + i, (gi * 3) * nh + h))

    def kvspec(gi, which):
        return pl.BlockSpec((s, dh), lambda b, h, i: (b, (gi * 3 + which) * nh + h))

    return pl.pallas_call(
        functools.partial(_attn_kernel, tq=tq, s=s, wins=wins),
        grid=(bsz, nh, nq),
        in_specs=[pl.BlockSpec(memory_space=pltpu.SMEM)]
        + [qspec(g) for g in range(3)] + [kvspec(g, 1) for g in range(3)] + [kvspec(g, 2) for g in range(3)],
        out_specs=pl.BlockSpec((tq, dh), lambda b, h, i: (b * nq + i, h)),
        out_shape=jax.ShapeDtypeStruct((t, nh * dh), BF16),
        compiler_params=_params(("parallel", "parallel", "parallel")),
        name="dilated_attention",
    )(slopes, qkv, qkv, qkv, qkv, qkv, qkv, qkv, qkv, qkv)


def _attn_layer(x2, bsz, hn_bf16, gate, p):
    qkv = _matmul(hn_bf16, p["w_qkv"], BF16)
    merged = _attention(qkv, bsz)
    return _proj_residual(merged, p["w_o"], x2, gate, bsz)


def _router_kernel(h_ref, wh_ref, wl_ref, b_ref, o_ref):
    hh, hl = _split2(h_ref[...])
    logits = _dot(hh, wh_ref[...]) + _dot(hl, wh_ref[...]) + _dot(hh, wl_ref[...]) + b_ref[...]
    lane = lax.broadcasted_iota(jnp.int32, logits.shape, 1)
    ng, ne = N_EXPERT_GROUPS, EXPERTS_PER_GROUP
    big = jnp.int32(ROUTER_LANES)
    is_g = lane < ng
    gl = jnp.where(is_g, logits, -jnp.inf)
    gmax = gl.max(axis=-1, keepdims=True)
    gsum = jnp.where(is_g, jnp.exp(logits - gmax), 0.0).sum(axis=-1, keepdims=True)
    p_group = 1.0 / gsum
    g_top = jnp.where(gl == gmax, lane, big).min(axis=-1, keepdims=True)
    lo = ng + ne * g_top
    es = jnp.where((lane >= lo) & (lane < lo + ne), logits, -jnp.inf)
    v1 = es.max(axis=-1, keepdims=True)
    i1 = jnp.where(es == v1, lane, big).min(axis=-1, keepdims=True)
    es2 = jnp.where(lane == i1, -jnp.inf, es)
    v2 = es2.max(axis=-1, keepdims=True)
    i2 = jnp.where(es2 == v2, lane, big).min(axis=-1, keepdims=True)
    e2 = jnp.exp(v2 - v1)
    w1 = p_group / (1.0 + e2)
    w2 = p_group * e2 / (1.0 + e2)
    o_ref[...] = jnp.where(lane == i1, w1, 0.0) + jnp.where(lane == i2, w2, 0.0)


def _router(hn, wr_hi, wr_lo, bias, tm=512):
    t, d = hn.shape
    tm = min(tm, t)
    full = lambda a: pl.BlockSpec(a.shape, lambda i: (0, 0))
    return pl.pallas_call(
        _router_kernel,
        grid=(t // tm,),
        in_specs=[pl.BlockSpec((tm, d), lambda i: (i, 0)), full(wr_hi), full(wr_lo), full(bias)],
        out_specs=pl.BlockSpec((tm, ROUTER_LANES), lambda i: (i, 0)),
        out_shape=jax.ShapeDtypeStruct((t, ROUTER_LANES), F32),
        compiler_params=_params(("parallel",)),
        name="moe_router",
    )(hn, wr_hi, wr_lo, bias)


def _experts_kernel(h_ref, gates_ref, wgu_ref, wd_ref, x_ref, ga_ref, o_ref, acc_ref, *, f):
    e = pl.program_id(1)

    @pl.when(e == 0)
    def _():
        acc_ref[...] = jnp.zeros_like(acc_ref)

    gates = gates_ref[...]
    lane = lax.broadcasted_iota(jnp.int32, gates.shape, 1)
    ge = jnp.where(lane == N_EXPERT_GROUPS + e, gates, 0.0).sum(axis=-1, keepdims=True)
    gu = _dot(h_ref[...], wgu_ref[...])
    gp = gu[:, :f]
    hid = gp * _sigmoid(gp) * gu[:, f:] * ge
    acc_ref[...] += _dot(hid.astype(BF16), wd_ref[...])

    @pl.when(e == pl.num_programs(1) - 1)
    def _():
        o_ref[...] = x_ref[...] + ga_ref[...] * acc_ref[...]


def _experts(hn_bf16, gates, wgu, wd, x2, gate, bsz, tm=1024):
    t, d = x2.shape
    s = t // bsz
    tm = min(tm, s)
    ns = s // tm
    ne, _, f2 = wgu.shape
    f = f2 // 2
    row = lambda i, e: (i, 0)
    return pl.pallas_call(
        functools.partial(_experts_kernel, f=f),
        grid=(t // tm, ne),
        in_specs=[
            pl.BlockSpec((tm, d), row),
            pl.BlockSpec((tm, ROUTER_LANES), row),
            pl.BlockSpec((None, d, f2), lambda i, e: (e, 0, 0)),
            pl.BlockSpec((None, f, d), lambda i, e: (e, 0, 0)),
            pl.BlockSpec((tm, d), row),
            pl.BlockSpec((None, 1, d), lambda i, e: (i // ns, 0, 0)),
        ],
        out_specs=pl.BlockSpec((tm, d), row),
        out_shape=jax.ShapeDtypeStruct((t, d), F32),
        scratch_shapes=[pltpu.VMEM((tm, d), F32)],
        compiler_params=_params(("parallel", "arbitrary")),
        name="moe_experts",
    )(hn_bf16, gates, wgu, wd, x2, gate)


def _moe_layer(x2, bsz, hn, hn_bf16, gate, p):
    gates = _router(hn, p["wr_hi"], p["wr_lo"], p["r_bias"])
    return _experts(hn_bf16, gates, p["wgu"], p["wd"], x2, gate, bsz)


def _pad_rows(rows, d):
    out = jnp.zeros((8, d), F32)
    return out.at[:len(rows)].set(jnp.stack([r.reshape(d).astype(F32) for r in rows]))


def _pack_rwkv(j, mu, w_rkv, w0, w1, w2, a0, a1, a2, g1, g2, k_k, k_a, r_k, ln_w, ln_b, w_o):
    d = w_o.shape[-1]
    cat = lambda m: jnp.concatenate([m[j, 0], m[j, 1]], axis=1).astype(BF16)
    stack = lambda m: jnp.concatenate([m[j, 0], m[j, 1]], axis=0).astype(BF16)
    return {
        "mu8": _pad_rows(list(mu[j]), d),
        "w_rkv": w_rkv[j].astype(BF16),
        "w1c": cat(w1), "a1c": cat(a1), "g1c": cat(g1),
        "w2s": stack(w2), "a2s": stack(a2), "g2": g2[j].astype(BF16),
        "vec8": _pad_rows([w0[j, 0], w0[j, 1], a0[j, 0], a0[j, 1], k_k[j], k_a[j], r_k[j]], d),
        "ln8": _pad_rows([ln_w[j], ln_b[j]], d),
        "w_o": w_o[j].astype(BF16),
    }


def _pack_moe(i, router_g, router_g_b, router_e, router_e_b, w_gate, w_up, w_down):
    d = router_g.shape[1]
    ng, ne = N_EXPERT_GROUPS, EXPERTS_PER_GROUP
    wr = jnp.zeros((d, ROUTER_LANES), F32)
    wr = wr.at[:, :ng].set(router_g[i])
    wr = wr.at[:, ng:ng + ng * ne].set(router_e[i].transpose(1, 0, 2).reshape(d, ng * ne))
    bias = jnp.zeros((1, ROUTER_LANES), F32)
    bias = bias.at[0, :ng].set(router_g_b[i])
    bias = bias.at[0, ng:ng + ng * ne].set(router_e_b[i].reshape(ng * ne))
    wr_hi = wr.astype(BF16)
    wr_lo = (wr - wr_hi.astype(F32)).astype(BF16)
    f = w_gate.shape[-1]
    wgu = jnp.concatenate([w_gate[i], w_up[i]], axis=-1).reshape(ng * ne, d, 2 * f).astype(BF16)
    wd = w_down[i].reshape(ng * ne, f, d).astype(BF16)
    return {"wr_hi": wr_hi, "wr_lo": wr_lo, "r_bias": bias, "wgu": wgu, "wd": wd}


def kernel(x, c, ada_w, ada_b, norm_tm_g, norm_cm_g, rw_mu, rw_w_rkv, rw_w0, rw_w1, rw_w2, rw_a0, rw_a1, rw_a2, rw_g1, rw_g2, rw_k_k, rw_k_a, rw_r_k, rw_ln_w, rw_ln_b, rw_w_o, at_w_qkv, at_w_o, moe_router_g, moe_router_g_b, moe_router_e, moe_router_e_b, moe_w_gate, moe_w_up, moe_w_down, final_g):
    bsz, s, d = x.shape
    depth = ada_w.shape[0]
    x2 = x.reshape(bsz * s, d)
    mod = _ada_mod(c, ada_w, ada_b)
    for i in range(depth):
        sh_t, sc_t, ga_t, sh_c, sc_c, ga_c = (mod[i, m] for m in range(N_MOD))
        j = i // 2
        if i % 2 == 0:
            (hn,) = _norm_mod(x2, norm_tm_g[i], sh_t, sc_t, bsz, (F32,))
            p = _pack_rwkv(j, rw_mu, rw_w_rkv, rw_w0, rw_w1, rw_w2, rw_a0, rw_a1, rw_a2, rw_g1, rw_g2,
                           rw_k_k, rw_k_a, rw_r_k, rw_ln_w, rw_ln_b, rw_w_o)
            x2 = _rwkv_layer(x2, bsz, hn, ga_t, p)
        else:
            (hn_b,) = _norm_mod(x2, norm_tm_g[i], sh_t, sc_t, bsz, (BF16,))
            p = {"w_qkv": at_w_qkv[j].astype(BF16), "w_o": at_w_o[j].astype(BF16)}
            x2 = _attn_layer(x2, bsz, hn_b, ga_t, p)
        hn, hn_b = _norm_mod(x2, norm_cm_g[i], sh_c, sc_c, bsz, (F32, BF16))
        pm = _pack_moe(i, moe_router_g, moe_router_g_b, moe_router_e, moe_router_e_b,
                       moe_w_gate, moe_w_up, moe_w_down)
        x2 = _moe_layer(x2, bsz, hn, hn_b, ga_c, pm)
    (out,) = _norm_mod(x2, final_g, None, None, bsz, (F32,))
    return out.reshape(bsz, s, d)
```

```python
import functools
import math

import numpy as np
import jax
import jax.numpy as jnp
from jax import lax
from jax.experimental import pallas as pl
from jax.experimental.pallas import tpu as pltpu

F32 = jnp.float32
BF16 = jnp.bfloat16

RMS_EPS = 1e-6
N_MOD = 6
RWKV_HEAD = 64
RWKV_GN_EPS = 64e-5
ATTN_GROUPS = ((128, 1), (512, 4), (2048, 16))
ATTN_HEADS = 8
ATTN_HEAD_DIM = 128
NEG_INF = -1e30
N_EXPERT_GROUPS = 4
EXPERTS_PER_GROUP = 8
N_EXPERTS = N_EXPERT_GROUPS * EXPERTS_PER_GROUP
ROUTER_LANES = 128

SCAN_CHUNK = 64
V7X_VMEM_LIMIT = 56 * 1024 * 1024
SCAN_DT = BF16


def _params(sem):
    return pltpu.CompilerParams(dimension_semantics=sem, vmem_limit_bytes=V7X_VMEM_LIMIT)


def _dot(a, b):
    return jnp.dot(a, b, preferred_element_type=F32)


def _sigmoid(x):
    return 1.0 / (1.0 + jnp.exp(-x))


def _split2(x):
    hi = x.astype(BF16)
    lo = (x - hi.astype(F32)).astype(BF16)
    return hi, lo


def _split3(x):
    h1 = x.astype(BF16)
    r1 = x - h1.astype(F32)
    h2 = r1.astype(BF16)
    h3 = (r1 - h2.astype(F32)).astype(BF16)
    return h1, h2, h3


def _ada_kernel(c_ref, w_ref, b_ref, o_ref):
    c = c_ref[...]
    sc = c * _sigmoid(c)
    o_ref[...] = _dot(sc.astype(BF16), w_ref[...].astype(BF16)) + b_ref[...]


def _ada_mod(c, ada_w, ada_b):
    depth, d, n = ada_w.shape
    bsz = c.shape[0]
    tn = n // 4
    out = pl.pallas_call(
        _ada_kernel,
        grid=(depth, n // tn),
        in_specs=[
            pl.BlockSpec((bsz, d), lambda i, j: (0, 0)),
            pl.BlockSpec((None, d, tn), lambda i, j: (i, 0, j)),
            pl.BlockSpec((None, 1, tn), lambda i, j: (i, 0, j)),
        ],
        out_specs=pl.BlockSpec((None, bsz, tn), lambda i, j: (i, 0, j)),
        out_shape=jax.ShapeDtypeStruct((depth, bsz, n), F32),
        compiler_params=_params(("parallel", "parallel")),
        name="ada_mod",
    )(c, ada_w, ada_b.reshape(depth, 1, n))
    return out.reshape(depth, bsz, N_MOD, 1, d).transpose(0, 2, 1, 3, 4)


def _norm_kernel(*refs, modulated, n_out):
    if modulated:
        x_ref, g_ref, sh_ref, sc_ref = refs[:4]
    else:
        x_ref, g_ref = refs[:2]
    outs = refs[-n_out:]
    x = x_ref[...]
    ms = jnp.mean(x * x, axis=-1, keepdims=True)
    y = x * lax.rsqrt(ms + RMS_EPS) * g_ref[...]
    if modulated:
        y = y * (1.0 + sc_ref[...]) + sh_ref[...]
    for o in outs:
        o[...] = y.astype(o.dtype)


def _norm_mod(x2, g, shift, scale, bsz, out_dtypes, ts=512):
    t, d = x2.shape
    s = t // bsz
    ts = min(ts, s)
    ns = s // ts
    modulated = shift is not None
    row_spec = pl.BlockSpec((ts, d), lambda b, i: (b * ns + i, 0))
    in_specs = [row_spec, pl.BlockSpec((1, d), lambda b, i: (0, 0))]
    args = [x2, g.reshape(1, d)]
    if modulated:
        vec = pl.BlockSpec((None, 1, d), lambda b, i: (b, 0, 0))
        in_specs += [vec, vec]
        args += [shift, scale]
    outs = pl.pallas_call(
        functools.partial(_norm_kernel, modulated=modulated, n_out=len(out_dtypes)),
        grid=(bsz, ns),
        in_specs=in_specs,
        out_specs=[row_spec] * len(out_dtypes),
        out_shape=[jax.ShapeDtypeStruct((t, d), dt) for dt in out_dtypes],
        compiler_params=_params(("parallel", "parallel")),
        name="norm_mod",
    )(*args)
    return outs


def _proj_res_kernel(a_ref, w_ref, x_ref, ga_ref, o_ref):
    o_ref[...] = x_ref[...] + ga_ref[...] * _dot(a_ref[...], w_ref[...])


def _proj_residual(a, w, x2, gate, bsz, tm=512):
    t, d = x2.shape
    s = t // bsz
    tm = min(tm, s)
    ns = s // tm
    row = lambda b, i: (b * ns + i, 0)
    return pl.pallas_call(
        _proj_res_kernel,
        grid=(bsz, ns),
        in_specs=[
            pl.BlockSpec((tm, a.shape[1]), row),
            pl.BlockSpec(w.shape, lambda b, i: (0, 0)),
            pl.BlockSpec((tm, d), row),
            pl.BlockSpec((None, 1, d), lambda b, i: (b, 0, 0)),
        ],
        out_specs=pl.BlockSpec((tm, d), row),
        out_shape=jax.ShapeDtypeStruct((t, d), F32),
        compiler_params=_params(("parallel", "parallel")),
        name="proj_residual",
    )(a, w, x2, gate)


def _mm_kernel(a_ref, b_ref, o_ref):
    o_ref[...] = _dot(a_ref[...], b_ref[...]).astype(o_ref.dtype)


def _matmul(a, b, out_dtype, tm=512, tn=1024):
    m, k = a.shape
    n = b.shape[1]
    tm, tn = min(tm, m), min(tn, n)
    return pl.pallas_call(
        _mm_kernel,
        grid=(n // tn, m // tm),
        in_specs=[
            pl.BlockSpec((tm, k), lambda j, i: (i, 0)),
            pl.BlockSpec((k, tn), lambda j, i: (0, j)),
        ],
        out_specs=pl.BlockSpec((tm, tn), lambda j, i: (i, j)),
        out_shape=jax.ShapeDtypeStruct((m, n), out_dtype),
        compiler_params=_params(("parallel", "parallel")),
        name="matmul",
    )(a, b)


def _softplus(z):
    return jnp.maximum(z, 0.0) + jnp.log(1.0 + jnp.exp(-jnp.abs(z)))


def _seg_sum(x, seg_ref, segt_ref):
    xh, xl = _split2(x)
    s = _dot(xh, seg_ref[...]) + _dot(xl, seg_ref[...])
    sh, sl = _split2(s)
    return _dot(sh, segt_ref[...]) + _dot(sl, segt_ref[...])


def _rwkv_prep_kernel(hn_ref, prev_ref, next_ref, mu_ref, wrkv_ref, w1_ref, a1_ref, g1_ref,
                      w2_ref, a2_ref, g2_ref, vec_ref, cm_ref, edm_ref, seg_ref, segt_ref,
                      at_ref, rt_ref, bt_ref, kt_ref, v_ref, ed_ref, g_ref, gb_ref, *, ts, ns):
    i = pl.program_id(1)
    cur = hn_ref[...]
    row = lax.broadcasted_iota(jnp.int32, (ts, 1), 0)
    prev_row = jnp.where(i > 0, prev_ref[7:8, :], 0.0)
    next_row = jnp.where(i < ns - 1, next_ref[0:1, :], 0.0)
    x_prev = jnp.where(row == 0, prev_row, pltpu.roll(cur, 1, 0))
    x_next = jnp.where(row == ts - 1, next_row, pltpu.roll(cur, ts - 1, 0))
    xx = 0.5 * (x_prev + x_next) - cur

    def mix(j):
        return (cur + xx * mu_ref[j:j + 1, :]).astype(BF16)

    r = _dot(mix(0), wrkv_ref[0])
    k = _dot(mix(1), wrkv_ref[1])
    v = _dot(mix(2), wrkv_ref[2])
    tw = jnp.tanh(_dot(mix(3), w1_ref[...]))
    ta = _dot(mix(4), a1_ref[...])
    tg = _sigmoid(_dot(mix(5), g1_ref[...]))
    lane = lax.broadcasted_iota(jnp.int32, (1, tw.shape[1]), 1)
    half = tw.shape[1] // 2
    hg = tg.shape[1] // 2

    k_k = vec_ref[4:5, :]
    k_a = vec_ref[5:6, :]
    r_k = vec_ref[6:7, :]
    kk = k * k_k
    ss = _seg_sum(kk * kk, seg_ref, segt_ref)
    kkn = kk / jnp.maximum(jnp.sqrt(ss), 1e-12)
    v_ref[...] = v.astype(v_ref.dtype)

    gb = jnp.zeros_like(cur)
    for z in range(2):
        sel = (lane < half) if z == 0 else (lane >= half)
        lw = _dot(jnp.where(sel, tw, 0.0).astype(BF16), w2_ref[...])
        la = _dot(jnp.where(sel, ta, 0.0).astype(BF16), a2_ref[...])
        g = _dot(tg[:, z * hg:(z + 1) * hg].astype(BF16), g2_ref[z])
        w_log = -_softplus(-(vec_ref[z:z + 1, :] + lw)) - 0.5
        ld = -jnp.exp(w_log)
        a = _sigmoid(vec_ref[2 + z:3 + z, :] + la)
        kdir = k * (1.0 + (a - 1.0) * k_a)
        b = kkn * a
        parts = _split3(ld)
        cmz = cm_ref[z]
        lm = _dot(cmz, parts[0]) + _dot(cmz, parts[1]) + _dot(cmz, parts[2])
        e_in = jnp.exp(lm)
        e_inv = jnp.exp(-lm)
        e_ex = jnp.exp(lm - ld)
        rt_ref[z] = (r * e_in).astype(rt_ref.dtype)
        at_ref[z] = (-kkn * e_ex).astype(at_ref.dtype)
        bt_ref[z] = (b * e_inv).astype(bt_ref.dtype)
        kt_ref[z] = (kdir * e_inv).astype(kt_ref.dtype)
        edz = edm_ref[z]
        ed = jnp.exp(_dot(edz, parts[0]) + _dot(edz, parts[1]) + _dot(edz, parts[2]))
        for ck in range(ed.shape[0] // 8):
            ed_ref[z, ck] = ed[ck * 8:(ck + 1) * 8]
        g_ref[z] = g.astype(g_ref.dtype)
        bonus = _seg_sum(r * kdir * r_k, seg_ref, segt_ref) * v
        gb = gb + g * bonus
    gb_ref[...] = gb


def _chunk_matrices(ts, c):
    t = np.arange(ts)
    same = (t[:, None] // c) == (t[None, :] // c)
    pos = t % c
    cm = np.zeros((2, ts, ts), np.float32)
    nck = ts // c
    edm = np.zeros((2, 8 * nck, ts), np.float32)
    tri_f = same & (t[None, :] <= t[:, None])
    sel_f = same & (pos[None, :] <= c // 2 - 1)
    cm[0] = tri_f.astype(np.float32) - sel_f.astype(np.float32)
    tri_b = same & (t[None, :] >= t[:, None])
    sel_b = same & (pos[None, :] >= c // 2)
    cm[1] = tri_b.astype(np.float32) - sel_b.astype(np.float32)
    for ck in range(nck):
        inck = (t // c) == ck
        edm[0, 8 * ck] = inck & (pos <= c // 2 - 1)
        edm[0, 8 * ck + 1] = inck & (pos > c // 2 - 1)
        edm[1, 8 * ck] = inck & (pos >= c // 2)
        edm[1, 8 * ck + 1] = inck & (pos < c // 2)
    return jnp.asarray(cm, BF16), jnp.asarray(edm, BF16)


def _seg_matrices(d, head):
    seg = np.zeros((d, 128), np.float32)
    seg[np.arange(d), np.arange(d) // head] = 1.0
    return jnp.asarray(seg, BF16), jnp.asarray(seg.T.copy(), BF16)


def _rwkv_prep(hn, bsz, p, scan_dtype, ts=128):
    t, d = hn.shape
    s = t // bsz
    ts = min(ts, s)
    ns = s // ts
    c = SCAN_CHUNK
    assert ts % c == 0 and ts % 8 == 0
    nck = ts // c
    cm, edm = _chunk_matrices(ts, c)
    seg, segt = _seg_matrices(d, RWKV_HEAD)
    r8 = ts // 8
    nblk8 = t // 8
    row = lambda b, i: (b * ns + i, 0)
    full = lambda a: pl.BlockSpec(a.shape, lambda b, i, _n=a.ndim: (0,) * _n)
    weights = [p["mu8"], p["w_rkv"], p["w1c"], p["a1c"], p["g1c"], p["w2s"], p["a2s"], p["g2"],
               p["vec8"], cm, edm, seg, segt]
    dir_spec = pl.BlockSpec((2, ts, d), lambda b, i: (0, b * ns + i, 0))
    outs = pl.pallas_call(
        functools.partial(_rwkv_prep_kernel, ts=ts, ns=ns),
        grid=(bsz, ns),
        in_specs=[
            pl.BlockSpec((ts, d), row),
            pl.BlockSpec((8, d), lambda b, i: (jnp.maximum((b * ns + i) * r8 - 1, 0), 0)),
            pl.BlockSpec((8, d), lambda b, i: (jnp.minimum((b * ns + i + 1) * r8, nblk8 - 1), 0)),
        ] + [full(a) for a in weights],
        out_specs=[dir_spec, dir_spec, dir_spec, dir_spec,
                   pl.BlockSpec((ts, d), row),
                   pl.BlockSpec((2, nck, 8, d), lambda b, i: (0, b * ns + i, 0, 0)),
                   dir_spec,
                   pl.BlockSpec((ts, d), row)],
        out_shape=[jax.ShapeDtypeStruct((2, t, d), scan_dtype)] * 4 + [
            jax.ShapeDtypeStruct((t, d), scan_dtype),
            jax.ShapeDtypeStruct((2, t // c, 8, d), F32),
            jax.ShapeDtypeStruct((2, t, d), F32),
            jax.ShapeDtypeStruct((t, d), F32)],
        compiler_params=_params(("parallel", "parallel")),
        name="rwkv_prep",
    )(hn, hn, hn, *weights)
    return outs


def _sdot(a, b, dims):
    return lax.dot_general(a.astype(SCAN_DT), b.astype(SCAN_DT), (dims, ((), ())),
                           preferred_element_type=F32)


def _scan_kernel(at_ref, rt_ref, bt_ref, kt_ref, v_ref, ed_ref, y_ref, s_ref, *, c, n_heads):
    z = pl.program_id(1)
    ci = pl.program_id(2)

    @pl.when(ci == 0)
    def _():
        s_ref[...] = jnp.zeros_like(s_ref)

    c2 = 2 * c
    pw_lanes = 2 * RWKV_HEAD
    row = lax.broadcasted_iota(jnp.int32, (c2, c2), 0)
    col = lax.broadcasted_iota(jnp.int32, (c2, c2), 1)
    same_blk = (row // c) == (col // c)
    diff = jnp.where(z == 0, row - col, col - row)
    strict = same_blk & (diff > 0)
    incl = same_blk & (diff >= 0)
    incl2 = jnp.concatenate([incl, incl], axis=1)
    lane_a = lax.broadcasted_iota(jnp.int32, (c, pw_lanes), 1) < RWKV_HEAD
    same_head = ((lax.broadcasted_iota(jnp.int32, (pw_lanes, pw_lanes), 0) // RWKV_HEAD)
                 == (lax.broadcasted_iota(jnp.int32, (pw_lanes, pw_lanes), 1) // RWKV_HEAD))
    n_steps = int(math.log2(c))
    nt = (((1,), (1,)))
    nn = (((1,), (0,)))
    tn = (((0,), (0,)))
    pairs = range(n_heads // 2)
    sls = [pl.ds(hp * pw_lanes, pw_lanes) for hp in pairs]
    zero = jnp.zeros((c, pw_lanes), SCAN_DT)
    lhs, rhs, vv, st = [], [], [], []
    for hp in pairs:
        a2 = at_ref[:, sls[hp]]
        r2 = rt_ref[:, sls[hp]]
        b2 = bt_ref[:, sls[hp]]
        k2 = kt_ref[:, sls[hp]]
        v2 = v_ref[:, sls[hp]]
        lhs.append(jnp.concatenate([jnp.where(lane_a, a2, zero), jnp.where(lane_a, zero, a2),
                                    jnp.where(lane_a, r2, zero), jnp.where(lane_a, zero, r2)], axis=0))
        rhs.append(jnp.concatenate([b2, b2, k2, k2], axis=0))
        vv.append(jnp.concatenate([v2, v2], axis=0))
        st.append(s_ref[hp] * ed_ref[0:1, sls[hp]])
    gm = [_sdot(lhs[hp], rhs[hp], nt) for hp in pairs]
    q = [_sdot(lhs[hp], st[hp], nt) for hp in pairs]
    pw = [jnp.where(strict, gm[hp][:c2, :c2], 0.0).astype(SCAN_DT) for hp in pairs]
    m2 = [jnp.where(strict, gm[hp][:c2, c2:], 0.0).astype(SCAN_DT) for hp in pairs]
    m34 = [jnp.where(incl2, gm[hp][c2:, :], 0.0).astype(SCAN_DT) for hp in pairs]
    x = [q[hp][:c2] + _sdot(m2[hp], vv[hp], nn) for hp in pairs]
    for step in range(n_steps):
        x = [x[hp] + _sdot(pw[hp], x[hp], nn) for hp in pairs]
        if step + 1 < n_steps:
            pw = [_sdot(pw[hp], pw[hp], nn).astype(SCAN_DT) for hp in pairs]
    for hp in pairs:
        xs = x[hp].astype(SCAN_DT)
        uvs = jnp.concatenate([xs, vv[hp]], axis=0)
        ys = q[hp][c2:] + _sdot(m34[hp], uvs, nn)
        y_ref[:, sls[hp]] = jnp.where(lane_a, ys[:c], ys[c:])
        u2 = jnp.where(lane_a, xs[:c], xs[c:])
        upd = _sdot(jnp.concatenate([u2, vv[hp][:c]], axis=0), rhs[hp][c:3 * c], tn)
        s_ref[hp] = (st[hp] + jnp.where(same_head, upd, 0.0)) * ed_ref[1:2, sls[hp]]


def _rwkv_scan(at, rt, bt, kt, v, ed, bsz):
    _, t, d = at.shape
    s = t // bsz
    c = SCAN_CHUNK
    nc = s // c
    n_heads = d // RWKV_HEAD

    def cidx(b, z, ci):
        return b * nc + ci + z * (nc - 1 - 2 * ci)

    dspec = pl.BlockSpec((None, c, d), lambda b, z, ci: (z, cidx(b, z, ci), 0))
    return pl.pallas_call(
        functools.partial(_scan_kernel, c=c, n_heads=n_heads),
        grid=(bsz, 2, nc),
        in_specs=[dspec, dspec, dspec, dspec,
                  pl.BlockSpec((c, d), lambda b, z, ci: (cidx(b, z, ci), 0)),
                  pl.BlockSpec((None, None, 8, d), lambda b, z, ci: (z, cidx(b, z, ci), 0, 0))],
        out_specs=dspec,
        out_shape=jax.ShapeDtypeStruct((2, t, d), F32),
        scratch_shapes=[pltpu.VMEM((n_heads // 2, 2 * RWKV_HEAD, 2 * RWKV_HEAD), F32)],
        compiler_params=_params(("parallel", "parallel", "arbitrary")),
        name="rwkv_scan",
    )(at, rt, bt, kt, v, ed)


def _rwkv_out_kernel(y_ref, g_ref, gb_ref, x_ref, ga_ref, ln_ref, wo_ref, seg_ref, segt_ref, o_ref):
    o = gb_ref[...]
    inv_n = 1.0 / RWKV_HEAD
    for z in range(2):
        y = y_ref[z]
        mean = _seg_sum(y, seg_ref, segt_ref) * inv_n
        yc = y - mean
        var = _seg_sum(yc * yc, seg_ref, segt_ref) * inv_n
        yn = yc * lax.rsqrt(var + RWKV_GN_EPS) * ln_ref[0:1, :] + ln_ref[1:2, :]
        o = o + g_ref[z] * yn
    o_ref[...] = x_ref[...] + ga_ref[...] * _dot(o.astype(BF16), wo_ref[...])


def _rwkv_out(y, g, gb, x2, gate, ln8, w_o, bsz, tm=256):
    t, d = x2.shape
    s = t // bsz
    tm = min(tm, s)
    ns = s // tm
    seg, segt = _seg_matrices(d, RWKV_HEAD)
    row = lambda b, i: (b * ns + i, 0)
    dir_spec = pl.BlockSpec((2, tm, d), lambda b, i: (0, b * ns + i, 0))
    full = lambda a: pl.BlockSpec(a.shape, lambda b, i, _n=a.ndim: (0,) * _n)
    return pl.pallas_call(
        _rwkv_out_kernel,
        grid=(bsz, ns),
        in_specs=[dir_spec, dir_spec, pl.BlockSpec((tm, d), row), pl.BlockSpec((tm, d), row),
                  pl.BlockSpec((None, 1, d), lambda b, i: (b, 0, 0)),
                  full(ln8), full(w_o), full(seg), full(segt)],
        out_specs=pl.BlockSpec((tm, d), row),
        out_shape=jax.ShapeDtypeStruct((t, d), F32),
        compiler_params=_params(("parallel", "parallel")),
        name="rwkv_out",
    )(y, g, gb, x2, gate, ln8, w_o, seg, segt)


def _rwkv_layer(x2, bsz, hn, gate, p):
    at, rt, bt, kt, v, ed, g, gb = _rwkv_prep(hn, bsz, p, SCAN_DT)
    y = _rwkv_scan(at, rt, bt, kt, v, ed, bsz)
    return _rwkv_out(y, g, gb, x2, gate, p["ln8"], p["w_o"], bsz)


def _attn_plan(s, tq):
    nq = s // tq
    plan = []
    for window, dil in ATTN_GROUPS:
        lim = window // 2
        w = min(s, tq + 2 * lim)
        if w == s:
            plan.append(dict(lim=lim, dil=dil, w=w, whole=True, offs=(), vid=(0,) * nq))
        else:
            offs = [int(np.clip(i * tq - lim, 0, s - w)) - i * tq for i in range(nq)]
            uniq = sorted(set(offs))
            plan.append(dict(lim=lim, dil=dil, w=w, whole=False, offs=tuple(uniq),
                             vid=tuple(uniq.index(o) for o in offs)))
    return plan


def _bias_table(shape, off, lim, dil, slope):
    rel = lax.broadcasted_iota(jnp.int32, shape, 1) - lax.broadcasted_iota(jnp.int32, shape, 0) + off
    dist = jnp.abs(rel)
    valid = (dist <= lim) & ((rel & (dil - 1)) == 0)
    return jnp.where(valid, -slope * dist.astype(F32), NEG_INF)


def _attn_kernel(slope_ref, vid_ref, q1, q2, q3, k1, k2, k3, v1, v2, v3, o_ref, t1, t2, t3, *, tq, s, plan):
    h = pl.program_id(0)
    b = pl.program_id(1)
    i = pl.program_id(2)
    nq = s // tq
    t0 = i * tq
    qs = (q1, q2, q3)
    ks = (k1, k2, k3)
    vs = (v1, v2, v3)
    tabs = (t1, t2, t3)

    @pl.when((b == 0) & (i == 0))
    def _():
        for gi, g in enumerate(plan):
            slope = slope_ref[gi, h]
            if g["whole"]:
                for j in range(2 * nq - 1):
                    tabs[gi][j] = _bias_table((tq, tq), (j - (nq - 1)) * tq, g["lim"], g["dil"], slope)
            else:
                for v, off in enumerate(g["offs"]):
                    tabs[gi][v] = _bias_table((tq, g["w"]), off, g["lim"], g["dil"], slope)

    nt = (((1,), (1,)), ((), ()))
    scores = []
    vals = []
    for gi, g in enumerate(plan):
        q = qs[gi][...]
        if g["whole"]:
            for j in range(nq):
                kblk = ks[gi][j * tq:(j + 1) * tq, :]
                sc = lax.dot_general(q, kblk, nt, preferred_element_type=F32)
                scores.append(sc + tabs[gi][j + (nq - 1) - i])
                vals.append((gi, j * tq, tq))
        else:
            ws = pl.multiple_of(jnp.clip(t0 - g["lim"], 0, s - g["w"]), 16)
            kblk = ks[gi][pl.ds(ws, g["w"]), :]
            sc = lax.dot_general(q, kblk, nt, preferred_element_type=F32)
            scores.append(sc + tabs[gi][vid_ref[gi, i]])
            vals.append((gi, ws, g["w"]))
    m = scores[0].max(axis=-1, keepdims=True)
    for sc in scores[1:]:
        m = jnp.maximum(m, sc.max(axis=-1, keepdims=True))
    den = jnp.zeros_like(m)
    acc = jnp.zeros((tq, ATTN_HEAD_DIM), F32)
    for sc, (gi, start, width) in zip(scores, vals):
        p = jnp.exp(sc - m)
        den = den + p.sum(axis=-1, keepdims=True)
        acc = acc + _dot(p.astype(BF16), vs[gi][pl.ds(start, width), :])
    o_ref[...] = (acc / den).astype(o_ref.dtype)


def _alibi_slopes(n):
    return 2.0 ** (-8.0 * jnp.arange(1, n + 1, dtype=F32) / n)


def _attention(qkv, bsz, tq=256):
    t = qkv.shape[0]
    s = t // bsz
    tq = min(tq, s)
    nq = s // tq
    dh = ATTN_HEAD_DIM
    nh = ATTN_HEADS
    plan = _attn_plan(s, tq)
    slopes = _alibi_slopes(len(ATTN_GROUPS) * nh).reshape(len(ATTN_GROUPS), nh)
    vid = jnp.asarray(np.array([g["vid"] for g in plan], np.int32))
    tables = [pltpu.VMEM((2 * nq - 1, tq, tq) if g["whole"] else (len(g["offs"]), tq, g["w"]), F32) for g in plan]

    def qspec(gi):
        return pl.BlockSpec((tq, dh), lambda h, b, i: (b * nq + i, (gi * 3) * nh + h))

    def kvspec(gi, which):
        return pl.BlockSpec((s, dh), lambda h, b, i: (b, (gi * 3 + which) * nh + h))

    smem = pl.BlockSpec(memory_space=pltpu.SMEM)
    return pl.pallas_call(
        functools.partial(_attn_kernel, tq=tq, s=s, plan=plan),
        grid=(nh, bsz, nq),
        in_specs=[smem, smem]
        + [qspec(g) for g in range(3)] + [kvspec(g, 1) for g in range(3)] + [kvspec(g, 2) for g in range(3)],
        out_specs=pl.BlockSpec((tq, dh), lambda h, b, i: (b * nq + i, h)),
        out_shape=jax.ShapeDtypeStruct((t, nh * dh), BF16),
        scratch_shapes=tables,
        compiler_params=_params(("arbitrary", "arbitrary", "arbitrary")),
        name="dilated_attention",
    )(slopes, vid, qkv, qkv, qkv, qkv, qkv, qkv, qkv, qkv, qkv)


def _pack_attn(j, w_qkv, w_o):
    d = w_qkv.shape[1]
    w = w_qkv[j].reshape(d, len(ATTN_GROUPS), 3, ATTN_HEADS * ATTN_HEAD_DIM)
    w = w.at[:, :, 0].multiply(ATTN_HEAD_DIM ** -0.5)
    return {"w_qkv": w.reshape(d, -1).astype(BF16), "w_o": w_o[j].astype(BF16)}


def _attn_layer(x2, bsz, hn_bf16, gate, p):
    qkv = _matmul(hn_bf16, p["w_qkv"], BF16)
    merged = _attention(qkv, bsz)
    return _proj_residual(merged, p["w_o"], x2, gate, bsz)


def _router_kernel(h_ref, wh_ref, wl_ref, b_ref, o_ref):
    hh, hl = _split2(h_ref[...])
    logits = _dot(hh, wh_ref[...]) + _dot(hl, wh_ref[...]) + _dot(hh, wl_ref[...]) + b_ref[...]
    lane = lax.broadcasted_iota(jnp.int32, logits.shape, 1)
    ng, ne = N_EXPERT_GROUPS, EXPERTS_PER_GROUP
    big = jnp.int32(ROUTER_LANES)
    is_g = lane < ng
    gl = jnp.where(is_g, logits, -jnp.inf)
    gmax = gl.max(axis=-1, keepdims=True)
    gsum = jnp.where(is_g, jnp.exp(logits - gmax), 0.0).sum(axis=-1, keepdims=True)
    p_group = 1.0 / gsum
    g_top = jnp.where(gl == gmax, lane, big).min(axis=-1, keepdims=True)
    lo = ng + ne * g_top
    es = jnp.where((lane >= lo) & (lane < lo + ne), logits, -jnp.inf)
    v1 = es.max(axis=-1, keepdims=True)
    i1 = jnp.where(es == v1, lane, big).min(axis=-1, keepdims=True)
    es2 = jnp.where(lane == i1, -jnp.inf, es)
    v2 = es2.max(axis=-1, keepdims=True)
    i2 = jnp.where(es2 == v2, lane, big).min(axis=-1, keepdims=True)
    e2 = jnp.exp(v2 - v1)
    w1 = p_group / (1.0 + e2)
    w2 = p_group * e2 / (1.0 + e2)
    o_ref[...] = jnp.where(lane == i1, w1, 0.0) + jnp.where(lane == i2, w2, 0.0)


def _router(hn, wr_hi, wr_lo, bias, tm=512):
    t, d = hn.shape
    tm = min(tm, t)
    full = lambda a: pl.BlockSpec(a.shape, lambda i: (0, 0))
    return pl.pallas_call(
        _router_kernel,
        grid=(t // tm,),
        in_specs=[pl.BlockSpec((tm, d), lambda i: (i, 0)), full(wr_hi), full(wr_lo), full(bias)],
        out_specs=pl.BlockSpec((tm, ROUTER_LANES), lambda i: (i, 0)),
        out_shape=jax.ShapeDtypeStruct((t, ROUTER_LANES), F32),
        compiler_params=_params(("parallel",)),
        name="moe_router",
    )(hn, wr_hi, wr_lo, bias)


def _experts_kernel(h_ref, gates_ref, wgu_ref, wd_ref, x_ref, ga_ref, o_ref, acc_ref, *, f):
    e = pl.program_id(1)

    @pl.when(e == 0)
    def _():
        acc_ref[...] = jnp.zeros_like(acc_ref)

    gates = gates_ref[...]
    lane = lax.broadcasted_iota(jnp.int32, gates.shape, 1)
    ge = jnp.where(lane == N_EXPERT_GROUPS + e, gates, 0.0).sum(axis=-1, keepdims=True)
    gu = _dot(h_ref[...], wgu_ref[...])
    gp = gu[:, :f]
    hid = gp * _sigmoid(gp) * gu[:, f:] * ge
    acc_ref[...] += _dot(hid.astype(BF16), wd_ref[...])

    @pl.when(e == pl.num_programs(1) - 1)
    def _():
        o_ref[...] = x_ref[...] + ga_ref[...] * acc_ref[...]


def _experts(hn_bf16, gates, wgu, wd, x2, gate, bsz, tm=1024):
    t, d = x2.shape
    s = t // bsz
    tm = min(tm, s)
    ns = s // tm
    ne, _, f2 = wgu.shape
    f = f2 // 2
    row = lambda i, e: (i, 0)
    return pl.pallas_call(
        functools.partial(_experts_kernel, f=f),
        grid=(t // tm, ne),
        in_specs=[
            pl.BlockSpec((tm, d), row),
            pl.BlockSpec((tm, ROUTER_LANES), row),
            pl.BlockSpec((None, d, f2), lambda i, e: (e, 0, 0)),
            pl.BlockSpec((None, f, d), lambda i, e: (e, 0, 0)),
            pl.BlockSpec((tm, d), row),
            pl.BlockSpec((None, 1, d), lambda i, e: (i // ns, 0, 0)),
        ],
        out_specs=pl.BlockSpec((tm, d), row),
        out_shape=jax.ShapeDtypeStruct((t, d), F32),
        scratch_shapes=[pltpu.VMEM((tm, d), F32)],
        compiler_params=_params(("parallel", "arbitrary")),
        name="moe_experts",
    )(hn_bf16, gates, wgu, wd, x2, gate)


def _moe_layer(x2, bsz, hn, hn_bf16, gate, p):
    gates = _router(hn, p["wr_hi"], p["wr_lo"], p["r_bias"])
    return _experts(hn_bf16, gates, p["wgu"], p["wd"], x2, gate, bsz)


def _pad_rows(rows, d):
    out = jnp.zeros((8, d), F32)
    return out.at[:len(rows)].set(jnp.stack([r.reshape(d).astype(F32) for r in rows]))


def _pack_rwkv(j, mu, w_rkv, w0, w1, w2, a0, a1, a2, g1, g2, k_k, k_a, r_k, ln_w, ln_b, w_o):
    d = w_o.shape[-1]
    cat = lambda m: jnp.concatenate([m[j, 0], m[j, 1]], axis=1).astype(BF16)
    stack = lambda m: jnp.concatenate([m[j, 0], m[j, 1]], axis=0).astype(BF16)
    return {
        "mu8": _pad_rows(list(mu[j]), d),
        "w_rkv": w_rkv[j].astype(BF16),
        "w1c": cat(w1), "a1c": cat(a1), "g1c": cat(g1),
        "w2s": stack(w2), "a2s": stack(a2), "g2": g2[j].astype(BF16),
        "vec8": _pad_rows([w0[j, 0], w0[j, 1], a0[j, 0], a0[j, 1], k_k[j], k_a[j], r_k[j]], d),
        "ln8": _pad_rows([ln_w[j], ln_b[j]], d),
        "w_o": w_o[j].astype(BF16),
    }


def _pack_moe(i, router_g, router_g_b, router_e, router_e_b, w_gate, w_up, w_down):
    d = router_g.shape[1]
    ng, ne = N_EXPERT_GROUPS, EXPERTS_PER_GROUP
    wr = jnp.zeros((d, ROUTER_LANES), F32)
    wr = wr.at[:, :ng].set(router_g[i])
    wr = wr.at[:, ng:ng + ng * ne].set(router_e[i].transpose(1, 0, 2).reshape(d, ng * ne))
    bias = jnp.zeros((1, ROUTER_LANES), F32)
    bias = bias.at[0, :ng].set(router_g_b[i])
    bias = bias.at[0, ng:ng + ng * ne].set(router_e_b[i].reshape(ng * ne))
    wr_hi = wr.astype(BF16)
    wr_lo = (wr - wr_hi.astype(F32)).astype(BF16)
    f = w_gate.shape[-1]
    wgu = jnp.concatenate([w_gate[i], w_up[i]], axis=-1).reshape(ng * ne, d, 2 * f).astype(BF16)
    wd = w_down[i].reshape(ng * ne, f, d).astype(BF16)
    return {"wr_hi": wr_hi, "wr_lo": wr_lo, "r_bias": bias, "wgu": wgu, "wd": wd}


def kernel(x, c, ada_w, ada_b, norm_tm_g, norm_cm_g, rw_mu, rw_w_rkv, rw_w0, rw_w1, rw_w2, rw_a0, rw_a1, rw_a2, rw_g1, rw_g2, rw_k_k, rw_k_a, rw_r_k, rw_ln_w, rw_ln_b, rw_w_o, at_w_qkv, at_w_o, moe_router_g, moe_router_g_b, moe_router_e, moe_router_e_b, moe_w_gate, moe_w_up, moe_w_down, final_g):
    bsz, s, d = x.shape
    depth = ada_w.shape[0]
    x2 = x.reshape(bsz * s, d)
    mod = _ada_mod(c, ada_w, ada_b)
    for i in range(depth):
        sh_t, sc_t, ga_t, sh_c, sc_c, ga_c = (mod[i, m] for m in range(N_MOD))
        j = i // 2
        if i % 2 == 0:
            (hn,) = _norm_mod(x2, norm_tm_g[i], sh_t, sc_t, bsz, (F32,))
            p = _pack_rwkv(j, rw_mu, rw_w_rkv, rw_w0, rw_w1, rw_w2, rw_a0, rw_a1, rw_a2, rw_g1, rw_g2,
                           rw_k_k, rw_k_a, rw_r_k, rw_ln_w, rw_ln_b, rw_w_o)
            x2 = _rwkv_layer(x2, bsz, hn, ga_t, p)
        else:
            (hn_b,) = _norm_mod(x2, norm_tm_g[i], sh_t, sc_t, bsz, (BF16,))
            p = _pack_attn(j, at_w_qkv, at_w_o)
            x2 = _attn_layer(x2, bsz, hn_b, ga_t, p)
        hn, hn_b = _norm_mod(x2, norm_cm_g[i], sh_c, sc_c, bsz, (F32, BF16))
        pm = _pack_moe(i, moe_router_g, moe_router_g_b, moe_router_e, moe_router_e_b,
                       moe_w_gate, moe_w_up, moe_w_down)
        x2 = _moe_layer(x2, bsz, hn, hn_b, ga_c, pm)
    (out,) = _norm_mod(x2, final_g, None, None, bsz, (F32,))
    return out.reshape(bsz, s, d)
```

```python
import functools
import math

import numpy as np
import jax
import jax.numpy as jnp
from jax import lax
from jax.experimental import pallas as pl
from jax.experimental.pallas import tpu as pltpu

F32 = jnp.float32
BF16 = jnp.bfloat16

RMS_EPS = 1e-6
N_MOD = 6
RWKV_HEAD = 64
RWKV_GN_EPS = 64e-5
ATTN_GROUPS = ((128, 1), (512, 4), (2048, 16))
ATTN_HEADS = 8
ATTN_HEAD_DIM = 128
NEG_INF = -1e30
N_EXPERT_GROUPS = 4
EXPERTS_PER_GROUP = 8
N_EXPERTS = N_EXPERT_GROUPS * EXPERTS_PER_GROUP
ROUTER_LANES = 128

SCAN_CHUNK = 64
V7X_VMEM_LIMIT = 56 * 1024 * 1024
SCAN_DT = BF16


def _params(sem):
    return pltpu.CompilerParams(dimension_semantics=sem, vmem_limit_bytes=V7X_VMEM_LIMIT)


def _dot(a, b):
    return jnp.dot(a, b, preferred_element_type=F32)


def _sigmoid(x):
    return 1.0 / (1.0 + jnp.exp(-x))


def _split2(x):
    hi = x.astype(BF16)
    lo = (x - hi.astype(F32)).astype(BF16)
    return hi, lo


def _split3(x):
    h1 = x.astype(BF16)
    r1 = x - h1.astype(F32)
    h2 = r1.astype(BF16)
    h3 = (r1 - h2.astype(F32)).astype(BF16)
    return h1, h2, h3


def _ada_kernel(c_ref, w_ref, b_ref, o_ref):
    c = c_ref[...]
    sc = c * _sigmoid(c)
    o_ref[...] = _dot(sc.astype(BF16), w_ref[...].astype(BF16)) + b_ref[...]


def _ada_mod(c, ada_w, ada_b):
    depth, d, n = ada_w.shape
    bsz = c.shape[0]
    tn = n // 4
    out = pl.pallas_call(
        _ada_kernel,
        grid=(depth, n // tn),
        in_specs=[
            pl.BlockSpec((bsz, d), lambda i, j: (0, 0)),
            pl.BlockSpec((None, d, tn), lambda i, j: (i, 0, j)),
            pl.BlockSpec((None, 1, tn), lambda i, j: (i, 0, j)),
        ],
        out_specs=pl.BlockSpec((None, bsz, tn), lambda i, j: (i, 0, j)),
        out_shape=jax.ShapeDtypeStruct((depth, bsz, n), F32),
        compiler_params=_params(("parallel", "parallel")),
        name="ada_mod",
    )(c, ada_w, ada_b.reshape(depth, 1, n))
    return out.reshape(depth, bsz, N_MOD, 1, d).transpose(0, 2, 1, 3, 4)


def _norm_kernel(*refs, modulated, n_out):
    if modulated:
        x_ref, g_ref, sh_ref, sc_ref = refs[:4]
    else:
        x_ref, g_ref = refs[:2]
    outs = refs[-n_out:]
    x = x_ref[...]
    ms = jnp.mean(x * x, axis=-1, keepdims=True)
    y = x * lax.rsqrt(ms + RMS_EPS) * g_ref[...]
    if modulated:
        y = y * (1.0 + sc_ref[...]) + sh_ref[...]
    for o in outs:
        o[...] = y.astype(o.dtype)


def _norm_mod(x2, g, shift, scale, bsz, out_dtypes, ts=512):
    t, d = x2.shape
    s = t // bsz
    ts = min(ts, s)
    ns = s // ts
    modulated = shift is not None
    row_spec = pl.BlockSpec((ts, d), lambda b, i: (b * ns + i, 0))
    in_specs = [row_spec, pl.BlockSpec((1, d), lambda b, i: (0, 0))]
    args = [x2, g.reshape(1, d)]
    if modulated:
        vec = pl.BlockSpec((None, 1, d), lambda b, i: (b, 0, 0))
        in_specs += [vec, vec]
        args += [shift, scale]
    outs = pl.pallas_call(
        functools.partial(_norm_kernel, modulated=modulated, n_out=len(out_dtypes)),
        grid=(bsz, ns),
        in_specs=in_specs,
        out_specs=[row_spec] * len(out_dtypes),
        out_shape=[jax.ShapeDtypeStruct((t, d), dt) for dt in out_dtypes],
        compiler_params=_params(("parallel", "parallel")),
        name="norm_mod",
    )(*args)
    return outs


def _proj_res_kernel(a_ref, w_ref, x_ref, ga_ref, o_ref):
    o_ref[...] = x_ref[...] + ga_ref[...] * _dot(a_ref[...], w_ref[...])


def _proj_residual(a, w, x2, gate, bsz, tm=512):
    t, d = x2.shape
    s = t // bsz
    tm = min(tm, s)
    ns = s // tm
    row = lambda b, i: (b * ns + i, 0)
    return pl.pallas_call(
        _proj_res_kernel,
        grid=(bsz, ns),
        in_specs=[
            pl.BlockSpec((tm, a.shape[1]), row),
            pl.BlockSpec(w.shape, lambda b, i: (0, 0)),
            pl.BlockSpec((tm, d), row),
            pl.BlockSpec((None, 1, d), lambda b, i: (b, 0, 0)),
        ],
        out_specs=pl.BlockSpec((tm, d), row),
        out_shape=jax.ShapeDtypeStruct((t, d), F32),
        compiler_params=_params(("parallel", "parallel")),
        name="proj_residual",
    )(a, w, x2, gate)


def _mm_kernel(a_ref, b_ref, o_ref):
    o_ref[...] = _dot(a_ref[...], b_ref[...]).astype(o_ref.dtype)


def _matmul(a, b, out_dtype, tm=1024, tn=1024):
    m, k = a.shape
    n = b.shape[1]
    tm, tn = min(tm, m), min(tn, n)
    return pl.pallas_call(
        _mm_kernel,
        grid=(n // tn, m // tm),
        in_specs=[
            pl.BlockSpec((tm, k), lambda j, i: (i, 0)),
            pl.BlockSpec((k, tn), lambda j, i: (0, j)),
        ],
        out_specs=pl.BlockSpec((tm, tn), lambda j, i: (i, j)),
        out_shape=jax.ShapeDtypeStruct((m, n), out_dtype),
        compiler_params=_params(("parallel", "parallel")),
        name="matmul",
    )(a, b)


def _softplus(z):
    return jnp.maximum(z, 0.0) + jnp.log(1.0 + jnp.exp(-jnp.abs(z)))


def _seg_sum(x, seg_ref, segt_ref):
    xh, xl = _split2(x)
    s = _dot(xh, seg_ref[...]) + _dot(xl, seg_ref[...])
    sh, sl = _split2(s)
    return _dot(sh, segt_ref[...]) + _dot(sl, segt_ref[...])


def _rwkv_prep_kernel(hn_ref, prev_ref, next_ref, mu_ref, wrkv_ref, w1_ref, a1_ref, g1_ref,
                      w2_ref, a2_ref, g2_ref, vec_ref, cm_ref, edm_ref, seg_ref, segt_ref,
                      at_ref, rt_ref, bt_ref, kt_ref, v_ref, ed_ref, g_ref, gb_ref, *, ts, ns):
    i = pl.program_id(1)
    cur = hn_ref[...]
    row = lax.broadcasted_iota(jnp.int32, (ts, 1), 0)
    prev_row = jnp.where(i > 0, prev_ref[7:8, :], 0.0)
    next_row = jnp.where(i < ns - 1, next_ref[0:1, :], 0.0)
    x_prev = jnp.where(row == 0, prev_row, pltpu.roll(cur, 1, 0))
    x_next = jnp.where(row == ts - 1, next_row, pltpu.roll(cur, ts - 1, 0))
    xx = 0.5 * (x_prev + x_next) - cur

    def mix(j):
        return (cur + xx * mu_ref[j:j + 1, :]).astype(BF16)

    r = _dot(mix(0), wrkv_ref[0])
    k = _dot(mix(1), wrkv_ref[1])
    v = _dot(mix(2), wrkv_ref[2])
    tw = jnp.tanh(_dot(mix(3), w1_ref[...]))
    ta = _dot(mix(4), a1_ref[...])
    tg = _sigmoid(_dot(mix(5), g1_ref[...]))
    lane = lax.broadcasted_iota(jnp.int32, (1, tw.shape[1]), 1)
    half = tw.shape[1] // 2
    hg = tg.shape[1] // 2

    k_k = vec_ref[4:5, :]
    k_a = vec_ref[5:6, :]
    r_k = vec_ref[6:7, :]
    kk = k * k_k
    ss = _seg_sum(kk * kk, seg_ref, segt_ref)
    kkn = kk / jnp.maximum(jnp.sqrt(ss), 1e-12)
    v_ref[...] = v.astype(v_ref.dtype)

    gb = jnp.zeros_like(cur)
    for z in range(2):
        sel = (lane < half) if z == 0 else (lane >= half)
        lw = _dot(jnp.where(sel, tw, 0.0).astype(BF16), w2_ref[...])
        la = _dot(jnp.where(sel, ta, 0.0).astype(BF16), a2_ref[...])
        g = _dot(tg[:, z * hg:(z + 1) * hg].astype(BF16), g2_ref[z])
        w_log = -_softplus(-(vec_ref[z:z + 1, :] + lw)) - 0.5
        ld = -jnp.exp(w_log)
        a = _sigmoid(vec_ref[2 + z:3 + z, :] + la)
        kdir = k * (1.0 + (a - 1.0) * k_a)
        b = kkn * a
        parts = _split3(ld)
        cmz = cm_ref[z]
        lm = _dot(cmz, parts[0]) + _dot(cmz, parts[1]) + _dot(cmz, parts[2])
        e_in = jnp.exp(lm)
        e_inv = jnp.exp(-lm)
        e_ex = jnp.exp(lm - ld)
        rt_ref[z] = (r * e_in).astype(rt_ref.dtype)
        at_ref[z] = (-kkn * e_ex).astype(at_ref.dtype)
        bt_ref[z] = (b * e_inv).astype(bt_ref.dtype)
        kt_ref[z] = (kdir * e_inv).astype(kt_ref.dtype)
        edz = edm_ref[z]
        ed = jnp.exp(_dot(edz, parts[0]) + _dot(edz, parts[1]) + _dot(edz, parts[2]))
        for ck in range(ed.shape[0] // 8):
            ed_ref[z, ck] = ed[ck * 8:(ck + 1) * 8]
        g_ref[z] = g.astype(g_ref.dtype)
        bonus = _seg_sum(r * kdir * r_k, seg_ref, segt_ref) * v
        gb = gb + g * bonus
    gb_ref[...] = gb


def _chunk_matrices(ts, c):
    t = np.arange(ts)
    same = (t[:, None] // c) == (t[None, :] // c)
    pos = t % c
    cm = np.zeros((2, ts, ts), np.float32)
    nck = ts // c
    edm = np.zeros((2, 8 * nck, ts), np.float32)
    tri_f = same & (t[None, :] <= t[:, None])
    sel_f = same & (pos[None, :] <= c // 2 - 1)
    cm[0] = tri_f.astype(np.float32) - sel_f.astype(np.float32)
    tri_b = same & (t[None, :] >= t[:, None])
    sel_b = same & (pos[None, :] >= c // 2)
    cm[1] = tri_b.astype(np.float32) - sel_b.astype(np.float32)
    for ck in range(nck):
        inck = (t // c) == ck
        edm[0, 8 * ck] = inck & (pos <= c // 2 - 1)
        edm[0, 8 * ck + 1] = inck & (pos > c // 2 - 1)
        edm[1, 8 * ck] = inck & (pos >= c // 2)
        edm[1, 8 * ck + 1] = inck & (pos < c // 2)
    return jnp.asarray(cm, BF16), jnp.asarray(edm, BF16)


def _seg_matrices(d, head):
    seg = np.zeros((d, 128), np.float32)
    seg[np.arange(d), np.arange(d) // head] = 1.0
    return jnp.asarray(seg, BF16), jnp.asarray(seg.T.copy(), BF16)


def _rwkv_prep(hn, bsz, p, scan_dtype, ts=256):
    t, d = hn.shape
    s = t // bsz
    ts = min(ts, s)
    ns = s // ts
    c = SCAN_CHUNK
    assert ts % c == 0 and ts % 8 == 0
    nck = ts // c
    cm, edm = _chunk_matrices(ts, c)
    seg, segt = _seg_matrices(d, RWKV_HEAD)
    r8 = ts // 8
    nblk8 = t // 8
    row = lambda b, i: (b * ns + i, 0)
    full = lambda a: pl.BlockSpec(a.shape, lambda b, i, _n=a.ndim: (0,) * _n)
    weights = [p["mu8"], p["w_rkv"], p["w1c"], p["a1c"], p["g1c"], p["w2s"], p["a2s"], p["g2"],
               p["vec8"], cm, edm, seg, segt]
    dir_spec = pl.BlockSpec((2, ts, d), lambda b, i: (0, b * ns + i, 0))
    outs = pl.pallas_call(
        functools.partial(_rwkv_prep_kernel, ts=ts, ns=ns),
        grid=(bsz, ns),
        in_specs=[
            pl.BlockSpec((ts, d), row),
            pl.BlockSpec((8, d), lambda b, i: (jnp.maximum((b * ns + i) * r8 - 1, 0), 0)),
            pl.BlockSpec((8, d), lambda b, i: (jnp.minimum((b * ns + i + 1) * r8, nblk8 - 1), 0)),
        ] + [full(a) for a in weights],
        out_specs=[dir_spec, dir_spec, dir_spec, dir_spec,
                   pl.BlockSpec((ts, d), row),
                   pl.BlockSpec((2, nck, 8, d), lambda b, i: (0, b * ns + i, 0, 0)),
                   dir_spec,
                   pl.BlockSpec((ts, d), row)],
        out_shape=[jax.ShapeDtypeStruct((2, t, d), scan_dtype)] * 4 + [
            jax.ShapeDtypeStruct((t, d), scan_dtype),
            jax.ShapeDtypeStruct((2, t // c, 8, d), F32),
            jax.ShapeDtypeStruct((2, t, d), F32),
            jax.ShapeDtypeStruct((t, d), F32)],
        compiler_params=_params(("parallel", "parallel")),
        name="rwkv_prep",
    )(hn, hn, hn, *weights)
    return outs


def _sdot(a, b, dims):
    return lax.dot_general(a.astype(SCAN_DT), b.astype(SCAN_DT), (dims, ((), ())),
                           preferred_element_type=F32)


def _scan_kernel(at_ref, rt_ref, bt_ref, kt_ref, v_ref, ed_ref, y_ref, s_ref, *, c, n_heads):
    z = pl.program_id(1)
    ci = pl.program_id(2)

    @pl.when(ci == 0)
    def _():
        s_ref[...] = jnp.zeros_like(s_ref)

    c2 = 2 * c
    pw_lanes = 2 * RWKV_HEAD
    row = lax.broadcasted_iota(jnp.int32, (c2, c2), 0)
    col = lax.broadcasted_iota(jnp.int32, (c2, c2), 1)
    same_blk = (row // c) == (col // c)
    diff = jnp.where(z == 0, row - col, col - row)
    strict = same_blk & (diff > 0)
    incl = same_blk & (diff >= 0)
    incl2 = jnp.concatenate([incl, incl], axis=1)
    lane_a = lax.broadcasted_iota(jnp.int32, (c, pw_lanes), 1) < RWKV_HEAD
    same_head = ((lax.broadcasted_iota(jnp.int32, (pw_lanes, pw_lanes), 0) // RWKV_HEAD)
                 == (lax.broadcasted_iota(jnp.int32, (pw_lanes, pw_lanes), 1) // RWKV_HEAD))
    n_steps = int(math.log2(c))
    nt = (((1,), (1,)))
    nn = (((1,), (0,)))
    tn = (((0,), (0,)))
    pairs = range(n_heads // 2)
    sls = [pl.ds(hp * pw_lanes, pw_lanes) for hp in pairs]
    zero = jnp.zeros((c, pw_lanes), SCAN_DT)
    lhs, rhs, vv, st = [], [], [], []
    for hp in pairs:
        a2 = at_ref[:, sls[hp]]
        r2 = rt_ref[:, sls[hp]]
        b2 = bt_ref[:, sls[hp]]
        k2 = kt_ref[:, sls[hp]]
        v2 = v_ref[:, sls[hp]]
        lhs.append(jnp.concatenate([jnp.where(lane_a, a2, zero), jnp.where(lane_a, zero, a2),
                                    jnp.where(lane_a, r2, zero), jnp.where(lane_a, zero, r2)], axis=0))
        rhs.append(jnp.concatenate([b2, b2, k2, k2], axis=0))
        vv.append(jnp.concatenate([v2, v2], axis=0))
        st.append(s_ref[hp] * ed_ref[0:1, sls[hp]])
    gm = [_sdot(lhs[hp], rhs[hp], nt) for hp in pairs]
    q = [_sdot(lhs[hp], st[hp], nt) for hp in pairs]
    pw = [jnp.where(strict, gm[hp][:c2, :c2], 0.0).astype(SCAN_DT) for hp in pairs]
    m2 = [jnp.where(strict, gm[hp][:c2, c2:], 0.0).astype(SCAN_DT) for hp in pairs]
    m34 = [jnp.where(incl2, gm[hp][c2:, :], 0.0).astype(SCAN_DT) for hp in pairs]
    x = [q[hp][:c2] + _sdot(m2[hp], vv[hp], nn) for hp in pairs]
    for step in range(n_steps):
        x = [x[hp] + _sdot(pw[hp], x[hp], nn) for hp in pairs]
        if step + 1 < n_steps:
            pw = [_sdot(pw[hp], pw[hp], nn).astype(SCAN_DT) for hp in pairs]
    for hp in pairs:
        xs = x[hp].astype(SCAN_DT)
        uvs = jnp.concatenate([xs, vv[hp]], axis=0)
        ys = q[hp][c2:] + _sdot(m34[hp], uvs, nn)
        y_ref[:, sls[hp]] = jnp.where(lane_a, ys[:c], ys[c:])
        u2 = jnp.where(lane_a, xs[:c], xs[c:])
        upd = _sdot(jnp.concatenate([u2, vv[hp][:c]], axis=0), rhs[hp][c:3 * c], tn)
        s_ref[hp] = (st[hp] + jnp.where(same_head, upd, 0.0)) * ed_ref[1:2, sls[hp]]


def _rwkv_scan(at, rt, bt, kt, v, ed, bsz):
    _, t, d = at.shape
    s = t // bsz
    c = SCAN_CHUNK
    nc = s // c
    n_heads = d // RWKV_HEAD

    def cidx(b, z, ci):
        return b * nc + ci + z * (nc - 1 - 2 * ci)

    dspec = pl.BlockSpec((None, c, d), lambda b, z, ci: (z, cidx(b, z, ci), 0))
    return pl.pallas_call(
        functools.partial(_scan_kernel, c=c, n_heads=n_heads),
        grid=(bsz, 2, nc),
        in_specs=[dspec, dspec, dspec, dspec,
                  pl.BlockSpec((c, d), lambda b, z, ci: (cidx(b, z, ci), 0)),
                  pl.BlockSpec((None, None, 8, d), lambda b, z, ci: (z, cidx(b, z, ci), 0, 0))],
        out_specs=dspec,
        out_shape=jax.ShapeDtypeStruct((2, t, d), F32),
        scratch_shapes=[pltpu.VMEM((n_heads // 2, 2 * RWKV_HEAD, 2 * RWKV_HEAD), F32)],
        compiler_params=_params(("parallel", "parallel", "arbitrary")),
        name="rwkv_scan",
    )(at, rt, bt, kt, v, ed)


def _rwkv_out_kernel(y_ref, g_ref, gb_ref, x_ref, ga_ref, ln_ref, wo_ref, seg_ref, segt_ref, o_ref):
    o = gb_ref[...]
    inv_n = 1.0 / RWKV_HEAD
    for z in range(2):
        y = y_ref[z]
        mean = _seg_sum(y, seg_ref, segt_ref) * inv_n
        yc = y - mean
        var = _seg_sum(yc * yc, seg_ref, segt_ref) * inv_n
        yn = yc * lax.rsqrt(var + RWKV_GN_EPS) * ln_ref[0:1, :] + ln_ref[1:2, :]
        o = o + g_ref[z] * yn
    o_ref[...] = x_ref[...] + ga_ref[...] * _dot(o.astype(BF16), wo_ref[...])


def _rwkv_out(y, g, gb, x2, gate, ln8, w_o, bsz, tm=256):
    t, d = x2.shape
    s = t // bsz
    tm = min(tm, s)
    ns = s // tm
    seg, segt = _seg_matrices(d, RWKV_HEAD)
    row = lambda b, i: (b * ns + i, 0)
    dir_spec = pl.BlockSpec((2, tm, d), lambda b, i: (0, b * ns + i, 0))
    full = lambda a: pl.BlockSpec(a.shape, lambda b, i, _n=a.ndim: (0,) * _n)
    return pl.pallas_call(
        _rwkv_out_kernel,
        grid=(bsz, ns),
        in_specs=[dir_spec, dir_spec, pl.BlockSpec((tm, d), row), pl.BlockSpec((tm, d), row),
                  pl.BlockSpec((None, 1, d), lambda b, i: (b, 0, 0)),
                  full(ln8), full(w_o), full(seg), full(segt)],
        out_specs=pl.BlockSpec((tm, d), row),
        out_shape=jax.ShapeDtypeStruct((t, d), F32),
        compiler_params=_params(("parallel", "parallel")),
        name="rwkv_out",
    )(y, g, gb, x2, gate, ln8, w_o, seg, segt)


def _rwkv_layer(x2, bsz, hn, gate, p):
    at, rt, bt, kt, v, ed, g, gb = _rwkv_prep(hn, bsz, p, SCAN_DT)
    y = _rwkv_scan(at, rt, bt, kt, v, ed, bsz)
    return _rwkv_out(y, g, gb, x2, gate, p["ln8"], p["w_o"], bsz)


def _attn_plan(s, tq):
    nq = s // tq
    plan = []
    for window, dil in ATTN_GROUPS:
        lim = window // 2
        w = min(s, tq + 2 * lim)
        if w == s:
            plan.append(dict(lim=lim, dil=dil, w=w, whole=True, offs=(), vid=(0,) * nq))
        else:
            offs = [int(np.clip(i * tq - lim, 0, s - w)) - i * tq for i in range(nq)]
            uniq = sorted(set(offs))
            plan.append(dict(lim=lim, dil=dil, w=w, whole=False, offs=tuple(uniq),
                             vid=tuple(uniq.index(o) for o in offs)))
    return plan


def _bias_table(shape, off, lim, dil, slope):
    rel = lax.broadcasted_iota(jnp.int32, shape, 1) - lax.broadcasted_iota(jnp.int32, shape, 0) + off
    dist = jnp.abs(rel)
    valid = (dist <= lim) & ((rel & (dil - 1)) == 0)
    return jnp.where(valid, -slope * dist.astype(F32), NEG_INF)


def _attn_kernel(slope_ref, vid_ref, q1, q2, q3, k1, k2, k3, v1, v2, v3, o_ref, t1, t2, t3, *, tq, s, plan):
    h = pl.program_id(0)
    b = pl.program_id(1)
    i = pl.program_id(2)
    nq = s // tq
    t0 = i * tq
    qs = (q1, q2, q3)
    ks = (k1, k2, k3)
    vs = (v1, v2, v3)
    tabs = (t1, t2, t3)

    @pl.when((b == 0) & (i == 0))
    def _():
        for gi, g in enumerate(plan):
            slope = slope_ref[gi, h]
            if g["whole"]:
                for j in range(2 * nq - 1):
                    tabs[gi][j] = _bias_table((tq, tq), (j - (nq - 1)) * tq, g["lim"], g["dil"], slope)
            else:
                for v, off in enumerate(g["offs"]):
                    tabs[gi][v] = _bias_table((tq, g["w"]), off, g["lim"], g["dil"], slope)

    nt = (((1,), (1,)), ((), ()))
    scores = []
    vals = []
    for gi, g in enumerate(plan):
        q = qs[gi][...]
        if g["whole"]:
            for j in range(nq):
                kblk = ks[gi][j * tq:(j + 1) * tq, :]
                sc = lax.dot_general(q, kblk, nt, preferred_element_type=F32)
                scores.append(sc + tabs[gi][j + (nq - 1) - i])
                vals.append((gi, j * tq, tq))
        else:
            ws = pl.multiple_of(jnp.clip(t0 - g["lim"], 0, s - g["w"]), 16)
            kblk = ks[gi][pl.ds(ws, g["w"]), :]
            sc = lax.dot_general(q, kblk, nt, preferred_element_type=F32)
            scores.append(sc + tabs[gi][vid_ref[gi, i]])
            vals.append((gi, ws, g["w"]))
    m = scores[0].max(axis=-1, keepdims=True)
    for sc in scores[1:]:
        m = jnp.maximum(m, sc.max(axis=-1, keepdims=True))
    den = jnp.zeros_like(m)
    acc = jnp.zeros((tq, ATTN_HEAD_DIM), F32)
    for sc, (gi, start, width) in zip(scores, vals):
        p = jnp.exp(sc - m)
        den = den + p.sum(axis=-1, keepdims=True)
        acc = acc + _dot(p.astype(BF16), vs[gi][pl.ds(start, width), :])
    o_ref[...] = (acc / den).astype(o_ref.dtype)


def _alibi_slopes(n):
    return 2.0 ** (-8.0 * jnp.arange(1, n + 1, dtype=F32) / n)


def _attention(qkv, bsz, tq=256):
    t = qkv.shape[0]
    s = t // bsz
    tq = min(tq, s)
    nq = s // tq
    dh = ATTN_HEAD_DIM
    nh = ATTN_HEADS
    plan = _attn_plan(s, tq)
    slopes = _alibi_slopes(len(ATTN_GROUPS) * nh).reshape(len(ATTN_GROUPS), nh)
    vid = jnp.asarray(np.array([g["vid"] for g in plan], np.int32))
    tables = [pltpu.VMEM((2 * nq - 1, tq, tq) if g["whole"] else (len(g["offs"]), tq, g["w"]), F32) for g in plan]

    def qspec(gi):
        return pl.BlockSpec((tq, dh), lambda h, b, i: (b * nq + i, (gi * 3) * nh + h))

    def kvspec(gi, which):
        return pl.BlockSpec((s, dh), lambda h, b, i: (b, (gi * 3 + which) * nh + h))

    smem = pl.BlockSpec(memory_space=pltpu.SMEM)
    return pl.pallas_call(
        functools.partial(_attn_kernel, tq=tq, s=s, plan=plan),
        grid=(nh, bsz, nq),
        in_specs=[smem, smem]
        + [qspec(g) for g in range(3)] + [kvspec(g, 1) for g in range(3)] + [kvspec(g, 2) for g in range(3)],
        out_specs=pl.BlockSpec((tq, dh), lambda h, b, i: (b * nq + i, h)),
        out_shape=jax.ShapeDtypeStruct((t, nh * dh), BF16),
        scratch_shapes=tables,
        compiler_params=_params(("arbitrary", "arbitrary", "arbitrary")),
        name="dilated_attention",
    )(slopes, vid, qkv, qkv, qkv, qkv, qkv, qkv, qkv, qkv, qkv)


def _pack_attn(j, w_qkv, w_o):
    d = w_qkv.shape[1]
    w = w_qkv[j].reshape(d, len(ATTN_GROUPS), 3, ATTN_HEADS * ATTN_HEAD_DIM)
    w = w.at[:, :, 0].multiply(ATTN_HEAD_DIM ** -0.5)
    return {"w_qkv": w.reshape(d, -1).astype(BF16), "w_o": w_o[j].astype(BF16)}


def _attn_layer(x2, bsz, hn_bf16, gate, p):
    qkv = _matmul(hn_bf16, p["w_qkv"], BF16)
    merged = _attention(qkv, bsz)
    return _proj_residual(merged, p["w_o"], x2, gate, bsz)


def _router_kernel(h_ref, wh_ref, wl_ref, b_ref, o_ref):
    hh, hl = _split2(h_ref[...])
    logits = _dot(hh, wh_ref[...]) + _dot(hl, wh_ref[...]) + _dot(hh, wl_ref[...]) + b_ref[...]
    lane = lax.broadcasted_iota(jnp.int32, logits.shape, 1)
    ng, ne = N_EXPERT_GROUPS, EXPERTS_PER_GROUP
    big = jnp.int32(ROUTER_LANES)
    is_g = lane < ng
    gl = jnp.where(is_g, logits, -jnp.inf)
    gmax = gl.max(axis=-1, keepdims=True)
    gsum = jnp.where(is_g, jnp.exp(logits - gmax), 0.0).sum(axis=-1, keepdims=True)
    p_group = 1.0 / gsum
    g_top = jnp.where(gl == gmax, lane, big).min(axis=-1, keepdims=True)
    lo = ng + ne * g_top
    es = jnp.where((lane >= lo) & (lane < lo + ne), logits, -jnp.inf)
    v1 = es.max(axis=-1, keepdims=True)
    i1 = jnp.where(es == v1, lane, big).min(axis=-1, keepdims=True)
    es2 = jnp.where(lane == i1, -jnp.inf, es)
    v2 = es2.max(axis=-1, keepdims=True)
    i2 = jnp.where(es2 == v2, lane, big).min(axis=-1, keepdims=True)
    e2 = jnp.exp(v2 - v1)
    w1 = p_group / (1.0 + e2)
    w2 = p_group * e2 / (1.0 + e2)
    o_ref[...] = jnp.where(lane == i1, w1, 0.0) + jnp.where(lane == i2, w2, 0.0)


def _router(hn, wr_hi, wr_lo, bias, tm=512):
    t, d = hn.shape
    tm = min(tm, t)
    full = lambda a: pl.BlockSpec(a.shape, lambda i: (0, 0))
    return pl.pallas_call(
        _router_kernel,
        grid=(t // tm,),
        in_specs=[pl.BlockSpec((tm, d), lambda i: (i, 0)), full(wr_hi), full(wr_lo), full(bias)],
        out_specs=pl.BlockSpec((tm, ROUTER_LANES), lambda i: (i, 0)),
        out_shape=jax.ShapeDtypeStruct((t, ROUTER_LANES), F32),
        compiler_params=_params(("parallel",)),
        name="moe_router",
    )(hn, wr_hi, wr_lo, bias)


def _experts_kernel(h_ref, gates_ref, wgu_ref, wd_ref, x_ref, ga_ref, o_ref, *, f, per_step):
    step = pl.program_id(1)
    gates = gates_ref[...]
    lane = lax.broadcasted_iota(jnp.int32, gates.shape, 1)
    h = h_ref[...]
    hids = []
    for j in range(per_step):
        e = step * per_step + j
        ge = jnp.where(lane == N_EXPERT_GROUPS + e, gates, 0.0).sum(axis=-1, keepdims=True)
        gu = _dot(h, wgu_ref[j])
        gp = gu[:, :f]
        hids.append((gp * _sigmoid(gp) * gu[:, f:] * ge).astype(BF16))
    part = _dot(jnp.concatenate(hids, axis=1), wd_ref[...])

    @pl.when(step == 0)
    def _():
        o_ref[...] = part

    @pl.when(step > 0)
    def _():
        o_ref[...] += part

    @pl.when(step == pl.num_programs(1) - 1)
    def _():
        o_ref[...] = x_ref[...] + ga_ref[...] * o_ref[...]


def _experts(hn_bf16, gates, wgu, wd, x2, gate, bsz, tm=1024, per_step=4):
    t, d = x2.shape
    s = t // bsz
    tm = min(tm, s)
    ns = s // tm
    ne, _, f2 = wgu.shape
    f = f2 // 2
    row = lambda i, e: (i, 0)
    return pl.pallas_call(
        functools.partial(_experts_kernel, f=f, per_step=per_step),
        grid=(t // tm, ne // per_step),
        in_specs=[
            pl.BlockSpec((tm, d), row),
            pl.BlockSpec((tm, ROUTER_LANES), row),
            pl.BlockSpec((per_step, d, f2), lambda i, e: (e, 0, 0)),
            pl.BlockSpec((per_step * f, d), lambda i, e: (e, 0)),
            pl.BlockSpec((tm, d), row),
            pl.BlockSpec((None, 1, d), lambda i, e: (i // ns, 0, 0)),
        ],
        out_specs=pl.BlockSpec((tm, d), row),
        out_shape=jax.ShapeDtypeStruct((t, d), F32),
        compiler_params=_params(("parallel", "arbitrary")),
        name="moe_experts",
    )(hn_bf16, gates, wgu, wd.reshape(ne * f, d), x2, gate)


def _moe_layer(x2, bsz, hn, hn_bf16, gate, p):
    gates = _router(hn, p["wr_hi"], p["wr_lo"], p["r_bias"])
    return _experts(hn_bf16, gates, p["wgu"], p["wd"], x2, gate, bsz)


def _pad_rows(rows, d):
    out = jnp.zeros((8, d), F32)
    return out.at[:len(rows)].set(jnp.stack([r.reshape(d).astype(F32) for r in rows]))


def _pack_rwkv(j, mu, w_rkv, w0, w1, w2, a0, a1, a2, g1, g2, k_k, k_a, r_k, ln_w, ln_b, w_o):
    d = w_o.shape[-1]
    cat = lambda m: jnp.concatenate([m[j, 0], m[j, 1]], axis=1).astype(BF16)
    stack = lambda m: jnp.concatenate([m[j, 0], m[j, 1]], axis=0).astype(BF16)
    return {
        "mu8": _pad_rows(list(mu[j]), d),
        "w_rkv": w_rkv[j].astype(BF16),
        "w1c": cat(w1), "a1c": cat(a1), "g1c": cat(g1),
        "w2s": stack(w2), "a2s": stack(a2), "g2": g2[j].astype(BF16),
        "vec8": _pad_rows([w0[j, 0], w0[j, 1], a0[j, 0], a0[j, 1], k_k[j], k_a[j], r_k[j]], d),
        "ln8": _pad_rows([ln_w[j], ln_b[j]], d),
        "w_o": w_o[j].astype(BF16),
    }


def _pack_moe(i, router_g, router_g_b, router_e, router_e_b, w_gate, w_up, w_down):
    d = router_g.shape[1]
    ng, ne = N_EXPERT_GROUPS, EXPERTS_PER_GROUP
    wr = jnp.zeros((d, ROUTER_LANES), F32)
    wr = wr.at[:, :ng].set(router_g[i])
    wr = wr.at[:, ng:ng + ng * ne].set(router_e[i].transpose(1, 0, 2).reshape(d, ng * ne))
    bias = jnp.zeros((1, ROUTER_LANES), F32)
    bias = bias.at[0, :ng].set(router_g_b[i])
    bias = bias.at[0, ng:ng + ng * ne].set(router_e_b[i].reshape(ng * ne))
    wr_hi = wr.astype(BF16)
    wr_lo = (wr - wr_hi.astype(F32)).astype(BF16)
    f = w_gate.shape[-1]
    wgu = jnp.concatenate([w_gate[i], w_up[i]], axis=-1).reshape(ng * ne, d, 2 * f).astype(BF16)
    wd = w_down[i].reshape(ng * ne, f, d).astype(BF16)
    return {"wr_hi": wr_hi, "wr_lo": wr_lo, "r_bias": bias, "wgu": wgu, "wd": wd}


def kernel(x, c, ada_w, ada_b, norm_tm_g, norm_cm_g, rw_mu, rw_w_rkv, rw_w0, rw_w1, rw_w2, rw_a0, rw_a1, rw_a2, rw_g1, rw_g2, rw_k_k, rw_k_a, rw_r_k, rw_ln_w, rw_ln_b, rw_w_o, at_w_qkv, at_w_o, moe_router_g, moe_router_g_b, moe_router_e, moe_router_e_b, moe_w_gate, moe_w_up, moe_w_down, final_g):
    bsz, s, d = x.shape
    depth = ada_w.shape[0]
    x2 = x.reshape(bsz * s, d)
    mod = _ada_mod(c, ada_w, ada_b)
    for i in range(depth):
        sh_t, sc_t, ga_t, sh_c, sc_c, ga_c = (mod[i, m] for m in range(N_MOD))
        j = i // 2
        if i % 2 == 0:
            (hn,) = _norm_mod(x2, norm_tm_g[i], sh_t, sc_t, bsz, (F32,))
            p = _pack_rwkv(j, rw_mu, rw_w_rkv, rw_w0, rw_w1, rw_w2, rw_a0, rw_a1, rw_a2, rw_g1, rw_g2,
                           rw_k_k, rw_k_a, rw_r_k, rw_ln_w, rw_ln_b, rw_w_o)
            x2 = _rwkv_layer(x2, bsz, hn, ga_t, p)
        else:
            (hn_b,) = _norm_mod(x2, norm_tm_g[i], sh_t, sc_t, bsz, (BF16,))
            p = _pack_attn(j, at_w_qkv, at_w_o)
            x2 = _attn_layer(x2, bsz, hn_b, ga_t, p)
        hn, hn_b = _norm_mod(x2, norm_cm_g[i], sh_c, sc_c, bsz, (F32, BF16))
        pm = _pack_moe(i, moe_router_g, moe_router_g_b, moe_router_e, moe_router_e_b,
                       moe_w_gate, moe_w_up, moe_w_down)
        x2 = _moe_layer(x2, bsz, hn, hn_b, ga_c, pm)
    (out,) = _norm_mod(x2, final_g, None, None, bsz, (F32,))
    return out.reshape(bsz, s, d)
```

```python
import functools
import math

import numpy as np
import jax
import jax.numpy as jnp
from jax import lax
from jax.experimental import pallas as pl
from jax.experimental.pallas import tpu as pltpu

F32 = jnp.float32
BF16 = jnp.bfloat16

RMS_EPS = 1e-6
N_MOD = 6
RWKV_HEAD = 64
RWKV_GN_EPS = 64e-5
ATTN_GROUPS = ((128, 1), (512, 4), (2048, 16))
ATTN_HEADS = 8
ATTN_HEAD_DIM = 128
NEG_INF = -1e30
N_EXPERT_GROUPS = 4
EXPERTS_PER_GROUP = 8
N_EXPERTS = N_EXPERT_GROUPS * EXPERTS_PER_GROUP
ROUTER_LANES = 128

SCAN_CHUNK = 64
V7X_VMEM_LIMIT = 56 * 1024 * 1024
SCAN_DT = BF16


def _params(sem):
    return pltpu.CompilerParams(dimension_semantics=sem, vmem_limit_bytes=V7X_VMEM_LIMIT)


def _dot(a, b):
    return jnp.dot(a, b, preferred_element_type=F32)


def _sigmoid(x):
    return 1.0 / (1.0 + jnp.exp(-x))


def _split2(x):
    hi = x.astype(BF16)
    lo = (x - hi.astype(F32)).astype(BF16)
    return hi, lo


def _split3(x):
    h1 = x.astype(BF16)
    r1 = x - h1.astype(F32)
    h2 = r1.astype(BF16)
    h3 = (r1 - h2.astype(F32)).astype(BF16)
    return h1, h2, h3


def _ada_kernel(c_ref, w_ref, b_ref, o_ref):
    c = c_ref[...]
    sc = c * _sigmoid(c)
    o_ref[...] = _dot(sc.astype(BF16), w_ref[...].astype(BF16)) + b_ref[...]


def _ada_mod(c, ada_w, ada_b):
    depth, d, n = ada_w.shape
    bsz = c.shape[0]
    tn = n // 4
    out = pl.pallas_call(
        _ada_kernel,
        grid=(depth, n // tn),
        in_specs=[
            pl.BlockSpec((bsz, d), lambda i, j: (0, 0)),
            pl.BlockSpec((None, d, tn), lambda i, j: (i, 0, j)),
            pl.BlockSpec((None, 1, tn), lambda i, j: (i, 0, j)),
        ],
        out_specs=pl.BlockSpec((None, bsz, tn), lambda i, j: (i, 0, j)),
        out_shape=jax.ShapeDtypeStruct((depth, bsz, n), F32),
        compiler_params=_params(("parallel", "parallel")),
        name="ada_mod",
    )(c, ada_w, ada_b.reshape(depth, 1, n))
    return out.reshape(depth, bsz, N_MOD, 1, d).transpose(0, 2, 1, 3, 4)


def _norm_kernel(*refs, modulated, n_out):
    if modulated:
        x_ref, g_ref, sh_ref, sc_ref = refs[:4]
    else:
        x_ref, g_ref = refs[:2]
    outs = refs[-n_out:]
    x = x_ref[...]
    ms = jnp.mean(x * x, axis=-1, keepdims=True)
    y = x * lax.rsqrt(ms + RMS_EPS) * g_ref[...]
    if modulated:
        y = y * (1.0 + sc_ref[...]) + sh_ref[...]
    for o in outs:
        o[...] = y.astype(o.dtype)


def _norm_mod(x2, g, shift, scale, bsz, out_dtypes, ts=512):
    t, d = x2.shape
    s = t // bsz
    ts = min(ts, s)
    ns = s // ts
    modulated = shift is not None
    row_spec = pl.BlockSpec((ts, d), lambda b, i: (b * ns + i, 0))
    in_specs = [row_spec, pl.BlockSpec((1, d), lambda b, i: (0, 0))]
    args = [x2, g.reshape(1, d)]
    if modulated:
        vec = pl.BlockSpec((None, 1, d), lambda b, i: (b, 0, 0))
        in_specs += [vec, vec]
        args += [shift, scale]
    outs = pl.pallas_call(
        functools.partial(_norm_kernel, modulated=modulated, n_out=len(out_dtypes)),
        grid=(bsz, ns),
        in_specs=in_specs,
        out_specs=[row_spec] * len(out_dtypes),
        out_shape=[jax.ShapeDtypeStruct((t, d), dt) for dt in out_dtypes],
        compiler_params=_params(("parallel", "parallel")),
        name="norm_mod",
    )(*args)
    return outs


def _proj_res_kernel(a_ref, w_ref, x_ref, ga_ref, o_ref):
    o_ref[...] = x_ref[...] + ga_ref[...] * _dot(a_ref[...], w_ref[...])


def _proj_residual(a, w, x2, gate, bsz, tm=512):
    t, d = x2.shape
    s = t // bsz
    tm = min(tm, s)
    ns = s // tm
    row = lambda b, i: (b * ns + i, 0)
    return pl.pallas_call(
        _proj_res_kernel,
        grid=(bsz, ns),
        in_specs=[
            pl.BlockSpec((tm, a.shape[1]), row),
            pl.BlockSpec(w.shape, lambda b, i: (0, 0)),
            pl.BlockSpec((tm, d), row),
            pl.BlockSpec((None, 1, d), lambda b, i: (b, 0, 0)),
        ],
        out_specs=pl.BlockSpec((tm, d), row),
        out_shape=jax.ShapeDtypeStruct((t, d), F32),
        compiler_params=_params(("parallel", "parallel")),
        name="proj_residual",
    )(a, w, x2, gate)


def _mm_kernel(a_ref, b_ref, o_ref):
    o_ref[...] = _dot(a_ref[...], b_ref[...]).astype(o_ref.dtype)


def _matmul(a, b, out_dtype, tm=1024, tn=1024):
    m, k = a.shape
    n = b.shape[1]
    tm, tn = min(tm, m), min(tn, n)
    return pl.pallas_call(
        _mm_kernel,
        grid=(n // tn, m // tm),
        in_specs=[
            pl.BlockSpec((tm, k), lambda j, i: (i, 0)),
            pl.BlockSpec((k, tn), lambda j, i: (0, j)),
        ],
        out_specs=pl.BlockSpec((tm, tn), lambda j, i: (i, j)),
        out_shape=jax.ShapeDtypeStruct((m, n), out_dtype),
        compiler_params=_params(("parallel", "parallel")),
        name="matmul",
    )(a, b)


def _softplus(z):
    return jnp.maximum(z, 0.0) + jnp.log(1.0 + jnp.exp(-jnp.abs(z)))


def _seg_sum(x, seg_ref, segt_ref):
    xh, xl = _split2(x)
    s = _dot(xh, seg_ref[...]) + _dot(xl, seg_ref[...])
    sh, sl = _split2(s)
    return _dot(sh, segt_ref[...]) + _dot(sl, segt_ref[...])


def _rwkv_prep_kernel(hn_ref, prev_ref, next_ref, mu_ref, wrkv_ref, w1_ref, a1_ref, g1_ref,
                      w2_ref, a2_ref, g2_ref, vec_ref, cm_ref, edm_ref, seg_ref, segt_ref,
                      at_ref, rt_ref, bt_ref, kt_ref, v_ref, ed_ref, g_ref, gb_ref, *, ts, ns):
    i = pl.program_id(1)
    cur = hn_ref[...]
    row = lax.broadcasted_iota(jnp.int32, (ts, 1), 0)
    prev_row = jnp.where(i > 0, prev_ref[7:8, :], 0.0)
    next_row = jnp.where(i < ns - 1, next_ref[0:1, :], 0.0)
    x_prev = jnp.where(row == 0, prev_row, pltpu.roll(cur, 1, 0))
    x_next = jnp.where(row == ts - 1, next_row, pltpu.roll(cur, ts - 1, 0))
    xx = 0.5 * (x_prev + x_next) - cur

    def mix(j):
        return (cur + xx * mu_ref[j:j + 1, :]).astype(BF16)

    r = _dot(mix(0), wrkv_ref[0])
    k = _dot(mix(1), wrkv_ref[1])
    v = _dot(mix(2), wrkv_ref[2])
    tw = jnp.tanh(_dot(mix(3), w1_ref[...]))
    ta = _dot(mix(4), a1_ref[...])
    tg = _sigmoid(_dot(mix(5), g1_ref[...]))
    lane = lax.broadcasted_iota(jnp.int32, (1, tw.shape[1]), 1)
    half = tw.shape[1] // 2
    hg = tg.shape[1] // 2

    k_k = vec_ref[4:5, :]
    k_a = vec_ref[5:6, :]
    r_k = vec_ref[6:7, :]
    kk = k * k_k
    ss = _seg_sum(kk * kk, seg_ref, segt_ref)
    kkn = kk / jnp.maximum(jnp.sqrt(ss), 1e-12)
    v_ref[...] = v.astype(v_ref.dtype)

    gb = jnp.zeros_like(cur)
    for z in range(2):
        sel = (lane < half) if z == 0 else (lane >= half)
        lw = _dot(jnp.where(sel, tw, 0.0).astype(BF16), w2_ref[...])
        la = _dot(jnp.where(sel, ta, 0.0).astype(BF16), a2_ref[...])
        g = _dot(tg[:, z * hg:(z + 1) * hg].astype(BF16), g2_ref[z])
        w_log = -_softplus(-(vec_ref[z:z + 1, :] + lw)) - 0.5
        ld = -jnp.exp(w_log)
        a = _sigmoid(vec_ref[2 + z:3 + z, :] + la)
        kdir = k * (1.0 + (a - 1.0) * k_a)
        b = kkn * a
        parts = _split3(ld)
        cmz = cm_ref[z]
        lm = _dot(cmz, parts[0]) + _dot(cmz, parts[1]) + _dot(cmz, parts[2])
        e_in = jnp.exp(lm)
        e_inv = jnp.exp(-lm)
        e_ex = jnp.exp(lm - ld)
        rt_ref[z] = (r * e_in).astype(rt_ref.dtype)
        at_ref[z] = (-kkn * e_ex).astype(at_ref.dtype)
        bt_ref[z] = (b * e_inv).astype(bt_ref.dtype)
        kt_ref[z] = (kdir * e_inv).astype(kt_ref.dtype)
        edz = edm_ref[z]
        ed = jnp.exp(_dot(edz, parts[0]) + _dot(edz, parts[1]) + _dot(edz, parts[2]))
        for ck in range(ed.shape[0] // 8):
            ed_ref[z, ck] = ed[ck * 8:(ck + 1) * 8]
        g_ref[z] = g.astype(g_ref.dtype)
        bonus = _seg_sum(r * kdir * r_k, seg_ref, segt_ref) * v
        gb = gb + g * bonus
    gb_ref[...] = gb


def _chunk_matrices(ts, c):
    t = np.arange(ts)
    same = (t[:, None] // c) == (t[None, :] // c)
    pos = t % c
    cm = np.zeros((2, ts, ts), np.float32)
    nck = ts // c
    edm = np.zeros((2, 8 * nck, ts), np.float32)
    tri_f = same & (t[None, :] <= t[:, None])
    sel_f = same & (pos[None, :] <= c // 2 - 1)
    cm[0] = tri_f.astype(np.float32) - sel_f.astype(np.float32)
    tri_b = same & (t[None, :] >= t[:, None])
    sel_b = same & (pos[None, :] >= c // 2)
    cm[1] = tri_b.astype(np.float32) - sel_b.astype(np.float32)
    for ck in range(nck):
        inck = (t // c) == ck
        edm[0, 8 * ck] = inck & (pos <= c // 2 - 1)
        edm[0, 8 * ck + 1] = inck & (pos > c // 2 - 1)
        edm[1, 8 * ck] = inck & (pos >= c // 2)
        edm[1, 8 * ck + 1] = inck & (pos < c // 2)
    return jnp.asarray(cm, BF16), jnp.asarray(edm, BF16)


def _seg_matrices(d, head):
    seg = np.zeros((d, 128), np.float32)
    seg[np.arange(d), np.arange(d) // head] = 1.0
    return jnp.asarray(seg, BF16), jnp.asarray(seg.T.copy(), BF16)


def _rwkv_prep(hn, bsz, p, scan_dtype, ts=256):
    t, d = hn.shape
    s = t // bsz
    ts = min(ts, s)
    ns = s // ts
    c = SCAN_CHUNK
    assert ts % c == 0 and ts % 8 == 0
    nck = ts // c
    cm, edm = _chunk_matrices(ts, c)
    seg, segt = _seg_matrices(d, RWKV_HEAD)
    r8 = ts // 8
    nblk8 = t // 8
    row = lambda b, i: (b * ns + i, 0)
    full = lambda a: pl.BlockSpec(a.shape, lambda b, i, _n=a.ndim: (0,) * _n)
    weights = [p["mu8"], p["w_rkv"], p["w1c"], p["a1c"], p["g1c"], p["w2s"], p["a2s"], p["g2"],
               p["vec8"], cm, edm, seg, segt]
    dir_spec = pl.BlockSpec((2, ts, d), lambda b, i: (0, b * ns + i, 0))
    outs = pl.pallas_call(
        functools.partial(_rwkv_prep_kernel, ts=ts, ns=ns),
        grid=(bsz, ns),
        in_specs=[
            pl.BlockSpec((ts, d), row),
            pl.BlockSpec((8, d), lambda b, i: (jnp.maximum((b * ns + i) * r8 - 1, 0), 0)),
            pl.BlockSpec((8, d), lambda b, i: (jnp.minimum((b * ns + i + 1) * r8, nblk8 - 1), 0)),
        ] + [full(a) for a in weights],
        out_specs=[dir_spec, dir_spec, dir_spec, dir_spec,
                   pl.BlockSpec((ts, d), row),
                   pl.BlockSpec((2, nck, 8, d), lambda b, i: (0, b * ns + i, 0, 0)),
                   dir_spec,
                   pl.BlockSpec((ts, d), row)],
        out_shape=[jax.ShapeDtypeStruct((2, t, d), scan_dtype)] * 4 + [
            jax.ShapeDtypeStruct((t, d), scan_dtype),
            jax.ShapeDtypeStruct((2, t // c, 8, d), F32),
            jax.ShapeDtypeStruct((2, t, d), F32),
            jax.ShapeDtypeStruct((t, d), F32)],
        compiler_params=_params(("parallel", "parallel")),
        name="rwkv_prep",
    )(hn, hn, hn, *weights)
    return outs


def _sdot(a, b, dims):
    return lax.dot_general(a.astype(SCAN_DT), b.astype(SCAN_DT), (dims, ((), ())),
                           preferred_element_type=F32)


def _scan_kernel(at_ref, rt_ref, bt_ref, kt_ref, v_ref, ed_ref, y_ref, s_ref, *, c, n_heads):
    z = pl.program_id(1)
    ci = pl.program_id(2)

    @pl.when(ci == 0)
    def _():
        s_ref[...] = jnp.zeros_like(s_ref)

    c2 = 2 * c
    pw_lanes = 2 * RWKV_HEAD
    row = lax.broadcasted_iota(jnp.int32, (c2, c2), 0)
    col = lax.broadcasted_iota(jnp.int32, (c2, c2), 1)
    same_blk = (row // c) == (col // c)
    diff = jnp.where(z == 0, row - col, col - row)
    strict = same_blk & (diff > 0)
    incl = same_blk & (diff >= 0)
    incl2 = jnp.concatenate([incl, incl], axis=1)
    lane_a = lax.broadcasted_iota(jnp.int32, (c, pw_lanes), 1) < RWKV_HEAD
    same_head = ((lax.broadcasted_iota(jnp.int32, (pw_lanes, pw_lanes), 0) // RWKV_HEAD)
                 == (lax.broadcasted_iota(jnp.int32, (pw_lanes, pw_lanes), 1) // RWKV_HEAD))
    n_steps = int(math.log2(c))
    nt = (((1,), (1,)))
    nn = (((1,), (0,)))
    tn = (((0,), (0,)))
    pairs = range(n_heads // 2)
    sls = [pl.ds(hp * pw_lanes, pw_lanes) for hp in pairs]
    zero = jnp.zeros((c, pw_lanes), SCAN_DT)
    lhs, rhs, vv, st = [], [], [], []
    for hp in pairs:
        a2 = at_ref[:, sls[hp]]
        r2 = rt_ref[:, sls[hp]]
        b2 = bt_ref[:, sls[hp]]
        k2 = kt_ref[:, sls[hp]]
        v2 = v_ref[:, sls[hp]]
        lhs.append(jnp.concatenate([jnp.where(lane_a, a2, zero), jnp.where(lane_a, zero, a2),
                                    jnp.where(lane_a, r2, zero), jnp.where(lane_a, zero, r2)], axis=0))
        rhs.append(jnp.concatenate([b2, b2, k2, k2], axis=0))
        vv.append(jnp.concatenate([v2, v2], axis=0))
        st.append(s_ref[hp] * ed_ref[0:1, sls[hp]])
    gm = [_sdot(lhs[hp], rhs[hp], nt) for hp in pairs]
    q = [_sdot(lhs[hp], st[hp], nt) for hp in pairs]
    pw = [jnp.where(strict, gm[hp][:c2, :c2], 0.0).astype(SCAN_DT) for hp in pairs]
    m2 = [jnp.where(strict, gm[hp][:c2, c2:], 0.0).astype(SCAN_DT) for hp in pairs]
    m34 = [jnp.where(incl2, gm[hp][c2:, :], 0.0).astype(SCAN_DT) for hp in pairs]
    x = [q[hp][:c2] + _sdot(m2[hp], vv[hp], nn) for hp in pairs]
    for step in range(n_steps):
        x = [x[hp] + _sdot(pw[hp], x[hp], nn) for hp in pairs]
        if step + 1 < n_steps:
            pw = [_sdot(pw[hp], pw[hp], nn).astype(SCAN_DT) for hp in pairs]
    for hp in pairs:
        xs = x[hp].astype(SCAN_DT)
        uvs = jnp.concatenate([xs, vv[hp]], axis=0)
        ys = q[hp][c2:] + _sdot(m34[hp], uvs, nn)
        y_ref[:, sls[hp]] = jnp.where(lane_a, ys[:c], ys[c:])
        u2 = jnp.where(lane_a, xs[:c], xs[c:])
        upd = _sdot(jnp.concatenate([u2, vv[hp][:c]], axis=0), rhs[hp][c:3 * c], tn)
        s_ref[hp] = (st[hp] + jnp.where(same_head, upd, 0.0)) * ed_ref[1:2, sls[hp]]


def _rwkv_scan(at, rt, bt, kt, v, ed, bsz):
    _, t, d = at.shape
    s = t // bsz
    c = SCAN_CHUNK
    nc = s // c
    n_heads = d // RWKV_HEAD

    def cidx(b, z, ci):
        return b * nc + ci + z * (nc - 1 - 2 * ci)

    dspec = pl.BlockSpec((None, c, d), lambda b, z, ci: (z, cidx(b, z, ci), 0))
    return pl.pallas_call(
        functools.partial(_scan_kernel, c=c, n_heads=n_heads),
        grid=(bsz, 2, nc),
        in_specs=[dspec, dspec, dspec, dspec,
                  pl.BlockSpec((c, d), lambda b, z, ci: (cidx(b, z, ci), 0)),
                  pl.BlockSpec((None, None, 8, d), lambda b, z, ci: (z, cidx(b, z, ci), 0, 0))],
        out_specs=dspec,
        out_shape=jax.ShapeDtypeStruct((2, t, d), F32),
        scratch_shapes=[pltpu.VMEM((n_heads // 2, 2 * RWKV_HEAD, 2 * RWKV_HEAD), F32)],
        compiler_params=_params(("parallel", "parallel", "arbitrary")),
        name="rwkv_scan",
    )(at, rt, bt, kt, v, ed)


def _rwkv_out_kernel(y_ref, g_ref, gb_ref, x_ref, ga_ref, ln_ref, wo_ref, seg_ref, segt_ref, o_ref):
    o = gb_ref[...]
    inv_n = 1.0 / RWKV_HEAD
    for z in range(2):
        y = y_ref[z]
        mean = _seg_sum(y, seg_ref, segt_ref) * inv_n
        yc = y - mean
        var = _seg_sum(yc * yc, seg_ref, segt_ref) * inv_n
        yn = yc * lax.rsqrt(var + RWKV_GN_EPS) * ln_ref[0:1, :] + ln_ref[1:2, :]
        o = o + g_ref[z] * yn
    o_ref[...] = x_ref[...] + ga_ref[...] * _dot(o.astype(BF16), wo_ref[...])


def _rwkv_out(y, g, gb, x2, gate, ln8, w_o, bsz, tm=256):
    t, d = x2.shape
    s = t // bsz
    tm = min(tm, s)
    ns = s // tm
    seg, segt = _seg_matrices(d, RWKV_HEAD)
    row = lambda b, i: (b * ns + i, 0)
    dir_spec = pl.BlockSpec((2, tm, d), lambda b, i: (0, b * ns + i, 0))
    full = lambda a: pl.BlockSpec(a.shape, lambda b, i, _n=a.ndim: (0,) * _n)
    return pl.pallas_call(
        _rwkv_out_kernel,
        grid=(bsz, ns),
        in_specs=[dir_spec, dir_spec, pl.BlockSpec((tm, d), row), pl.BlockSpec((tm, d), row),
                  pl.BlockSpec((None, 1, d), lambda b, i: (b, 0, 0)),
                  full(ln8), full(w_o), full(seg), full(segt)],
        out_specs=pl.BlockSpec((tm, d), row),
        out_shape=jax.ShapeDtypeStruct((t, d), F32),
        compiler_params=_params(("parallel", "parallel")),
        name="rwkv_out",
    )(y, g, gb, x2, gate, ln8, w_o, seg, segt)


def _rwkv_layer(x2, bsz, hn, gate, p):
    at, rt, bt, kt, v, ed, g, gb = _rwkv_prep(hn, bsz, p, SCAN_DT)
    y = _rwkv_scan(at, rt, bt, kt, v, ed, bsz)
    return _rwkv_out(y, g, gb, x2, gate, p["ln8"], p["w_o"], bsz)


def _attn_group_kernel(slope_ref, vid_ref, q_ref, k_ref, v_ref, o_ref, lse_ref, tab_ref, *,
                       tq, sub, w, half, offs, nh, dh, hg):
    first = (pl.program_id(0) == 0) & (pl.program_id(1) == 0) & (pl.program_id(2) == 0)
    i = pl.program_id(2)

    @pl.when(first)
    def _():
        col = lax.broadcasted_iota(jnp.int32, (tq, w), 1)
        row = lax.broadcasted_iota(jnp.int32, (tq, w), 0)
        for v, off in enumerate(offs):
            dist = jnp.abs(col - row + off)
            distf = dist.astype(F32)
            for h in range(nh):
                tab_ref[h * len(offs) + v] = jnp.where(dist <= half, -slope_ref[h] * distf, NEG_INF)

    if w == sub:
        ws = 0
    else:
        ws = pl.multiple_of(jnp.clip(i * tq - half, 0, sub - w), 16)
    var = vid_ref[i]
    nt = (((1,), (1,)), ((), ()))
    lane = lax.broadcasted_iota(jnp.int32, (tq, 128), 1)
    lse = jnp.zeros((tq, 128), F32)
    for h0 in range(0, nh, hg):
        heads = range(h0, min(h0 + hg, nh))
        hs = {h: slice(h * dh, (h + 1) * dh) for h in heads}
        sc = {h: lax.dot_general(q_ref[:, hs[h]], k_ref[pl.ds(ws, w), hs[h]], nt, preferred_element_type=F32)
              + tab_ref[h * len(offs) + var] for h in heads}
        m = {h: sc[h].max(axis=-1, keepdims=True) for h in heads}
        p = {h: jnp.exp(sc[h] - m[h]) for h in heads}
        den = {h: p[h].sum(axis=-1, keepdims=True) for h in heads}
        acc = {h: _dot(p[h].astype(BF16), v_ref[pl.ds(ws, w), hs[h]]) for h in heads}
        for h in heads:
            o_ref[:, hs[h]] = (acc[h] / den[h]).astype(o_ref.dtype)
            lse = jnp.where(lane == h, m[h] + jnp.log(den[h]), lse)
    lse_ref[...] = lse


def _alibi_slopes(n):
    return 2.0 ** (-8.0 * jnp.arange(1, n + 1, dtype=F32) / n)


def _attention_group(qkv, bsz, gi, tq=256):
    t, width = qkv.shape
    s = t // bsz
    window, dil = ATTN_GROUPS[gi]
    half = window // (2 * dil)
    sub = s // dil
    tq = min(tq, sub)
    nq = sub // tq
    w = min(sub, tq + 2 * half)
    nh, dh = ATTN_HEADS, ATTN_HEAD_DIM
    hw = nh * dh
    ncol = width // hw
    offs = sorted({int(np.clip(i * tq - half, 0, sub - w)) - i * tq for i in range(nq)})
    vid = jnp.asarray([offs.index(int(np.clip(i * tq - half, 0, sub - w)) - i * tq) for i in range(nq)], jnp.int32)
    slopes = _alibi_slopes(len(ATTN_GROUPS) * nh).reshape(len(ATTN_GROUPS), nh)[gi] * dil
    qkv_sub = qkv.reshape(bsz * sub, dil * width)
    smem = pl.BlockSpec(memory_space=pltpu.SMEM)
    o, lse = pl.pallas_call(
        functools.partial(_attn_group_kernel, tq=tq, sub=sub, w=w, half=half, offs=tuple(offs), nh=nh, dh=dh,
                          hg=nh if tq * w <= 128 * 128 else 4),
        grid=(bsz, dil, nq),
        in_specs=[smem, smem,
                  pl.BlockSpec((tq, hw), lambda b, r, i: (b * nq + i, r * ncol + gi * 3)),
                  pl.BlockSpec((sub, hw), lambda b, r, i: (b, r * ncol + gi * 3 + 1)),
                  pl.BlockSpec((sub, hw), lambda b, r, i: (b, r * ncol + gi * 3 + 2))],
        out_specs=[pl.BlockSpec((tq, hw), lambda b, r, i: (b * nq + i, r)),
                   pl.BlockSpec((tq, 128), lambda b, r, i: (b * nq + i, r))],
        out_shape=[jax.ShapeDtypeStruct((bsz * sub, dil * hw), BF16),
                   jax.ShapeDtypeStruct((bsz * sub, dil * 128), F32)],
        scratch_shapes=[pltpu.VMEM((nh * len(offs), tq, w), F32)],
        compiler_params=_params(("arbitrary", "arbitrary", "arbitrary")),
        name=f"dilated_attention_g{gi}",
    )(slopes, vid, qkv_sub, qkv_sub, qkv_sub)
    return o.reshape(t, hw), lse.reshape(t, 128)


def _attn_out_kernel(o1, o2, o3, l1, l2, l3, ex_ref, w_ref, x_ref, ga_ref, out_ref):
    ls = [l1[...], l2[...], l3[...]]
    m = jnp.maximum(jnp.maximum(ls[0], ls[1]), ls[2])
    es = [jnp.exp(l - m) for l in ls]
    inv = 1.0 / (es[0] + es[1] + es[2])
    merged = None
    for e, o_ref in zip(es, (o1, o2, o3)):
        ah, al = _split2(e * inv)
        alpha = _dot(ah, ex_ref[...]) + _dot(al, ex_ref[...])
        term = alpha * o_ref[...].astype(F32)
        merged = term if merged is None else merged + term
    out_ref[...] = x_ref[...] + ga_ref[...] * _dot(merged.astype(BF16), w_ref[...])


def _attn_out(os, lses, w_o, x2, gate, bsz, tm=512):
    t, d = x2.shape
    s = t // bsz
    tm = min(tm, s)
    ns = s // tm
    hw = os[0].shape[1]
    ex = np.zeros((128, hw), np.float32)
    ex[np.arange(hw) // ATTN_HEAD_DIM, np.arange(hw)] = 1.0
    ex = jnp.asarray(ex, BF16)
    row = lambda b, i: (b * ns + i, 0)
    return pl.pallas_call(
        _attn_out_kernel,
        grid=(bsz, ns),
        in_specs=[pl.BlockSpec((tm, hw), row)] * 3 + [pl.BlockSpec((tm, 128), row)] * 3 + [
            pl.BlockSpec(ex.shape, lambda b, i: (0, 0)),
            pl.BlockSpec(w_o.shape, lambda b, i: (0, 0)),
            pl.BlockSpec((tm, d), row),
            pl.BlockSpec((None, 1, d), lambda b, i: (b, 0, 0))],
        out_specs=pl.BlockSpec((tm, d), row),
        out_shape=jax.ShapeDtypeStruct((t, d), F32),
        compiler_params=_params(("parallel", "parallel")),
        name="attn_out",
    )(*os, *lses, ex, w_o, x2, gate)


def _pack_attn(j, w_qkv, w_o):
    d = w_qkv.shape[1]
    w = w_qkv[j].reshape(d, len(ATTN_GROUPS), 3, ATTN_HEADS * ATTN_HEAD_DIM)
    w = w.at[:, :, 0].multiply(ATTN_HEAD_DIM ** -0.5)
    return {"w_qkv": w.reshape(d, -1).astype(BF16), "w_o": w_o[j].astype(BF16)}


def _attn_layer(x2, bsz, hn_bf16, gate, p):
    qkv = _matmul(hn_bf16, p["w_qkv"], BF16)
    outs = [_attention_group(qkv, bsz, gi) for gi in range(len(ATTN_GROUPS))]
    return _attn_out([o for o, _ in outs], [l for _, l in outs], p["w_o"], x2, gate, bsz)


def _router_kernel(h_ref, wh_ref, wl_ref, b_ref, o_ref):
    hh, hl = _split2(h_ref[...])
    logits = _dot(hh, wh_ref[...]) + _dot(hl, wh_ref[...]) + _dot(hh, wl_ref[...]) + b_ref[...]
    lane = lax.broadcasted_iota(jnp.int32, logits.shape, 1)
    ng, ne = N_EXPERT_GROUPS, EXPERTS_PER_GROUP
    big = jnp.int32(ROUTER_LANES)
    is_g = lane < ng
    gl = jnp.where(is_g, logits, -jnp.inf)
    gmax = gl.max(axis=-1, keepdims=True)
    gsum = jnp.where(is_g, jnp.exp(logits - gmax), 0.0).sum(axis=-1, keepdims=True)
    p_group = 1.0 / gsum
    g_top = jnp.where(gl == gmax, lane, big).min(axis=-1, keepdims=True)
    lo = ng + ne * g_top
    es = jnp.where((lane >= lo) & (lane < lo + ne), logits, -jnp.inf)
    v1 = es.max(axis=-1, keepdims=True)
    i1 = jnp.where(es == v1, lane, big).min(axis=-1, keepdims=True)
    es2 = jnp.where(lane == i1, -jnp.inf, es)
    v2 = es2.max(axis=-1, keepdims=True)
    i2 = jnp.where(es2 == v2, lane, big).min(axis=-1, keepdims=True)
    e2 = jnp.exp(v2 - v1)
    w1 = p_group / (1.0 + e2)
    w2 = p_group * e2 / (1.0 + e2)
    o_ref[...] = jnp.where(lane == i1, w1, 0.0) + jnp.where(lane == i2, w2, 0.0)


def _router(hn, wr_hi, wr_lo, bias, tm=512):
    t, d = hn.shape
    tm = min(tm, t)
    full = lambda a: pl.BlockSpec(a.shape, lambda i: (0, 0))
    return pl.pallas_call(
        _router_kernel,
        grid=(t // tm,),
        in_specs=[pl.BlockSpec((tm, d), lambda i: (i, 0)), full(wr_hi), full(wr_lo), full(bias)],
        out_specs=pl.BlockSpec((tm, ROUTER_LANES), lambda i: (i, 0)),
        out_shape=jax.ShapeDtypeStruct((t, ROUTER_LANES), F32),
        compiler_params=_params(("parallel",)),
        name="moe_router",
    )(hn, wr_hi, wr_lo, bias)


def _experts_kernel(h_ref, gates_ref, wgu_ref, wd_ref, x_ref, ga_ref, o_ref, *, f, per_step):
    step = pl.program_id(1)
    gates = gates_ref[...]
    lane = lax.broadcasted_iota(jnp.int32, gates.shape, 1)
    h = h_ref[...]
    hids = []
    for j in range(per_step):
        e = step * per_step + j
        ge = jnp.where(lane == N_EXPERT_GROUPS + e, gates, 0.0).sum(axis=-1, keepdims=True)
        gu = _dot(h, wgu_ref[j])
        gp = gu[:, :f]
        hids.append((gp * _sigmoid(gp) * gu[:, f:] * ge).astype(BF16))
    part = _dot(jnp.concatenate(hids, axis=1), wd_ref[...])

    @pl.when(step == 0)
    def _():
        o_ref[...] = part

    @pl.when(step > 0)
    def _():
        o_ref[...] += part

    @pl.when(step == pl.num_programs(1) - 1)
    def _():
        o_ref[...] = x_ref[...] + ga_ref[...] * o_ref[...]


def _experts(hn_bf16, gates, wgu, wd, x2, gate, bsz, tm=1024, per_step=4):
    t, d = x2.shape
    s = t // bsz
    tm = min(tm, s)
    ns = s // tm
    ne, _, f2 = wgu.shape
    f = f2 // 2
    row = lambda i, e: (i, 0)
    return pl.pallas_call(
        functools.partial(_experts_kernel, f=f, per_step=per_step),
        grid=(t // tm, ne // per_step),
        in_specs=[
            pl.BlockSpec((tm, d), row),
            pl.BlockSpec((tm, ROUTER_LANES), row),
            pl.BlockSpec((per_step, d, f2), lambda i, e: (e, 0, 0)),
            pl.BlockSpec((per_step * f, d), lambda i, e: (e, 0)),
            pl.BlockSpec((tm, d), row),
            pl.BlockSpec((None, 1, d), lambda i, e: (i // ns, 0, 0)),
        ],
        out_specs=pl.BlockSpec((tm, d), row),
        out_shape=jax.ShapeDtypeStruct((t, d), F32),
        compiler_params=_params(("parallel", "arbitrary")),
        name="moe_experts",
    )(hn_bf16, gates, wgu, wd.reshape(ne * f, d), x2, gate)


def _moe_layer(x2, bsz, hn, hn_bf16, gate, p):
    gates = _router(hn, p["wr_hi"], p["wr_lo"], p["r_bias"])
    return _experts(hn_bf16, gates, p["wgu"], p["wd"], x2, gate, bsz)


def _pad_rows(rows, d):
    out = jnp.zeros((8, d), F32)
    return out.at[:len(rows)].set(jnp.stack([r.reshape(d).astype(F32) for r in rows]))


def _pack_rwkv(j, mu, w_rkv, w0, w1, w2, a0, a1, a2, g1, g2, k_k, k_a, r_k, ln_w, ln_b, w_o):
    d = w_o.shape[-1]
    cat = lambda m: jnp.concatenate([m[j, 0], m[j, 1]], axis=1).astype(BF16)
    stack = lambda m: jnp.concatenate([m[j, 0], m[j, 1]], axis=0).astype(BF16)
    return {
        "mu8": _pad_rows(list(mu[j]), d),
        "w_rkv": w_rkv[j].astype(BF16),
        "w1c": cat(w1), "a1c": cat(a1), "g1c": cat(g1),
        "w2s": stack(w2), "a2s": stack(a2), "g2": g2[j].astype(BF16),
        "vec8": _pad_rows([w0[j, 0], w0[j, 1], a0[j, 0], a0[j, 1], k_k[j], k_a[j], r_k[j]], d),
        "ln8": _pad_rows([ln_w[j], ln_b[j]], d),
        "w_o": w_o[j].astype(BF16),
    }


def _pack_moe(i, router_g, router_g_b, router_e, router_e_b, w_gate, w_up, w_down):
    d = router_g.shape[1]
    ng, ne = N_EXPERT_GROUPS, EXPERTS_PER_GROUP
    wr = jnp.zeros((d, ROUTER_LANES), F32)
    wr = wr.at[:, :ng].set(router_g[i])
    wr = wr.at[:, ng:ng + ng * ne].set(router_e[i].transpose(1, 0, 2).reshape(d, ng * ne))
    bias = jnp.zeros((1, ROUTER_LANES), F32)
    bias = bias.at[0, :ng].set(router_g_b[i])
    bias = bias.at[0, ng:ng + ng * ne].set(router_e_b[i].reshape(ng * ne))
    wr_hi = wr.astype(BF16)
    wr_lo = (wr - wr_hi.astype(F32)).astype(BF16)
    f = w_gate.shape[-1]
    wgu = jnp.concatenate([w_gate[i], w_up[i]], axis=-1).reshape(ng * ne, d, 2 * f).astype(BF16)
    wd = w_down[i].reshape(ng * ne, f, d).astype(BF16)
    return {"wr_hi": wr_hi, "wr_lo": wr_lo, "r_bias": bias, "wgu": wgu, "wd": wd}


def kernel(x, c, ada_w, ada_b, norm_tm_g, norm_cm_g, rw_mu, rw_w_rkv, rw_w0, rw_w1, rw_w2, rw_a0, rw_a1, rw_a2, rw_g1, rw_g2, rw_k_k, rw_k_a, rw_r_k, rw_ln_w, rw_ln_b, rw_w_o, at_w_qkv, at_w_o, moe_router_g, moe_router_g_b, moe_router_e, moe_router_e_b, moe_w_gate, moe_w_up, moe_w_down, final_g):
    bsz, s, d = x.shape
    depth = ada_w.shape[0]
    x2 = x.reshape(bsz * s, d)
    mod = _ada_mod(c, ada_w, ada_b)
    for i in range(depth):
        sh_t, sc_t, ga_t, sh_c, sc_c, ga_c = (mod[i, m] for m in range(N_MOD))
        j = i // 2
        if i % 2 == 0:
            (hn,) = _norm_mod(x2, norm_tm_g[i], sh_t, sc_t, bsz, (F32,))
            p = _pack_rwkv(j, rw_mu, rw_w_rkv, rw_w0, rw_w1, rw_w2, rw_a0, rw_a1, rw_a2, rw_g1, rw_g2,
                           rw_k_k, rw_k_a, rw_r_k, rw_ln_w, rw_ln_b, rw_w_o)
            x2 = _rwkv_layer(x2, bsz, hn, ga_t, p)
        else:
            (hn_b,) = _norm_mod(x2, norm_tm_g[i], sh_t, sc_t, bsz, (BF16,))
            p = _pack_attn(j, at_w_qkv, at_w_o)
            x2 = _attn_layer(x2, bsz, hn_b, ga_t, p)
        hn, hn_b = _norm_mod(x2, norm_cm_g[i], sh_c, sc_c, bsz, (F32, BF16))
        pm = _pack_moe(i, moe_router_g, moe_router_g_b, moe_router_e, moe_router_e_b,
                       moe_w_gate, moe_w_up, moe_w_down)
        x2 = _moe_layer(x2, bsz, hn, hn_b, ga_c, pm)
    (out,) = _norm_mod(x2, final_g, None, None, bsz, (F32,))
    return out.reshape(bsz, s, d)
```

```python
import functools
import math

import numpy as np
import jax
import jax.numpy as jnp
from jax import lax
from jax.experimental import pallas as pl
from jax.experimental.pallas import tpu as pltpu

F32 = jnp.float32
BF16 = jnp.bfloat16

RMS_EPS = 1e-6
N_MOD = 6
RWKV_HEAD = 64
RWKV_GN_EPS = 64e-5
ATTN_GROUPS = ((128, 1), (512, 4), (2048, 16))
ATTN_HEADS = 8
ATTN_HEAD_DIM = 128
NEG_INF = -1e30
N_EXPERT_GROUPS = 4
EXPERTS_PER_GROUP = 8
N_EXPERTS = N_EXPERT_GROUPS * EXPERTS_PER_GROUP
ROUTER_LANES = 128

SCAN_CHUNK = 64
V7X_VMEM_LIMIT = 56 * 1024 * 1024
SCAN_DT = BF16


def _params(sem):
    return pltpu.CompilerParams(dimension_semantics=sem, vmem_limit_bytes=V7X_VMEM_LIMIT)


def _dot(a, b):
    return jnp.dot(a, b, preferred_element_type=F32)


def _sigmoid(x):
    return 1.0 / (1.0 + jnp.exp(-x))


def _split2(x):
    hi = x.astype(BF16)
    lo = (x - hi.astype(F32)).astype(BF16)
    return hi, lo


def _split3(x):
    h1 = x.astype(BF16)
    r1 = x - h1.astype(F32)
    h2 = r1.astype(BF16)
    h3 = (r1 - h2.astype(F32)).astype(BF16)
    return h1, h2, h3


def _ada_kernel(c_ref, w_ref, b_ref, o_ref):
    c = c_ref[...]
    sc = c * _sigmoid(c)
    o_ref[...] = _dot(sc.astype(BF16), w_ref[...].astype(BF16)) + b_ref[...]


def _ada_mod(c, ada_w, ada_b):
    depth, d, n = ada_w.shape
    bsz = c.shape[0]
    tn = n // 4
    out = pl.pallas_call(
        _ada_kernel,
        grid=(depth, n // tn),
        in_specs=[
            pl.BlockSpec((bsz, d), lambda i, j: (0, 0)),
            pl.BlockSpec((None, d, tn), lambda i, j: (i, 0, j)),
            pl.BlockSpec((None, 1, tn), lambda i, j: (i, 0, j)),
        ],
        out_specs=pl.BlockSpec((None, bsz, tn), lambda i, j: (i, 0, j)),
        out_shape=jax.ShapeDtypeStruct((depth, bsz, n), F32),
        compiler_params=_params(("parallel", "parallel")),
        name="ada_mod",
    )(c, ada_w, ada_b.reshape(depth, 1, n))
    return out.reshape(depth, bsz, N_MOD, 1, d).transpose(0, 2, 1, 3, 4)


def _norm_kernel(*refs, modulated, n_out):
    if modulated:
        x_ref, g_ref, sh_ref, sc_ref = refs[:4]
    else:
        x_ref, g_ref = refs[:2]
    outs = refs[-n_out:]
    x = x_ref[...]
    ms = jnp.mean(x * x, axis=-1, keepdims=True)
    y = x * lax.rsqrt(ms + RMS_EPS) * g_ref[...]
    if modulated:
        y = y * (1.0 + sc_ref[...]) + sh_ref[...]
    for o in outs:
        o[...] = y.astype(o.dtype)


def _norm_mod(x2, g, shift, scale, bsz, out_dtypes, ts=512):
    t, d = x2.shape
    s = t // bsz
    ts = min(ts, s)
    ns = s // ts
    modulated = shift is not None
    row_spec = pl.BlockSpec((ts, d), lambda b, i: (b * ns + i, 0))
    in_specs = [row_spec, pl.BlockSpec((1, d), lambda b, i: (0, 0))]
    args = [x2, g.reshape(1, d)]
    if modulated:
        vec = pl.BlockSpec((None, 1, d), lambda b, i: (b, 0, 0))
        in_specs += [vec, vec]
        args += [shift, scale]
    outs = pl.pallas_call(
        functools.partial(_norm_kernel, modulated=modulated, n_out=len(out_dtypes)),
        grid=(bsz, ns),
        in_specs=in_specs,
        out_specs=[row_spec] * len(out_dtypes),
        out_shape=[jax.ShapeDtypeStruct((t, d), dt) for dt in out_dtypes],
        compiler_params=_params(("parallel", "parallel")),
        name="norm_mod",
    )(*args)
    return outs


def _proj_res_kernel(a_ref, w_ref, x_ref, ga_ref, o_ref):
    o_ref[...] = x_ref[...] + ga_ref[...] * _dot(a_ref[...], w_ref[...])


def _proj_residual(a, w, x2, gate, bsz, tm=512):
    t, d = x2.shape
    s = t // bsz
    tm = min(tm, s)
    ns = s // tm
    row = lambda b, i: (b * ns + i, 0)
    return pl.pallas_call(
        _proj_res_kernel,
        grid=(bsz, ns),
        in_specs=[
            pl.BlockSpec((tm, a.shape[1]), row),
            pl.BlockSpec(w.shape, lambda b, i: (0, 0)),
            pl.BlockSpec((tm, d), row),
            pl.BlockSpec((None, 1, d), lambda b, i: (b, 0, 0)),
        ],
        out_specs=pl.BlockSpec((tm, d), row),
        out_shape=jax.ShapeDtypeStruct((t, d), F32),
        compiler_params=_params(("parallel", "parallel")),
        name="proj_residual",
    )(a, w, x2, gate)


def _mm_kernel(a_ref, b_ref, o_ref):
    o_ref[...] = _dot(a_ref[...], b_ref[...]).astype(o_ref.dtype)


def _matmul(a, b, out_dtype, tm=1024, tn=1024):
    m, k = a.shape
    n = b.shape[1]
    tm, tn = min(tm, m), min(tn, n)
    return pl.pallas_call(
        _mm_kernel,
        grid=(n // tn, m // tm),
        in_specs=[
            pl.BlockSpec((tm, k), lambda j, i: (i, 0)),
            pl.BlockSpec((k, tn), lambda j, i: (0, j)),
        ],
        out_specs=pl.BlockSpec((tm, tn), lambda j, i: (i, j)),
        out_shape=jax.ShapeDtypeStruct((m, n), out_dtype),
        compiler_params=_params(("parallel", "parallel")),
        name="matmul",
    )(a, b)


def _softplus(z):
    return jnp.maximum(z, 0.0) + jnp.log(1.0 + jnp.exp(-jnp.abs(z)))


def _seg_sum(x, seg_ref, segt_ref):
    xh, xl = _split2(x)
    s = _dot(xh, seg_ref[...]) + _dot(xl, seg_ref[...])
    sh, sl = _split2(s)
    return _dot(sh, segt_ref[...]) + _dot(sl, segt_ref[...])


def _rwkv_prep_kernel(hn_ref, prev_ref, next_ref, mu_ref, wrkv_ref, w1_ref, a1_ref, g1_ref,
                      w2_ref, a2_ref, g2_ref, vec_ref, cm_ref, edm_ref, seg_ref, segt_ref,
                      at_ref, rt_ref, bt_ref, kt_ref, v_ref, ed_ref, g_ref, gb_ref, *, ts, ns):
    i = pl.program_id(1)
    cur = hn_ref[...]
    row = lax.broadcasted_iota(jnp.int32, (ts, 1), 0)
    prev_row = jnp.where(i > 0, prev_ref[7:8, :], 0.0)
    next_row = jnp.where(i < ns - 1, next_ref[0:1, :], 0.0)
    x_prev = jnp.where(row == 0, prev_row, pltpu.roll(cur, 1, 0))
    x_next = jnp.where(row == ts - 1, next_row, pltpu.roll(cur, ts - 1, 0))
    xx = 0.5 * (x_prev + x_next) - cur

    def mix(j):
        return (cur + xx * mu_ref[j:j + 1, :]).astype(BF16)

    r = _dot(mix(0), wrkv_ref[0])
    k = _dot(mix(1), wrkv_ref[1])
    v = _dot(mix(2), wrkv_ref[2])
    tw = jnp.tanh(_dot(mix(3), w1_ref[...]))
    ta = _dot(mix(4), a1_ref[...])
    tg = _sigmoid(_dot(mix(5), g1_ref[...]))
    lane = lax.broadcasted_iota(jnp.int32, (1, tw.shape[1]), 1)
    half = tw.shape[1] // 2
    hg = tg.shape[1] // 2

    k_k = vec_ref[4:5, :]
    k_a = vec_ref[5:6, :]
    r_k = vec_ref[6:7, :]
    kk = k * k_k
    ss = _seg_sum(kk * kk, seg_ref, segt_ref)
    kkn = kk / jnp.maximum(jnp.sqrt(ss), 1e-12)
    v_ref[...] = v.astype(v_ref.dtype)

    gb = jnp.zeros_like(cur)
    for z in range(2):
        sel = (lane < half) if z == 0 else (lane >= half)
        lw = _dot(jnp.where(sel, tw, 0.0).astype(BF16), w2_ref[...])
        la = _dot(jnp.where(sel, ta, 0.0).astype(BF16), a2_ref[...])
        g = _dot(tg[:, z * hg:(z + 1) * hg].astype(BF16), g2_ref[z])
        w_log = -_softplus(-(vec_ref[z:z + 1, :] + lw)) - 0.5
        ld = -jnp.exp(w_log)
        a = _sigmoid(vec_ref[2 + z:3 + z, :] + la)
        kdir = k * (1.0 + (a - 1.0) * k_a)
        b = kkn * a
        parts = _split3(ld)
        cmz = cm_ref[z]
        lm = _dot(cmz, parts[0]) + _dot(cmz, parts[1]) + _dot(cmz, parts[2])
        e_in = jnp.exp(lm)
        e_inv = jnp.exp(-lm)
        e_ex = jnp.exp(lm - ld)
        rt_ref[z] = (r * e_in).astype(rt_ref.dtype)
        at_ref[z] = (-kkn * e_ex).astype(at_ref.dtype)
        bt_ref[z] = (b * e_inv).astype(bt_ref.dtype)
        kt_ref[z] = (kdir * e_inv).astype(kt_ref.dtype)
        edz = edm_ref[z]
        ed = jnp.exp(_dot(edz, parts[0]) + _dot(edz, parts[1]) + _dot(edz, parts[2]))
        for ck in range(ed.shape[0] // 8):
            ed_ref[z, ck] = ed[ck * 8:(ck + 1) * 8]
        g_ref[z] = g.astype(g_ref.dtype)
        bonus = _seg_sum(r * kdir * r_k, seg_ref, segt_ref) * v
        gb = gb + g * bonus
    gb_ref[...] = gb


def _chunk_matrices(ts, c):
    t = np.arange(ts)
    same = (t[:, None] // c) == (t[None, :] // c)
    pos = t % c
    cm = np.zeros((2, ts, ts), np.float32)
    nck = ts // c
    edm = np.zeros((2, 8 * nck, ts), np.float32)
    tri_f = same & (t[None, :] <= t[:, None])
    sel_f = same & (pos[None, :] <= c // 2 - 1)
    cm[0] = tri_f.astype(np.float32) - sel_f.astype(np.float32)
    tri_b = same & (t[None, :] >= t[:, None])
    sel_b = same & (pos[None, :] >= c // 2)
    cm[1] = tri_b.astype(np.float32) - sel_b.astype(np.float32)
    for ck in range(nck):
        inck = (t // c) == ck
        edm[0, 8 * ck] = inck & (pos <= c // 2 - 1)
        edm[0, 8 * ck + 1] = inck & (pos > c // 2 - 1)
        edm[1, 8 * ck] = inck & (pos >= c // 2)
        edm[1, 8 * ck + 1] = inck & (pos < c // 2)
    return jnp.asarray(cm, BF16), jnp.asarray(edm, BF16)


def _seg_matrices(d, head):
    seg = np.zeros((d, 128), np.float32)
    seg[np.arange(d), np.arange(d) // head] = 1.0
    return jnp.asarray(seg, BF16), jnp.asarray(seg.T.copy(), BF16)


def _rwkv_prep(hn, bsz, p, scan_dtype, ts=256):
    t, d = hn.shape
    s = t // bsz
    ts = min(ts, s)
    ns = s // ts
    c = SCAN_CHUNK
    assert ts % c == 0 and ts % 8 == 0
    nck = ts // c
    cm, edm = _chunk_matrices(ts, c)
    seg, segt = _seg_matrices(d, RWKV_HEAD)
    r8 = ts // 8
    nblk8 = t // 8
    row = lambda b, i: (b * ns + i, 0)
    full = lambda a: pl.BlockSpec(a.shape, lambda b, i, _n=a.ndim: (0,) * _n)
    weights = [p["mu8"], p["w_rkv"], p["w1c"], p["a1c"], p["g1c"], p["w2s"], p["a2s"], p["g2"],
               p["vec8"], cm, edm, seg, segt]
    dir_spec = pl.BlockSpec((2, ts, d), lambda b, i: (0, b * ns + i, 0))
    outs = pl.pallas_call(
        functools.partial(_rwkv_prep_kernel, ts=ts, ns=ns),
        grid=(bsz, ns),
        in_specs=[
            pl.BlockSpec((ts, d), row),
            pl.BlockSpec((8, d), lambda b, i: (jnp.maximum((b * ns + i) * r8 - 1, 0), 0)),
            pl.BlockSpec((8, d), lambda b, i: (jnp.minimum((b * ns + i + 1) * r8, nblk8 - 1), 0)),
        ] + [full(a) for a in weights],
        out_specs=[dir_spec, dir_spec, dir_spec, dir_spec,
                   pl.BlockSpec((ts, d), row),
                   pl.BlockSpec((2, nck, 8, d), lambda b, i: (0, b * ns + i, 0, 0)),
                   dir_spec,
                   pl.BlockSpec((ts, d), row)],
        out_shape=[jax.ShapeDtypeStruct((2, t, d), scan_dtype)] * 4 + [
            jax.ShapeDtypeStruct((t, d), scan_dtype),
            jax.ShapeDtypeStruct((2, t // c, 8, d), F32),
            jax.ShapeDtypeStruct((2, t, d), F32),
            jax.ShapeDtypeStruct((t, d), F32)],
        compiler_params=_params(("parallel", "parallel")),
        name="rwkv_prep",
    )(hn, hn, hn, *weights)
    return outs


def _sdot(a, b, dims):
    return lax.dot_general(a.astype(SCAN_DT), b.astype(SCAN_DT), (dims, ((), ())),
                           preferred_element_type=F32)


def _scan_kernel(at_ref, rt_ref, bt_ref, kt_ref, v_ref, ed_ref, y_ref, s_ref, *, c, n_heads):
    z = pl.program_id(1)
    ci = pl.program_id(2)

    @pl.when(ci == 0)
    def _():
        s_ref[...] = jnp.zeros_like(s_ref)

    c2 = 2 * c
    pw_lanes = 2 * RWKV_HEAD
    row = lax.broadcasted_iota(jnp.int32, (c2, c2), 0)
    col = lax.broadcasted_iota(jnp.int32, (c2, c2), 1)
    same_blk = (row // c) == (col // c)
    diff = jnp.where(z == 0, row - col, col - row)
    strict = same_blk & (diff > 0)
    incl = same_blk & (diff >= 0)
    incl2 = jnp.concatenate([incl, incl], axis=1)
    lane_a = lax.broadcasted_iota(jnp.int32, (c, pw_lanes), 1) < RWKV_HEAD
    same_head = ((lax.broadcasted_iota(jnp.int32, (pw_lanes, pw_lanes), 0) // RWKV_HEAD)
                 == (lax.broadcasted_iota(jnp.int32, (pw_lanes, pw_lanes), 1) // RWKV_HEAD))
    n_steps = int(math.log2(c))
    nt = (((1,), (1,)))
    nn = (((1,), (0,)))
    tn = (((0,), (0,)))
    pairs = range(n_heads // 2)
    sls = [pl.ds(hp * pw_lanes, pw_lanes) for hp in pairs]
    zero = jnp.zeros((c, pw_lanes), SCAN_DT)
    lhs, rhs, vv, st = [], [], [], []
    for hp in pairs:
        a2 = at_ref[:, sls[hp]]
        r2 = rt_ref[:, sls[hp]]
        b2 = bt_ref[:, sls[hp]]
        k2 = kt_ref[:, sls[hp]]
        v2 = v_ref[:, sls[hp]]
        lhs.append(jnp.concatenate([jnp.where(lane_a, a2, zero), jnp.where(lane_a, zero, a2),
                                    jnp.where(lane_a, r2, zero), jnp.where(lane_a, zero, r2)], axis=0))
        rhs.append(jnp.concatenate([b2, b2, k2, k2], axis=0))
        vv.append(jnp.concatenate([v2, v2], axis=0))
        st.append(s_ref[hp] * ed_ref[0:1, sls[hp]])
    gm = [_sdot(lhs[hp], rhs[hp], nt) for hp in pairs]
    q = [_sdot(lhs[hp], st[hp], nt) for hp in pairs]
    pw = [jnp.where(strict, gm[hp][:c2, :c2], 0.0).astype(SCAN_DT) for hp in pairs]
    m2 = [jnp.where(strict, gm[hp][:c2, c2:], 0.0).astype(SCAN_DT) for hp in pairs]
    m34 = [jnp.where(incl2, gm[hp][c2:, :], 0.0).astype(SCAN_DT) for hp in pairs]
    x = [q[hp][:c2] + _sdot(m2[hp], vv[hp], nn) for hp in pairs]
    for step in range(n_steps):
        x = [x[hp] + _sdot(pw[hp], x[hp], nn) for hp in pairs]
        if step + 1 < n_steps:
            pw = [_sdot(pw[hp], pw[hp], nn).astype(SCAN_DT) for hp in pairs]
    for hp in pairs:
        xs = x[hp].astype(SCAN_DT)
        uvs = jnp.concatenate([xs, vv[hp]], axis=0)
        ys = q[hp][c2:] + _sdot(m34[hp], uvs, nn)
        y_ref[:, sls[hp]] = jnp.where(lane_a, ys[:c], ys[c:])
        u2 = jnp.where(lane_a, xs[:c], xs[c:])
        upd = _sdot(jnp.concatenate([u2, vv[hp][:c]], axis=0), rhs[hp][c:3 * c], tn)
        s_ref[hp] = (st[hp] + jnp.where(same_head, upd, 0.0)) * ed_ref[1:2, sls[hp]]


def _rwkv_scan(at, rt, bt, kt, v, ed, bsz):
    _, t, d = at.shape
    s = t // bsz
    c = SCAN_CHUNK
    nc = s // c
    n_heads = d // RWKV_HEAD

    def cidx(b, z, ci):
        return b * nc + ci + z * (nc - 1 - 2 * ci)

    dspec = pl.BlockSpec((None, c, d), lambda b, z, ci: (z, cidx(b, z, ci), 0))
    return pl.pallas_call(
        functools.partial(_scan_kernel, c=c, n_heads=n_heads),
        grid=(bsz, 2, nc),
        in_specs=[dspec, dspec, dspec, dspec,
                  pl.BlockSpec((c, d), lambda b, z, ci: (cidx(b, z, ci), 0)),
                  pl.BlockSpec((None, None, 8, d), lambda b, z, ci: (z, cidx(b, z, ci), 0, 0))],
        out_specs=dspec,
        out_shape=jax.ShapeDtypeStruct((2, t, d), F32),
        scratch_shapes=[pltpu.VMEM((n_heads // 2, 2 * RWKV_HEAD, 2 * RWKV_HEAD), F32)],
        compiler_params=_params(("parallel", "parallel", "arbitrary")),
        name="rwkv_scan",
    )(at, rt, bt, kt, v, ed)


def _rwkv_out_kernel(y_ref, g_ref, gb_ref, x_ref, ga_ref, ln_ref, wo_ref, seg_ref, segt_ref, o_ref):
    o = gb_ref[...]
    inv_n = 1.0 / RWKV_HEAD
    for z in range(2):
        y = y_ref[z]
        mean = _seg_sum(y, seg_ref, segt_ref) * inv_n
        yc = y - mean
        var = _seg_sum(yc * yc, seg_ref, segt_ref) * inv_n
        yn = yc * lax.rsqrt(var + RWKV_GN_EPS) * ln_ref[0:1, :] + ln_ref[1:2, :]
        o = o + g_ref[z] * yn
    o_ref[...] = x_ref[...] + ga_ref[...] * _dot(o.astype(BF16), wo_ref[...])


def _rwkv_out(y, g, gb, x2, gate, ln8, w_o, bsz, tm=256):
    t, d = x2.shape
    s = t // bsz
    tm = min(tm, s)
    ns = s // tm
    seg, segt = _seg_matrices(d, RWKV_HEAD)
    row = lambda b, i: (b * ns + i, 0)
    dir_spec = pl.BlockSpec((2, tm, d), lambda b, i: (0, b * ns + i, 0))
    full = lambda a: pl.BlockSpec(a.shape, lambda b, i, _n=a.ndim: (0,) * _n)
    return pl.pallas_call(
        _rwkv_out_kernel,
        grid=(bsz, ns),
        in_specs=[dir_spec, dir_spec, pl.BlockSpec((tm, d), row), pl.BlockSpec((tm, d), row),
                  pl.BlockSpec((None, 1, d), lambda b, i: (b, 0, 0)),
                  full(ln8), full(w_o), full(seg), full(segt)],
        out_specs=pl.BlockSpec((tm, d), row),
        out_shape=jax.ShapeDtypeStruct((t, d), F32),
        compiler_params=_params(("parallel", "parallel")),
        name="rwkv_out",
    )(y, g, gb, x2, gate, ln8, w_o, seg, segt)


def _rwkv_layer(x2, bsz, hn, gate, p):
    at, rt, bt, kt, v, ed, g, gb = _rwkv_prep(hn, bsz, p, SCAN_DT)
    y = _rwkv_scan(at, rt, bt, kt, v, ed, bsz)
    return _rwkv_out(y, g, gb, x2, gate, p["ln8"], p["w_o"], bsz)


def _attn_group_kernel(slope_ref, vid_ref, q_ref, k_ref, v_ref, o_ref, lse_ref, tab_ref, *,
                       tq, sub, w, half, offs, nh, dh, hg):
    first = (pl.program_id(0) == 0) & (pl.program_id(1) == 0) & (pl.program_id(2) == 0)
    i = pl.program_id(2)

    @pl.when(first)
    def _():
        col = lax.broadcasted_iota(jnp.int32, (tq, w), 1)
        row = lax.broadcasted_iota(jnp.int32, (tq, w), 0)
        for v, off in enumerate(offs):
            dist = jnp.abs(col - row + off)
            distf = dist.astype(F32)
            for h in range(nh):
                tab_ref[h * len(offs) + v] = jnp.where(dist <= half, -slope_ref[h] * distf, NEG_INF)

    if w == sub:
        ws = 0
    else:
        ws = pl.multiple_of(jnp.clip(i * tq - half, 0, sub - w), 16)
    var = vid_ref[i]
    nt = (((1,), (1,)), ((), ()))
    lane = lax.broadcasted_iota(jnp.int32, (tq, 128), 1)
    lse = jnp.zeros((tq, 128), F32)
    for h0 in range(0, nh, hg):
        heads = range(h0, min(h0 + hg, nh))
        hs = {h: slice(h * dh, (h + 1) * dh) for h in heads}
        sc = {h: lax.dot_general(q_ref[:, hs[h]], k_ref[pl.ds(ws, w), hs[h]], nt, preferred_element_type=F32)
              + tab_ref[h * len(offs) + var] for h in heads}
        m = {h: sc[h].max(axis=-1, keepdims=True) for h in heads}
        p = {h: jnp.exp(sc[h] - m[h]) for h in heads}
        den = {h: p[h].sum(axis=-1, keepdims=True) for h in heads}
        acc = {h: _dot(p[h].astype(BF16), v_ref[pl.ds(ws, w), hs[h]]) for h in heads}
        for h in heads:
            o_ref[:, hs[h]] = (acc[h] / den[h]).astype(o_ref.dtype)
            lse = jnp.where(lane == h, m[h] + jnp.log(den[h]), lse)
    lse_ref[...] = lse


def _alibi_slopes(n):
    return 2.0 ** (-8.0 * jnp.arange(1, n + 1, dtype=F32) / n)


def _attention_group(qkv, bsz, gi, tq=256):
    t, width = qkv.shape
    s = t // bsz
    window, dil = ATTN_GROUPS[gi]
    half = window // (2 * dil)
    sub = s // dil
    tq = min(tq, sub)
    nq = sub // tq
    w = min(sub, tq + 2 * half)
    nh, dh = ATTN_HEADS, ATTN_HEAD_DIM
    hw = nh * dh
    offs = sorted({int(np.clip(i * tq - half, 0, sub - w)) - i * tq for i in range(nq)})
    vid = jnp.asarray([offs.index(int(np.clip(i * tq - half, 0, sub - w)) - i * tq) for i in range(nq)], jnp.int32)
    slopes = _alibi_slopes(len(ATTN_GROUPS) * nh).reshape(len(ATTN_GROUPS), nh)[gi] * dil
    smem = pl.BlockSpec(memory_space=pltpu.SMEM)
    qrow = lambda b, r, i: ((b * dil + r) * nq + i, 0)
    return pl.pallas_call(
        functools.partial(_attn_group_kernel, tq=tq, sub=sub, w=w, half=half, offs=tuple(offs), nh=nh, dh=dh,
                          hg=nh if tq * w <= 128 * 128 else 4),
        grid=(bsz, dil, nq),
        in_specs=[smem, smem,
                  pl.BlockSpec((tq, hw), qrow),
                  pl.BlockSpec((sub, hw), lambda b, r, i: (b * dil + r, 1)),
                  pl.BlockSpec((sub, hw), lambda b, r, i: (b * dil + r, 2))],
        out_specs=[pl.BlockSpec((tq, hw), qrow), pl.BlockSpec((tq, 128), qrow)],
        out_shape=[jax.ShapeDtypeStruct((t, hw), BF16), jax.ShapeDtypeStruct((t, 128), F32)],
        scratch_shapes=[pltpu.VMEM((nh * len(offs), tq, w), F32)],
        compiler_params=_params(("arbitrary", "arbitrary", "arbitrary")),
        name=f"dilated_attention_g{gi}",
    )(slopes, vid, qkv, qkv, qkv)


def _to_token_order(blk_ref, scr_ref, dil):
    if dil == 1:
        return blk_ref[0].astype(F32)
    n = blk_ref.shape[1]
    slabs = scr_ref.shape[0]
    for r in range(dil):
        rows = blk_ref[r].astype(F32)
        for c in range(slabs):
            scr_ref[c, pl.ds(r, n, stride=dil), :] = rows[:, c * 128:(c + 1) * 128]
    if slabs == 1:
        return scr_ref[0]
    return jnp.concatenate([scr_ref[c] for c in range(slabs)], axis=1)


def _attn_out_kernel(o1, o2, o3, l1, l2, l3, ex_ref, w_ref, x_ref, ga_ref, out_ref, so_ref, sl_ref, *, dils):
    ls = [_to_token_order(l, sl_ref, dil) for l, dil in zip((l1, l2, l3), dils)]
    m = jnp.maximum(jnp.maximum(ls[0], ls[1]), ls[2])
    es = [jnp.exp(l - m) for l in ls]
    inv = 1.0 / (es[0] + es[1] + es[2])
    merged = None
    for e, o_ref, dil in zip(es, (o1, o2, o3), dils):
        ah, al = _split2(e * inv)
        alpha = _dot(ah, ex_ref[...]) + _dot(al, ex_ref[...])
        term = alpha * _to_token_order(o_ref, so_ref, dil)
        merged = term if merged is None else merged + term
    out_ref[...] = x_ref[...] + ga_ref[...] * _dot(merged.astype(BF16), w_ref[...])


def _attn_out(os, lses, w_o, x2, gate, bsz, tm=512):
    t, d = x2.shape
    s = t // bsz
    tm = min(tm, s)
    ns = s // tm
    hw = os[0].shape[1]
    dils = tuple(dil for _, dil in ATTN_GROUPS)
    ex = np.zeros((128, hw), np.float32)
    ex[np.arange(hw) // ATTN_HEAD_DIM, np.arange(hw)] = 1.0
    ex = jnp.asarray(ex, BF16)
    row = lambda b, i: (b * ns + i, 0)

    def res_major(a, dil):
        c = a.shape[1]
        return a.reshape(bsz, dil, s // dil, c), pl.BlockSpec((None, dil, tm // dil, c), lambda b, i: (b, 0, i, 0))

    o_args, o_specs = zip(*[res_major(o, dil) for o, dil in zip(os, dils)])
    l_args, l_specs = zip(*[res_major(l, dil) for l, dil in zip(lses, dils)])
    return pl.pallas_call(
        functools.partial(_attn_out_kernel, dils=dils),
        grid=(bsz, ns),
        in_specs=list(o_specs) + list(l_specs) + [
            pl.BlockSpec(ex.shape, lambda b, i: (0, 0)),
            pl.BlockSpec(w_o.shape, lambda b, i: (0, 0)),
            pl.BlockSpec((tm, d), row),
            pl.BlockSpec((None, 1, d), lambda b, i: (b, 0, 0))],
        out_specs=pl.BlockSpec((tm, d), row),
        out_shape=jax.ShapeDtypeStruct((t, d), F32),
        scratch_shapes=[pltpu.VMEM((hw // 128, tm, 128), F32), pltpu.VMEM((1, tm, 128), F32)],
        compiler_params=_params(("parallel", "parallel")),
        name="attn_out",
    )(*o_args, *l_args, ex, w_o, x2, gate)


def _norm_attn_kernel(x_ref, g_ref, sh_ref, sc_ref, *rest, dils):
    outs, scr_ref = rest[:-1], rest[-1]
    x = x_ref[...]
    ms = jnp.mean(x * x, axis=-1, keepdims=True)
    y = x * lax.rsqrt(ms + RMS_EPS) * g_ref[...]
    y = y * (1.0 + sc_ref[...]) + sh_ref[...]
    slabs = scr_ref.shape[0]
    if any(dil > 1 for dil in dils):
        for c in range(slabs):
            scr_ref[c] = y[:, c * 128:(c + 1) * 128]
    for o, dil in zip(outs, dils):
        if dil == 1:
            o[0] = y.astype(o.dtype)
        else:
            n = o.shape[1]
            for r in range(dil):
                for c in range(slabs):
                    o[r, :, c * 128:(c + 1) * 128] = scr_ref[c, pl.ds(r, n, stride=dil), :].astype(o.dtype)


def _norm_mod_attn(x2, g, shift, scale, bsz, ts=512):
    t, d = x2.shape
    s = t // bsz
    ts = min(ts, s)
    ns = s // ts
    dils = tuple(dil for _, dil in ATTN_GROUPS)
    vec = pl.BlockSpec((None, 1, d), lambda b, i: (b, 0, 0))
    outs = pl.pallas_call(
        functools.partial(_norm_attn_kernel, dils=dils),
        grid=(bsz, ns),
        in_specs=[pl.BlockSpec((ts, d), lambda b, i: (b * ns + i, 0)),
                  pl.BlockSpec((1, d), lambda b, i: (0, 0)), vec, vec],
        out_specs=[pl.BlockSpec((None, dil, ts // dil, d), lambda b, i: (b, 0, i, 0)) for dil in dils],
        out_shape=[jax.ShapeDtypeStruct((bsz, dil, s // dil, d), BF16) for dil in dils],
        scratch_shapes=[pltpu.VMEM((d // 128, ts, 128), F32)],
        compiler_params=_params(("parallel", "parallel")),
        name="norm_mod_attn",
    )(x2, g.reshape(1, d), shift, scale)
    return [o.reshape(t, d) for o in outs]


def _pack_attn(j, w_qkv, w_o):
    d = w_qkv.shape[1]
    ng = len(ATTN_GROUPS)
    w = w_qkv[j].reshape(d, ng, 3, ATTN_HEADS * ATTN_HEAD_DIM)
    w = w.at[:, :, 0].multiply(ATTN_HEAD_DIM ** -0.5)
    return {"w_qkv": w.transpose(1, 0, 2, 3).reshape(ng, d, -1).astype(BF16), "w_o": w_o[j].astype(BF16)}


def _attn_layer(x2, bsz, hns, gate, p):
    outs = [_attention_group(_matmul(hn, p["w_qkv"][gi], BF16), bsz, gi) for gi, hn in enumerate(hns)]
    return _attn_out([o for o, _ in outs], [l for _, l in outs], p["w_o"], x2, gate, bsz)


def _router_kernel(h_ref, wh_ref, wl_ref, b_ref, o_ref):
    hh, hl = _split2(h_ref[...])
    logits = _dot(hh, wh_ref[...]) + _dot(hl, wh_ref[...]) + _dot(hh, wl_ref[...]) + b_ref[...]
    lane = lax.broadcasted_iota(jnp.int32, logits.shape, 1)
    ng, ne = N_EXPERT_GROUPS, EXPERTS_PER_GROUP
    big = jnp.int32(ROUTER_LANES)
    is_g = lane < ng
    gl = jnp.where(is_g, logits, -jnp.inf)
    gmax = gl.max(axis=-1, keepdims=True)
    gsum = jnp.where(is_g, jnp.exp(logits - gmax), 0.0).sum(axis=-1, keepdims=True)
    p_group = 1.0 / gsum
    g_top = jnp.where(gl == gmax, lane, big).min(axis=-1, keepdims=True)
    lo = ng + ne * g_top
    es = jnp.where((lane >= lo) & (lane < lo + ne), logits, -jnp.inf)
    v1 = es.max(axis=-1, keepdims=True)
    i1 = jnp.where(es == v1, lane, big).min(axis=-1, keepdims=True)
    es2 = jnp.where(lane == i1, -jnp.inf, es)
    v2 = es2.max(axis=-1, keepdims=True)
    i2 = jnp.where(es2 == v2, lane, big).min(axis=-1, keepdims=True)
    e2 = jnp.exp(v2 - v1)
    w1 = p_group / (1.0 + e2)
    w2 = p_group * e2 / (1.0 + e2)
    o_ref[...] = jnp.where(lane == i1, w1, 0.0) + jnp.where(lane == i2, w2, 0.0)


def _router(hn, wr_hi, wr_lo, bias, tm=512):
    t, d = hn.shape
    tm = min(tm, t)
    full = lambda a: pl.BlockSpec(a.shape, lambda i: (0, 0))
    return pl.pallas_call(
        _router_kernel,
        grid=(t // tm,),
        in_specs=[pl.BlockSpec((tm, d), lambda i: (i, 0)), full(wr_hi), full(wr_lo), full(bias)],
        out_specs=pl.BlockSpec((tm, ROUTER_LANES), lambda i: (i, 0)),
        out_shape=jax.ShapeDtypeStruct((t, ROUTER_LANES), F32),
        compiler_params=_params(("parallel",)),
        name="moe_router",
    )(hn, wr_hi, wr_lo, bias)


def _experts_kernel(h_ref, gates_ref, wgu_ref, wd_ref, x_ref, ga_ref, o_ref, *, f, per_step):
    step = pl.program_id(1)
    gates = gates_ref[...]
    lane = lax.broadcasted_iota(jnp.int32, gates.shape, 1)
    h = h_ref[...]
    hids = []
    for j in range(per_step):
        e = step * per_step + j
        ge = jnp.where(lane == N_EXPERT_GROUPS + e, gates, 0.0).sum(axis=-1, keepdims=True)
        gu = _dot(h, wgu_ref[j])
        gp = gu[:, :f]
        hids.append((gp * _sigmoid(gp) * gu[:, f:] * ge).astype(BF16))
    part = _dot(jnp.concatenate(hids, axis=1), wd_ref[...])

    @pl.when(step == 0)
    def _():
        o_ref[...] = part

    @pl.when(step > 0)
    def _():
        o_ref[...] += part

    @pl.when(step == pl.num_programs(1) - 1)
    def _():
        o_ref[...] = x_ref[...] + ga_ref[...] * o_ref[...]


def _experts(hn_bf16, gates, wgu, wd, x2, gate, bsz, tm=1024, per_step=4):
    t, d = x2.shape
    s = t // bsz
    tm = min(tm, s)
    ns = s // tm
    ne, _, f2 = wgu.shape
    f = f2 // 2
    row = lambda i, e: (i, 0)
    return pl.pallas_call(
        functools.partial(_experts_kernel, f=f, per_step=per_step),
        grid=(t // tm, ne // per_step),
        in_specs=[
            pl.BlockSpec((tm, d), row),
            pl.BlockSpec((tm, ROUTER_LANES), row),
            pl.BlockSpec((per_step, d, f2), lambda i, e: (e, 0, 0)),
            pl.BlockSpec((per_step * f, d), lambda i, e: (e, 0)),
            pl.BlockSpec((tm, d), row),
            pl.BlockSpec((None, 1, d), lambda i, e: (i // ns, 0, 0)),
        ],
        out_specs=pl.BlockSpec((tm, d), row),
        out_shape=jax.ShapeDtypeStruct((t, d), F32),
        compiler_params=_params(("parallel", "arbitrary")),
        name="moe_experts",
    )(hn_bf16, gates, wgu, wd.reshape(ne * f, d), x2, gate)


def _moe_layer(x2, bsz, hn, hn_bf16, gate, p):
    gates = _router(hn, p["wr_hi"], p["wr_lo"], p["r_bias"])
    return _experts(hn_bf16, gates, p["wgu"], p["wd"], x2, gate, bsz)


def _pad_rows(rows, d):
    out = jnp.zeros((8, d), F32)
    return out.at[:len(rows)].set(jnp.stack([r.reshape(d).astype(F32) for r in rows]))


def _pack_rwkv(j, mu, w_rkv, w0, w1, w2, a0, a1, a2, g1, g2, k_k, k_a, r_k, ln_w, ln_b, w_o):
    d = w_o.shape[-1]
    cat = lambda m: jnp.concatenate([m[j, 0], m[j, 1]], axis=1).astype(BF16)
    stack = lambda m: jnp.concatenate([m[j, 0], m[j, 1]], axis=0).astype(BF16)
    return {
        "mu8": _pad_rows(list(mu[j]), d),
        "w_rkv": w_rkv[j].astype(BF16),
        "w1c": cat(w1), "a1c": cat(a1), "g1c": cat(g1),
        "w2s": stack(w2), "a2s": stack(a2), "g2": g2[j].astype(BF16),
        "vec8": _pad_rows([w0[j, 0], w0[j, 1], a0[j, 0], a0[j, 1], k_k[j], k_a[j], r_k[j]], d),
        "ln8": _pad_rows([ln_w[j], ln_b[j]], d),
        "w_o": w_o[j].astype(BF16),
    }


def _pack_moe(i, router_g, router_g_b, router_e, router_e_b, w_gate, w_up, w_down):
    d = router_g.shape[1]
    ng, ne = N_EXPERT_GROUPS, EXPERTS_PER_GROUP
    wr = jnp.zeros((d, ROUTER_LANES), F32)
    wr = wr.at[:, :ng].set(router_g[i])
    wr = wr.at[:, ng:ng + ng * ne].set(router_e[i].transpose(1, 0, 2).reshape(d, ng * ne))
    bias = jnp.zeros((1, ROUTER_LANES), F32)
    bias = bias.at[0, :ng].set(router_g_b[i])
    bias = bias.at[0, ng:ng + ng * ne].set(router_e_b[i].reshape(ng * ne))
    wr_hi = wr.astype(BF16)
    wr_lo = (wr - wr_hi.astype(F32)).astype(BF16)
    f = w_gate.shape[-1]
    wgu = jnp.concatenate([w_gate[i], w_up[i]], axis=-1).reshape(ng * ne, d, 2 * f).astype(BF16)
    wd = w_down[i].reshape(ng * ne, f, d).astype(BF16)
    return {"wr_hi": wr_hi, "wr_lo": wr_lo, "r_bias": bias, "wgu": wgu, "wd": wd}


def kernel(x, c, ada_w, ada_b, norm_tm_g, norm_cm_g, rw_mu, rw_w_rkv, rw_w0, rw_w1, rw_w2, rw_a0, rw_a1, rw_a2, rw_g1, rw_g2, rw_k_k, rw_k_a, rw_r_k, rw_ln_w, rw_ln_b, rw_w_o, at_w_qkv, at_w_o, moe_router_g, moe_router_g_b, moe_router_e, moe_router_e_b, moe_w_gate, moe_w_up, moe_w_down, final_g):
    bsz, s, d = x.shape
    depth = ada_w.shape[0]
    x2 = x.reshape(bsz * s, d)
    mod = _ada_mod(c, ada_w, ada_b)
    for i in range(depth):
        sh_t, sc_t, ga_t, sh_c, sc_c, ga_c = (mod[i, m] for m in range(N_MOD))
        j = i // 2
        if i % 2 == 0:
            (hn,) = _norm_mod(x2, norm_tm_g[i], sh_t, sc_t, bsz, (F32,))
            p = _pack_rwkv(j, rw_mu, rw_w_rkv, rw_w0, rw_w1, rw_w2, rw_a0, rw_a1, rw_a2, rw_g1, rw_g2,
                           rw_k_k, rw_k_a, rw_r_k, rw_ln_w, rw_ln_b, rw_w_o)
            x2 = _rwkv_layer(x2, bsz, hn, ga_t, p)
        else:
            hns = _norm_mod_attn(x2, norm_tm_g[i], sh_t, sc_t, bsz)
            p = _pack_attn(j, at_w_qkv, at_w_o)
            x2 = _attn_layer(x2, bsz, hns, ga_t, p)
        hn, hn_b = _norm_mod(x2, norm_cm_g[i], sh_c, sc_c, bsz, (F32, BF16))
        pm = _pack_moe(i, moe_router_g, moe_router_g_b, moe_router_e, moe_router_e_b,
                       moe_w_gate, moe_w_up, moe_w_down)
        x2 = _moe_layer(x2, bsz, hn, hn_b, ga_c, pm)
    (out,) = _norm_mod(x2, final_g, None, None, bsz, (F32,))
    return out.reshape(bsz, s, d)
```

```python
import functools
import math

import numpy as np
import jax
import jax.numpy as jnp
from jax import lax
from jax.experimental import pallas as pl
from jax.experimental.pallas import tpu as pltpu

F32 = jnp.float32
BF16 = jnp.bfloat16

RMS_EPS = 1e-6
N_MOD = 6
RWKV_HEAD = 64
RWKV_GN_EPS = 64e-5
ATTN_GROUPS = ((128, 1), (512, 4), (2048, 16))
ATTN_HEADS = 8
ATTN_HEAD_DIM = 128
NEG_INF = -1e30
N_EXPERT_GROUPS = 4
EXPERTS_PER_GROUP = 8
N_EXPERTS = N_EXPERT_GROUPS * EXPERTS_PER_GROUP
ROUTER_LANES = 128

SCAN_CHUNK = 64
V7X_VMEM_LIMIT = 56 * 1024 * 1024
SCAN_DT = BF16


def _params(sem):
    return pltpu.CompilerParams(dimension_semantics=sem, vmem_limit_bytes=V7X_VMEM_LIMIT)


def _dot(a, b):
    return jnp.dot(a, b, preferred_element_type=F32)


def _sigmoid(x):
    return 1.0 / (1.0 + jnp.exp(-x))


def _split2(x):
    hi = x.astype(BF16)
    lo = (x - hi.astype(F32)).astype(BF16)
    return hi, lo


def _ada_kernel(c_ref, w_ref, b_ref, o_ref):
    c = c_ref[...]
    sc = c * _sigmoid(c)
    o_ref[...] = _dot(sc.astype(BF16), w_ref[...].astype(BF16)) + b_ref[...]


def _ada_mod(c, ada_w, ada_b):
    depth, d, n = ada_w.shape
    bsz = c.shape[0]
    tn = n // 4
    out = pl.pallas_call(
        _ada_kernel,
        grid=(depth, n // tn),
        in_specs=[
            pl.BlockSpec((bsz, d), lambda i, j: (0, 0)),
            pl.BlockSpec((None, d, tn), lambda i, j: (i, 0, j)),
            pl.BlockSpec((None, 1, tn), lambda i, j: (i, 0, j)),
        ],
        out_specs=pl.BlockSpec((None, bsz, tn), lambda i, j: (i, 0, j)),
        out_shape=jax.ShapeDtypeStruct((depth, bsz, n), F32),
        compiler_params=_params(("parallel", "parallel")),
        name="ada_mod",
    )(c, ada_w, ada_b.reshape(depth, 1, n))
    return out.reshape(depth, bsz, N_MOD, 1, d).transpose(0, 2, 1, 3, 4)


def _norm_kernel(*refs, modulated, n_out):
    if modulated:
        x_ref, g_ref, sh_ref, sc_ref = refs[:4]
    else:
        x_ref, g_ref = refs[:2]
    outs = refs[-n_out:]
    x = x_ref[...]
    ms = jnp.mean(x * x, axis=-1, keepdims=True)
    y = x * lax.rsqrt(ms + RMS_EPS) * g_ref[...]
    if modulated:
        y = y * (1.0 + sc_ref[...]) + sh_ref[...]
    for o in outs:
        o[...] = y.astype(o.dtype)


def _norm_mod(x2, g, shift, scale, bsz, out_dtypes, ts=512):
    t, d = x2.shape
    s = t // bsz
    ts = min(ts, s)
    ns = s // ts
    modulated = shift is not None
    row_spec = pl.BlockSpec((ts, d), lambda b, i: (b * ns + i, 0))
    in_specs = [row_spec, pl.BlockSpec((1, d), lambda b, i: (0, 0))]
    args = [x2, g.reshape(1, d)]
    if modulated:
        vec = pl.BlockSpec((None, 1, d), lambda b, i: (b, 0, 0))
        in_specs += [vec, vec]
        args += [shift, scale]
    outs = pl.pallas_call(
        functools.partial(_norm_kernel, modulated=modulated, n_out=len(out_dtypes)),
        grid=(bsz, ns),
        in_specs=in_specs,
        out_specs=[row_spec] * len(out_dtypes),
        out_shape=[jax.ShapeDtypeStruct((t, d), dt) for dt in out_dtypes],
        compiler_params=_params(("parallel", "parallel")),
        name="norm_mod",
    )(*args)
    return outs


def _proj_res_kernel(a_ref, w_ref, x_ref, ga_ref, o_ref):
    o_ref[...] = x_ref[...] + ga_ref[...] * _dot(a_ref[...], w_ref[...])


def _proj_residual(a, w, x2, gate, bsz, tm=512):
    t, d = x2.shape
    s = t // bsz
    tm = min(tm, s)
    ns = s // tm
    row = lambda b, i: (b * ns + i, 0)
    return pl.pallas_call(
        _proj_res_kernel,
        grid=(bsz, ns),
        in_specs=[
            pl.BlockSpec((tm, a.shape[1]), row),
            pl.BlockSpec(w.shape, lambda b, i: (0, 0)),
            pl.BlockSpec((tm, d), row),
            pl.BlockSpec((None, 1, d), lambda b, i: (b, 0, 0)),
        ],
        out_specs=pl.BlockSpec((tm, d), row),
        out_shape=jax.ShapeDtypeStruct((t, d), F32),
        compiler_params=_params(("parallel", "parallel")),
        name="proj_residual",
    )(a, w, x2, gate)


def _mm_kernel(a_ref, b_ref, o_ref):
    o_ref[...] = _dot(a_ref[...], b_ref[...]).astype(o_ref.dtype)


def _matmul(a, b, out_dtype, tm=1024, tn=1024):
    m, k = a.shape
    n = b.shape[1]
    tm, tn = min(tm, m), min(tn, n)
    return pl.pallas_call(
        _mm_kernel,
        grid=(n // tn, m // tm),
        in_specs=[
            pl.BlockSpec((tm, k), lambda j, i: (i, 0)),
            pl.BlockSpec((k, tn), lambda j, i: (0, j)),
        ],
        out_specs=pl.BlockSpec((tm, tn), lambda j, i: (i, j)),
        out_shape=jax.ShapeDtypeStruct((m, n), out_dtype),
        compiler_params=_params(("parallel", "parallel")),
        name="matmul",
    )(a, b)


def _seg_sum(x, seg_ref, segt_ref):
    xh, xl = _split2(x)
    s = _dot(xh, seg_ref[...]) + _dot(xl, seg_ref[...])
    sh, sl = _split2(s)
    return _dot(sh, segt_ref[...]) + _dot(sl, segt_ref[...])


def _rwkv_prep_kernel(hn_ref, prev_ref, next_ref, mu_ref, wrkv_ref, w1_ref, a1_ref, g1_ref,
                      w2_ref, a2_ref, g2_ref, vec_ref, cm_ref, edm_ref, seg_ref, segt_ref,
                      at_ref, rt_ref, bt_ref, kt_ref, v_ref, ed_ref, g_ref, gb_ref, *, ts, ns):
    i = pl.program_id(1)
    cur = hn_ref[...]
    row = lax.broadcasted_iota(jnp.int32, (ts, 1), 0)
    prev_row = jnp.where(i > 0, prev_ref[7:8, :], 0.0)
    next_row = jnp.where(i < ns - 1, next_ref[0:1, :], 0.0)
    x_prev = jnp.where(row == 0, prev_row, pltpu.roll(cur, 1, 0))
    x_next = jnp.where(row == ts - 1, next_row, pltpu.roll(cur, ts - 1, 0))
    xx = 0.5 * (x_prev + x_next) - cur

    def mix(j):
        return (cur + xx * mu_ref[j:j + 1, :]).astype(BF16)

    r = _dot(mix(0), wrkv_ref[0])
    k = _dot(mix(1), wrkv_ref[1])
    v = _dot(mix(2), wrkv_ref[2])
    tw = jnp.tanh(_dot(mix(3), w1_ref[...]))
    ta = _dot(mix(4), a1_ref[...])
    tg = _sigmoid(_dot(mix(5), g1_ref[...]))
    lane = lax.broadcasted_iota(jnp.int32, (1, tw.shape[1]), 1)
    half = tw.shape[1] // 2
    hg = tg.shape[1] // 2

    k_k = vec_ref[4:5, :]
    k_a = vec_ref[5:6, :]
    r_k = vec_ref[6:7, :]
    kk = k * k_k
    ss = _seg_sum(kk * kk, seg_ref, segt_ref)
    kkn = kk / jnp.maximum(jnp.sqrt(ss), 1e-12)
    v_ref[...] = v.astype(v_ref.dtype)

    gb = jnp.zeros_like(cur)
    for z in range(2):
        sel = (lane < half) if z == 0 else (lane >= half)
        lw = _dot(jnp.where(sel, tw, 0.0).astype(BF16), w2_ref[...])
        la = _dot(jnp.where(sel, ta, 0.0).astype(BF16), a2_ref[...])
        g = _dot(tg[:, z * hg:(z + 1) * hg].astype(BF16), g2_ref[z])
        ld = -math.exp(-0.5) * _sigmoid(vec_ref[z:z + 1, :] + lw)
        a = _sigmoid(vec_ref[2 + z:3 + z, :] + la)
        kdir = k * (1.0 + (a - 1.0) * k_a)
        b = kkn * a
        parts = _split2(ld)
        cmz = cm_ref[z]
        lm = _dot(cmz, parts[0]) + _dot(cmz, parts[1])
        e_in = jnp.exp(lm)
        e_inv = jnp.exp(-lm)
        e_ex = jnp.exp(lm - ld)
        rt_ref[z] = (r * e_in).astype(rt_ref.dtype)
        at_ref[z] = (-kkn * e_ex).astype(at_ref.dtype)
        bt_ref[z] = (b * e_inv).astype(bt_ref.dtype)
        kt_ref[z] = (kdir * e_inv).astype(kt_ref.dtype)
        edz = edm_ref[z]
        ed = jnp.exp(_dot(edz, parts[0]) + _dot(edz, parts[1]))
        for ck in range(ed.shape[0] // 8):
            ed_ref[z, ck] = ed[ck * 8:(ck + 1) * 8]
        g_ref[z] = g.astype(g_ref.dtype)
        bonus = _seg_sum(r * kdir * r_k, seg_ref, segt_ref) * v
        gb = gb + g * bonus
    gb_ref[...] = gb


def _chunk_matrices(ts, c):
    t = np.arange(ts)
    same = (t[:, None] // c) == (t[None, :] // c)
    pos = t % c
    cm = np.zeros((2, ts, ts), np.float32)
    nck = ts // c
    edm = np.zeros((2, 8 * nck, ts), np.float32)
    tri_f = same & (t[None, :] <= t[:, None])
    sel_f = same & (pos[None, :] <= c // 2 - 1)
    cm[0] = tri_f.astype(np.float32) - sel_f.astype(np.float32)
    tri_b = same & (t[None, :] >= t[:, None])
    sel_b = same & (pos[None, :] >= c // 2)
    cm[1] = tri_b.astype(np.float32) - sel_b.astype(np.float32)
    for ck in range(nck):
        inck = (t // c) == ck
        edm[0, 8 * ck] = inck & (pos <= c // 2 - 1)
        edm[0, 8 * ck + 1] = inck & (pos > c // 2 - 1)
        edm[1, 8 * ck] = inck & (pos >= c // 2)
        edm[1, 8 * ck + 1] = inck & (pos < c // 2)
    return jnp.asarray(cm, BF16), jnp.asarray(edm, BF16)


def _seg_matrices(d, head):
    seg = np.zeros((d, 128), np.float32)
    seg[np.arange(d), np.arange(d) // head] = 1.0
    return jnp.asarray(seg, BF16), jnp.asarray(seg.T.copy(), BF16)


def _rwkv_prep(hn, bsz, p, scan_dtype, ts=256):
    t, d = hn.shape
    s = t // bsz
    ts = min(ts, s)
    ns = s // ts
    c = SCAN_CHUNK
    assert ts % c == 0 and ts % 8 == 0
    nck = ts // c
    cm, edm = _chunk_matrices(ts, c)
    seg, segt = _seg_matrices(d, RWKV_HEAD)
    r8 = ts // 8
    nblk8 = t // 8
    row = lambda b, i: (b * ns + i, 0)
    full = lambda a: pl.BlockSpec(a.shape, lambda b, i, _n=a.ndim: (0,) * _n)
    weights = [p["mu8"], p["w_rkv"], p["w1c"], p["a1c"], p["g1c"], p["w2s"], p["a2s"], p["g2"],
               p["vec8"], cm, edm, seg, segt]
    dir_spec = pl.BlockSpec((2, ts, d), lambda b, i: (0, b * ns + i, 0))
    outs = pl.pallas_call(
        functools.partial(_rwkv_prep_kernel, ts=ts, ns=ns),
        grid=(bsz, ns),
        in_specs=[
            pl.BlockSpec((ts, d), row),
            pl.BlockSpec((8, d), lambda b, i: (jnp.maximum((b * ns + i) * r8 - 1, 0), 0)),
            pl.BlockSpec((8, d), lambda b, i: (jnp.minimum((b * ns + i + 1) * r8, nblk8 - 1), 0)),
        ] + [full(a) for a in weights],
        out_specs=[dir_spec, dir_spec, dir_spec, dir_spec,
                   pl.BlockSpec((ts, d), row),
                   pl.BlockSpec((2, nck, 8, d), lambda b, i: (0, b * ns + i, 0, 0)),
                   dir_spec,
                   pl.BlockSpec((ts, d), row)],
        out_shape=[jax.ShapeDtypeStruct((2, t, d), scan_dtype)] * 4 + [
            jax.ShapeDtypeStruct((t, d), scan_dtype),
            jax.ShapeDtypeStruct((2, t // c, 8, d), F32),
            jax.ShapeDtypeStruct((2, t, d), F32),
            jax.ShapeDtypeStruct((t, d), F32)],
        compiler_params=_params(("parallel", "parallel")),
        name="rwkv_prep",
    )(hn, hn, hn, *weights)
    return outs


def _sdot(a, b, dims):
    return lax.dot_general(a.astype(SCAN_DT), b.astype(SCAN_DT), (dims, ((), ())),
                           preferred_element_type=F32)


def _scan_kernel(at_ref, rt_ref, bt_ref, kt_ref, v_ref, ed_ref, y_ref, s_ref, *, c, n_heads):
    z = pl.program_id(1)
    ci = pl.program_id(2)

    @pl.when(ci == 0)
    def _():
        s_ref[...] = jnp.zeros_like(s_ref)

    c2 = 2 * c
    pw_lanes = 2 * RWKV_HEAD
    row = lax.broadcasted_iota(jnp.int32, (c2, c2), 0)
    col = lax.broadcasted_iota(jnp.int32, (c2, c2), 1)
    same_blk = (row // c) == (col // c)
    diff = jnp.where(z == 0, row - col, col - row)
    strict = same_blk & (diff > 0)
    incl = same_blk & (diff >= 0)
    incl2 = jnp.concatenate([incl, incl], axis=1)
    lane_a = lax.broadcasted_iota(jnp.int32, (c, pw_lanes), 1) < RWKV_HEAD
    same_head = ((lax.broadcasted_iota(jnp.int32, (pw_lanes, pw_lanes), 0) // RWKV_HEAD)
                 == (lax.broadcasted_iota(jnp.int32, (pw_lanes, pw_lanes), 1) // RWKV_HEAD))
    n_steps = int(math.log2(c))
    nt = (((1,), (1,)))
    nn = (((1,), (0,)))
    tn = (((0,), (0,)))
    pairs = range(n_heads // 2)
    sls = [pl.ds(hp * pw_lanes, pw_lanes) for hp in pairs]
    zero = jnp.zeros((c, pw_lanes), SCAN_DT)
    lhs, rhs, vv, st = [], [], [], []
    for hp in pairs:
        a2 = at_ref[:, sls[hp]]
        r2 = rt_ref[:, sls[hp]]
        b2 = bt_ref[:, sls[hp]]
        k2 = kt_ref[:, sls[hp]]
        v2 = v_ref[:, sls[hp]]
        lhs.append(jnp.concatenate([jnp.where(lane_a, a2, zero), jnp.where(lane_a, zero, a2),
                                    jnp.where(lane_a, r2, zero), jnp.where(lane_a, zero, r2)], axis=0))
        rhs.append(jnp.concatenate([b2, b2, k2, k2], axis=0))
        vv.append(jnp.concatenate([v2, v2], axis=0))
        st.append(s_ref[hp] * ed_ref[0:1, sls[hp]])
    gq = [_sdot(lhs[hp], jnp.concatenate([rhs[hp], st[hp].astype(SCAN_DT)], axis=0), nt) for hp in pairs]
    gm = [g[:, :2 * c2] for g in gq]
    q = [g[:, 2 * c2:] for g in gq]
    pw = [jnp.where(strict, gm[hp][:c2, :c2], 0.0).astype(SCAN_DT) for hp in pairs]
    m2 = [jnp.where(strict, gm[hp][:c2, c2:], 0.0).astype(SCAN_DT) for hp in pairs]
    m34 = [jnp.where(incl2, gm[hp][c2:, :], 0.0).astype(SCAN_DT) for hp in pairs]
    x = [q[hp][:c2] + _sdot(m2[hp], vv[hp], nn) for hp in pairs]
    for step in range(n_steps - 1):
        both = [_sdot(pw[hp], jnp.concatenate([x[hp].astype(SCAN_DT), pw[hp]], axis=1), nn) for hp in pairs]
        x = [x[hp] + both[hp][:, :pw_lanes] for hp in pairs]
        pw = [both[hp][:, pw_lanes:].astype(SCAN_DT) for hp in pairs]
    x = [x[hp] + _sdot(pw[hp], x[hp], nn) for hp in pairs]
    for hp in pairs:
        xs = x[hp].astype(SCAN_DT)
        uvs = jnp.concatenate([xs, vv[hp]], axis=0)
        ys = q[hp][c2:] + _sdot(m34[hp], uvs, nn)
        y_ref[:, sls[hp]] = jnp.where(lane_a, ys[:c], ys[c:])
        u2 = jnp.where(lane_a, xs[:c], xs[c:])
        upd = _sdot(jnp.concatenate([u2, vv[hp][:c]], axis=0), rhs[hp][c:3 * c], tn)
        s_ref[hp] = (st[hp] + jnp.where(same_head, upd, 0.0)) * ed_ref[1:2, sls[hp]]


def _rwkv_scan(at, rt, bt, kt, v, ed, bsz):
    _, t, d = at.shape
    s = t // bsz
    c = SCAN_CHUNK
    nc = s // c
    n_heads = d // RWKV_HEAD

    def cidx(b, z, ci):
        return b * nc + ci + z * (nc - 1 - 2 * ci)

    dspec = pl.BlockSpec((None, c, d), lambda b, z, ci: (z, cidx(b, z, ci), 0))
    return pl.pallas_call(
        functools.partial(_scan_kernel, c=c, n_heads=n_heads),
        grid=(bsz, 2, nc),
        in_specs=[dspec, dspec, dspec, dspec,
                  pl.BlockSpec((c, d), lambda b, z, ci: (cidx(b, z, ci), 0)),
                  pl.BlockSpec((None, None, 8, d), lambda b, z, ci: (z, cidx(b, z, ci), 0, 0))],
        out_specs=dspec,
        out_shape=jax.ShapeDtypeStruct((2, t, d), F32),
        scratch_shapes=[pltpu.VMEM((n_heads // 2, 2 * RWKV_HEAD, 2 * RWKV_HEAD), F32)],
        compiler_params=_params(("parallel", "parallel", "arbitrary")),
        name="rwkv_scan",
    )(at, rt, bt, kt, v, ed)


def _rwkv_out_kernel(y_ref, g_ref, gb_ref, x_ref, ga_ref, ln_ref, wo_ref, seg_ref, segt_ref, o_ref):
    o = gb_ref[...]
    inv_n = 1.0 / RWKV_HEAD
    for z in range(2):
        y = y_ref[z]
        mean = _seg_sum(y, seg_ref, segt_ref) * inv_n
        yc = y - mean
        var = _seg_sum(yc * yc, seg_ref, segt_ref) * inv_n
        yn = yc * lax.rsqrt(var + RWKV_GN_EPS) * ln_ref[0:1, :] + ln_ref[1:2, :]
        o = o + g_ref[z] * yn
    o_ref[...] = x_ref[...] + ga_ref[...] * _dot(o.astype(BF16), wo_ref[...])


def _rwkv_out(y, g, gb, x2, gate, ln8, w_o, bsz, tm=256):
    t, d = x2.shape
    s = t // bsz
    tm = min(tm, s)
    ns = s // tm
    seg, segt = _seg_matrices(d, RWKV_HEAD)
    row = lambda b, i: (b * ns + i, 0)
    dir_spec = pl.BlockSpec((2, tm, d), lambda b, i: (0, b * ns + i, 0))
    full = lambda a: pl.BlockSpec(a.shape, lambda b, i, _n=a.ndim: (0,) * _n)
    return pl.pallas_call(
        _rwkv_out_kernel,
        grid=(bsz, ns),
        in_specs=[dir_spec, dir_spec, pl.BlockSpec((tm, d), row), pl.BlockSpec((tm, d), row),
                  pl.BlockSpec((None, 1, d), lambda b, i: (b, 0, 0)),
                  full(ln8), full(w_o), full(seg), full(segt)],
        out_specs=pl.BlockSpec((tm, d), row),
        out_shape=jax.ShapeDtypeStruct((t, d), F32),
        compiler_params=_params(("parallel", "parallel")),
        name="rwkv_out",
    )(y, g, gb, x2, gate, ln8, w_o, seg, segt)


def _rwkv_layer(x2, bsz, hn, gate, p):
    at, rt, bt, kt, v, ed, g, gb = _rwkv_prep(hn, bsz, p, SCAN_DT)
    y = _rwkv_scan(at, rt, bt, kt, v, ed, bsz)
    return _rwkv_out(y, g, gb, x2, gate, p["ln8"], p["w_o"], bsz)


def _attn_group_kernel(slope_ref, vid_ref, q_ref, k_ref, v_ref, o_ref, lse_ref, tab_ref, *,
                       tq, sub, w, half, offs, nh, dh, hg):
    first = (pl.program_id(0) == 0) & (pl.program_id(1) == 0) & (pl.program_id(2) == 0)
    i = pl.program_id(2)

    @pl.when(first)
    def _():
        col = lax.broadcasted_iota(jnp.int32, (tq, w), 1)
        row = lax.broadcasted_iota(jnp.int32, (tq, w), 0)
        for v, off in enumerate(offs):
            dist = jnp.abs(col - row + off)
            distf = dist.astype(F32)
            for h in range(nh):
                tab_ref[h * len(offs) + v] = jnp.where(dist <= half, -slope_ref[h] * distf, NEG_INF)

    if w == sub:
        ws = 0
    else:
        ws = pl.multiple_of(jnp.clip(i * tq - half, 0, sub - w), 16)
    var = vid_ref[i]
    nt = (((1,), (1,)), ((), ()))
    lane = lax.broadcasted_iota(jnp.int32, (tq, 128), 1)
    lse = jnp.zeros((tq, 128), F32)
    for h0 in range(0, nh, hg):
        heads = range(h0, min(h0 + hg, nh))
        hs = {h: slice(h * dh, (h + 1) * dh) for h in heads}
        sc = {h: lax.dot_general(q_ref[:, hs[h]], k_ref[pl.ds(ws, w), hs[h]], nt, preferred_element_type=F32)
              + tab_ref[h * len(offs) + var] for h in heads}
        m = {h: sc[h].max(axis=-1, keepdims=True) for h in heads}
        p = {h: jnp.exp(sc[h] - m[h]) for h in heads}
        den = {h: p[h].sum(axis=-1, keepdims=True) for h in heads}
        acc = {h: _dot(p[h].astype(BF16), v_ref[pl.ds(ws, w), hs[h]]) for h in heads}
        for h in heads:
            o_ref[:, hs[h]] = (acc[h] / den[h]).astype(o_ref.dtype)
            lse = jnp.where(lane == h, m[h] + jnp.log(den[h]), lse)
    lse_ref[...] = lse


def _alibi_slopes(n):
    return 2.0 ** (-8.0 * jnp.arange(1, n + 1, dtype=F32) / n)


def _attention_group(qkv, bsz, gi, tq=256):
    t, width = qkv.shape
    s = t // bsz
    window, dil = ATTN_GROUPS[gi]
    half = window // (2 * dil)
    sub = s // dil
    tq = min(tq, sub)
    nq = sub // tq
    w = min(sub, tq + 2 * half)
    nh, dh = ATTN_HEADS, ATTN_HEAD_DIM
    hw = nh * dh
    offs = sorted({int(np.clip(i * tq - half, 0, sub - w)) - i * tq for i in range(nq)})
    vid = jnp.asarray([offs.index(int(np.clip(i * tq - half, 0, sub - w)) - i * tq) for i in range(nq)], jnp.int32)
    slopes = _alibi_slopes(len(ATTN_GROUPS) * nh).reshape(len(ATTN_GROUPS), nh)[gi] * dil
    smem = pl.BlockSpec(memory_space=pltpu.SMEM)
    qrow = lambda b, r, i: ((b * dil + r) * nq + i, 0)
    return pl.pallas_call(
        functools.partial(_attn_group_kernel, tq=tq, sub=sub, w=w, half=half, offs=tuple(offs), nh=nh, dh=dh,
                          hg=nh if tq * w <= 128 * 128 else 4),
        grid=(bsz, dil, nq),
        in_specs=[smem, smem,
                  pl.BlockSpec((tq, hw), qrow),
                  pl.BlockSpec((sub, hw), lambda b, r, i: (b * dil + r, 1)),
                  pl.BlockSpec((sub, hw), lambda b, r, i: (b * dil + r, 2))],
        out_specs=[pl.BlockSpec((tq, hw), qrow), pl.BlockSpec((tq, 128), qrow)],
        out_shape=[jax.ShapeDtypeStruct((t, hw), BF16), jax.ShapeDtypeStruct((t, 128), F32)],
        scratch_shapes=[pltpu.VMEM((nh * len(offs), tq, w), F32)],
        compiler_params=_params(("arbitrary", "arbitrary", "arbitrary")),
        name=f"dilated_attention_g{gi}",
    )(slopes, vid, qkv, qkv, qkv)


def _to_token_order(blk_ref, scr_ref, dil):
    if dil == 1:
        return blk_ref[0].astype(F32)
    n = blk_ref.shape[1]
    slabs = scr_ref.shape[0]
    for r in range(dil):
        rows = blk_ref[r].astype(F32)
        for c in range(slabs):
            scr_ref[c, pl.ds(r, n, stride=dil), :] = rows[:, c * 128:(c + 1) * 128]
    if slabs == 1:
        return scr_ref[0]
    return jnp.concatenate([scr_ref[c] for c in range(slabs)], axis=1)


def _attn_out_kernel(o1, o2, o3, l1, l2, l3, ex_ref, w_ref, x_ref, ga_ref, out_ref, so_ref, sl_ref, *, dils):
    ls = [_to_token_order(l, sl_ref, dil) for l, dil in zip((l1, l2, l3), dils)]
    m = jnp.maximum(jnp.maximum(ls[0], ls[1]), ls[2])
    es = [jnp.exp(l - m) for l in ls]
    inv = 1.0 / (es[0] + es[1] + es[2])
    merged = None
    for e, o_ref, dil in zip(es, (o1, o2, o3), dils):
        ah, al = _split2(e * inv)
        alpha = _dot(ah, ex_ref[...]) + _dot(al, ex_ref[...])
        term = alpha * _to_token_order(o_ref, so_ref, dil)
        merged = term if merged is None else merged + term
    out_ref[...] = x_ref[...] + ga_ref[...] * _dot(merged.astype(BF16), w_ref[...])


def _attn_out(os, lses, w_o, x2, gate, bsz, tm=512):
    t, d = x2.shape
    s = t // bsz
    tm = min(tm, s)
    ns = s // tm
    hw = os[0].shape[1]
    dils = tuple(dil for _, dil in ATTN_GROUPS)
    ex = np.zeros((128, hw), np.float32)
    ex[np.arange(hw) // ATTN_HEAD_DIM, np.arange(hw)] = 1.0
    ex = jnp.asarray(ex, BF16)
    row = lambda b, i: (b * ns + i, 0)

    def res_major(a, dil):
        c = a.shape[1]
        return a.reshape(bsz, dil, s // dil, c), pl.BlockSpec((None, dil, tm // dil, c), lambda b, i: (b, 0, i, 0))

    o_args, o_specs = zip(*[res_major(o, dil) for o, dil in zip(os, dils)])
    l_args, l_specs = zip(*[res_major(l, dil) for l, dil in zip(lses, dils)])
    return pl.pallas_call(
        functools.partial(_attn_out_kernel, dils=dils),
        grid=(bsz, ns),
        in_specs=list(o_specs) + list(l_specs) + [
            pl.BlockSpec(ex.shape, lambda b, i: (0, 0)),
            pl.BlockSpec(w_o.shape, lambda b, i: (0, 0)),
            pl.BlockSpec((tm, d), row),
            pl.BlockSpec((None, 1, d), lambda b, i: (b, 0, 0))],
        out_specs=pl.BlockSpec((tm, d), row),
        out_shape=jax.ShapeDtypeStruct((t, d), F32),
        scratch_shapes=[pltpu.VMEM((hw // 128, tm, 128), F32), pltpu.VMEM((1, tm, 128), F32)],
        compiler_params=_params(("parallel", "parallel")),
        name="attn_out",
    )(*o_args, *l_args, ex, w_o, x2, gate)


def _norm_attn_kernel(x_ref, g_ref, sh_ref, sc_ref, *rest, dils):
    outs, scr_ref = rest[:-1], rest[-1]
    x = x_ref[...]
    ms = jnp.mean(x * x, axis=-1, keepdims=True)
    y = x * lax.rsqrt(ms + RMS_EPS) * g_ref[...]
    y = y * (1.0 + sc_ref[...]) + sh_ref[...]
    slabs = scr_ref.shape[0]
    if any(dil > 1 for dil in dils):
        for c in range(slabs):
            scr_ref[c] = y[:, c * 128:(c + 1) * 128]
    for o, dil in zip(outs, dils):
        if dil == 1:
            o[0] = y.astype(o.dtype)
        else:
            n = o.shape[1]
            for r in range(dil):
                for c in range(slabs):
                    o[r, :, c * 128:(c + 1) * 128] = scr_ref[c, pl.ds(r, n, stride=dil), :].astype(o.dtype)


def _norm_mod_attn(x2, g, shift, scale, bsz, ts=512):
    t, d = x2.shape
    s = t // bsz
    ts = min(ts, s)
    ns = s // ts
    dils = tuple(dil for _, dil in ATTN_GROUPS)
    vec = pl.BlockSpec((None, 1, d), lambda b, i: (b, 0, 0))
    outs = pl.pallas_call(
        functools.partial(_norm_attn_kernel, dils=dils),
        grid=(bsz, ns),
        in_specs=[pl.BlockSpec((ts, d), lambda b, i: (b * ns + i, 0)),
                  pl.BlockSpec((1, d), lambda b, i: (0, 0)), vec, vec],
        out_specs=[pl.BlockSpec((None, dil, ts // dil, d), lambda b, i: (b, 0, i, 0)) for dil in dils],
        out_shape=[jax.ShapeDtypeStruct((bsz, dil, s // dil, d), BF16) for dil in dils],
        scratch_shapes=[pltpu.VMEM((d // 128, ts, 128), F32)],
        compiler_params=_params(("parallel", "parallel")),
        name="norm_mod_attn",
    )(x2, g.reshape(1, d), shift, scale)
    return [o.reshape(t, d) for o in outs]


def _pack_attn(j, w_qkv, w_o):
    d = w_qkv.shape[1]
    ng = len(ATTN_GROUPS)
    w = w_qkv[j].reshape(d, ng, 3, ATTN_HEADS * ATTN_HEAD_DIM)
    w = w.at[:, :, 0].multiply(ATTN_HEAD_DIM ** -0.5)
    return {"w_qkv": w.transpose(1, 0, 2, 3).reshape(ng, d, -1).astype(BF16), "w_o": w_o[j].astype(BF16)}


def _attn_layer(x2, bsz, hns, gate, p):
    outs = [_attention_group(_matmul(hn, p["w_qkv"][gi], BF16), bsz, gi) for gi, hn in enumerate(hns)]
    return _attn_out([o for o, _ in outs], [l for _, l in outs], p["w_o"], x2, gate, bsz)


def _router_kernel(x_ref, g_ref, sh_ref, sc_ref, wh_ref, wl_ref, b_ref, o_ref, hb_ref):
    x = x_ref[...]
    ms = jnp.mean(x * x, axis=-1, keepdims=True)
    hn = x * lax.rsqrt(ms + RMS_EPS) * g_ref[...]
    hn = hn * (1.0 + sc_ref[...]) + sh_ref[...]
    hb_ref[...] = hn.astype(hb_ref.dtype)
    hh, hl = _split2(hn)
    logits = _dot(hh, wh_ref[...]) + _dot(hl, wh_ref[...]) + _dot(hh, wl_ref[...]) + b_ref[...]
    lane = lax.broadcasted_iota(jnp.int32, logits.shape, 1)
    ng, ne = N_EXPERT_GROUPS, EXPERTS_PER_GROUP
    big = jnp.int32(ROUTER_LANES)
    is_g = lane < ng
    gl = jnp.where(is_g, logits, -jnp.inf)
    gmax = gl.max(axis=-1, keepdims=True)
    gsum = jnp.where(is_g, jnp.exp(logits - gmax), 0.0).sum(axis=-1, keepdims=True)
    p_group = 1.0 / gsum
    g_top = jnp.where(gl == gmax, lane, big).min(axis=-1, keepdims=True)
    lo = ng + ne * g_top
    es = jnp.where((lane >= lo) & (lane < lo + ne), logits, -jnp.inf)
    v1 = es.max(axis=-1, keepdims=True)
    i1 = jnp.where(es == v1, lane, big).min(axis=-1, keepdims=True)
    es2 = jnp.where(lane == i1, -jnp.inf, es)
    v2 = es2.max(axis=-1, keepdims=True)
    i2 = jnp.where(es2 == v2, lane, big).min(axis=-1, keepdims=True)
    e2 = jnp.exp(v2 - v1)
    w1 = p_group / (1.0 + e2)
    w2 = p_group * e2 / (1.0 + e2)
    o_ref[...] = jnp.where(lane == i1, w1, 0.0) + jnp.where(lane == i2, w2, 0.0)


def _router(x2, g, shift, scale, wr_hi, wr_lo, bias, bsz, tm=512):
    t, d = x2.shape
    s = t // bsz
    tm = min(tm, s)
    ns = s // tm
    row = lambda b, i: (b * ns + i, 0)
    full = lambda a: pl.BlockSpec(a.shape, lambda b, i: (0, 0))
    vec = pl.BlockSpec((None, 1, d), lambda b, i: (b, 0, 0))
    g = g.reshape(1, d)
    return pl.pallas_call(
        _router_kernel,
        grid=(bsz, ns),
        in_specs=[pl.BlockSpec((tm, d), row), full(g), vec, vec, full(wr_hi), full(wr_lo), full(bias)],
        out_specs=[pl.BlockSpec((tm, ROUTER_LANES), row), pl.BlockSpec((tm, d), row)],
        out_shape=[jax.ShapeDtypeStruct((t, ROUTER_LANES), F32), jax.ShapeDtypeStruct((t, d), BF16)],
        compiler_params=_params(("parallel", "parallel")),
        name="moe_router",
    )(x2, g, shift, scale, wr_hi, wr_lo, bias)


def _experts_kernel(h_ref, gates_ref, wg_ref, wu_ref, wd_ref, x_ref, ga_ref, o_ref, *, per_step):
    step = pl.program_id(1)
    gates = gates_ref[...]
    lane = lax.broadcasted_iota(jnp.int32, gates.shape, 1)
    h = h_ref[...]
    hids = []
    for j in range(per_step):
        e = step * per_step + j
        ge = jnp.where(lane == N_EXPERT_GROUPS + e, gates, 0.0).sum(axis=-1, keepdims=True)
        gp = _dot(h, wg_ref[j].astype(BF16))
        up = _dot(h, wu_ref[j].astype(BF16))
        hids.append((gp * _sigmoid(gp) * up * ge).astype(BF16))
    part = _dot(jnp.concatenate(hids, axis=1), wd_ref[...].astype(BF16))

    @pl.when(step == 0)
    def _():
        o_ref[...] = part

    @pl.when(step > 0)
    def _():
        o_ref[...] += part

    @pl.when(step == pl.num_programs(1) - 1)
    def _():
        o_ref[...] = x_ref[...] + ga_ref[...] * o_ref[...]


def _experts(hn_bf16, gates, wg, wu, wd, x2, gate, bsz, tm=1024, per_step=4):
    t, d = x2.shape
    s = t // bsz
    tm = min(tm, s)
    ns = s // tm
    ne, _, f = wg.shape
    row = lambda i, e: (i, 0)
    wspec = pl.BlockSpec((per_step, d, f), lambda i, e: (e, 0, 0))
    return pl.pallas_call(
        functools.partial(_experts_kernel, per_step=per_step),
        grid=(t // tm, ne // per_step),
        in_specs=[
            pl.BlockSpec((tm, d), row),
            pl.BlockSpec((tm, ROUTER_LANES), row),
            wspec, wspec,
            pl.BlockSpec((per_step * f, d), lambda i, e: (e, 0)),
            pl.BlockSpec((tm, d), row),
            pl.BlockSpec((None, 1, d), lambda i, e: (i // ns, 0, 0)),
        ],
        out_specs=pl.BlockSpec((tm, d), row),
        out_shape=jax.ShapeDtypeStruct((t, d), F32),
        compiler_params=_params(("parallel", "arbitrary")),
        name="moe_experts",
    )(hn_bf16, gates, wg, wu, wd, x2, gate)


def _moe_layer(x2, bsz, g, shift, scale, gate, p):
    gates, hn_bf16 = _router(x2, g, shift, scale, p["wr_hi"], p["wr_lo"], p["r_bias"], bsz)
    return _experts(hn_bf16, gates, p["wg"], p["wu"], p["wd"], x2, gate, bsz)


def _pad_rows(rows, d):
    out = jnp.zeros((8, d), F32)
    return out.at[:len(rows)].set(jnp.stack([r.reshape(d).astype(F32) for r in rows]))


def _pack_rwkv(j, mu, w_rkv, w0, w1, w2, a0, a1, a2, g1, g2, k_k, k_a, r_k, ln_w, ln_b, w_o):
    d = w_o.shape[-1]
    cat = lambda m: jnp.concatenate([m[j, 0], m[j, 1]], axis=1).astype(BF16)
    stack = lambda m: jnp.concatenate([m[j, 0], m[j, 1]], axis=0).astype(BF16)
    return {
        "mu8": _pad_rows(list(mu[j]), d),
        "w_rkv": w_rkv[j].astype(BF16),
        "w1c": cat(w1), "a1c": cat(a1), "g1c": cat(g1),
        "w2s": stack(w2), "a2s": stack(a2), "g2": g2[j].astype(BF16),
        "vec8": _pad_rows([w0[j, 0], w0[j, 1], a0[j, 0], a0[j, 1], k_k[j], k_a[j], r_k[j]], d),
        "ln8": _pad_rows([ln_w[j], ln_b[j]], d),
        "w_o": w_o[j].astype(BF16),
    }


def _pack_moe(i, router_g, router_g_b, router_e, router_e_b, w_gate, w_up, w_down):
    d = router_g.shape[1]
    ng, ne = N_EXPERT_GROUPS, EXPERTS_PER_GROUP
    wr = jnp.zeros((d, ROUTER_LANES), F32)
    wr = wr.at[:, :ng].set(router_g[i])
    wr = wr.at[:, ng:ng + ng * ne].set(router_e[i].transpose(1, 0, 2).reshape(d, ng * ne))
    bias = jnp.zeros((1, ROUTER_LANES), F32)
    bias = bias.at[0, :ng].set(router_g_b[i])
    bias = bias.at[0, ng:ng + ng * ne].set(router_e_b[i].reshape(ng * ne))
    wr_hi = wr.astype(BF16)
    wr_lo = (wr - wr_hi.astype(F32)).astype(BF16)
    f = w_gate.shape[-1]
    return {"wr_hi": wr_hi, "wr_lo": wr_lo, "r_bias": bias,
            "wg": w_gate[i].reshape(ng * ne, d, f), "wu": w_up[i].reshape(ng * ne, d, f),
            "wd": w_down[i].reshape(ng * ne * f, d)}


def kernel(x, c, ada_w, ada_b, norm_tm_g, norm_cm_g, rw_mu, rw_w_rkv, rw_w0, rw_w1, rw_w2, rw_a0, rw_a1, rw_a2, rw_g1, rw_g2, rw_k_k, rw_k_a, rw_r_k, rw_ln_w, rw_ln_b, rw_w_o, at_w_qkv, at_w_o, moe_router_g, moe_router_g_b, moe_router_e, moe_router_e_b, moe_w_gate, moe_w_up, moe_w_down, final_g):
    bsz, s, d = x.shape
    depth = ada_w.shape[0]
    x2 = x.reshape(bsz * s, d)
    mod = _ada_mod(c, ada_w, ada_b)
    for i in range(depth):
        sh_t, sc_t, ga_t, sh_c, sc_c, ga_c = (mod[i, m] for m in range(N_MOD))
        j = i // 2
        if i % 2 == 0:
            (hn,) = _norm_mod(x2, norm_tm_g[i], sh_t, sc_t, bsz, (F32,))
            p = _pack_rwkv(j, rw_mu, rw_w_rkv, rw_w0, rw_w1, rw_w2, rw_a0, rw_a1, rw_a2, rw_g1, rw_g2,
                           rw_k_k, rw_k_a, rw_r_k, rw_ln_w, rw_ln_b, rw_w_o)
            x2 = _rwkv_layer(x2, bsz, hn, ga_t, p)
        else:
            hns = _norm_mod_attn(x2, norm_tm_g[i], sh_t, sc_t, bsz)
            p = _pack_attn(j, at_w_qkv, at_w_o)
            x2 = _attn_layer(x2, bsz, hns, ga_t, p)
        pm = _pack_moe(i, moe_router_g, moe_router_g_b, moe_router_e, moe_router_e_b,
                       moe_w_gate, moe_w_up, moe_w_down)
        x2 = _moe_layer(x2, bsz, norm_cm_g[i], sh_c, sc_c, ga_c, pm)
    (out,) = _norm_mod(x2, final_g, None, None, bsz, (F32,))
    return out.reshape(bsz, s, d)
```

```python
import functools
import math

import numpy as np
import jax
import jax.numpy as jnp
from jax import lax
from jax.experimental import pallas as pl
from jax.experimental.pallas import tpu as pltpu

F32 = jnp.float32
BF16 = jnp.bfloat16

RMS_EPS = 1e-6
N_MOD = 6
RWKV_HEAD = 64
RWKV_GN_EPS = 64e-5
ATTN_GROUPS = ((128, 1), (512, 4), (2048, 16))
ATTN_HEADS = 8
ATTN_HEAD_DIM = 128
NEG_INF = -1e30
N_EXPERT_GROUPS = 4
EXPERTS_PER_GROUP = 8
N_EXPERTS = N_EXPERT_GROUPS * EXPERTS_PER_GROUP
ROUTER_LANES = 128
DEST_LANE = 64
MOE_TILE = 1024
MOE_BLOCK = 128

SCAN_CHUNK = 64
V7X_VMEM_LIMIT = 56 * 1024 * 1024
SCAN_DT = BF16


def _params(sem):
    return pltpu.CompilerParams(dimension_semantics=sem, vmem_limit_bytes=V7X_VMEM_LIMIT)


def _dot(a, b):
    return jnp.dot(a, b, preferred_element_type=F32)


def _sigmoid(x):
    return 1.0 / (1.0 + jnp.exp(-x))


def _split2(x):
    hi = x.astype(BF16)
    lo = (x - hi.astype(F32)).astype(BF16)
    return hi, lo


def _ada_kernel(c_ref, w_ref, b_ref, o_ref):
    c = c_ref[...]
    sc = c * _sigmoid(c)
    o_ref[...] = _dot(sc.astype(BF16), w_ref[...].astype(BF16)) + b_ref[...]


def _ada_mod(c, ada_w, ada_b):
    depth, d, n = ada_w.shape
    bsz = c.shape[0]
    tn = n // 4
    out = pl.pallas_call(
        _ada_kernel,
        grid=(depth, n // tn),
        in_specs=[
            pl.BlockSpec((bsz, d), lambda i, j: (0, 0)),
            pl.BlockSpec((None, d, tn), lambda i, j: (i, 0, j)),
            pl.BlockSpec((None, 1, tn), lambda i, j: (i, 0, j)),
        ],
        out_specs=pl.BlockSpec((None, bsz, tn), lambda i, j: (i, 0, j)),
        out_shape=jax.ShapeDtypeStruct((depth, bsz, n), F32),
        compiler_params=_params(("parallel", "parallel")),
        name="ada_mod",
    )(c, ada_w, ada_b.reshape(depth, 1, n))
    return out.reshape(depth, bsz, N_MOD, 1, d).transpose(0, 2, 1, 3, 4)


def _norm_kernel(*refs, modulated, n_out):
    if modulated:
        x_ref, g_ref, sh_ref, sc_ref = refs[:4]
    else:
        x_ref, g_ref = refs[:2]
    outs = refs[-n_out:]
    x = x_ref[...]
    ms = jnp.mean(x * x, axis=-1, keepdims=True)
    y = x * lax.rsqrt(ms + RMS_EPS) * g_ref[...]
    if modulated:
        y = y * (1.0 + sc_ref[...]) + sh_ref[...]
    for o in outs:
        o[...] = y.astype(o.dtype)


def _norm_mod(x2, g, shift, scale, bsz, out_dtypes, ts=512):
    t, d = x2.shape
    s = t // bsz
    ts = min(ts, s)
    ns = s // ts
    modulated = shift is not None
    row_spec = pl.BlockSpec((ts, d), lambda b, i: (b * ns + i, 0))
    in_specs = [row_spec, pl.BlockSpec((1, d), lambda b, i: (0, 0))]
    args = [x2, g.reshape(1, d)]
    if modulated:
        vec = pl.BlockSpec((None, 1, d), lambda b, i: (b, 0, 0))
        in_specs += [vec, vec]
        args += [shift, scale]
    outs = pl.pallas_call(
        functools.partial(_norm_kernel, modulated=modulated, n_out=len(out_dtypes)),
        grid=(bsz, ns),
        in_specs=in_specs,
        out_specs=[row_spec] * len(out_dtypes),
        out_shape=[jax.ShapeDtypeStruct((t, d), dt) for dt in out_dtypes],
        compiler_params=_params(("parallel", "parallel")),
        name="norm_mod",
    )(*args)
    return outs


def _proj_res_kernel(a_ref, w_ref, x_ref, ga_ref, o_ref):
    o_ref[...] = x_ref[...] + ga_ref[...] * _dot(a_ref[...], w_ref[...])


def _proj_residual(a, w, x2, gate, bsz, tm=512):
    t, d = x2.shape
    s = t // bsz
    tm = min(tm, s)
    ns = s // tm
    row = lambda b, i: (b * ns + i, 0)
    return pl.pallas_call(
        _proj_res_kernel,
        grid=(bsz, ns),
        in_specs=[
            pl.BlockSpec((tm, a.shape[1]), row),
            pl.BlockSpec(w.shape, lambda b, i: (0, 0)),
            pl.BlockSpec((tm, d), row),
            pl.BlockSpec((None, 1, d), lambda b, i: (b, 0, 0)),
        ],
        out_specs=pl.BlockSpec((tm, d), row),
        out_shape=jax.ShapeDtypeStruct((t, d), F32),
        compiler_params=_params(("parallel", "parallel")),
        name="proj_residual",
    )(a, w, x2, gate)


def _mm_kernel(a_ref, b_ref, o_ref):
    o_ref[...] = _dot(a_ref[...], b_ref[...]).astype(o_ref.dtype)


def _matmul(a, b, out_dtype, tm=1024, tn=1024):
    m, k = a.shape
    n = b.shape[1]
    tm, tn = min(tm, m), min(tn, n)
    return pl.pallas_call(
        _mm_kernel,
        grid=(n // tn, m // tm),
        in_specs=[
            pl.BlockSpec((tm, k), lambda j, i: (i, 0)),
            pl.BlockSpec((k, tn), lambda j, i: (0, j)),
        ],
        out_specs=pl.BlockSpec((tm, tn), lambda j, i: (i, j)),
        out_shape=jax.ShapeDtypeStruct((m, n), out_dtype),
        compiler_params=_params(("parallel", "parallel")),
        name="matmul",
    )(a, b)


def _seg_sum(x, seg_ref, segt_ref):
    xh, xl = _split2(x)
    s = _dot(xh, seg_ref[...]) + _dot(xl, seg_ref[...])
    sh, sl = _split2(s)
    return _dot(sh, segt_ref[...]) + _dot(sl, segt_ref[...])


def _rwkv_prep_kernel(hn_ref, prev_ref, next_ref, mu_ref, wrkv_ref, w1_ref, a1_ref, g1_ref,
                      w2_ref, a2_ref, g2_ref, vec_ref, cm_ref, edm_ref, seg_ref, segt_ref,
                      at_ref, rt_ref, bt_ref, kt_ref, v_ref, ed_ref, g_ref, gb_ref, *, ts, ns):
    i = pl.program_id(1)
    cur = hn_ref[...]
    row = lax.broadcasted_iota(jnp.int32, (ts, 1), 0)
    prev_row = jnp.where(i > 0, prev_ref[7:8, :], 0.0)
    next_row = jnp.where(i < ns - 1, next_ref[0:1, :], 0.0)
    x_prev = jnp.where(row == 0, prev_row, pltpu.roll(cur, 1, 0))
    x_next = jnp.where(row == ts - 1, next_row, pltpu.roll(cur, ts - 1, 0))
    xx = 0.5 * (x_prev + x_next) - cur

    def mix(j):
        return (cur + xx * mu_ref[j:j + 1, :]).astype(BF16)

    r = _dot(mix(0), wrkv_ref[0])
    k = _dot(mix(1), wrkv_ref[1])
    v = _dot(mix(2), wrkv_ref[2])
    tw = jnp.tanh(_dot(mix(3), w1_ref[...]))
    ta = _dot(mix(4), a1_ref[...])
    tg = _sigmoid(_dot(mix(5), g1_ref[...]))
    lane = lax.broadcasted_iota(jnp.int32, (1, tw.shape[1]), 1)
    half = tw.shape[1] // 2
    hg = tg.shape[1] // 2

    k_k = vec_ref[4:5, :]
    k_a = vec_ref[5:6, :]
    r_k = vec_ref[6:7, :]
    kk = k * k_k
    ss = _seg_sum(kk * kk, seg_ref, segt_ref)
    kkn = kk / jnp.maximum(jnp.sqrt(ss), 1e-12)
    v_ref[...] = v.astype(v_ref.dtype)

    gb = jnp.zeros_like(cur)
    for z in range(2):
        sel = (lane < half) if z == 0 else (lane >= half)
        lw = _dot(jnp.where(sel, tw, 0.0).astype(BF16), w2_ref[...])
        la = _dot(jnp.where(sel, ta, 0.0).astype(BF16), a2_ref[...])
        g = _dot(tg[:, z * hg:(z + 1) * hg].astype(BF16), g2_ref[z])
        ld = -math.exp(-0.5) * _sigmoid(vec_ref[z:z + 1, :] + lw)
        a = _sigmoid(vec_ref[2 + z:3 + z, :] + la)
        kdir = k * (1.0 + (a - 1.0) * k_a)
        b = kkn * a
        parts = _split2(ld)
        cmz = cm_ref[z]
        lm = _dot(cmz, parts[0]) + _dot(cmz, parts[1])
        e_in = jnp.exp(lm)
        e_inv = jnp.exp(-lm)
        e_ex = jnp.exp(lm - ld)
        rt_ref[z] = (r * e_in).astype(rt_ref.dtype)
        at_ref[z] = (-kkn * e_ex).astype(at_ref.dtype)
        bt_ref[z] = (b * e_inv).astype(bt_ref.dtype)
        kt_ref[z] = (kdir * e_inv).astype(kt_ref.dtype)
        edz = edm_ref[z]
        ed = jnp.exp(_dot(edz, parts[0]) + _dot(edz, parts[1]))
        for ck in range(ed.shape[0] // 8):
            ed_ref[z, ck] = ed[ck * 8:(ck + 1) * 8]
        g_ref[z] = g.astype(g_ref.dtype)
        bonus = _seg_sum(r * kdir * r_k, seg_ref, segt_ref) * v
        gb = gb + g * bonus
    gb_ref[...] = gb


def _chunk_matrices(ts, c):
    t = np.arange(ts)
    same = (t[:, None] // c) == (t[None, :] // c)
    pos = t % c
    cm = np.zeros((2, ts, ts), np.float32)
    nck = ts // c
    edm = np.zeros((2, 8 * nck, ts), np.float32)
    tri_f = same & (t[None, :] <= t[:, None])
    sel_f = same & (pos[None, :] <= c // 2 - 1)
    cm[0] = tri_f.astype(np.float32) - sel_f.astype(np.float32)
    tri_b = same & (t[None, :] >= t[:, None])
    sel_b = same & (pos[None, :] >= c // 2)
    cm[1] = tri_b.astype(np.float32) - sel_b.astype(np.float32)
    for ck in range(nck):
        inck = (t // c) == ck
        edm[0, 8 * ck] = inck & (pos <= c // 2 - 1)
        edm[0, 8 * ck + 1] = inck & (pos > c // 2 - 1)
        edm[1, 8 * ck] = inck & (pos >= c // 2)
        edm[1, 8 * ck + 1] = inck & (pos < c // 2)
    return jnp.asarray(cm, BF16), jnp.asarray(edm, BF16)


def _seg_matrices(d, head):
    seg = np.zeros((d, 128), np.float32)
    seg[np.arange(d), np.arange(d) // head] = 1.0
    return jnp.asarray(seg, BF16), jnp.asarray(seg.T.copy(), BF16)


def _rwkv_prep(hn, bsz, p, scan_dtype, ts=256):
    t, d = hn.shape
    s = t // bsz
    ts = min(ts, s)
    ns = s // ts
    c = SCAN_CHUNK
    assert ts % c == 0 and ts % 8 == 0
    nck = ts // c
    cm, edm = _chunk_matrices(ts, c)
    seg, segt = _seg_matrices(d, RWKV_HEAD)
    r8 = ts // 8
    nblk8 = t // 8
    row = lambda b, i: (b * ns + i, 0)
    full = lambda a: pl.BlockSpec(a.shape, lambda b, i, _n=a.ndim: (0,) * _n)
    weights = [p["mu8"], p["w_rkv"], p["w1c"], p["a1c"], p["g1c"], p["w2s"], p["a2s"], p["g2"],
               p["vec8"], cm, edm, seg, segt]
    dir_spec = pl.BlockSpec((2, ts, d), lambda b, i: (0, b * ns + i, 0))
    outs = pl.pallas_call(
        functools.partial(_rwkv_prep_kernel, ts=ts, ns=ns),
        grid=(bsz, ns),
        in_specs=[
            pl.BlockSpec((ts, d), row),
            pl.BlockSpec((8, d), lambda b, i: (jnp.maximum((b * ns + i) * r8 - 1, 0), 0)),
            pl.BlockSpec((8, d), lambda b, i: (jnp.minimum((b * ns + i + 1) * r8, nblk8 - 1), 0)),
        ] + [full(a) for a in weights],
        out_specs=[dir_spec, dir_spec, dir_spec, dir_spec,
                   pl.BlockSpec((ts, d), row),
                   pl.BlockSpec((2, nck, 8, d), lambda b, i: (0, b * ns + i, 0, 0)),
                   dir_spec,
                   pl.BlockSpec((ts, d), row)],
        out_shape=[jax.ShapeDtypeStruct((2, t, d), scan_dtype)] * 4 + [
            jax.ShapeDtypeStruct((t, d), scan_dtype),
            jax.ShapeDtypeStruct((2, t // c, 8, d), F32),
            jax.ShapeDtypeStruct((2, t, d), F32),
            jax.ShapeDtypeStruct((t, d), F32)],
        compiler_params=_params(("parallel", "parallel")),
        name="rwkv_prep",
    )(hn, hn, hn, *weights)
    return outs


def _sdot(a, b, dims):
    return lax.dot_general(a.astype(SCAN_DT), b.astype(SCAN_DT), (dims, ((), ())),
                           preferred_element_type=F32)


def _scan_kernel(at_ref, rt_ref, bt_ref, kt_ref, v_ref, ed_ref, y_ref, s_ref, *, c, n_heads):
    z = pl.program_id(1)
    ci = pl.program_id(2)

    @pl.when(ci == 0)
    def _():
        s_ref[...] = jnp.zeros_like(s_ref)

    c2 = 2 * c
    pw_lanes = 2 * RWKV_HEAD
    row = lax.broadcasted_iota(jnp.int32, (c2, c2), 0)
    col = lax.broadcasted_iota(jnp.int32, (c2, c2), 1)
    same_blk = (row // c) == (col // c)
    diff = jnp.where(z == 0, row - col, col - row)
    strict = same_blk & (diff > 0)
    incl = same_blk & (diff >= 0)
    incl2 = jnp.concatenate([incl, incl], axis=1)
    lane_a = lax.broadcasted_iota(jnp.int32, (c, pw_lanes), 1) < RWKV_HEAD
    same_head = ((lax.broadcasted_iota(jnp.int32, (pw_lanes, pw_lanes), 0) // RWKV_HEAD)
                 == (lax.broadcasted_iota(jnp.int32, (pw_lanes, pw_lanes), 1) // RWKV_HEAD))
    n_steps = int(math.log2(c))
    nt = (((1,), (1,)))
    nn = (((1,), (0,)))
    tn = (((0,), (0,)))
    pairs = range(n_heads // 2)
    sls = [pl.ds(hp * pw_lanes, pw_lanes) for hp in pairs]
    zero = jnp.zeros((c, pw_lanes), SCAN_DT)
    lhs, rhs, vv, st = [], [], [], []
    for hp in pairs:
        a2 = at_ref[:, sls[hp]]
        r2 = rt_ref[:, sls[hp]]
        b2 = bt_ref[:, sls[hp]]
        k2 = kt_ref[:, sls[hp]]
        v2 = v_ref[:, sls[hp]]
        lhs.append(jnp.concatenate([jnp.where(lane_a, a2, zero), jnp.where(lane_a, zero, a2),
                                    jnp.where(lane_a, r2, zero), jnp.where(lane_a, zero, r2)], axis=0))
        rhs.append(jnp.concatenate([b2, b2, k2, k2], axis=0))
        vv.append(jnp.concatenate([v2, v2], axis=0))
        st.append(s_ref[hp] * ed_ref[0:1, sls[hp]])
    gq = [_sdot(lhs[hp], jnp.concatenate([rhs[hp], st[hp].astype(SCAN_DT)], axis=0), nt) for hp in pairs]
    gm = [g[:, :2 * c2] for g in gq]
    q = [g[:, 2 * c2:] for g in gq]
    pw = [jnp.where(strict, gm[hp][:c2, :c2], 0.0).astype(SCAN_DT) for hp in pairs]
    m2 = [jnp.where(strict, gm[hp][:c2, c2:], 0.0).astype(SCAN_DT) for hp in pairs]
    m34 = [jnp.where(incl2, gm[hp][c2:, :], 0.0).astype(SCAN_DT) for hp in pairs]
    x = [q[hp][:c2] + _sdot(m2[hp], vv[hp], nn) for hp in pairs]
    for step in range(n_steps - 1):
        both = [_sdot(pw[hp], jnp.concatenate([x[hp].astype(SCAN_DT), pw[hp]], axis=1), nn) for hp in pairs]
        x = [x[hp] + both[hp][:, :pw_lanes] for hp in pairs]
        pw = [both[hp][:, pw_lanes:].astype(SCAN_DT) for hp in pairs]
    x = [x[hp] + _sdot(pw[hp], x[hp], nn) for hp in pairs]
    for hp in pairs:
        xs = x[hp].astype(SCAN_DT)
        uvs = jnp.concatenate([xs, vv[hp]], axis=0)
        ys = q[hp][c2:] + _sdot(m34[hp], uvs, nn)
        y_ref[:, sls[hp]] = jnp.where(lane_a, ys[:c], ys[c:])
        u2 = jnp.where(lane_a, xs[:c], xs[c:])
        upd = _sdot(jnp.concatenate([u2, vv[hp][:c]], axis=0), rhs[hp][c:3 * c], tn)
        s_ref[hp] = (st[hp] + jnp.where(same_head, upd, 0.0)) * ed_ref[1:2, sls[hp]]


def _rwkv_scan(at, rt, bt, kt, v, ed, bsz):
    _, t, d = at.shape
    s = t // bsz
    c = SCAN_CHUNK
    nc = s // c
    n_heads = d // RWKV_HEAD

    def cidx(b, z, ci):
        return b * nc + ci + z * (nc - 1 - 2 * ci)

    dspec = pl.BlockSpec((None, c, d), lambda b, z, ci: (z, cidx(b, z, ci), 0))
    return pl.pallas_call(
        functools.partial(_scan_kernel, c=c, n_heads=n_heads),
        grid=(bsz, 2, nc),
        in_specs=[dspec, dspec, dspec, dspec,
                  pl.BlockSpec((c, d), lambda b, z, ci: (cidx(b, z, ci), 0)),
                  pl.BlockSpec((None, None, 8, d), lambda b, z, ci: (z, cidx(b, z, ci), 0, 0))],
        out_specs=dspec,
        out_shape=jax.ShapeDtypeStruct((2, t, d), F32),
        scratch_shapes=[pltpu.VMEM((n_heads // 2, 2 * RWKV_HEAD, 2 * RWKV_HEAD), F32)],
        compiler_params=_params(("parallel", "parallel", "arbitrary")),
        name="rwkv_scan",
    )(at, rt, bt, kt, v, ed)


def _rwkv_out_kernel(y_ref, g_ref, gb_ref, x_ref, ga_ref, ln_ref, wo_ref, seg_ref, segt_ref, o_ref):
    o = gb_ref[...]
    inv_n = 1.0 / RWKV_HEAD
    for z in range(2):
        y = y_ref[z]
        mean = _seg_sum(y, seg_ref, segt_ref) * inv_n
        yc = y - mean
        var = _seg_sum(yc * yc, seg_ref, segt_ref) * inv_n
        yn = yc * lax.rsqrt(var + RWKV_GN_EPS) * ln_ref[0:1, :] + ln_ref[1:2, :]
        o = o + g_ref[z] * yn
    o_ref[...] = x_ref[...] + ga_ref[...] * _dot(o.astype(BF16), wo_ref[...])


def _rwkv_out(y, g, gb, x2, gate, ln8, w_o, bsz, tm=256):
    t, d = x2.shape
    s = t // bsz
    tm = min(tm, s)
    ns = s // tm
    seg, segt = _seg_matrices(d, RWKV_HEAD)
    row = lambda b, i: (b * ns + i, 0)
    dir_spec = pl.BlockSpec((2, tm, d), lambda b, i: (0, b * ns + i, 0))
    full = lambda a: pl.BlockSpec(a.shape, lambda b, i, _n=a.ndim: (0,) * _n)
    return pl.pallas_call(
        _rwkv_out_kernel,
        grid=(bsz, ns),
        in_specs=[dir_spec, dir_spec, pl.BlockSpec((tm, d), row), pl.BlockSpec((tm, d), row),
                  pl.BlockSpec((None, 1, d), lambda b, i: (b, 0, 0)),
                  full(ln8), full(w_o), full(seg), full(segt)],
        out_specs=pl.BlockSpec((tm, d), row),
        out_shape=jax.ShapeDtypeStruct((t, d), F32),
        compiler_params=_params(("parallel", "parallel")),
        name="rwkv_out",
    )(y, g, gb, x2, gate, ln8, w_o, seg, segt)


def _rwkv_layer(x2, bsz, hn, gate, p):
    at, rt, bt, kt, v, ed, g, gb = _rwkv_prep(hn, bsz, p, SCAN_DT)
    y = _rwkv_scan(at, rt, bt, kt, v, ed, bsz)
    return _rwkv_out(y, g, gb, x2, gate, p["ln8"], p["w_o"], bsz)


def _attn_group_kernel(slope_ref, vid_ref, q_ref, k_ref, v_ref, o_ref, lse_ref, tab_ref, *,
                       tq, sub, w, half, offs, nh, dh, hg):
    first = (pl.program_id(0) == 0) & (pl.program_id(1) == 0) & (pl.program_id(2) == 0)
    i = pl.program_id(2)

    @pl.when(first)
    def _():
        col = lax.broadcasted_iota(jnp.int32, (tq, w), 1)
        row = lax.broadcasted_iota(jnp.int32, (tq, w), 0)
        for v, off in enumerate(offs):
            dist = jnp.abs(col - row + off)
            distf = dist.astype(F32)
            for h in range(nh):
                tab_ref[h * len(offs) + v] = jnp.where(dist <= half, -slope_ref[h] * distf, NEG_INF)

    if w == sub:
        ws = 0
    else:
        ws = pl.multiple_of(jnp.clip(i * tq - half, 0, sub - w), 16)
    var = vid_ref[i]
    nt = (((1,), (1,)), ((), ()))
    lane = lax.broadcasted_iota(jnp.int32, (tq, 128), 1)
    lse = jnp.zeros((tq, 128), F32)
    for h0 in range(0, nh, hg):
        heads = range(h0, min(h0 + hg, nh))
        hs = {h: slice(h * dh, (h + 1) * dh) for h in heads}
        sc = {h: lax.dot_general(q_ref[:, hs[h]], k_ref[pl.ds(ws, w), hs[h]], nt, preferred_element_type=F32)
              + tab_ref[h * len(offs) + var] for h in heads}
        m = {h: sc[h].max(axis=-1, keepdims=True) for h in heads}
        p = {h: jnp.exp(sc[h] - m[h]) for h in heads}
        den = {h: p[h].sum(axis=-1, keepdims=True) for h in heads}
        acc = {h: _dot(p[h].astype(BF16), v_ref[pl.ds(ws, w), hs[h]]) for h in heads}
        for h in heads:
            o_ref[:, hs[h]] = (acc[h] / den[h]).astype(o_ref.dtype)
            lse = jnp.where(lane == h, m[h] + jnp.log(den[h]), lse)
    lse_ref[...] = lse


def _alibi_slopes(n):
    return 2.0 ** (-8.0 * jnp.arange(1, n + 1, dtype=F32) / n)


def _attention_group(qkv, bsz, gi, tq=256):
    t, width = qkv.shape
    s = t // bsz
    window, dil = ATTN_GROUPS[gi]
    half = window // (2 * dil)
    sub = s // dil
    tq = min(tq, sub)
    nq = sub // tq
    w = min(sub, tq + 2 * half)
    nh, dh = ATTN_HEADS, ATTN_HEAD_DIM
    hw = nh * dh
    offs = sorted({int(np.clip(i * tq - half, 0, sub - w)) - i * tq for i in range(nq)})
    vid = jnp.asarray([offs.index(int(np.clip(i * tq - half, 0, sub - w)) - i * tq) for i in range(nq)], jnp.int32)
    slopes = _alibi_slopes(len(ATTN_GROUPS) * nh).reshape(len(ATTN_GROUPS), nh)[gi] * dil
    smem = pl.BlockSpec(memory_space=pltpu.SMEM)
    qrow = lambda b, r, i: ((b * dil + r) * nq + i, 0)
    return pl.pallas_call(
        functools.partial(_attn_group_kernel, tq=tq, sub=sub, w=w, half=half, offs=tuple(offs), nh=nh, dh=dh,
                          hg=nh if tq * w <= 128 * 128 else 4),
        grid=(bsz, dil, nq),
        in_specs=[smem, smem,
                  pl.BlockSpec((tq, hw), qrow),
                  pl.BlockSpec((sub, hw), lambda b, r, i: (b * dil + r, 1)),
                  pl.BlockSpec((sub, hw), lambda b, r, i: (b * dil + r, 2))],
        out_specs=[pl.BlockSpec((tq, hw), qrow), pl.BlockSpec((tq, 128), qrow)],
        out_shape=[jax.ShapeDtypeStruct((t, hw), BF16), jax.ShapeDtypeStruct((t, 128), F32)],
        scratch_shapes=[pltpu.VMEM((nh * len(offs), tq, w), F32)],
        compiler_params=_params(("arbitrary", "arbitrary", "arbitrary")),
        name=f"dilated_attention_g{gi}",
    )(slopes, vid, qkv, qkv, qkv)


def _to_token_order(blk_ref, scr_ref, dil):
    if dil == 1:
        return blk_ref[0].astype(F32)
    n = blk_ref.shape[1]
    slabs = scr_ref.shape[0]
    for r in range(dil):
        rows = blk_ref[r].astype(F32)
        for c in range(slabs):
            scr_ref[c, pl.ds(r, n, stride=dil), :] = rows[:, c * 128:(c + 1) * 128]
    if slabs == 1:
        return scr_ref[0]
    return jnp.concatenate([scr_ref[c] for c in range(slabs)], axis=1)


def _attn_out_kernel(o1, o2, o3, l1, l2, l3, ex_ref, w_ref, x_ref, ga_ref, out_ref, so_ref, sl_ref, *, dils):
    ls = [_to_token_order(l, sl_ref, dil) for l, dil in zip((l1, l2, l3), dils)]
    m = jnp.maximum(jnp.maximum(ls[0], ls[1]), ls[2])
    es = [jnp.exp(l - m) for l in ls]
    inv = 1.0 / (es[0] + es[1] + es[2])
    merged = None
    for e, o_ref, dil in zip(es, (o1, o2, o3), dils):
        ah, al = _split2(e * inv)
        alpha = _dot(ah, ex_ref[...]) + _dot(al, ex_ref[...])
        term = alpha * _to_token_order(o_ref, so_ref, dil)
        merged = term if merged is None else merged + term
    out_ref[...] = x_ref[...] + ga_ref[...] * _dot(merged.astype(BF16), w_ref[...])


def _attn_out(os, lses, w_o, x2, gate, bsz, tm=512):
    t, d = x2.shape
    s = t // bsz
    tm = min(tm, s)
    ns = s // tm
    hw = os[0].shape[1]
    dils = tuple(dil for _, dil in ATTN_GROUPS)
    ex = np.zeros((128, hw), np.float32)
    ex[np.arange(hw) // ATTN_HEAD_DIM, np.arange(hw)] = 1.0
    ex = jnp.asarray(ex, BF16)
    row = lambda b, i: (b * ns + i, 0)

    def res_major(a, dil):
        c = a.shape[1]
        return a.reshape(bsz, dil, s // dil, c), pl.BlockSpec((None, dil, tm // dil, c), lambda b, i: (b, 0, i, 0))

    o_args, o_specs = zip(*[res_major(o, dil) for o, dil in zip(os, dils)])
    l_args, l_specs = zip(*[res_major(l, dil) for l, dil in zip(lses, dils)])
    return pl.pallas_call(
        functools.partial(_attn_out_kernel, dils=dils),
        grid=(bsz, ns),
        in_specs=list(o_specs) + list(l_specs) + [
            pl.BlockSpec(ex.shape, lambda b, i: (0, 0)),
            pl.BlockSpec(w_o.shape, lambda b, i: (0, 0)),
            pl.BlockSpec((tm, d), row),
            pl.BlockSpec((None, 1, d), lambda b, i: (b, 0, 0))],
        out_specs=pl.BlockSpec((tm, d), row),
        out_shape=jax.ShapeDtypeStruct((t, d), F32),
        scratch_shapes=[pltpu.VMEM((hw // 128, tm, 128), F32), pltpu.VMEM((1, tm, 128), F32)],
        compiler_params=_params(("parallel", "parallel")),
        name="attn_out",
    )(*o_args, *l_args, ex, w_o, x2, gate)


def _norm_attn_kernel(x_ref, g_ref, sh_ref, sc_ref, *rest, dils):
    outs, scr_ref = rest[:-1], rest[-1]
    x = x_ref[...]
    ms = jnp.mean(x * x, axis=-1, keepdims=True)
    y = x * lax.rsqrt(ms + RMS_EPS) * g_ref[...]
    y = y * (1.0 + sc_ref[...]) + sh_ref[...]
    slabs = scr_ref.shape[0]
    if any(dil > 1 for dil in dils):
        for c in range(slabs):
            scr_ref[c] = y[:, c * 128:(c + 1) * 128]
    for o, dil in zip(outs, dils):
        if dil == 1:
            o[0] = y.astype(o.dtype)
        else:
            n = o.shape[1]
            for r in range(dil):
                for c in range(slabs):
                    o[r, :, c * 128:(c + 1) * 128] = scr_ref[c, pl.ds(r, n, stride=dil), :].astype(o.dtype)


def _norm_mod_attn(x2, g, shift, scale, bsz, ts=512):
    t, d = x2.shape
    s = t // bsz
    ts = min(ts, s)
    ns = s // ts
    dils = tuple(dil for _, dil in ATTN_GROUPS)
    vec = pl.BlockSpec((None, 1, d), lambda b, i: (b, 0, 0))
    outs = pl.pallas_call(
        functools.partial(_norm_attn_kernel, dils=dils),
        grid=(bsz, ns),
        in_specs=[pl.BlockSpec((ts, d), lambda b, i: (b * ns + i, 0)),
                  pl.BlockSpec((1, d), lambda b, i: (0, 0)), vec, vec],
        out_specs=[pl.BlockSpec((None, dil, ts // dil, d), lambda b, i: (b, 0, i, 0)) for dil in dils],
        out_shape=[jax.ShapeDtypeStruct((bsz, dil, s // dil, d), BF16) for dil in dils],
        scratch_shapes=[pltpu.VMEM((d // 128, ts, 128), F32)],
        compiler_params=_params(("parallel", "parallel")),
        name="norm_mod_attn",
    )(x2, g.reshape(1, d), shift, scale)
    return [o.reshape(t, d) for o in outs]


def _pack_attn(j, w_qkv, w_o):
    d = w_qkv.shape[1]
    ng = len(ATTN_GROUPS)
    w = w_qkv[j].reshape(d, ng, 3, ATTN_HEADS * ATTN_HEAD_DIM)
    w = w.at[:, :, 0].multiply(ATTN_HEAD_DIM ** -0.5)
    return {"w_qkv": w.transpose(1, 0, 2, 3).reshape(ng, d, -1).astype(BF16), "w_o": w_o[j].astype(BF16)}


def _attn_layer(x2, bsz, hns, gate, p):
    outs = [_attention_group(_matmul(hn, p["w_qkv"][gi], BF16), bsz, gi) for gi, hn in enumerate(hns)]
    return _attn_out([o for o, _ in outs], [l for _, l in outs], p["w_o"], x2, gate, bsz)


def _router_kernel(x_ref, g_ref, sh_ref, sc_ref, wh_ref, wl_ref, b_ref, o_ref, hb_ref, seg_ref):
    x = x_ref[...]
    ms = jnp.mean(x * x, axis=-1, keepdims=True)
    hn = x * lax.rsqrt(ms + RMS_EPS) * g_ref[...]
    hn = hn * (1.0 + sc_ref[...]) + sh_ref[...]
    hb_ref[...] = hn.astype(hb_ref.dtype)
    hh, hl = _split2(hn)
    logits = _dot(hh, wh_ref[...]) + _dot(hl, wh_ref[...]) + _dot(hh, wl_ref[...]) + b_ref[...]
    lane = lax.broadcasted_iota(jnp.int32, logits.shape, 1)
    ng, ne = N_EXPERT_GROUPS, EXPERTS_PER_GROUP
    big = jnp.int32(ROUTER_LANES)
    is_g = lane < ng
    gl = jnp.where(is_g, logits, -jnp.inf)
    gmax = gl.max(axis=-1, keepdims=True)
    gsum = jnp.where(is_g, jnp.exp(logits - gmax), 0.0).sum(axis=-1, keepdims=True)
    p_group = 1.0 / gsum
    g_top = jnp.where(gl == gmax, lane, big).min(axis=-1, keepdims=True)
    lo = ng + ne * g_top
    es = jnp.where((lane >= lo) & (lane < lo + ne), logits, -jnp.inf)
    v1 = es.max(axis=-1, keepdims=True)
    i1 = jnp.where(es == v1, lane, big).min(axis=-1, keepdims=True)
    es2 = jnp.where(lane == i1, -jnp.inf, es)
    v2 = es2.max(axis=-1, keepdims=True)
    i2 = jnp.where(es2 == v2, lane, big).min(axis=-1, keepdims=True)
    e2 = jnp.exp(v2 - v1)
    w1 = p_group / (1.0 + e2)
    w2 = p_group * e2 / (1.0 + e2)
    gates = jnp.where(lane == i1, w1, 0.0) + jnp.where(lane == i2, w2, 0.0)

    tm = x.shape[0]
    onehot = jnp.where(lane == g_top, 1.0, 0.0)
    tri = jnp.where(lax.broadcasted_iota(jnp.int32, (tm, tm), 1) < lax.broadcasted_iota(jnp.int32, (tm, tm), 0),
                    1.0, 0.0).astype(BF16)
    before = _dot(tri, onehot.astype(BF16))
    counts = jnp.broadcast_to(onehot.sum(axis=0, keepdims=True), (8, ROUTER_LANES))
    upper = jnp.where(lax.broadcasted_iota(jnp.int32, (ROUTER_LANES, ROUTER_LANES), 0)
                      < lax.broadcasted_iota(jnp.int32, (ROUTER_LANES, ROUTER_LANES), 1), 1.0, 0.0).astype(BF16)
    ch, cl = _split2(counts)
    seg_lo = _dot(ch, upper) + _dot(cl, upper)
    dest = (onehot * (before + seg_lo[0:1])).sum(axis=-1, keepdims=True)
    o_ref[...] = gates + jnp.where(lane == DEST_LANE, dest, 0.0)
    row8 = lax.broadcasted_iota(jnp.int32, (8, ROUTER_LANES), 0)
    seg_ref[...] = jnp.where(row8 == 0, seg_lo, jnp.where(row8 == 1, seg_lo + counts, 0.0))


def _router(x2, g, shift, scale, wr_hi, wr_lo, bias, bsz):
    t, d = x2.shape
    s = t // bsz
    tm = min(MOE_TILE, s)
    ns = s // tm
    row = lambda b, i: (b * ns + i, 0)
    full = lambda a: pl.BlockSpec(a.shape, lambda b, i: (0, 0))
    vec = pl.BlockSpec((None, 1, d), lambda b, i: (b, 0, 0))
    g = g.reshape(1, d)
    return pl.pallas_call(
        _router_kernel,
        grid=(bsz, ns),
        in_specs=[pl.BlockSpec((tm, d), row), full(g), vec, vec, full(wr_hi), full(wr_lo), full(bias)],
        out_specs=[pl.BlockSpec((tm, ROUTER_LANES), row), pl.BlockSpec((tm, d), row),
                   pl.BlockSpec((None, 8, ROUTER_LANES), lambda b, i: (b * ns + i, 0, 0))],
        out_shape=[jax.ShapeDtypeStruct((t, ROUTER_LANES), F32), jax.ShapeDtypeStruct((t, d), BF16),
                   jax.ShapeDtypeStruct((t // tm, 8, ROUTER_LANES), F32)],
        compiler_params=_params(("parallel", "parallel")),
        name="moe_router",
    )(x2, g, shift, scale, wr_hi, wr_lo, bias)


def _experts_kernel(seg_ref, h_ref, gates_ref, wg_ref, wu_ref, wd_ref, x_ref, ga_ref, o_ref,
                    pt_ref, xs_ref, gs_ref, y_ref, *, per_step, blk):
    tile = pl.program_id(0)
    step = pl.program_id(1)
    tm = h_ref.shape[0]
    grp = step // (EXPERTS_PER_GROUP // per_step)
    tn = (((0,), (0,)), ((), ()))

    @pl.when(step == 0)
    def _():
        gates = gates_ref[...]
        lane = lax.broadcasted_iota(jnp.int32, gates.shape, 1)
        dest = jnp.where(lane == DEST_LANE, gates, 0.0).sum(axis=-1, keepdims=True)
        col = lax.broadcasted_iota(jnp.int32, (tm, tm), 1).astype(F32)
        pt = jnp.where(dest == col, 1.0, 0.0).astype(BF16)
        pt_ref[...] = pt
        xs_ref[...] = lax.dot_general(pt, h_ref[...], tn, preferred_element_type=F32).astype(BF16)
        g1 = gates.astype(BF16)
        r1 = gates - g1.astype(F32)
        g2 = r1.astype(BF16)
        g3 = (r1 - g2.astype(F32)).astype(BF16)
        gs_ref[...] = (lax.dot_general(pt, g1, tn, preferred_element_type=F32)
                       + lax.dot_general(pt, g2, tn, preferred_element_type=F32)
                       + lax.dot_general(pt, g3, tn, preferred_element_type=F32))
        y_ref[...] = jnp.zeros_like(y_ref)

    lo = seg_ref[tile * 8 + grp]
    hi = seg_ref[tile * 8 + N_EXPERT_GROUPS + grp]
    for b in range(tm // blk):
        @pl.when((lo < (b + 1) * blk) & (hi > b * blk))
        def _(b=b):
            rows = pl.ds(b * blk, blk)
            h = xs_ref[rows, :]
            gsb = gs_ref[rows, :]
            lane = lax.broadcasted_iota(jnp.int32, gsb.shape, 1)
            hids = []
            for j in range(per_step):
                e = step * per_step + j
                ge = jnp.where(lane == N_EXPERT_GROUPS + e, gsb, 0.0).sum(axis=-1, keepdims=True)
                gp = _dot(h, wg_ref[j])
                up = _dot(h, wu_ref[j])
                hids.append((gp * _sigmoid(gp) * up * ge).astype(BF16))
            y_ref[rows, :] += _dot(jnp.concatenate(hids, axis=1), wd_ref[...])

    @pl.when(step == pl.num_programs(1) - 1)
    def _():
        yh, yl = _split2(y_ref[...])
        pt = pt_ref[...]
        o_ref[...] = x_ref[...] + ga_ref[...] * (_dot(pt, yh) + _dot(pt, yl))


def _experts(hn_bf16, gates, seg, wg, wu, wd, x2, gate, bsz, per_step=4):
    t, d = x2.shape
    s = t // bsz
    tm = min(MOE_TILE, s)
    ns = s // tm
    ne, _, f = wg.shape
    blk = min(MOE_BLOCK, tm)
    row = lambda i, e, seg: (i, 0)
    wspec = pl.BlockSpec((per_step, d, f), lambda i, e, seg: (e, 0, 0))
    grid_spec = pltpu.PrefetchScalarGridSpec(
        num_scalar_prefetch=1,
        grid=(t // tm, ne // per_step),
        in_specs=[
            pl.BlockSpec((tm, d), row),
            pl.BlockSpec((tm, ROUTER_LANES), row),
            wspec, wspec,
            pl.BlockSpec((per_step * f, d), lambda i, e, seg: (e, 0)),
            pl.BlockSpec((tm, d), row),
            pl.BlockSpec((None, 1, d), lambda i, e, seg: (i // ns, 0, 0)),
        ],
        out_specs=pl.BlockSpec((tm, d), row),
        scratch_shapes=[pltpu.VMEM((tm, tm), BF16), pltpu.VMEM((tm, d), BF16),
                        pltpu.VMEM((tm, ROUTER_LANES), F32), pltpu.VMEM((tm, d), F32)],
    )
    return pl.pallas_call(
        functools.partial(_experts_kernel, per_step=per_step, blk=blk),
        grid_spec=grid_spec,
        out_shape=jax.ShapeDtypeStruct((t, d), F32),
        compiler_params=_params(("parallel", "arbitrary")),
        name="moe_experts",
    )(seg, hn_bf16, gates, wg, wu, wd, x2, gate)


def _moe_layer(x2, bsz, g, shift, scale, gate, p):
    gates, hn_bf16, seg = _router(x2, g, shift, scale, p["wr_hi"], p["wr_lo"], p["r_bias"], bsz)
    ng = N_EXPERT_GROUPS
    seg = jnp.concatenate([seg[:, 0, :ng], seg[:, 1, :ng]], axis=1).astype(jnp.int32).reshape(-1)
    return _experts(hn_bf16, gates, seg, p["wg"], p["wu"], p["wd"], x2, gate, bsz)


def _pad_rows(rows, d):
    out = jnp.zeros((8, d), F32)
    return out.at[:len(rows)].set(jnp.stack([r.reshape(d).astype(F32) for r in rows]))


def _pack_rwkv(j, mu, w_rkv, w0, w1, w2, a0, a1, a2, g1, g2, k_k, k_a, r_k, ln_w, ln_b, w_o):
    d = w_o.shape[-1]
    cat = lambda m: jnp.concatenate([m[j, 0], m[j, 1]], axis=1).astype(BF16)
    stack = lambda m: jnp.concatenate([m[j, 0], m[j, 1]], axis=0).astype(BF16)
    return {
        "mu8": _pad_rows(list(mu[j]), d),
        "w_rkv": w_rkv[j].astype(BF16),
        "w1c": cat(w1), "a1c": cat(a1), "g1c": cat(g1),
        "w2s": stack(w2), "a2s": stack(a2), "g2": g2[j].astype(BF16),
        "vec8": _pad_rows([w0[j, 0], w0[j, 1], a0[j, 0], a0[j, 1], k_k[j], k_a[j], r_k[j]], d),
        "ln8": _pad_rows([ln_w[j], ln_b[j]], d),
        "w_o": w_o[j].astype(BF16),
    }


def _pack_moe(i, router_g, router_g_b, router_e, router_e_b, w_gate, w_up, w_down):
    d = router_g.shape[1]
    ng, ne = N_EXPERT_GROUPS, EXPERTS_PER_GROUP
    wr = jnp.zeros((d, ROUTER_LANES), F32)
    wr = wr.at[:, :ng].set(router_g[i])
    wr = wr.at[:, ng:ng + ng * ne].set(router_e[i].transpose(1, 0, 2).reshape(d, ng * ne))
    bias = jnp.zeros((1, ROUTER_LANES), F32)
    bias = bias.at[0, :ng].set(router_g_b[i])
    bias = bias.at[0, ng:ng + ng * ne].set(router_e_b[i].reshape(ng * ne))
    wr_hi = wr.astype(BF16)
    wr_lo = (wr - wr_hi.astype(F32)).astype(BF16)
    f = w_gate.shape[-1]
    return {"wr_hi": wr_hi, "wr_lo": wr_lo, "r_bias": bias,
            "wg": w_gate[i].reshape(ng * ne, d, f).astype(BF16), "wu": w_up[i].reshape(ng * ne, d, f).astype(BF16),
            "wd": w_down[i].reshape(ng * ne * f, d).astype(BF16)}


def kernel(x, c, ada_w, ada_b, norm_tm_g, norm_cm_g, rw_mu, rw_w_rkv, rw_w0, rw_w1, rw_w2, rw_a0, rw_a1, rw_a2, rw_g1, rw_g2, rw_k_k, rw_k_a, rw_r_k, rw_ln_w, rw_ln_b, rw_w_o, at_w_qkv, at_w_o, moe_router_g, moe_router_g_b, moe_router_e, moe_router_e_b, moe_w_gate, moe_w_up, moe_w_down, final_g):
    bsz, s, d = x.shape
    depth = ada_w.shape[0]
    x2 = x.reshape(bsz * s, d)
    mod = _ada_mod(c, ada_w, ada_b)
    for i in range(depth):
        sh_t, sc_t, ga_t, sh_c, sc_c, ga_c = (mod[i, m] for m in range(N_MOD))
        j = i // 2
        if i % 2 == 0:
            (hn,) = _norm_mod(x2, norm_tm_g[i], sh_t, sc_t, bsz, (F32,))
            p = _pack_rwkv(j, rw_mu, rw_w_rkv, rw_w0, rw_w1, rw_w2, rw_a0, rw_a1, rw_a2, rw_g1, rw_g2,
                           rw_k_k, rw_k_a, rw_r_k, rw_ln_w, rw_ln_b, rw_w_o)
            x2 = _rwkv_layer(x2, bsz, hn, ga_t, p)
        else:
            hns = _norm_mod_attn(x2, norm_tm_g[i], sh_t, sc_t, bsz)
            p = _pack_attn(j, at_w_qkv, at_w_o)
            x2 = _attn_layer(x2, bsz, hns, ga_t, p)
        pm = _pack_moe(i, moe_router_g, moe_router_g_b, moe_router_e, moe_router_e_b,
                       moe_w_gate, moe_w_up, moe_w_down)
        x2 = _moe_layer(x2, bsz, norm_cm_g[i], sh_c, sc_c, ga_c, pm)
    (out,) = _norm_mod(x2, final_g, None, None, bsz, (F32,))
    return out.reshape(bsz, s, d)
```

```python
import functools
import math

import numpy as np
import jax
import jax.numpy as jnp
from jax import lax
from jax.experimental import pallas as pl
from jax.experimental.pallas import tpu as pltpu

F32 = jnp.float32
BF16 = jnp.bfloat16

RMS_EPS = 1e-6
N_MOD = 6
RWKV_HEAD = 64
RWKV_GN_EPS = 64e-5
ATTN_GROUPS = ((128, 1), (512, 4), (2048, 16))
ATTN_HEADS = 8
ATTN_HEAD_DIM = 128
NEG_INF = -1e30
N_EXPERT_GROUPS = 4
EXPERTS_PER_GROUP = 8
N_EXPERTS = N_EXPERT_GROUPS * EXPERTS_PER_GROUP
ROUTER_LANES = 128
DEST_LANE = 64
MOE_TILE = 1024
MOE_BLOCK = 128

SCAN_CHUNK = 64
SCAN_CHUNKS_PER_STEP = 4
V7X_VMEM_LIMIT = 56 * 1024 * 1024
SCAN_DT = BF16


def _params(sem):
    return pltpu.CompilerParams(dimension_semantics=sem, vmem_limit_bytes=V7X_VMEM_LIMIT)


def _dot(a, b):
    return jnp.dot(a, b, preferred_element_type=F32)


def _sigmoid(x):
    return 0.5 * jnp.tanh(0.5 * x) + 0.5


def _split2(x):
    hi = x.astype(BF16)
    lo = (x - hi.astype(F32)).astype(BF16)
    return hi, lo


def _ada_kernel(c_ref, w_ref, b_ref, o_ref):
    c = c_ref[...]
    sc = c * _sigmoid(c)
    o_ref[...] = _dot(sc.astype(BF16), w_ref[...].astype(BF16)) + b_ref[...]


def _ada_mod(c, ada_w, ada_b):
    depth, d, n = ada_w.shape
    bsz = c.shape[0]
    tn = n // 4
    out = pl.pallas_call(
        _ada_kernel,
        grid=(depth, n // tn),
        in_specs=[
            pl.BlockSpec((bsz, d), lambda i, j: (0, 0)),
            pl.BlockSpec((None, d, tn), lambda i, j: (i, 0, j)),
            pl.BlockSpec((None, 1, tn), lambda i, j: (i, 0, j)),
        ],
        out_specs=pl.BlockSpec((None, bsz, tn), lambda i, j: (i, 0, j)),
        out_shape=jax.ShapeDtypeStruct((depth, bsz, n), F32),
        compiler_params=_params(("parallel", "parallel")),
        name="ada_mod",
    )(c, ada_w, ada_b.reshape(depth, 1, n))
    return out.reshape(depth, bsz, N_MOD, 1, d).transpose(0, 2, 1, 3, 4)


def _norm_kernel(*refs, modulated, n_out):
    if modulated:
        x_ref, g_ref, sh_ref, sc_ref = refs[:4]
    else:
        x_ref, g_ref = refs[:2]
    outs = refs[-n_out:]
    x = x_ref[...]
    ms = jnp.mean(x * x, axis=-1, keepdims=True)
    y = x * lax.rsqrt(ms + RMS_EPS) * g_ref[...]
    if modulated:
        y = y * (1.0 + sc_ref[...]) + sh_ref[...]
    for o in outs:
        o[...] = y.astype(o.dtype)


def _norm_mod(x2, g, shift, scale, bsz, out_dtypes, ts=512):
    t, d = x2.shape
    s = t // bsz
    ts = min(ts, s)
    ns = s // ts
    modulated = shift is not None
    row_spec = pl.BlockSpec((ts, d), lambda b, i: (b * ns + i, 0))
    in_specs = [row_spec, pl.BlockSpec((1, d), lambda b, i: (0, 0))]
    args = [x2, g.reshape(1, d)]
    if modulated:
        vec = pl.BlockSpec((None, 1, d), lambda b, i: (b, 0, 0))
        in_specs += [vec, vec]
        args += [shift, scale]
    outs = pl.pallas_call(
        functools.partial(_norm_kernel, modulated=modulated, n_out=len(out_dtypes)),
        grid=(bsz, ns),
        in_specs=in_specs,
        out_specs=[row_spec] * len(out_dtypes),
        out_shape=[jax.ShapeDtypeStruct((t, d), dt) for dt in out_dtypes],
        compiler_params=_params(("parallel", "parallel")),
        name="norm_mod",
    )(*args)
    return outs


def _proj_res_kernel(a_ref, w_ref, x_ref, ga_ref, o_ref):
    o_ref[...] = x_ref[...] + ga_ref[...] * _dot(a_ref[...], w_ref[...])


def _proj_residual(a, w, x2, gate, bsz, tm=512):
    t, d = x2.shape
    s = t // bsz
    tm = min(tm, s)
    ns = s // tm
    row = lambda b, i: (b * ns + i, 0)
    return pl.pallas_call(
        _proj_res_kernel,
        grid=(bsz, ns),
        in_specs=[
            pl.BlockSpec((tm, a.shape[1]), row),
            pl.BlockSpec(w.shape, lambda b, i: (0, 0)),
            pl.BlockSpec((tm, d), row),
            pl.BlockSpec((None, 1, d), lambda b, i: (b, 0, 0)),
        ],
        out_specs=pl.BlockSpec((tm, d), row),
        out_shape=jax.ShapeDtypeStruct((t, d), F32),
        compiler_params=_params(("parallel", "parallel")),
        name="proj_residual",
    )(a, w, x2, gate)


def _mm_kernel(a_ref, b_ref, o_ref):
    o_ref[...] = _dot(a_ref[...], b_ref[...]).astype(o_ref.dtype)


def _matmul(a, b, out_dtype, tm=1024, tn=1024):
    m, k = a.shape
    n = b.shape[1]
    tm, tn = min(tm, m), min(tn, n)
    return pl.pallas_call(
        _mm_kernel,
        grid=(n // tn, m // tm),
        in_specs=[
            pl.BlockSpec((tm, k), lambda j, i: (i, 0)),
            pl.BlockSpec((k, tn), lambda j, i: (0, j)),
        ],
        out_specs=pl.BlockSpec((tm, tn), lambda j, i: (i, j)),
        out_shape=jax.ShapeDtypeStruct((m, n), out_dtype),
        compiler_params=_params(("parallel", "parallel")),
        name="matmul",
    )(a, b)


def _seg_sum(x, seg_ref, segt_ref):
    xh, xl = _split2(x)
    s = _dot(xh, seg_ref[...]) + _dot(xl, seg_ref[...])
    sh, sl = _split2(s)
    return _dot(sh, segt_ref[...]) + _dot(sl, segt_ref[...])


def _rwkv_prep_kernel(hn_ref, prev_ref, next_ref, mu_ref, wrkv_ref, w1_ref, a1_ref, g1_ref,
                      w2_ref, a2_ref, g2_ref, vec_ref, cm_ref, edm_ref, seg_ref, segt_ref,
                      at_ref, rt_ref, bt_ref, kt_ref, v_ref, ed_ref, g_ref, gb_ref, *, ts, ns):
    i = pl.program_id(1)
    cur = hn_ref[...]
    row = lax.broadcasted_iota(jnp.int32, (ts, 1), 0)
    prev_row = jnp.where(i > 0, prev_ref[7:8, :], 0.0)
    next_row = jnp.where(i < ns - 1, next_ref[0:1, :], 0.0)
    x_prev = jnp.where(row == 0, prev_row, pltpu.roll(cur, 1, 0))
    x_next = jnp.where(row == ts - 1, next_row, pltpu.roll(cur, ts - 1, 0))
    xx = 0.5 * (x_prev + x_next) - cur

    def mix(j):
        return (cur + xx * mu_ref[j:j + 1, :]).astype(BF16)

    r = _dot(mix(0), wrkv_ref[0])
    k = _dot(mix(1), wrkv_ref[1])
    v = _dot(mix(2), wrkv_ref[2])
    tw = jnp.tanh(_dot(mix(3), w1_ref[...]))
    ta = _dot(mix(4), a1_ref[...])
    tg = _sigmoid(_dot(mix(5), g1_ref[...]))
    lane = lax.broadcasted_iota(jnp.int32, (1, tw.shape[1]), 1)
    half = tw.shape[1] // 2
    hg = tg.shape[1] // 2

    k_k = vec_ref[4:5, :]
    k_a = vec_ref[5:6, :]
    r_k = vec_ref[6:7, :]
    kk = k * k_k
    ss = _seg_sum(kk * kk, seg_ref, segt_ref)
    kkn = kk * lax.rsqrt(jnp.maximum(ss, 1e-24))
    v_ref[...] = v.astype(v_ref.dtype)

    gb = jnp.zeros_like(cur)
    for z in range(2):
        sel = (lane < half) if z == 0 else (lane >= half)
        lw = _dot(jnp.where(sel, tw, 0.0).astype(BF16), w2_ref[...])
        la = _dot(jnp.where(sel, ta, 0.0).astype(BF16), a2_ref[...])
        g = _dot(tg[:, z * hg:(z + 1) * hg].astype(BF16), g2_ref[z])
        ld = -math.exp(-0.5) * _sigmoid(vec_ref[z:z + 1, :] + lw)
        a = _sigmoid(vec_ref[2 + z:3 + z, :] + la)
        kdir = k * (1.0 + (a - 1.0) * k_a)
        b = kkn * a
        parts = _split2(ld)
        cmz = cm_ref[z]
        lm = _dot(cmz, parts[0]) + _dot(cmz, parts[1])
        e_in = jnp.exp(lm)
        e_inv = jnp.exp(-lm)
        e_ex = jnp.exp(lm - ld)
        rt_ref[z] = (r * e_in).astype(rt_ref.dtype)
        at_ref[z] = (-kkn * e_ex).astype(at_ref.dtype)
        bt_ref[z] = (b * e_inv).astype(bt_ref.dtype)
        kt_ref[z] = (kdir * e_inv).astype(kt_ref.dtype)
        edz = edm_ref[z]
        ed = jnp.exp(_dot(edz, parts[0]) + _dot(edz, parts[1]))
        for ck in range(ed.shape[0] // 8):
            ed_ref[z, ck] = ed[ck * 8:(ck + 1) * 8]
        g_ref[z] = g.astype(g_ref.dtype)
        bonus = _seg_sum(r * kdir * r_k, seg_ref, segt_ref) * v
        gb = gb + g * bonus
    gb_ref[...] = gb


def _chunk_matrices(ts, c):
    t = np.arange(ts)
    same = (t[:, None] // c) == (t[None, :] // c)
    pos = t % c
    cm = np.zeros((2, ts, ts), np.float32)
    nck = ts // c
    edm = np.zeros((2, 8 * nck, ts), np.float32)
    tri_f = same & (t[None, :] <= t[:, None])
    sel_f = same & (pos[None, :] <= c // 2 - 1)
    cm[0] = tri_f.astype(np.float32) - sel_f.astype(np.float32)
    tri_b = same & (t[None, :] >= t[:, None])
    sel_b = same & (pos[None, :] >= c // 2)
    cm[1] = tri_b.astype(np.float32) - sel_b.astype(np.float32)
    for ck in range(nck):
        inck = (t // c) == ck
        edm[0, 8 * ck] = inck & (pos <= c // 2 - 1)
        edm[0, 8 * ck + 1] = inck & (pos > c // 2 - 1)
        edm[1, 8 * ck] = inck & (pos >= c // 2)
        edm[1, 8 * ck + 1] = inck & (pos < c // 2)
    return jnp.asarray(cm, BF16), jnp.asarray(edm, BF16)


def _seg_matrices(d, head):
    seg = np.zeros((d, 128), np.float32)
    seg[np.arange(d), np.arange(d) // head] = 1.0
    return jnp.asarray(seg, BF16), jnp.asarray(seg.T.copy(), BF16)


def _rwkv_prep(hn, bsz, p, scan_dtype, ts=256):
    t, d = hn.shape
    s = t // bsz
    ts = min(ts, s)
    ns = s // ts
    c = SCAN_CHUNK
    assert ts % c == 0 and ts % 8 == 0
    nck = ts // c
    cm, edm = _chunk_matrices(ts, c)
    seg, segt = _seg_matrices(d, RWKV_HEAD)
    r8 = ts // 8
    nblk8 = t // 8
    row = lambda b, i: (b * ns + i, 0)
    full = lambda a: pl.BlockSpec(a.shape, lambda b, i, _n=a.ndim: (0,) * _n)
    weights = [p["mu8"], p["w_rkv"], p["w1c"], p["a1c"], p["g1c"], p["w2s"], p["a2s"], p["g2"],
               p["vec8"], cm, edm, seg, segt]
    dir_spec = pl.BlockSpec((2, ts, d), lambda b, i: (0, b * ns + i, 0))
    outs = pl.pallas_call(
        functools.partial(_rwkv_prep_kernel, ts=ts, ns=ns),
        grid=(bsz, ns),
        in_specs=[
            pl.BlockSpec((ts, d), row),
            pl.BlockSpec((8, d), lambda b, i: (jnp.maximum((b * ns + i) * r8 - 1, 0), 0)),
            pl.BlockSpec((8, d), lambda b, i: (jnp.minimum((b * ns + i + 1) * r8, nblk8 - 1), 0)),
        ] + [full(a) for a in weights],
        out_specs=[dir_spec, dir_spec, dir_spec, dir_spec,
                   pl.BlockSpec((ts, d), row),
                   pl.BlockSpec((2, nck, 8, d), lambda b, i: (0, b * ns + i, 0, 0)),
                   dir_spec,
                   pl.BlockSpec((ts, d), row)],
        out_shape=[jax.ShapeDtypeStruct((2, t, d), scan_dtype)] * 4 + [
            jax.ShapeDtypeStruct((t, d), scan_dtype),
            jax.ShapeDtypeStruct((2, t // c, 8, d), F32),
            jax.ShapeDtypeStruct((2, t, d), F32),
            jax.ShapeDtypeStruct((t, d), F32)],
        compiler_params=_params(("parallel", "parallel")),
        name="rwkv_prep",
    )(hn, hn, hn, *weights)
    return outs


def _sdot(a, b, dims):
    return lax.dot_general(a.astype(SCAN_DT), b.astype(SCAN_DT), (dims, ((), ())),
                           preferred_element_type=F32)


def _scan_kernel(at_ref, rt_ref, bt_ref, kt_ref, v_ref, ed_ref, y_ref, s_ref, *, c, n_heads, per_step):
    z = pl.program_id(1)
    ci = pl.program_id(2)

    @pl.when(ci == 0)
    def _():
        s_ref[...] = jnp.zeros_like(s_ref)

    c2 = 2 * c
    pw_lanes = 2 * RWKV_HEAD
    row = lax.broadcasted_iota(jnp.int32, (c2, c2), 0)
    col = lax.broadcasted_iota(jnp.int32, (c2, c2), 1)
    same_blk = (row // c) == (col // c)
    diff = jnp.where(z == 0, row - col, col - row)
    strict = same_blk & (diff > 0)
    incl = same_blk & (diff >= 0)
    incl2 = jnp.concatenate([incl, incl], axis=1)
    lane_a = lax.broadcasted_iota(jnp.int32, (c, pw_lanes), 1) < RWKV_HEAD
    same_head = ((lax.broadcasted_iota(jnp.int32, (pw_lanes, pw_lanes), 0) // RWKV_HEAD)
                 == (lax.broadcasted_iota(jnp.int32, (pw_lanes, pw_lanes), 1) // RWKV_HEAD))
    n_steps = int(math.log2(c))
    pairs = range(n_heads // 2)
    sls = [pl.ds(hp * pw_lanes, pw_lanes) for hp in pairs]
    zero = jnp.zeros((c, pw_lanes), SCAN_DT)
    for j in range(per_step):
        jj = jnp.where(z == 0, j, per_step - 1 - j)
        rows = pl.ds(pl.multiple_of(jj * c, c), c)
        _scan_chunk(at_ref, rt_ref, bt_ref, kt_ref, v_ref, ed_ref.at[jj], y_ref, s_ref, rows, pairs, sls, zero,
                    lane_a, strict, incl2, same_head, c, c2, pw_lanes, n_steps)


def _scan_chunk(at_ref, rt_ref, bt_ref, kt_ref, v_ref, ed_ref, y_ref, s_ref, rows, pairs, sls, zero,
                lane_a, strict, incl2, same_head, c, c2, pw_lanes, n_steps):
    nt = (((1,), (1,)))
    nn = (((1,), (0,)))
    tn = (((0,), (0,)))
    lhs, rhs, vv, st = [], [], [], []
    for hp in pairs:
        a2 = at_ref[rows, sls[hp]]
        r2 = rt_ref[rows, sls[hp]]
        b2 = bt_ref[rows, sls[hp]]
        k2 = kt_ref[rows, sls[hp]]
        v2 = v_ref[rows, sls[hp]]
        lhs.append(jnp.concatenate([jnp.where(lane_a, a2, zero), jnp.where(lane_a, zero, a2),
                                    jnp.where(lane_a, r2, zero), jnp.where(lane_a, zero, r2)], axis=0))
        rhs.append(jnp.concatenate([b2, b2, k2, k2], axis=0))
        vv.append(jnp.concatenate([v2, v2], axis=0))
        st.append(s_ref[hp] * ed_ref[0:1, sls[hp]])
    gq = [_sdot(lhs[hp], jnp.concatenate([rhs[hp], st[hp].astype(SCAN_DT)], axis=0), nt) for hp in pairs]
    gm = [g[:, :2 * c2] for g in gq]
    q = [g[:, 2 * c2:] for g in gq]
    pw = [jnp.where(strict, gm[hp][:c2, :c2], 0.0).astype(SCAN_DT) for hp in pairs]
    m2 = [jnp.where(strict, gm[hp][:c2, c2:], 0.0).astype(SCAN_DT) for hp in pairs]
    m34 = [jnp.where(incl2, gm[hp][c2:, :], 0.0).astype(SCAN_DT) for hp in pairs]
    x = [q[hp][:c2] + _sdot(m2[hp], vv[hp], nn) for hp in pairs]
    for step in range(n_steps - 1):
        both = [_sdot(pw[hp], jnp.concatenate([x[hp].astype(SCAN_DT), pw[hp]], axis=1), nn) for hp in pairs]
        x = [x[hp] + both[hp][:, :pw_lanes] for hp in pairs]
        pw = [both[hp][:, pw_lanes:].astype(SCAN_DT) for hp in pairs]
    x = [x[hp] + _sdot(pw[hp], x[hp], nn) for hp in pairs]
    for hp in pairs:
        xs = x[hp].astype(SCAN_DT)
        uvs = jnp.concatenate([xs, vv[hp]], axis=0)
        ys = q[hp][c2:] + _sdot(m34[hp], uvs, nn)
        y_ref[rows, sls[hp]] = jnp.where(lane_a, ys[:c], ys[c:])
        u2 = jnp.where(lane_a, xs[:c], xs[c:])
        upd = _sdot(jnp.concatenate([u2, vv[hp][:c]], axis=0), rhs[hp][c:3 * c], tn)
        s_ref[hp] = (st[hp] + jnp.where(same_head, upd, 0.0)) * ed_ref[1:2, sls[hp]]


def _rwkv_scan(at, rt, bt, kt, v, ed, bsz):
    _, t, d = at.shape
    s = t // bsz
    c = SCAN_CHUNK
    per_step = min(SCAN_CHUNKS_PER_STEP, s // c)
    nc = s // (c * per_step)
    n_heads = d // RWKV_HEAD

    def cidx(b, z, ci):
        return b * nc + ci + z * (nc - 1 - 2 * ci)

    dspec = pl.BlockSpec((None, c * per_step, d), lambda b, z, ci: (z, cidx(b, z, ci), 0))
    return pl.pallas_call(
        functools.partial(_scan_kernel, c=c, n_heads=n_heads, per_step=per_step),
        grid=(bsz, 2, nc),
        in_specs=[dspec, dspec, dspec, dspec,
                  pl.BlockSpec((c * per_step, d), lambda b, z, ci: (cidx(b, z, ci), 0)),
                  pl.BlockSpec((None, per_step, 8, d), lambda b, z, ci: (z, cidx(b, z, ci), 0, 0))],
        out_specs=dspec,
        out_shape=jax.ShapeDtypeStruct((2, t, d), F32),
        scratch_shapes=[pltpu.VMEM((n_heads // 2, 2 * RWKV_HEAD, 2 * RWKV_HEAD), F32)],
        compiler_params=_params(("parallel", "parallel", "arbitrary")),
        name="rwkv_scan",
    )(at, rt, bt, kt, v, ed)


def _rwkv_out_kernel(y_ref, g_ref, gb_ref, x_ref, ga_ref, ln_ref, wo_ref, seg_ref, segt_ref, o_ref):
    o = gb_ref[...]
    inv_n = 1.0 / RWKV_HEAD
    for z in range(2):
        y = y_ref[z]
        mean = _seg_sum(y, seg_ref, segt_ref) * inv_n
        yc = y - mean
        var = _seg_sum(yc * yc, seg_ref, segt_ref) * inv_n
        yn = yc * lax.rsqrt(var + RWKV_GN_EPS) * ln_ref[0:1, :] + ln_ref[1:2, :]
        o = o + g_ref[z] * yn
    o_ref[...] = x_ref[...] + ga_ref[...] * _dot(o.astype(BF16), wo_ref[...])


def _rwkv_out(y, g, gb, x2, gate, ln8, w_o, bsz, tm=256):
    t, d = x2.shape
    s = t // bsz
    tm = min(tm, s)
    ns = s // tm
    seg, segt = _seg_matrices(d, RWKV_HEAD)
    row = lambda b, i: (b * ns + i, 0)
    dir_spec = pl.BlockSpec((2, tm, d), lambda b, i: (0, b * ns + i, 0))
    full = lambda a: pl.BlockSpec(a.shape, lambda b, i, _n=a.ndim: (0,) * _n)
    return pl.pallas_call(
        _rwkv_out_kernel,
        grid=(bsz, ns),
        in_specs=[dir_spec, dir_spec, pl.BlockSpec((tm, d), row), pl.BlockSpec((tm, d), row),
                  pl.BlockSpec((None, 1, d), lambda b, i: (b, 0, 0)),
                  full(ln8), full(w_o), full(seg), full(segt)],
        out_specs=pl.BlockSpec((tm, d), row),
        out_shape=jax.ShapeDtypeStruct((t, d), F32),
        compiler_params=_params(("parallel", "parallel")),
        name="rwkv_out",
    )(y, g, gb, x2, gate, ln8, w_o, seg, segt)


def _rwkv_layer(x2, bsz, hn, gate, p):
    at, rt, bt, kt, v, ed, g, gb = _rwkv_prep(hn, bsz, p, SCAN_DT)
    y = _rwkv_scan(at, rt, bt, kt, v, ed, bsz)
    return _rwkv_out(y, g, gb, x2, gate, p["ln8"], p["w_o"], bsz)


def _attn_group_kernel(slope_ref, vid_ref, q_ref, k_ref, v_ref, o_ref, lse_ref, tab_ref, *,
                       tq, sub, w, half, offs, nh, dh, hg):
    first = (pl.program_id(0) == 0) & (pl.program_id(1) == 0) & (pl.program_id(2) == 0)
    i = pl.program_id(2)

    @pl.when(first)
    def _():
        col = lax.broadcasted_iota(jnp.int32, (tq, w), 1)
        row = lax.broadcasted_iota(jnp.int32, (tq, w), 0)
        for v, off in enumerate(offs):
            dist = jnp.abs(col - row + off)
            distf = dist.astype(F32)
            for h in range(nh):
                tab_ref[h * len(offs) + v] = jnp.where(dist <= half, -slope_ref[h] * distf, NEG_INF)

    if w == sub:
        ws = 0
    else:
        ws = pl.multiple_of(jnp.clip(i * tq - half, 0, sub - w), 16)
    var = vid_ref[i]
    nt = (((1,), (1,)), ((), ()))
    lane = lax.broadcasted_iota(jnp.int32, (tq, 128), 1)
    lse = jnp.zeros((tq, 128), F32)
    for h0 in range(0, nh, hg):
        heads = range(h0, min(h0 + hg, nh))
        hs = {h: slice(h * dh, (h + 1) * dh) for h in heads}
        sc = {h: lax.dot_general(q_ref[:, hs[h]], k_ref[pl.ds(ws, w), hs[h]], nt, preferred_element_type=F32)
              + tab_ref[h * len(offs) + var] for h in heads}
        m = {h: sc[h].max(axis=-1, keepdims=True) for h in heads}
        p = {h: jnp.exp(sc[h] - m[h]) for h in heads}
        den = {h: p[h].sum(axis=-1, keepdims=True) for h in heads}
        acc = {h: _dot(p[h].astype(BF16), v_ref[pl.ds(ws, w), hs[h]]) for h in heads}
        for h in heads:
            o_ref[:, hs[h]] = (acc[h] / den[h]).astype(o_ref.dtype)
            lse = jnp.where(lane == h, m[h] + jnp.log(den[h]), lse)
    lse_ref[...] = lse


def _alibi_slopes(n):
    return 2.0 ** (-8.0 * jnp.arange(1, n + 1, dtype=F32) / n)


def _attention_group(qkv, bsz, gi, tq=256):
    t, width = qkv.shape
    s = t // bsz
    window, dil = ATTN_GROUPS[gi]
    half = window // (2 * dil)
    sub = s // dil
    tq = min(tq, sub)
    nq = sub // tq
    w = min(sub, tq + 2 * half)
    nh, dh = ATTN_HEADS, ATTN_HEAD_DIM
    hw = nh * dh
    offs = sorted({int(np.clip(i * tq - half, 0, sub - w)) - i * tq for i in range(nq)})
    vid = jnp.asarray([offs.index(int(np.clip(i * tq - half, 0, sub - w)) - i * tq) for i in range(nq)], jnp.int32)
    slopes = _alibi_slopes(len(ATTN_GROUPS) * nh).reshape(len(ATTN_GROUPS), nh)[gi] * dil
    smem = pl.BlockSpec(memory_space=pltpu.SMEM)
    qrow = lambda b, r, i: ((b * dil + r) * nq + i, 0)
    return pl.pallas_call(
        functools.partial(_attn_group_kernel, tq=tq, sub=sub, w=w, half=half, offs=tuple(offs), nh=nh, dh=dh,
                          hg=nh if tq * w <= 128 * 128 else 4),
        grid=(bsz, dil, nq),
        in_specs=[smem, smem,
                  pl.BlockSpec((tq, hw), qrow),
                  pl.BlockSpec((sub, hw), lambda b, r, i: (b * dil + r, 1)),
                  pl.BlockSpec((sub, hw), lambda b, r, i: (b * dil + r, 2))],
        out_specs=[pl.BlockSpec((tq, hw), qrow), pl.BlockSpec((tq, 128), qrow)],
        out_shape=[jax.ShapeDtypeStruct((t, hw), BF16), jax.ShapeDtypeStruct((t, 128), F32)],
        scratch_shapes=[pltpu.VMEM((nh * len(offs), tq, w), F32)],
        compiler_params=_params(("arbitrary", "arbitrary", "arbitrary")),
        name=f"dilated_attention_g{gi}",
    )(slopes, vid, qkv, qkv, qkv)


def _to_token_order(blk_ref, scr_ref, dil):
    if dil == 1:
        return blk_ref[0].astype(F32)
    n = blk_ref.shape[1]
    slabs = scr_ref.shape[0]
    for r in range(dil):
        rows = blk_ref[r].astype(F32)
        for c in range(slabs):
            scr_ref[c, pl.ds(r, n, stride=dil), :] = rows[:, c * 128:(c + 1) * 128]
    if slabs == 1:
        return scr_ref[0]
    return jnp.concatenate([scr_ref[c] for c in range(slabs)], axis=1)


def _attn_out_kernel(o1, o2, o3, l1, l2, l3, ex_ref, w_ref, x_ref, ga_ref, out_ref, so_ref, sl_ref, *, dils):
    ls = [_to_token_order(l, sl_ref, dil) for l, dil in zip((l1, l2, l3), dils)]
    m = jnp.maximum(jnp.maximum(ls[0], ls[1]), ls[2])
    es = [jnp.exp(l - m) for l in ls]
    inv = 1.0 / (es[0] + es[1] + es[2])
    merged = None
    for e, o_ref, dil in zip(es, (o1, o2, o3), dils):
        ah, al = _split2(e * inv)
        alpha = _dot(ah, ex_ref[...]) + _dot(al, ex_ref[...])
        term = alpha * _to_token_order(o_ref, so_ref, dil)
        merged = term if merged is None else merged + term
    out_ref[...] = x_ref[...] + ga_ref[...] * _dot(merged.astype(BF16), w_ref[...])


def _attn_out(os, lses, w_o, x2, gate, bsz, tm=512):
    t, d = x2.shape
    s = t // bsz
    tm = min(tm, s)
    ns = s // tm
    hw = os[0].shape[1]
    dils = tuple(dil for _, dil in ATTN_GROUPS)
    ex = np.zeros((128, hw), np.float32)
    ex[np.arange(hw) // ATTN_HEAD_DIM, np.arange(hw)] = 1.0
    ex = jnp.asarray(ex, BF16)
    row = lambda b, i: (b * ns + i, 0)

    def res_major(a, dil):
        c = a.shape[1]
        return a.reshape(bsz, dil, s // dil, c), pl.BlockSpec((None, dil, tm // dil, c), lambda b, i: (b, 0, i, 0))

    o_args, o_specs = zip(*[res_major(o, dil) for o, dil in zip(os, dils)])
    l_args, l_specs = zip(*[res_major(l, dil) for l, dil in zip(lses, dils)])
    return pl.pallas_call(
        functools.partial(_attn_out_kernel, dils=dils),
        grid=(bsz, ns),
        in_specs=list(o_specs) + list(l_specs) + [
            pl.BlockSpec(ex.shape, lambda b, i: (0, 0)),
            pl.BlockSpec(w_o.shape, lambda b, i: (0, 0)),
            pl.BlockSpec((tm, d), row),
            pl.BlockSpec((None, 1, d), lambda b, i: (b, 0, 0))],
        out_specs=pl.BlockSpec((tm, d), row),
        out_shape=jax.ShapeDtypeStruct((t, d), F32),
        scratch_shapes=[pltpu.VMEM((hw // 128, tm, 128), F32), pltpu.VMEM((1, tm, 128), F32)],
        compiler_params=_params(("parallel", "parallel")),
        name="attn_out",
    )(*o_args, *l_args, ex, w_o, x2, gate)


def _norm_attn_kernel(x_ref, g_ref, sh_ref, sc_ref, *rest, dils):
    outs, scr_ref = rest[:-1], rest[-1]
    x = x_ref[...]
    ms = jnp.mean(x * x, axis=-1, keepdims=True)
    y = x * lax.rsqrt(ms + RMS_EPS) * g_ref[...]
    y = y * (1.0 + sc_ref[...]) + sh_ref[...]
    slabs = scr_ref.shape[0]
    if any(dil > 1 for dil in dils):
        for c in range(slabs):
            scr_ref[c] = y[:, c * 128:(c + 1) * 128]
    for o, dil in zip(outs, dils):
        if dil == 1:
            o[0] = y.astype(o.dtype)
        else:
            n = o.shape[1]
            for r in range(dil):
                for c in range(slabs):
                    o[r, :, c * 128:(c + 1) * 128] = scr_ref[c, pl.ds(r, n, stride=dil), :].astype(o.dtype)


def _norm_mod_attn(x2, g, shift, scale, bsz, ts=512):
    t, d = x2.shape
    s = t // bsz
    ts = min(ts, s)
    ns = s // ts
    dils = tuple(dil for _, dil in ATTN_GROUPS)
    vec = pl.BlockSpec((None, 1, d), lambda b, i: (b, 0, 0))
    outs = pl.pallas_call(
        functools.partial(_norm_attn_kernel, dils=dils),
        grid=(bsz, ns),
        in_specs=[pl.BlockSpec((ts, d), lambda b, i: (b * ns + i, 0)),
                  pl.BlockSpec((1, d), lambda b, i: (0, 0)), vec, vec],
        out_specs=[pl.BlockSpec((None, dil, ts // dil, d), lambda b, i: (b, 0, i, 0)) for dil in dils],
        out_shape=[jax.ShapeDtypeStruct((bsz, dil, s // dil, d), BF16) for dil in dils],
        scratch_shapes=[pltpu.VMEM((d // 128, ts, 128), F32)],
        compiler_params=_params(("parallel", "parallel")),
        name="norm_mod_attn",
    )(x2, g.reshape(1, d), shift, scale)
    return [o.reshape(t, d) for o in outs]


def _pack_attn(j, w_qkv, w_o):
    d = w_qkv.shape[1]
    ng = len(ATTN_GROUPS)
    w = w_qkv[j].reshape(d, ng, 3, ATTN_HEADS * ATTN_HEAD_DIM)
    w = w.at[:, :, 0].multiply(ATTN_HEAD_DIM ** -0.5)
    return {"w_qkv": w.transpose(1, 0, 2, 3).reshape(ng, d, -1).astype(BF16), "w_o": w_o[j].astype(BF16)}


def _attn_layer(x2, bsz, hns, gate, p):
    outs = [_attention_group(_matmul(hn, p["w_qkv"][gi], BF16), bsz, gi) for gi, hn in enumerate(hns)]
    return _attn_out([o for o, _ in outs], [l for _, l in outs], p["w_o"], x2, gate, bsz)


def _router_kernel(x_ref, g_ref, sh_ref, sc_ref, wh_ref, wl_ref, b_ref, o_ref, hb_ref, seg_ref):
    x = x_ref[...]
    ms = jnp.mean(x * x, axis=-1, keepdims=True)
    hn = x * lax.rsqrt(ms + RMS_EPS) * g_ref[...]
    hn = hn * (1.0 + sc_ref[...]) + sh_ref[...]
    hb_ref[...] = hn.astype(hb_ref.dtype)
    hh, hl = _split2(hn)
    logits = _dot(hh, wh_ref[...]) + _dot(hl, wh_ref[...]) + _dot(hh, wl_ref[...]) + b_ref[...]
    lane = lax.broadcasted_iota(jnp.int32, logits.shape, 1)
    ng, ne = N_EXPERT_GROUPS, EXPERTS_PER_GROUP
    big = jnp.int32(ROUTER_LANES)
    is_g = lane < ng
    gl = jnp.where(is_g, logits, -jnp.inf)
    gmax = gl.max(axis=-1, keepdims=True)
    gsum = jnp.where(is_g, jnp.exp(logits - gmax), 0.0).sum(axis=-1, keepdims=True)
    p_group = 1.0 / gsum
    g_top = jnp.where(gl == gmax, lane, big).min(axis=-1, keepdims=True)
    lo = ng + ne * g_top
    es = jnp.where((lane >= lo) & (lane < lo + ne), logits, -jnp.inf)
    v1 = es.max(axis=-1, keepdims=True)
    i1 = jnp.where(es == v1, lane, big).min(axis=-1, keepdims=True)
    es2 = jnp.where(lane == i1, -jnp.inf, es)
    v2 = es2.max(axis=-1, keepdims=True)
    i2 = jnp.where(es2 == v2, lane, big).min(axis=-1, keepdims=True)
    e2 = jnp.exp(v2 - v1)
    w1 = p_group / (1.0 + e2)
    w2 = p_group * e2 / (1.0 + e2)
    gates = jnp.where(lane == i1, w1, 0.0) + jnp.where(lane == i2, w2, 0.0)

    tm = x.shape[0]
    onehot = jnp.where(lane == g_top, 1.0, 0.0)
    tri = jnp.where(lax.broadcasted_iota(jnp.int32, (tm, tm), 1) < lax.broadcasted_iota(jnp.int32, (tm, tm), 0),
                    1.0, 0.0).astype(BF16)
    before = _dot(tri, onehot.astype(BF16))
    counts = jnp.broadcast_to(onehot.sum(axis=0, keepdims=True), (8, ROUTER_LANES))
    upper = jnp.where(lax.broadcasted_iota(jnp.int32, (ROUTER_LANES, ROUTER_LANES), 0)
                      < lax.broadcasted_iota(jnp.int32, (ROUTER_LANES, ROUTER_LANES), 1), 1.0, 0.0).astype(BF16)
    ch, cl = _split2(counts)
    seg_lo = _dot(ch, upper) + _dot(cl, upper)
    dest = (onehot * (before + seg_lo[0:1])).sum(axis=-1, keepdims=True)
    o_ref[...] = gates + jnp.where(lane == DEST_LANE, dest, 0.0)
    row8 = lax.broadcasted_iota(jnp.int32, (8, ROUTER_LANES), 0)
    seg_ref[...] = jnp.where(row8 == 0, seg_lo, jnp.where(row8 == 1, seg_lo + counts, 0.0))


def _router(x2, g, shift, scale, wr_hi, wr_lo, bias, bsz):
    t, d = x2.shape
    s = t // bsz
    tm = min(MOE_TILE, s)
    ns = s // tm
    row = lambda b, i: (b * ns + i, 0)
    full = lambda a: pl.BlockSpec(a.shape, lambda b, i: (0, 0))
    vec = pl.BlockSpec((None, 1, d), lambda b, i: (b, 0, 0))
    g = g.reshape(1, d)
    return pl.pallas_call(
        _router_kernel,
        grid=(bsz, ns),
        in_specs=[pl.BlockSpec((tm, d), row), full(g), vec, vec, full(wr_hi), full(wr_lo), full(bias)],
        out_specs=[pl.BlockSpec((tm, ROUTER_LANES), row), pl.BlockSpec((tm, d), row),
                   pl.BlockSpec((None, 8, ROUTER_LANES), lambda b, i: (b * ns + i, 0, 0))],
        out_shape=[jax.ShapeDtypeStruct((t, ROUTER_LANES), F32), jax.ShapeDtypeStruct((t, d), BF16),
                   jax.ShapeDtypeStruct((t // tm, 8, ROUTER_LANES), F32)],
        compiler_params=_params(("parallel", "parallel")),
        name="moe_router",
    )(x2, g, shift, scale, wr_hi, wr_lo, bias)


def _experts_kernel(seg_ref, h_ref, gates_ref, wg_ref, wu_ref, wd_ref, x_ref, ga_ref, o_ref,
                    pt_ref, xs_ref, gs_ref, y_ref, *, per_step, blk):
    tile = pl.program_id(0)
    step = pl.program_id(1)
    tm = h_ref.shape[0]
    grp = step // (EXPERTS_PER_GROUP // per_step)
    tn = (((0,), (0,)), ((), ()))

    @pl.when(step == 0)
    def _():
        gates = gates_ref[...]
        lane = lax.broadcasted_iota(jnp.int32, gates.shape, 1)
        dest = jnp.where(lane == DEST_LANE, gates, 0.0).sum(axis=-1, keepdims=True)
        col = lax.broadcasted_iota(jnp.int32, (tm, tm), 1).astype(F32)
        pt = jnp.where(dest == col, 1.0, 0.0).astype(BF16)
        pt_ref[...] = pt
        xs_ref[...] = lax.dot_general(pt, h_ref[...], tn, preferred_element_type=F32).astype(BF16)
        g1 = gates.astype(BF16)
        r1 = gates - g1.astype(F32)
        g2 = r1.astype(BF16)
        g3 = (r1 - g2.astype(F32)).astype(BF16)
        gs_ref[...] = (lax.dot_general(pt, g1, tn, preferred_element_type=F32)
                       + lax.dot_general(pt, g2, tn, preferred_element_type=F32)
                       + lax.dot_general(pt, g3, tn, preferred_element_type=F32))
        y_ref[...] = jnp.zeros_like(y_ref)

    lo = seg_ref[tile * 8 + grp]
    hi = seg_ref[tile * 8 + N_EXPERT_GROUPS + grp]
    for b in range(tm // blk):
        @pl.when((lo < (b + 1) * blk) & (hi > b * blk))
        def _(b=b):
            rows = pl.ds(b * blk, blk)
            h = xs_ref[rows, :]
            gsb = gs_ref[rows, :]
            lane = lax.broadcasted_iota(jnp.int32, gsb.shape, 1)
            hids = []
            for j in range(per_step):
                e = step * per_step + j
                ge = jnp.where(lane == N_EXPERT_GROUPS + e, gsb, 0.0).sum(axis=-1, keepdims=True)
                gp = _dot(h, wg_ref[j])
                up = _dot(h, wu_ref[j])
                hids.append((gp * _sigmoid(gp) * up * ge).astype(BF16))
            y_ref[rows, :] += _dot(jnp.concatenate(hids, axis=1), wd_ref[...])

    @pl.when(step == pl.num_programs(1) - 1)
    def _():
        yh, yl = _split2(y_ref[...])
        pt = pt_ref[...]
        o_ref[...] = x_ref[...] + ga_ref[...] * (_dot(pt, yh) + _dot(pt, yl))


def _experts(hn_bf16, gates, seg, wg, wu, wd, x2, gate, bsz, per_step=4):
    t, d = x2.shape
    s = t // bsz
    tm = min(MOE_TILE, s)
    ns = s // tm
    ne, _, f = wg.shape
    blk = min(MOE_BLOCK, tm)
    row = lambda i, e, seg: (i, 0)
    wspec = pl.BlockSpec((per_step, d, f), lambda i, e, seg: (e, 0, 0))
    grid_spec = pltpu.PrefetchScalarGridSpec(
        num_scalar_prefetch=1,
        grid=(t // tm, ne // per_step),
        in_specs=[
            pl.BlockSpec((tm, d), row),
            pl.BlockSpec((tm, ROUTER_LANES), row),
            wspec, wspec,
            pl.BlockSpec((per_step * f, d), lambda i, e, seg: (e, 0)),
            pl.BlockSpec((tm, d), row),
            pl.BlockSpec((None, 1, d), lambda i, e, seg: (i // ns, 0, 0)),
        ],
        out_specs=pl.BlockSpec((tm, d), row),
        scratch_shapes=[pltpu.VMEM((tm, tm), BF16), pltpu.VMEM((tm, d), BF16),
                        pltpu.VMEM((tm, ROUTER_LANES), F32), pltpu.VMEM((tm, d), F32)],
    )
    return pl.pallas_call(
        functools.partial(_experts_kernel, per_step=per_step, blk=blk),
        grid_spec=grid_spec,
        out_shape=jax.ShapeDtypeStruct((t, d), F32),
        compiler_params=_params(("parallel", "arbitrary")),
        name="moe_experts",
    )(seg, hn_bf16, gates, wg, wu, wd, x2, gate)


def _moe_layer(x2, bsz, g, shift, scale, gate, p):
    gates, hn_bf16, seg = _router(x2, g, shift, scale, p["wr_hi"], p["wr_lo"], p["r_bias"], bsz)
    ng = N_EXPERT_GROUPS
    seg = jnp.concatenate([seg[:, 0, :ng], seg[:, 1, :ng]], axis=1).astype(jnp.int32).reshape(-1)
    return _experts(hn_bf16, gates, seg, p["wg"], p["wu"], p["wd"], x2, gate, bsz)


def _pad_rows(rows, d):
    out = jnp.zeros((8, d), F32)
    return out.at[:len(rows)].set(jnp.stack([r.reshape(d).astype(F32) for r in rows]))


def _pack_rwkv(j, mu, w_rkv, w0, w1, w2, a0, a1, a2, g1, g2, k_k, k_a, r_k, ln_w, ln_b, w_o):
    d = w_o.shape[-1]
    cat = lambda m: jnp.concatenate([m[j, 0], m[j, 1]], axis=1).astype(BF16)
    stack = lambda m: jnp.concatenate([m[j, 0], m[j, 1]], axis=0).astype(BF16)
    return {
        "mu8": _pad_rows(list(mu[j]), d),
        "w_rkv": w_rkv[j].astype(BF16),
        "w1c": cat(w1), "a1c": cat(a1), "g1c": cat(g1),
        "w2s": stack(w2), "a2s": stack(a2), "g2": g2[j].astype(BF16),
        "vec8": _pad_rows([w0[j, 0], w0[j, 1], a0[j, 0], a0[j, 1], k_k[j], k_a[j], r_k[j]], d),
        "ln8": _pad_rows([ln_w[j], ln_b[j]], d),
        "w_o": w_o[j].astype(BF16),
    }


def _pack_moe(i, router_g, router_g_b, router_e, router_e_b, w_gate, w_up, w_down):
    d = router_g.shape[1]
    ng, ne = N_EXPERT_GROUPS, EXPERTS_PER_GROUP
    wr = jnp.zeros((d, ROUTER_LANES), F32)
    wr = wr.at[:, :ng].set(router_g[i])
    wr = wr.at[:, ng:ng + ng * ne].set(router_e[i].transpose(1, 0, 2).reshape(d, ng * ne))
    bias = jnp.zeros((1, ROUTER_LANES), F32)
    bias = bias.at[0, :ng].set(router_g_b[i])
    bias = bias.at[0, ng:ng + ng * ne].set(router_e_b[i].reshape(ng * ne))
    wr_hi = wr.astype(BF16)
    wr_lo = (wr - wr_hi.astype(F32)).astype(BF16)
    f = w_gate.shape[-1]
    return {"wr_hi": wr_hi, "wr_lo": wr_lo, "r_bias": bias,
            "wg": w_gate[i].reshape(ng * ne, d, f).astype(BF16), "wu": w_up[i].reshape(ng * ne, d, f).astype(BF16),
            "wd": w_down[i].reshape(ng * ne * f, d).astype(BF16)}


def kernel(x, c, ada_w, ada_b, norm_tm_g, norm_cm_g, rw_mu, rw_w_rkv, rw_w0, rw_w1, rw_w2, rw_a0, rw_a1, rw_a2, rw_g1, rw_g2, rw_k_k, rw_k_a, rw_r_k, rw_ln_w, rw_ln_b, rw_w_o, at_w_qkv, at_w_o, moe_router_g, moe_router_g_b, moe_router_e, moe_router_e_b, moe_w_gate, moe_w_up, moe_w_down, final_g):
    bsz, s, d = x.shape
    depth = ada_w.shape[0]
    x2 = x.reshape(bsz * s, d)
    mod = _ada_mod(c, ada_w, ada_b)
    for i in range(depth):
        sh_t, sc_t, ga_t, sh_c, sc_c, ga_c = (mod[i, m] for m in range(N_MOD))
        j = i // 2
        if i % 2 == 0:
            (hn,) = _norm_mod(x2, norm_tm_g[i], sh_t, sc_t, bsz, (F32,))
            p = _pack_rwkv(j, rw_mu, rw_w_rkv, rw_w0, rw_w1, rw_w2, rw_a0, rw_a1, rw_a2, rw_g1, rw_g2,
                           rw_k_k, rw_k_a, rw_r_k, rw_ln_w, rw_ln_b, rw_w_o)
            x2 = _rwkv_layer(x2, bsz, hn, ga_t, p)
        else:
            hns = _norm_mod_attn(x2, norm_tm_g[i], sh_t, sc_t, bsz)
            p = _pack_attn(j, at_w_qkv, at_w_o)
            x2 = _attn_layer(x2, bsz, hns, ga_t, p)
        pm = _pack_moe(i, moe_router_g, moe_router_g_b, moe_router_e, moe_router_e_b,
                       moe_w_gate, moe_w_up, moe_w_down)
        x2 = _moe_layer(x2, bsz, norm_cm_g[i], sh_c, sc_c, ga_c, pm)
    (out,) = _norm_mod(x2, final_g, None, None, bsz, (F32,))
    return out.reshape(bsz, s, d)
```

```python
import functools
import math

import numpy as np
import jax
import jax.numpy as jnp
from jax import lax
from jax.experimental import pallas as pl
from jax.experimental.pallas import tpu as pltpu

F32 = jnp.float32
BF16 = jnp.bfloat16

RMS_EPS = 1e-6
N_MOD = 6
RWKV_HEAD = 64
RWKV_GN_EPS = 64e-5
ATTN_GROUPS = ((128, 1), (512, 4), (2048, 16))
ATTN_HEADS = 8
ATTN_HEAD_DIM = 128
NEG_INF = -1e30
N_EXPERT_GROUPS = 4
EXPERTS_PER_GROUP = 8
N_EXPERTS = N_EXPERT_GROUPS * EXPERTS_PER_GROUP
ROUTER_LANES = 128
DEST_LANE = 64
MOE_TILE = 1024
MOE_BLOCK = 128

SCAN_CHUNK = 64
SCAN_CHUNKS_PER_STEP = 4
V7X_VMEM_LIMIT = 56 * 1024 * 1024
SCAN_DT = BF16


def _params(sem):
    return pltpu.CompilerParams(dimension_semantics=sem, vmem_limit_bytes=V7X_VMEM_LIMIT)


def _dot(a, b):
    return jnp.dot(a, b, preferred_element_type=F32)


def _sigmoid(x):
    return 0.5 * jnp.tanh(0.5 * x) + 0.5


def _split2(x):
    hi = x.astype(BF16)
    lo = (x - hi.astype(F32)).astype(BF16)
    return hi, lo


def _ada_kernel(c_ref, w_ref, b_ref, o_ref):
    c = c_ref[...]
    sc = c * _sigmoid(c)
    o_ref[...] = _dot(sc.astype(BF16), w_ref[...].astype(BF16)) + b_ref[...]


def _ada_mod(c, ada_w, ada_b):
    depth, d, n = ada_w.shape
    bsz = c.shape[0]
    tn = n // 4
    out = pl.pallas_call(
        _ada_kernel,
        grid=(depth, n // tn),
        in_specs=[
            pl.BlockSpec((bsz, d), lambda i, j: (0, 0)),
            pl.BlockSpec((None, d, tn), lambda i, j: (i, 0, j)),
            pl.BlockSpec((None, 1, tn), lambda i, j: (i, 0, j)),
        ],
        out_specs=pl.BlockSpec((None, bsz, tn), lambda i, j: (i, 0, j)),
        out_shape=jax.ShapeDtypeStruct((depth, bsz, n), F32),
        compiler_params=_params(("parallel", "parallel")),
        name="ada_mod",
    )(c, ada_w, ada_b.reshape(depth, 1, n))
    return out.reshape(depth, bsz, N_MOD, 1, d).transpose(0, 2, 1, 3, 4)


def _norm_kernel(*refs, modulated, n_out):
    if modulated:
        x_ref, g_ref, sh_ref, sc_ref = refs[:4]
    else:
        x_ref, g_ref = refs[:2]
    outs = refs[-n_out:]
    x = x_ref[...]
    ms = jnp.mean(x * x, axis=-1, keepdims=True)
    y = x * lax.rsqrt(ms + RMS_EPS) * g_ref[...]
    if modulated:
        y = y * (1.0 + sc_ref[...]) + sh_ref[...]
    for o in outs:
        o[...] = y.astype(o.dtype)


def _norm_mod(x2, g, shift, scale, bsz, out_dtypes, ts=512):
    t, d = x2.shape
    s = t // bsz
    ts = min(ts, s)
    ns = s // ts
    modulated = shift is not None
    row_spec = pl.BlockSpec((ts, d), lambda b, i: (b * ns + i, 0))
    in_specs = [row_spec, pl.BlockSpec((1, d), lambda b, i: (0, 0))]
    args = [x2, g.reshape(1, d)]
    if modulated:
        vec = pl.BlockSpec((None, 1, d), lambda b, i: (b, 0, 0))
        in_specs += [vec, vec]
        args += [shift, scale]
    outs = pl.pallas_call(
        functools.partial(_norm_kernel, modulated=modulated, n_out=len(out_dtypes)),
        grid=(bsz, ns),
        in_specs=in_specs,
        out_specs=[row_spec] * len(out_dtypes),
        out_shape=[jax.ShapeDtypeStruct((t, d), dt) for dt in out_dtypes],
        compiler_params=_params(("parallel", "parallel")),
        name="norm_mod",
    )(*args)
    return outs


def _proj_res_kernel(a_ref, w_ref, x_ref, ga_ref, o_ref):
    o_ref[...] = x_ref[...] + ga_ref[...] * _dot(a_ref[...], w_ref[...])


def _proj_residual(a, w, x2, gate, bsz, tm=512):
    t, d = x2.shape
    s = t // bsz
    tm = min(tm, s)
    ns = s // tm
    row = lambda b, i: (b * ns + i, 0)
    return pl.pallas_call(
        _proj_res_kernel,
        grid=(bsz, ns),
        in_specs=[
            pl.BlockSpec((tm, a.shape[1]), row),
            pl.BlockSpec(w.shape, lambda b, i: (0, 0)),
            pl.BlockSpec((tm, d), row),
            pl.BlockSpec((None, 1, d), lambda b, i: (b, 0, 0)),
        ],
        out_specs=pl.BlockSpec((tm, d), row),
        out_shape=jax.ShapeDtypeStruct((t, d), F32),
        compiler_params=_params(("parallel", "parallel")),
        name="proj_residual",
    )(a, w, x2, gate)


def _mm_kernel(a_ref, b_ref, o_ref):
    o_ref[...] = _dot(a_ref[...], b_ref[...]).astype(o_ref.dtype)


def _matmul(a, b, out_dtype, tm=1024, tn=1024):
    m, k = a.shape
    n = b.shape[1]
    tm, tn = min(tm, m), min(tn, n)
    return pl.pallas_call(
        _mm_kernel,
        grid=(n // tn, m // tm),
        in_specs=[
            pl.BlockSpec((tm, k), lambda j, i: (i, 0)),
            pl.BlockSpec((k, tn), lambda j, i: (0, j)),
        ],
        out_specs=pl.BlockSpec((tm, tn), lambda j, i: (i, j)),
        out_shape=jax.ShapeDtypeStruct((m, n), out_dtype),
        compiler_params=_params(("parallel", "parallel")),
        name="matmul",
    )(a, b)


def _seg_sum(x, seg_ref, segt_ref):
    xh, xl = _split2(x)
    s = _dot(xh, seg_ref[...]) + _dot(xl, seg_ref[...])
    sh, sl = _split2(s)
    return _dot(sh, segt_ref[...]) + _dot(sl, segt_ref[...])


def _rwkv_prep_kernel(hn_ref, prev_ref, next_ref, mu_ref, wrkv_ref, w1_ref, a1_ref, g1_ref,
                      w2_ref, a2_ref, g2_ref, vec_ref, cm_ref, edm_ref, seg_ref, segt_ref,
                      at_ref, rt_ref, bt_ref, kt_ref, v_ref, ed_ref, g_ref, gb_ref, *, ts, ns):
    i = pl.program_id(1)
    cur = hn_ref[...]
    row = lax.broadcasted_iota(jnp.int32, (ts, 1), 0)
    prev_row = jnp.where(i > 0, prev_ref[7:8, :], 0.0)
    next_row = jnp.where(i < ns - 1, next_ref[0:1, :], 0.0)
    x_prev = jnp.where(row == 0, prev_row, pltpu.roll(cur, 1, 0))
    x_next = jnp.where(row == ts - 1, next_row, pltpu.roll(cur, ts - 1, 0))
    xx = 0.5 * (x_prev + x_next) - cur

    def mix(j):
        return (cur + xx * mu_ref[j:j + 1, :]).astype(BF16)

    r = _dot(mix(0), wrkv_ref[0])
    k = _dot(mix(1), wrkv_ref[1])
    v = _dot(mix(2), wrkv_ref[2])
    tw = jnp.tanh(_dot(mix(3), w1_ref[...]))
    ta = _dot(mix(4), a1_ref[...])
    tg = _sigmoid(_dot(mix(5), g1_ref[...]))
    lane = lax.broadcasted_iota(jnp.int32, (1, tw.shape[1]), 1)
    half = tw.shape[1] // 2
    hg = tg.shape[1] // 2

    k_k = vec_ref[4:5, :]
    k_a = vec_ref[5:6, :]
    r_k = vec_ref[6:7, :]
    kk = k * k_k
    ss = _seg_sum(kk * kk, seg_ref, segt_ref)
    kkn = kk * lax.rsqrt(jnp.maximum(ss, 1e-24))
    v_ref[...] = v.astype(v_ref.dtype)

    gb = jnp.zeros_like(cur)
    for z in range(2):
        sel = (lane < half) if z == 0 else (lane >= half)
        lw = _dot(jnp.where(sel, tw, 0.0).astype(BF16), w2_ref[...])
        la = _dot(jnp.where(sel, ta, 0.0).astype(BF16), a2_ref[...])
        g = _dot(tg[:, z * hg:(z + 1) * hg].astype(BF16), g2_ref[z])
        ld = -math.exp(-0.5) * _sigmoid(vec_ref[z:z + 1, :] + lw)
        a = _sigmoid(vec_ref[2 + z:3 + z, :] + la)
        kdir = k * (1.0 + (a - 1.0) * k_a)
        b = kkn * a
        parts = _split2(ld)
        cmz = cm_ref[z]
        lm = _dot(cmz, parts[0]) + _dot(cmz, parts[1])
        e_in = jnp.exp(lm)
        e_inv = jnp.exp(-lm)
        e_ex = jnp.exp(lm - ld)
        rt_ref[z] = (r * e_in).astype(rt_ref.dtype)
        at_ref[z] = (-kkn * e_ex).astype(at_ref.dtype)
        bt_ref[z] = (b * e_inv).astype(bt_ref.dtype)
        kt_ref[z] = (kdir * e_inv).astype(kt_ref.dtype)
        edz = edm_ref[z]
        ed = jnp.exp(_dot(edz, parts[0]) + _dot(edz, parts[1]))
        for ck in range(ed.shape[0] // 8):
            ed_ref[z, ck] = ed[ck * 8:(ck + 1) * 8]
        g_ref[z] = g.astype(g_ref.dtype)
        bonus = _seg_sum(r * kdir * r_k, seg_ref, segt_ref) * v
        gb = gb + g * bonus
    gb_ref[...] = gb


def _chunk_matrices(ts, c):
    t = np.arange(ts)
    same = (t[:, None] // c) == (t[None, :] // c)
    pos = t % c
    cm = np.zeros((2, ts, ts), np.float32)
    nck = ts // c
    edm = np.zeros((2, 8 * nck, ts), np.float32)
    tri_f = same & (t[None, :] <= t[:, None])
    sel_f = same & (pos[None, :] <= c // 2 - 1)
    cm[0] = tri_f.astype(np.float32) - sel_f.astype(np.float32)
    tri_b = same & (t[None, :] >= t[:, None])
    sel_b = same & (pos[None, :] >= c // 2)
    cm[1] = tri_b.astype(np.float32) - sel_b.astype(np.float32)
    for ck in range(nck):
        inck = (t // c) == ck
        edm[0, 8 * ck] = inck & (pos <= c // 2 - 1)
        edm[0, 8 * ck + 1] = inck & (pos > c // 2 - 1)
        edm[1, 8 * ck] = inck & (pos >= c // 2)
        edm[1, 8 * ck + 1] = inck & (pos < c // 2)
    return jnp.asarray(cm, BF16), jnp.asarray(edm, BF16)


def _seg_matrices(d, head):
    seg = np.zeros((d, 128), np.float32)
    seg[np.arange(d), np.arange(d) // head] = 1.0
    return jnp.asarray(seg, BF16), jnp.asarray(seg.T.copy(), BF16)


def _rwkv_prep(hn, bsz, p, scan_dtype, ts=256):
    t, d = hn.shape
    s = t // bsz
    ts = min(ts, s)
    ns = s // ts
    c = SCAN_CHUNK
    assert ts % c == 0 and ts % 8 == 0
    nck = ts // c
    cm, edm = _chunk_matrices(ts, c)
    seg, segt = _seg_matrices(d, RWKV_HEAD)
    r8 = ts // 8
    nblk8 = t // 8
    row = lambda b, i: (b * ns + i, 0)
    full = lambda a: pl.BlockSpec(a.shape, lambda b, i, _n=a.ndim: (0,) * _n)
    weights = [p["mu8"], p["w_rkv"], p["w1c"], p["a1c"], p["g1c"], p["w2s"], p["a2s"], p["g2"],
               p["vec8"], cm, edm, seg, segt]
    dir_spec = pl.BlockSpec((2, ts, d), lambda b, i: (0, b * ns + i, 0))
    outs = pl.pallas_call(
        functools.partial(_rwkv_prep_kernel, ts=ts, ns=ns),
        grid=(bsz, ns),
        in_specs=[
            pl.BlockSpec((ts, d), row),
            pl.BlockSpec((8, d), lambda b, i: (jnp.maximum((b * ns + i) * r8 - 1, 0), 0)),
            pl.BlockSpec((8, d), lambda b, i: (jnp.minimum((b * ns + i + 1) * r8, nblk8 - 1), 0)),
        ] + [full(a) for a in weights],
        out_specs=[dir_spec, dir_spec, dir_spec, dir_spec,
                   pl.BlockSpec((ts, d), row),
                   pl.BlockSpec((2, nck, 8, d), lambda b, i: (0, b * ns + i, 0, 0)),
                   dir_spec,
                   pl.BlockSpec((ts, d), row)],
        out_shape=[jax.ShapeDtypeStruct((2, t, d), scan_dtype)] * 4 + [
            jax.ShapeDtypeStruct((t, d), scan_dtype),
            jax.ShapeDtypeStruct((2, t // c, 8, d), F32),
            jax.ShapeDtypeStruct((2, t, d), F32),
            jax.ShapeDtypeStruct((t, d), F32)],
        compiler_params=_params(("parallel", "parallel")),
        name="rwkv_prep",
    )(hn, hn, hn, *weights)
    return outs


def _sdot(a, b, dims):
    return lax.dot_general(a.astype(SCAN_DT), b.astype(SCAN_DT), (dims, ((), ())),
                           preferred_element_type=F32)


def _scan_kernel(at_ref, rt_ref, bt_ref, kt_ref, v_ref, ed_ref, g_ref, ln_ref, y_ref, s_ref, *,
                 c, n_heads, per_step):
    z = pl.program_id(1)
    ci = pl.program_id(2)

    @pl.when(ci == 0)
    def _():
        s_ref[...] = jnp.zeros_like(s_ref)

    c2 = 2 * c
    pw_lanes = 2 * RWKV_HEAD
    row = lax.broadcasted_iota(jnp.int32, (c2, c2), 0)
    col = lax.broadcasted_iota(jnp.int32, (c2, c2), 1)
    same_blk = (row // c) == (col // c)
    diff = jnp.where(z == 0, row - col, col - row)
    strict = same_blk & (diff > 0)
    incl = same_blk & (diff >= 0)
    incl2 = jnp.concatenate([incl, incl], axis=1)
    lane_a = lax.broadcasted_iota(jnp.int32, (c, pw_lanes), 1) < RWKV_HEAD
    same_head = ((lax.broadcasted_iota(jnp.int32, (pw_lanes, pw_lanes), 0) // RWKV_HEAD)
                 == (lax.broadcasted_iota(jnp.int32, (pw_lanes, pw_lanes), 1) // RWKV_HEAD))
    n_steps = int(math.log2(c))
    pairs = range(n_heads // 2)
    sls = [pl.ds(hp * pw_lanes, pw_lanes) for hp in pairs]
    zero = jnp.zeros((c, pw_lanes), SCAN_DT)
    for j in range(per_step):
        jj = jnp.where(z == 0, j, per_step - 1 - j)
        rows = pl.ds(pl.multiple_of(jj * c, c), c)
        _scan_chunk(at_ref, rt_ref, bt_ref, kt_ref, v_ref, ed_ref.at[jj], g_ref, ln_ref, y_ref, s_ref, rows, pairs,
                    sls, zero, lane_a, strict, incl2, same_head, c, c2, pw_lanes, n_steps)


def _scan_chunk(at_ref, rt_ref, bt_ref, kt_ref, v_ref, ed_ref, g_ref, ln_ref, y_ref, s_ref, rows, pairs,
                sls, zero, lane_a, strict, incl2, same_head, c, c2, pw_lanes, n_steps):
    nt = (((1,), (1,)))
    nn = (((1,), (0,)))
    tn = (((0,), (0,)))
    lhs, rhs, vv, st = [], [], [], []
    for hp in pairs:
        a2 = at_ref[rows, sls[hp]]
        r2 = rt_ref[rows, sls[hp]]
        b2 = bt_ref[rows, sls[hp]]
        k2 = kt_ref[rows, sls[hp]]
        v2 = v_ref[rows, sls[hp]]
        lhs.append(jnp.concatenate([jnp.where(lane_a, a2, zero), jnp.where(lane_a, zero, a2),
                                    jnp.where(lane_a, r2, zero), jnp.where(lane_a, zero, r2)], axis=0))
        rhs.append(jnp.concatenate([b2, b2, k2, k2], axis=0))
        vv.append(jnp.concatenate([v2, v2], axis=0))
        st.append(s_ref[hp] * ed_ref[0:1, sls[hp]])
    gq = [_sdot(lhs[hp], jnp.concatenate([rhs[hp], st[hp].astype(SCAN_DT)], axis=0), nt) for hp in pairs]
    gm = [g[:, :2 * c2] for g in gq]
    q = [g[:, 2 * c2:] for g in gq]
    pw = [jnp.where(strict, gm[hp][:c2, :c2], 0.0).astype(SCAN_DT) for hp in pairs]
    m2 = [jnp.where(strict, gm[hp][:c2, c2:], 0.0).astype(SCAN_DT) for hp in pairs]
    m34 = [jnp.where(incl2, gm[hp][c2:, :], 0.0).astype(SCAN_DT) for hp in pairs]
    x = [q[hp][:c2] + _sdot(m2[hp], vv[hp], nn) for hp in pairs]
    for step in range(n_steps - 1):
        both = [_sdot(pw[hp], jnp.concatenate([x[hp].astype(SCAN_DT), pw[hp]], axis=1), nn) for hp in pairs]
        x = [x[hp] + both[hp][:, :pw_lanes] for hp in pairs]
        pw = [both[hp][:, pw_lanes:].astype(SCAN_DT) for hp in pairs]
    x = [x[hp] + _sdot(pw[hp], x[hp], nn) for hp in pairs]
    for hp in pairs:
        xs = x[hp].astype(SCAN_DT)
        uvs = jnp.concatenate([xs, vv[hp]], axis=0)
        ys = q[hp][c2:] + _sdot(m34[hp], uvs, nn)
        y = jnp.where(lane_a, ys[:c], ys[c:])
        inv_n = 1.0 / RWKV_HEAD
        sum_a = jnp.where(lane_a, y, 0.0).sum(axis=-1, keepdims=True)
        sum_b = jnp.where(lane_a, 0.0, y).sum(axis=-1, keepdims=True)
        yc = y - jnp.where(lane_a, sum_a, sum_b) * inv_n
        sq = yc * yc
        sq_a = jnp.where(lane_a, sq, 0.0).sum(axis=-1, keepdims=True)
        sq_b = jnp.where(lane_a, 0.0, sq).sum(axis=-1, keepdims=True)
        var = jnp.where(lane_a, sq_a, sq_b) * inv_n
        yn = yc * lax.rsqrt(var + RWKV_GN_EPS) * ln_ref[0:1, sls[hp]] + ln_ref[1:2, sls[hp]]
        y_ref[rows, sls[hp]] = g_ref[rows, sls[hp]] * yn
        u2 = jnp.where(lane_a, xs[:c], xs[c:])
        upd = _sdot(jnp.concatenate([u2, vv[hp][:c]], axis=0), rhs[hp][c:3 * c], tn)
        s_ref[hp] = (st[hp] + jnp.where(same_head, upd, 0.0)) * ed_ref[1:2, sls[hp]]


def _rwkv_scan(at, rt, bt, kt, v, ed, g, ln8, bsz):
    _, t, d = at.shape
    s = t // bsz
    c = SCAN_CHUNK
    per_step = min(SCAN_CHUNKS_PER_STEP, s // c)
    nc = s // (c * per_step)
    n_heads = d // RWKV_HEAD

    def cidx(b, z, ci):
        return b * nc + ci + z * (nc - 1 - 2 * ci)

    dspec = pl.BlockSpec((None, c * per_step, d), lambda b, z, ci: (z, cidx(b, z, ci), 0))
    return pl.pallas_call(
        functools.partial(_scan_kernel, c=c, n_heads=n_heads, per_step=per_step),
        grid=(bsz, 2, nc),
        in_specs=[dspec, dspec, dspec, dspec,
                  pl.BlockSpec((c * per_step, d), lambda b, z, ci: (cidx(b, z, ci), 0)),
                  pl.BlockSpec((None, per_step, 8, d), lambda b, z, ci: (z, cidx(b, z, ci), 0, 0)),
                  dspec,
                  pl.BlockSpec(ln8.shape, lambda b, z, ci: (0, 0))],
        out_specs=dspec,
        out_shape=jax.ShapeDtypeStruct((2, t, d), F32),
        scratch_shapes=[pltpu.VMEM((n_heads // 2, 2 * RWKV_HEAD, 2 * RWKV_HEAD), F32)],
        compiler_params=_params(("parallel", "parallel", "arbitrary")),
        name="rwkv_scan",
    )(at, rt, bt, kt, v, ed, g, ln8)


def _rwkv_out_kernel(gy_ref, gb_ref, x_ref, ga_ref, wo_ref, o_ref):
    o = gy_ref[0] + gy_ref[1] + gb_ref[...]
    o_ref[...] = x_ref[...] + ga_ref[...] * _dot(o.astype(BF16), wo_ref[...])


def _rwkv_out(gy, gb, x2, gate, w_o, bsz, tm=512):
    t, d = x2.shape
    s = t // bsz
    tm = min(tm, s)
    ns = s // tm
    row = lambda b, i: (b * ns + i, 0)
    return pl.pallas_call(
        _rwkv_out_kernel,
        grid=(bsz, ns),
        in_specs=[pl.BlockSpec((2, tm, d), lambda b, i: (0, b * ns + i, 0)),
                  pl.BlockSpec((tm, d), row), pl.BlockSpec((tm, d), row),
                  pl.BlockSpec((None, 1, d), lambda b, i: (b, 0, 0)),
                  pl.BlockSpec(w_o.shape, lambda b, i: (0, 0))],
        out_specs=pl.BlockSpec((tm, d), row),
        out_shape=jax.ShapeDtypeStruct((t, d), F32),
        compiler_params=_params(("parallel", "parallel")),
        name="rwkv_out",
    )(gy, gb, x2, gate, w_o)


def _rwkv_layer(x2, bsz, hn, gate, p):
    at, rt, bt, kt, v, ed, g, gb = _rwkv_prep(hn, bsz, p, SCAN_DT)
    gy = _rwkv_scan(at, rt, bt, kt, v, ed, g, p["ln8"], bsz)
    return _rwkv_out(gy, gb, x2, gate, p["w_o"], bsz)


def _attn_group_kernel(slope_ref, vid_ref, q_ref, k_ref, v_ref, o_ref, lse_ref, tab_ref, *,
                       tq, sub, w, half, offs, nh, dh, hg, rps):
    first = (pl.program_id(0) == 0) & (pl.program_id(1) == 0) & (pl.program_id(2) == 0)
    i = pl.program_id(2)

    @pl.when(first)
    def _():
        col = lax.broadcasted_iota(jnp.int32, (tq, w), 1)
        row = lax.broadcasted_iota(jnp.int32, (tq, w), 0)
        for v, off in enumerate(offs):
            dist = jnp.abs(col - row + off)
            distf = dist.astype(F32)
            for h in range(nh):
                tab_ref[h * len(offs) + v] = jnp.where(dist <= half, -slope_ref[h] * distf, NEG_INF)

    if w == sub:
        ws = 0
    else:
        ws = pl.multiple_of(jnp.clip(i * tq - half, 0, sub - w), 16)
    var = vid_ref[i]
    nt = (((1,), (1,)), ((), ()))
    lane = lax.broadcasted_iota(jnp.int32, (tq, 128), 1)
    for rr in range(rps):
        qrows = slice(rr * tq, (rr + 1) * tq)
        krows = pl.ds(rr * sub + ws, w)
        lse = jnp.zeros((tq, 128), F32)
        for h0 in range(0, nh, hg):
            heads = range(h0, min(h0 + hg, nh))
            hs = {h: slice(h * dh, (h + 1) * dh) for h in heads}
            sc = {h: lax.dot_general(q_ref[qrows, hs[h]], k_ref[krows, hs[h]], nt, preferred_element_type=F32)
                  + tab_ref[h * len(offs) + var] for h in heads}
            m = {h: sc[h].max(axis=-1, keepdims=True) for h in heads}
            p = {h: jnp.exp(sc[h] - m[h]) for h in heads}
            den = {h: p[h].sum(axis=-1, keepdims=True) for h in heads}
            acc = {h: _dot(p[h].astype(BF16), v_ref[krows, hs[h]]) for h in heads}
            for h in heads:
                o_ref[qrows, hs[h]] = (acc[h] / den[h]).astype(o_ref.dtype)
                lse = jnp.where(lane == h, m[h] + jnp.log(den[h]), lse)
        lse_ref[qrows, :] = lse


def _alibi_slopes(n):
    return 2.0 ** (-8.0 * jnp.arange(1, n + 1, dtype=F32) / n)


def _attention_group(qkv, bsz, gi, tq=256):
    t, width = qkv.shape
    s = t // bsz
    window, dil = ATTN_GROUPS[gi]
    half = window // (2 * dil)
    sub = s // dil
    tq = min(tq, sub)
    nq = sub // tq
    w = min(sub, tq + 2 * half)
    nh, dh = ATTN_HEADS, ATTN_HEAD_DIM
    hw = nh * dh
    offs = sorted({int(np.clip(i * tq - half, 0, sub - w)) - i * tq for i in range(nq)})
    vid = jnp.asarray([offs.index(int(np.clip(i * tq - half, 0, sub - w)) - i * tq) for i in range(nq)], jnp.int32)
    slopes = _alibi_slopes(len(ATTN_GROUPS) * nh).reshape(len(ATTN_GROUPS), nh)[gi] * dil
    smem = pl.BlockSpec(memory_space=pltpu.SMEM)
    rps = min(dil, 4) if nq == 1 else 1
    nr = dil // rps
    qrow = lambda b, r, i: ((b * nr + r) * nq + i, 0)
    return pl.pallas_call(
        functools.partial(_attn_group_kernel, tq=tq, sub=sub, w=w, half=half, offs=tuple(offs), nh=nh, dh=dh,
                          hg=nh if tq * w <= 128 * 128 else 4, rps=rps),
        grid=(bsz, nr, nq),
        in_specs=[smem, smem,
                  pl.BlockSpec((rps * tq, hw), qrow),
                  pl.BlockSpec((rps * sub, hw), lambda b, r, i: (b * nr + r, 1)),
                  pl.BlockSpec((rps * sub, hw), lambda b, r, i: (b * nr + r, 2))],
        out_specs=[pl.BlockSpec((rps * tq, hw), qrow), pl.BlockSpec((rps * tq, 128), qrow)],
        out_shape=[jax.ShapeDtypeStruct((t, hw), BF16), jax.ShapeDtypeStruct((t, 128), F32)],
        scratch_shapes=[pltpu.VMEM((nh * len(offs), tq, w), F32)],
        compiler_params=_params(("arbitrary", "arbitrary", "arbitrary")),
        name=f"dilated_attention_g{gi}",
    )(slopes, vid, qkv, qkv, qkv)


def _to_token_order(blk_ref, scr_ref, dil):
    if dil == 1:
        return blk_ref[0].astype(F32)
    n = blk_ref.shape[1]
    slabs = scr_ref.shape[0]
    for r in range(dil):
        rows = blk_ref[r].astype(F32)
        for c in range(slabs):
            scr_ref[c, pl.ds(r, n, stride=dil), :] = rows[:, c * 128:(c + 1) * 128]
    if slabs == 1:
        return scr_ref[0]
    return jnp.concatenate([scr_ref[c] for c in range(slabs)], axis=1)


def _attn_out_kernel(o1, o2, o3, l1, l2, l3, ex_ref, w_ref, x_ref, ga_ref, out_ref, so_ref, sl_ref, *, dils):
    ls = [_to_token_order(l, sl_ref, dil) for l, dil in zip((l1, l2, l3), dils)]
    m = jnp.maximum(jnp.maximum(ls[0], ls[1]), ls[2])
    es = [jnp.exp(l - m) for l in ls]
    inv = 1.0 / (es[0] + es[1] + es[2])
    merged = None
    for e, o_ref, dil in zip(es, (o1, o2, o3), dils):
        ah, al = _split2(e * inv)
        alpha = _dot(ah, ex_ref[...]) + _dot(al, ex_ref[...])
        term = alpha * _to_token_order(o_ref, so_ref, dil)
        merged = term if merged is None else merged + term
    out_ref[...] = x_ref[...] + ga_ref[...] * _dot(merged.astype(BF16), w_ref[...])


def _attn_out(os, lses, w_o, x2, gate, bsz, tm=512):
    t, d = x2.shape
    s = t // bsz
    tm = min(tm, s)
    ns = s // tm
    hw = os[0].shape[1]
    dils = tuple(dil for _, dil in ATTN_GROUPS)
    ex = np.zeros((128, hw), np.float32)
    ex[np.arange(hw) // ATTN_HEAD_DIM, np.arange(hw)] = 1.0
    ex = jnp.asarray(ex, BF16)
    row = lambda b, i: (b * ns + i, 0)

    def res_major(a, dil):
        c = a.shape[1]
        return a.reshape(bsz, dil, s // dil, c), pl.BlockSpec((None, dil, tm // dil, c), lambda b, i: (b, 0, i, 0))

    o_args, o_specs = zip(*[res_major(o, dil) for o, dil in zip(os, dils)])
    l_args, l_specs = zip(*[res_major(l, dil) for l, dil in zip(lses, dils)])
    return pl.pallas_call(
        functools.partial(_attn_out_kernel, dils=dils),
        grid=(bsz, ns),
        in_specs=list(o_specs) + list(l_specs) + [
            pl.BlockSpec(ex.shape, lambda b, i: (0, 0)),
            pl.BlockSpec(w_o.shape, lambda b, i: (0, 0)),
            pl.BlockSpec((tm, d), row),
            pl.BlockSpec((None, 1, d), lambda b, i: (b, 0, 0))],
        out_specs=pl.BlockSpec((tm, d), row),
        out_shape=jax.ShapeDtypeStruct((t, d), F32),
        scratch_shapes=[pltpu.VMEM((hw // 128, tm, 128), F32), pltpu.VMEM((1, tm, 128), F32)],
        compiler_params=_params(("parallel", "parallel")),
        name="attn_out",
    )(*o_args, *l_args, ex, w_o, x2, gate)


def _norm_attn_kernel(x_ref, g_ref, sh_ref, sc_ref, *rest, dils):
    outs, scr_ref = rest[:-1], rest[-1]
    x = x_ref[...]
    ms = jnp.mean(x * x, axis=-1, keepdims=True)
    y = x * lax.rsqrt(ms + RMS_EPS) * g_ref[...]
    y = y * (1.0 + sc_ref[...]) + sh_ref[...]
    slabs = scr_ref.shape[0]
    if any(dil > 1 for dil in dils):
        for c in range(slabs):
            scr_ref[c] = y[:, c * 128:(c + 1) * 128]
    for o, dil in zip(outs, dils):
        if dil == 1:
            o[0] = y.astype(o.dtype)
        else:
            n = o.shape[1]
            for r in range(dil):
                for c in range(slabs):
                    o[r, :, c * 128:(c + 1) * 128] = scr_ref[c, pl.ds(r, n, stride=dil), :].astype(o.dtype)


def _norm_mod_attn(x2, g, shift, scale, bsz, ts=512):
    t, d = x2.shape
    s = t // bsz
    ts = min(ts, s)
    ns = s // ts
    dils = tuple(dil for _, dil in ATTN_GROUPS)
    vec = pl.BlockSpec((None, 1, d), lambda b, i: (b, 0, 0))
    outs = pl.pallas_call(
        functools.partial(_norm_attn_kernel, dils=dils),
        grid=(bsz, ns),
        in_specs=[pl.BlockSpec((ts, d), lambda b, i: (b * ns + i, 0)),
                  pl.BlockSpec((1, d), lambda b, i: (0, 0)), vec, vec],
        out_specs=[pl.BlockSpec((None, dil, ts // dil, d), lambda b, i: (b, 0, i, 0)) for dil in dils],
        out_shape=[jax.ShapeDtypeStruct((bsz, dil, s // dil, d), BF16) for dil in dils],
        scratch_shapes=[pltpu.VMEM((d // 128, ts, 128), F32)],
        compiler_params=_params(("parallel", "parallel")),
        name="norm_mod_attn",
    )(x2, g.reshape(1, d), shift, scale)
    return [o.reshape(t, d) for o in outs]


def _pack_attn(j, w_qkv, w_o):
    d = w_qkv.shape[1]
    ng = len(ATTN_GROUPS)
    w = w_qkv[j].reshape(d, ng, 3, ATTN_HEADS * ATTN_HEAD_DIM)
    w = w.at[:, :, 0].multiply(ATTN_HEAD_DIM ** -0.5)
    return {"w_qkv": w.transpose(1, 0, 2, 3).reshape(ng, d, -1).astype(BF16), "w_o": w_o[j].astype(BF16)}


def _attn_layer(x2, bsz, hns, gate, p):
    outs = [_attention_group(_matmul(hn, p["w_qkv"][gi], BF16), bsz, gi) for gi, hn in enumerate(hns)]
    return _attn_out([o for o, _ in outs], [l for _, l in outs], p["w_o"], x2, gate, bsz)


def _router_kernel(x_ref, g_ref, sh_ref, sc_ref, wh_ref, wl_ref, b_ref, o_ref, hb_ref, seg_ref):
    x = x_ref[...]
    ms = jnp.mean(x * x, axis=-1, keepdims=True)
    hn = x * lax.rsqrt(ms + RMS_EPS) * g_ref[...]
    hn = hn * (1.0 + sc_ref[...]) + sh_ref[...]
    hb_ref[...] = hn.astype(hb_ref.dtype)
    hh, hl = _split2(hn)
    logits = _dot(hh, wh_ref[...]) + _dot(hl, wh_ref[...]) + _dot(hh, wl_ref[...]) + b_ref[...]
    lane = lax.broadcasted_iota(jnp.int32, logits.shape, 1)
    ng, ne = N_EXPERT_GROUPS, EXPERTS_PER_GROUP
    big = jnp.int32(ROUTER_LANES)
    is_g = lane < ng
    gl = jnp.where(is_g, logits, -jnp.inf)
    gmax = gl.max(axis=-1, keepdims=True)
    gsum = jnp.where(is_g, jnp.exp(logits - gmax), 0.0).sum(axis=-1, keepdims=True)
    p_group = 1.0 / gsum
    g_top = jnp.where(gl == gmax, lane, big).min(axis=-1, keepdims=True)
    lo = ng + ne * g_top
    es = jnp.where((lane >= lo) & (lane < lo + ne), logits, -jnp.inf)
    v1 = es.max(axis=-1, keepdims=True)
    i1 = jnp.where(es == v1, lane, big).min(axis=-1, keepdims=True)
    es2 = jnp.where(lane == i1, -jnp.inf, es)
    v2 = es2.max(axis=-1, keepdims=True)
    i2 = jnp.where(es2 == v2, lane, big).min(axis=-1, keepdims=True)
    e2 = jnp.exp(v2 - v1)
    w1 = p_group / (1.0 + e2)
    w2 = p_group * e2 / (1.0 + e2)
    gates = jnp.where(lane == i1, w1, 0.0) + jnp.where(lane == i2, w2, 0.0)

    tm = x.shape[0]
    onehot = jnp.where(lane == g_top, 1.0, 0.0)
    tri = jnp.where(lax.broadcasted_iota(jnp.int32, (tm, tm), 1) < lax.broadcasted_iota(jnp.int32, (tm, tm), 0),
                    1.0, 0.0).astype(BF16)
    before = _dot(tri, onehot.astype(BF16))
    counts = jnp.broadcast_to(onehot.sum(axis=0, keepdims=True), (8, ROUTER_LANES))
    upper = jnp.where(lax.broadcasted_iota(jnp.int32, (ROUTER_LANES, ROUTER_LANES), 0)
                      < lax.broadcasted_iota(jnp.int32, (ROUTER_LANES, ROUTER_LANES), 1), 1.0, 0.0).astype(BF16)
    ch, cl = _split2(counts)
    seg_lo = _dot(ch, upper) + _dot(cl, upper)
    dest = (onehot * (before + seg_lo[0:1])).sum(axis=-1, keepdims=True)
    o_ref[...] = gates + jnp.where(lane == DEST_LANE, dest, 0.0)
    row8 = lax.broadcasted_iota(jnp.int32, (8, ROUTER_LANES), 0)
    seg_ref[...] = jnp.where(row8 == 0, seg_lo, jnp.where(row8 == 1, seg_lo + counts, 0.0))


def _router(x2, g, shift, scale, wr_hi, wr_lo, bias, bsz):
    t, d = x2.shape
    s = t // bsz
    tm = min(MOE_TILE, s)
    ns = s // tm
    row = lambda b, i: (b * ns + i, 0)
    full = lambda a: pl.BlockSpec(a.shape, lambda b, i: (0, 0))
    vec = pl.BlockSpec((None, 1, d), lambda b, i: (b, 0, 0))
    g = g.reshape(1, d)
    return pl.pallas_call(
        _router_kernel,
        grid=(bsz, ns),
        in_specs=[pl.BlockSpec((tm, d), row), full(g), vec, vec, full(wr_hi), full(wr_lo), full(bias)],
        out_specs=[pl.BlockSpec((tm, ROUTER_LANES), row), pl.BlockSpec((tm, d), row),
                   pl.BlockSpec((None, 8, ROUTER_LANES), lambda b, i: (b * ns + i, 0, 0))],
        out_shape=[jax.ShapeDtypeStruct((t, ROUTER_LANES), F32), jax.ShapeDtypeStruct((t, d), BF16),
                   jax.ShapeDtypeStruct((t // tm, 8, ROUTER_LANES), F32)],
        compiler_params=_params(("parallel", "parallel")),
        name="moe_router",
    )(x2, g, shift, scale, wr_hi, wr_lo, bias)


def _experts_kernel(seg_ref, h_ref, gates_ref, wg_ref, wu_ref, wd_ref, x_ref, ga_ref, o_ref,
                    pt_ref, xs_ref, gs_ref, y_ref, *, per_step, blk):
    tile = pl.program_id(0)
    step = pl.program_id(1)
    tm = h_ref.shape[0]
    grp = step // (EXPERTS_PER_GROUP // per_step)
    tn = (((0,), (0,)), ((), ()))

    @pl.when(step == 0)
    def _():
        gates = gates_ref[...]
        lane = lax.broadcasted_iota(jnp.int32, gates.shape, 1)
        dest = jnp.where(lane == DEST_LANE, gates, 0.0).sum(axis=-1, keepdims=True)
        col = lax.broadcasted_iota(jnp.int32, (tm, tm), 1).astype(F32)
        pt = jnp.where(dest == col, 1.0, 0.0).astype(BF16)
        pt_ref[...] = pt
        xs_ref[...] = lax.dot_general(pt, h_ref[...], tn, preferred_element_type=F32).astype(BF16)
        g1 = gates.astype(BF16)
        r1 = gates - g1.astype(F32)
        g2 = r1.astype(BF16)
        g3 = (r1 - g2.astype(F32)).astype(BF16)
        gs_ref[...] = (lax.dot_general(pt, g1, tn, preferred_element_type=F32)
                       + lax.dot_general(pt, g2, tn, preferred_element_type=F32)
                       + lax.dot_general(pt, g3, tn, preferred_element_type=F32))
        y_ref[...] = jnp.zeros_like(y_ref)

    lo = seg_ref[tile * 8 + grp]
    hi = seg_ref[tile * 8 + N_EXPERT_GROUPS + grp]
    for b in range(tm // blk):
        @pl.when((lo < (b + 1) * blk) & (hi > b * blk))
        def _(b=b):
            rows = pl.ds(b * blk, blk)
            h = xs_ref[rows, :]
            gsb = gs_ref[rows, :]
            lane = lax.broadcasted_iota(jnp.int32, gsb.shape, 1)
            hids = []
            for j in range(per_step):
                e = step * per_step + j
                ge = jnp.where(lane == N_EXPERT_GROUPS + e, gsb, 0.0).sum(axis=-1, keepdims=True)
                gp = _dot(h, wg_ref[j])
                up = _dot(h, wu_ref[j])
                hids.append((gp * _sigmoid(gp) * up * ge).astype(BF16))
            y_ref[rows, :] += _dot(jnp.concatenate(hids, axis=1), wd_ref[...])

    @pl.when(step == pl.num_programs(1) - 1)
    def _():
        o_ref[...] = x_ref[...] + ga_ref[...] * _dot(pt_ref[...], y_ref[...].astype(BF16))


def _experts(hn_bf16, gates, seg, wg, wu, wd, x2, gate, bsz, per_step=4):
    t, d = x2.shape
    s = t // bsz
    tm = min(MOE_TILE, s)
    ns = s // tm
    ne, _, f = wg.shape
    blk = min(MOE_BLOCK, tm)
    row = lambda i, e, seg: (i, 0)
    wspec = pl.BlockSpec((per_step, d, f), lambda i, e, seg: (e, 0, 0))
    grid_spec = pltpu.PrefetchScalarGridSpec(
        num_scalar_prefetch=1,
        grid=(t // tm, ne // per_step),
        in_specs=[
            pl.BlockSpec((tm, d), row),
            pl.BlockSpec((tm, ROUTER_LANES), row),
            wspec, wspec,
            pl.BlockSpec((per_step * f, d), lambda i, e, seg: (e, 0)),
            pl.BlockSpec((tm, d), row),
            pl.BlockSpec((None, 1, d), lambda i, e, seg: (i // ns, 0, 0)),
        ],
        out_specs=pl.BlockSpec((tm, d), row),
        scratch_shapes=[pltpu.VMEM((tm, tm), BF16), pltpu.VMEM((tm, d), BF16),
                        pltpu.VMEM((tm, ROUTER_LANES), F32), pltpu.VMEM((tm, d), F32)],
    )
    return pl.pallas_call(
        functools.partial(_experts_kernel, per_step=per_step, blk=blk),
        grid_spec=grid_spec,
        out_shape=jax.ShapeDtypeStruct((t, d), F32),
        compiler_params=_params(("parallel", "arbitrary")),
        name="moe_experts",
    )(seg, hn_bf16, gates, wg, wu, wd, x2, gate)


def _moe_layer(x2, bsz, g, shift, scale, gate, p):
    gates, hn_bf16, seg = _router(x2, g, shift, scale, p["wr_hi"], p["wr_lo"], p["r_bias"], bsz)
    ng = N_EXPERT_GROUPS
    seg = jnp.concatenate([seg[:, 0, :ng], seg[:, 1, :ng]], axis=1).astype(jnp.int32).reshape(-1)
    return _experts(hn_bf16, gates, seg, p["wg"], p["wu"], p["wd"], x2, gate, bsz)


def _pad_rows(rows, d):
    out = jnp.zeros((8, d), F32)
    return out.at[:len(rows)].set(jnp.stack([r.reshape(d).astype(F32) for r in rows]))


def _pack_rwkv(j, mu, w_rkv, w0, w1, w2, a0, a1, a2, g1, g2, k_k, k_a, r_k, ln_w, ln_b, w_o):
    d = w_o.shape[-1]
    cat = lambda m: jnp.concatenate([m[j, 0], m[j, 1]], axis=1).astype(BF16)
    stack = lambda m: jnp.concatenate([m[j, 0], m[j, 1]], axis=0).astype(BF16)
    return {
        "mu8": _pad_rows(list(mu[j]), d),
        "w_rkv": w_rkv[j].astype(BF16),
        "w1c": cat(w1), "a1c": cat(a1), "g1c": cat(g1),
        "w2s": stack(w2), "a2s": stack(a2), "g2": g2[j].astype(BF16),
        "vec8": _pad_rows([w0[j, 0], w0[j, 1], a0[j, 0], a0[j, 1], k_k[j], k_a[j], r_k[j]], d),
        "ln8": _pad_rows([ln_w[j], ln_b[j]], d),
        "w_o": w_o[j].astype(BF16),
    }


def _pack_moe(i, router_g, router_g_b, router_e, router_e_b, w_gate, w_up, w_down):
    d = router_g.shape[1]
    ng, ne = N_EXPERT_GROUPS, EXPERTS_PER_GROUP
    wr = jnp.zeros((d, ROUTER_LANES), F32)
    wr = wr.at[:, :ng].set(router_g[i])
    wr = wr.at[:, ng:ng + ng * ne].set(router_e[i].transpose(1, 0, 2).reshape(d, ng * ne))
    bias = jnp.zeros((1, ROUTER_LANES), F32)
    bias = bias.at[0, :ng].set(router_g_b[i])
    bias = bias.at[0, ng:ng + ng * ne].set(router_e_b[i].reshape(ng * ne))
    wr_hi = wr.astype(BF16)
    wr_lo = (wr - wr_hi.astype(F32)).astype(BF16)
    f = w_gate.shape[-1]
    return {"wr_hi": wr_hi, "wr_lo": wr_lo, "r_bias": bias,
            "wg": w_gate[i].reshape(ng * ne, d, f).astype(BF16), "wu": w_up[i].reshape(ng * ne, d, f).astype(BF16),
            "wd": w_down[i].reshape(ng * ne * f, d).astype(BF16)}


def kernel(x, c, ada_w, ada_b, norm_tm_g, norm_cm_g, rw_mu, rw_w_rkv, rw_w0, rw_w1, rw_w2, rw_a0, rw_a1, rw_a2, rw_g1, rw_g2, rw_k_k, rw_k_a, rw_r_k, rw_ln_w, rw_ln_b, rw_w_o, at_w_qkv, at_w_o, moe_router_g, moe_router_g_b, moe_router_e, moe_router_e_b, moe_w_gate, moe_w_up, moe_w_down, final_g):
    bsz, s, d = x.shape
    depth = ada_w.shape[0]
    x2 = x.reshape(bsz * s, d)
    mod = _ada_mod(c, ada_w, ada_b)
    for i in range(depth):
        sh_t, sc_t, ga_t, sh_c, sc_c, ga_c = (mod[i, m] for m in range(N_MOD))
        j = i // 2
        if i % 2 == 0:
            (hn,) = _norm_mod(x2, norm_tm_g[i], sh_t, sc_t, bsz, (F32,))
            p = _pack_rwkv(j, rw_mu, rw_w_rkv, rw_w0, rw_w1, rw_w2, rw_a0, rw_a1, rw_a2, rw_g1, rw_g2,
                           rw_k_k, rw_k_a, rw_r_k, rw_ln_w, rw_ln_b, rw_w_o)
            x2 = _rwkv_layer(x2, bsz, hn, ga_t, p)
        else:
            hns = _norm_mod_attn(x2, norm_tm_g[i], sh_t, sc_t, bsz)
            p = _pack_attn(j, at_w_qkv, at_w_o)
            x2 = _attn_layer(x2, bsz, hns, ga_t, p)
        pm = _pack_moe(i, moe_router_g, moe_router_g_b, moe_router_e, moe_router_e_b,
                       moe_w_gate, moe_w_up, moe_w_down)
        x2 = _moe_layer(x2, bsz, norm_cm_g[i], sh_c, sc_c, ga_c, pm)
    (out,) = _norm_mod(x2, final_g, None, None, bsz, (F32,))
    return out.reshape(bsz, s, d)
```

```python
import functools
import math

import numpy as np
import jax
import jax.numpy as jnp
from jax import lax
from jax.experimental import pallas as pl
from jax.experimental.pallas import tpu as pltpu

F32 = jnp.float32
BF16 = jnp.bfloat16

RMS_EPS = 1e-6
N_MOD = 6
RWKV_HEAD = 64
RWKV_GN_EPS = 64e-5
ATTN_GROUPS = ((128, 1), (512, 4), (2048, 16))
ATTN_HEADS = 8
ATTN_HEAD_DIM = 128
NEG_INF = -1e30
N_EXPERT_GROUPS = 4
EXPERTS_PER_GROUP = 8
ROUTER_LANES = 128
DEST_LANE = 64
MOE_TILE = 1024
MOE_BLOCK = 128

SCAN_CHUNK = 64
SCAN_CHUNKS_PER_STEP = 4
V7X_VMEM_LIMIT = 56 * 1024 * 1024
SCAN_DT = BF16


def _params(sem):
    return pltpu.CompilerParams(dimension_semantics=sem, vmem_limit_bytes=V7X_VMEM_LIMIT)


def _dot(a, b):
    return jnp.dot(a, b, preferred_element_type=F32)


def _sigmoid(x):
    return 0.5 * jnp.tanh(0.5 * x) + 0.5


def _split2(x):
    hi = x.astype(BF16)
    lo = (x - hi.astype(F32)).astype(BF16)
    return hi, lo


def _ada_kernel(c_ref, w_ref, b_ref, o_ref):
    c = c_ref[...]
    sc = c * _sigmoid(c)
    o_ref[...] = _dot(sc.astype(BF16), w_ref[...].astype(BF16)) + b_ref[...]


def _ada_mod(c, ada_w, ada_b):
    depth, d, n = ada_w.shape
    bsz = c.shape[0]
    tn = n // 4
    out = pl.pallas_call(
        _ada_kernel,
        grid=(depth, n // tn),
        in_specs=[
            pl.BlockSpec((bsz, d), lambda i, j: (0, 0)),
            pl.BlockSpec((None, d, tn), lambda i, j: (i, 0, j)),
            pl.BlockSpec((None, 1, tn), lambda i, j: (i, 0, j)),
        ],
        out_specs=pl.BlockSpec((None, bsz, tn), lambda i, j: (i, 0, j)),
        out_shape=jax.ShapeDtypeStruct((depth, bsz, n), F32),
        compiler_params=_params(("parallel", "parallel")),
        name="ada_mod",
    )(c, ada_w, ada_b.reshape(depth, 1, n))
    return out.reshape(depth, bsz, N_MOD, 1, d).transpose(0, 2, 1, 3, 4)


def _norm_kernel(*refs, modulated, n_out):
    if modulated:
        x_ref, g_ref, sh_ref, sc_ref = refs[:4]
    else:
        x_ref, g_ref = refs[:2]
    outs = refs[-n_out:]
    x = x_ref[...]
    ms = jnp.mean(x * x, axis=-1, keepdims=True)
    y = x * lax.rsqrt(ms + RMS_EPS) * g_ref[...]
    if modulated:
        y = y * (1.0 + sc_ref[...]) + sh_ref[...]
    for o in outs:
        o[...] = y.astype(o.dtype)


def _norm_mod(x2, g, shift, scale, bsz, out_dtypes, ts=512):
    t, d = x2.shape
    s = t // bsz
    ts = min(ts, s)
    ns = s // ts
    modulated = shift is not None
    row_spec = pl.BlockSpec((ts, d), lambda b, i: (b * ns + i, 0))
    in_specs = [row_spec, pl.BlockSpec((1, d), lambda b, i: (0, 0))]
    args = [x2, g.reshape(1, d)]
    if modulated:
        vec = pl.BlockSpec((None, 1, d), lambda b, i: (b, 0, 0))
        in_specs += [vec, vec]
        args += [shift, scale]
    outs = pl.pallas_call(
        functools.partial(_norm_kernel, modulated=modulated, n_out=len(out_dtypes)),
        grid=(bsz, ns),
        in_specs=in_specs,
        out_specs=[row_spec] * len(out_dtypes),
        out_shape=[jax.ShapeDtypeStruct((t, d), dt) for dt in out_dtypes],
        compiler_params=_params(("parallel", "parallel")),
        name="norm_mod",
    )(*args)
    return outs


def _mm_kernel(a_ref, b_ref, o_ref):
    o_ref[...] = _dot(a_ref[...], b_ref[...]).astype(o_ref.dtype)


def _matmul(a, b, out_dtype, tm=1024, tn=1024):
    m, k = a.shape
    n = b.shape[1]
    tm, tn = min(tm, m), min(tn, n)
    return pl.pallas_call(
        _mm_kernel,
        grid=(n // tn, m // tm),
        in_specs=[
            pl.BlockSpec((tm, k), lambda j, i: (i, 0)),
            pl.BlockSpec((k, tn), lambda j, i: (0, j)),
        ],
        out_specs=pl.BlockSpec((tm, tn), lambda j, i: (i, j)),
        out_shape=jax.ShapeDtypeStruct((m, n), out_dtype),
        compiler_params=_params(("parallel", "parallel")),
        name="matmul",
    )(a, b)


def _seg_sum(x, seg_ref, segt_ref):
    xh, xl = _split2(x)
    s = _dot(xh, seg_ref[...]) + _dot(xl, seg_ref[...])
    sh, sl = _split2(s)
    return _dot(sh, segt_ref[...]) + _dot(sl, segt_ref[...])


def _rwkv_prep_kernel(hn_ref, prev_ref, next_ref, mu_ref, wrkv_ref, w1_ref, a1_ref, g1_ref,
                      w2_ref, a2_ref, g2_ref, vec_ref, cm_ref, edm_ref, seg_ref, segt_ref,
                      at_ref, rt_ref, bt_ref, kt_ref, v_ref, ed_ref, g_ref, gb_ref, *, ts, ns):
    i = pl.program_id(1)
    cur = hn_ref[...]
    row = lax.broadcasted_iota(jnp.int32, (ts, 1), 0)
    prev_row = jnp.where(i > 0, prev_ref[7:8, :], 0.0)
    next_row = jnp.where(i < ns - 1, next_ref[0:1, :], 0.0)
    x_prev = jnp.where(row == 0, prev_row, pltpu.roll(cur, 1, 0))
    x_next = jnp.where(row == ts - 1, next_row, pltpu.roll(cur, ts - 1, 0))
    xx = 0.5 * (x_prev + x_next) - cur

    def mix(j):
        return (cur + xx * mu_ref[j:j + 1, :]).astype(BF16)

    r = _dot(mix(0), wrkv_ref[0])
    k = _dot(mix(1), wrkv_ref[1])
    v = _dot(mix(2), wrkv_ref[2])
    tw = jnp.tanh(_dot(mix(3), w1_ref[...]))
    ta = _dot(mix(4), a1_ref[...])
    tg = _sigmoid(_dot(mix(5), g1_ref[...]))
    lane = lax.broadcasted_iota(jnp.int32, (1, tw.shape[1]), 1)
    half = tw.shape[1] // 2
    hg = tg.shape[1] // 2

    k_k = vec_ref[4:5, :]
    k_a = vec_ref[5:6, :]
    r_k = vec_ref[6:7, :]
    kk = k * k_k
    ss = _seg_sum(kk * kk, seg_ref, segt_ref)
    kkn = kk * lax.rsqrt(jnp.maximum(ss, 1e-24))
    v_ref[...] = v.astype(v_ref.dtype)

    gb = jnp.zeros_like(cur)
    for z in range(2):
        sel = (lane < half) if z == 0 else (lane >= half)
        lw = _dot(jnp.where(sel, tw, 0.0).astype(BF16), w2_ref[...])
        la = _dot(jnp.where(sel, ta, 0.0).astype(BF16), a2_ref[...])
        g = _dot(tg[:, z * hg:(z + 1) * hg].astype(BF16), g2_ref[z])
        ld = -math.exp(-0.5) * _sigmoid(vec_ref[z:z + 1, :] + lw)
        a = _sigmoid(vec_ref[2 + z:3 + z, :] + la)
        kdir = k * (1.0 + (a - 1.0) * k_a)
        b = kkn * a
        parts = _split2(ld)
        cmz = cm_ref[z]
        lm = _dot(cmz, parts[0]) + _dot(cmz, parts[1])
        e_in = jnp.exp(lm)
        e_inv = jnp.exp(-lm)
        e_ex = jnp.exp(lm - ld)
        rt_ref[z] = (r * e_in).astype(rt_ref.dtype)
        at_ref[z] = (-kkn * e_ex).astype(at_ref.dtype)
        bt_ref[z] = (b * e_inv).astype(bt_ref.dtype)
        kt_ref[z] = (kdir * e_inv).astype(kt_ref.dtype)
        edz = edm_ref[z]
        ed = jnp.exp(_dot(edz, parts[0]) + _dot(edz, parts[1]))
        for ck in range(ed.shape[0] // 8):
            ed_ref[z, ck] = ed[ck * 8:(ck + 1) * 8]
        g_ref[z] = g.astype(g_ref.dtype)
        bonus = _seg_sum(r * kdir * r_k, seg_ref, segt_ref) * v
        gb = gb + g * bonus
    gb_ref[...] = gb


def _chunk_matrices(ts, c):
    t = np.arange(ts)
    same = (t[:, None] // c) == (t[None, :] // c)
    pos = t % c
    cm = np.zeros((2, ts, ts), np.float32)
    nck = ts // c
    edm = np.zeros((2, 8 * nck, ts), np.float32)
    tri_f = same & (t[None, :] <= t[:, None])
    sel_f = same & (pos[None, :] <= c // 2 - 1)
    cm[0] = tri_f.astype(np.float32) - sel_f.astype(np.float32)
    tri_b = same & (t[None, :] >= t[:, None])
    sel_b = same & (pos[None, :] >= c // 2)
    cm[1] = tri_b.astype(np.float32) - sel_b.astype(np.float32)
    for ck in range(nck):
        inck = (t // c) == ck
        edm[0, 8 * ck] = inck & (pos <= c // 2 - 1)
        edm[0, 8 * ck + 1] = inck & (pos > c // 2 - 1)
        edm[1, 8 * ck] = inck & (pos >= c // 2)
        edm[1, 8 * ck + 1] = inck & (pos < c // 2)
    return jnp.asarray(cm, BF16), jnp.asarray(edm, BF16)


def _seg_matrices(d, head):
    seg = np.zeros((d, 128), np.float32)
    seg[np.arange(d), np.arange(d) // head] = 1.0
    return jnp.asarray(seg, BF16), jnp.asarray(seg.T.copy(), BF16)


def _rwkv_prep(hn, bsz, p, scan_dtype, ts=256):
    t, d = hn.shape
    s = t // bsz
    ts = min(ts, s)
    ns = s // ts
    c = SCAN_CHUNK
    assert ts % c == 0 and ts % 8 == 0
    nck = ts // c
    cm, edm = _chunk_matrices(ts, c)
    seg, segt = _seg_matrices(d, RWKV_HEAD)
    r8 = ts // 8
    nblk8 = t // 8
    row = lambda b, i: (b * ns + i, 0)
    full = lambda a: pl.BlockSpec(a.shape, lambda b, i, _n=a.ndim: (0,) * _n)
    weights = [p["mu8"], p["w_rkv"], p["w1c"], p["a1c"], p["g1c"], p["w2s"], p["a2s"], p["g2"],
               p["vec8"], cm, edm, seg, segt]
    dir_spec = pl.BlockSpec((2, ts, d), lambda b, i: (0, b * ns + i, 0))
    outs = pl.pallas_call(
        functools.partial(_rwkv_prep_kernel, ts=ts, ns=ns),
        grid=(bsz, ns),
        in_specs=[
            pl.BlockSpec((ts, d), row),
            pl.BlockSpec((8, d), lambda b, i: (jnp.maximum((b * ns + i) * r8 - 1, 0), 0)),
            pl.BlockSpec((8, d), lambda b, i: (jnp.minimum((b * ns + i + 1) * r8, nblk8 - 1), 0)),
        ] + [full(a) for a in weights],
        out_specs=[dir_spec, dir_spec, dir_spec, dir_spec,
                   pl.BlockSpec((ts, d), row),
                   pl.BlockSpec((2, nck, 8, d), lambda b, i: (0, b * ns + i, 0, 0)),
                   dir_spec,
                   pl.BlockSpec((ts, d), row)],
        out_shape=[jax.ShapeDtypeStruct((2, t, d), scan_dtype)] * 4 + [
            jax.ShapeDtypeStruct((t, d), scan_dtype),
            jax.ShapeDtypeStruct((2, t // c, 8, d), F32),
            jax.ShapeDtypeStruct((2, t, d), F32),
            jax.ShapeDtypeStruct((t, d), F32)],
        compiler_params=_params(("parallel", "parallel")),
        name="rwkv_prep",
    )(hn, hn, hn, *weights)
    return outs


def _sdot(a, b, dims):
    return lax.dot_general(a.astype(SCAN_DT), b.astype(SCAN_DT), (dims, ((), ())),
                           preferred_element_type=F32)


def _scan_kernel(at_ref, rt_ref, bt_ref, kt_ref, v_ref, ed_ref, g_ref, ln_ref, y_ref, s_ref, *,
                 c, n_heads, per_step):
    z = pl.program_id(1)
    ci = pl.program_id(2)

    @pl.when(ci == 0)
    def _():
        s_ref[...] = jnp.zeros_like(s_ref)

    c2 = 2 * c
    pw_lanes = 2 * RWKV_HEAD
    row = lax.broadcasted_iota(jnp.int32, (c2, c2), 0)
    col = lax.broadcasted_iota(jnp.int32, (c2, c2), 1)
    same_blk = (row // c) == (col // c)
    diff = jnp.where(z == 0, row - col, col - row)
    strict = same_blk & (diff > 0)
    incl = same_blk & (diff >= 0)
    incl2 = jnp.concatenate([incl, incl], axis=1)
    lane_a = lax.broadcasted_iota(jnp.int32, (c, pw_lanes), 1) < RWKV_HEAD
    same_head = ((lax.broadcasted_iota(jnp.int32, (pw_lanes, pw_lanes), 0) // RWKV_HEAD)
                 == (lax.broadcasted_iota(jnp.int32, (pw_lanes, pw_lanes), 1) // RWKV_HEAD))
    n_steps = int(math.log2(c))
    pairs = range(n_heads // 2)
    sls = [pl.ds(hp * pw_lanes, pw_lanes) for hp in pairs]
    zero = jnp.zeros((c, pw_lanes), SCAN_DT)
    for j in range(per_step):
        jj = jnp.where(z == 0, j, per_step - 1 - j)
        rows = pl.ds(pl.multiple_of(jj * c, c), c)
        _scan_chunk(at_ref, rt_ref, bt_ref, kt_ref, v_ref, ed_ref.at[jj], g_ref, ln_ref, y_ref, s_ref, rows, pairs,
                    sls, zero, lane_a, strict, incl2, same_head, c, c2, pw_lanes, n_steps)


def _scan_chunk(at_ref, rt_ref, bt_ref, kt_ref, v_ref, ed_ref, g_ref, ln_ref, y_ref, s_ref, rows, pairs,
                sls, zero, lane_a, strict, incl2, same_head, c, c2, pw_lanes, n_steps):
    nt = (((1,), (1,)))
    nn = (((1,), (0,)))
    tn = (((0,), (0,)))
    lhs, rhs, vv, st = [], [], [], []
    for hp in pairs:
        a2 = at_ref[rows, sls[hp]]
        r2 = rt_ref[rows, sls[hp]]
        b2 = bt_ref[rows, sls[hp]]
        k2 = kt_ref[rows, sls[hp]]
        v2 = v_ref[rows, sls[hp]]
        lhs.append(jnp.concatenate([jnp.where(lane_a, a2, zero), jnp.where(lane_a, zero, a2),
                                    jnp.where(lane_a, r2, zero), jnp.where(lane_a, zero, r2)], axis=0))
        rhs.append(jnp.concatenate([b2, b2, k2, k2], axis=0))
        vv.append(jnp.concatenate([v2, v2], axis=0))
        st.append(s_ref[hp] * ed_ref[0:1, sls[hp]])
    gq = [_sdot(lhs[hp], jnp.concatenate([rhs[hp], st[hp].astype(SCAN_DT)], axis=0), nt) for hp in pairs]
    gm = [g[:, :2 * c2] for g in gq]
    q = [g[:, 2 * c2:] for g in gq]
    pw = [jnp.where(strict, gm[hp][:c2, :c2], 0.0).astype(SCAN_DT) for hp in pairs]
    m2 = [jnp.where(strict, gm[hp][:c2, c2:], 0.0).astype(SCAN_DT) for hp in pairs]
    m34 = [jnp.where(incl2, gm[hp][c2:, :], 0.0).astype(SCAN_DT) for hp in pairs]
    x = [q[hp][:c2] + _sdot(m2[hp], vv[hp], nn) for hp in pairs]
    for step in range(n_steps - 1):
        both = [_sdot(pw[hp], jnp.concatenate([x[hp].astype(SCAN_DT), pw[hp]], axis=1), nn) for hp in pairs]
        x = [x[hp] + both[hp][:, :pw_lanes] for hp in pairs]
        pw = [both[hp][:, pw_lanes:].astype(SCAN_DT) for hp in pairs]
    x = [x[hp] + _sdot(pw[hp], x[hp], nn) for hp in pairs]
    for hp in pairs:
        xs = x[hp].astype(SCAN_DT)
        uvs = jnp.concatenate([xs, vv[hp]], axis=0)
        ys = q[hp][c2:] + _sdot(m34[hp], uvs, nn)
        y = jnp.where(lane_a, ys[:c], ys[c:])
        inv_n = 1.0 / RWKV_HEAD
        sum_a = jnp.where(lane_a, y, 0.0).sum(axis=-1, keepdims=True)
        sum_b = jnp.where(lane_a, 0.0, y).sum(axis=-1, keepdims=True)
        yc = y - jnp.where(lane_a, sum_a, sum_b) * inv_n
        sq = yc * yc
        sq_a = jnp.where(lane_a, sq, 0.0).sum(axis=-1, keepdims=True)
        sq_b = jnp.where(lane_a, 0.0, sq).sum(axis=-1, keepdims=True)
        var = jnp.where(lane_a, sq_a, sq_b) * inv_n
        yn = yc * lax.rsqrt(var + RWKV_GN_EPS) * ln_ref[0:1, sls[hp]] + ln_ref[1:2, sls[hp]]
        y_ref[rows, sls[hp]] = g_ref[rows, sls[hp]] * yn
        u2 = jnp.where(lane_a, xs[:c], xs[c:])
        upd = _sdot(jnp.concatenate([u2, vv[hp][:c]], axis=0), rhs[hp][c:3 * c], tn)
        s_ref[hp] = (st[hp] + jnp.where(same_head, upd, 0.0)) * ed_ref[1:2, sls[hp]]


def _rwkv_scan(at, rt, bt, kt, v, ed, g, ln8, bsz):
    _, t, d = at.shape
    s = t // bsz
    c = SCAN_CHUNK
    per_step = min(SCAN_CHUNKS_PER_STEP, s // c)
    nc = s // (c * per_step)
    n_heads = d // RWKV_HEAD

    def cidx(b, z, ci):
        return b * nc + ci + z * (nc - 1 - 2 * ci)

    dspec = pl.BlockSpec((None, c * per_step, d), lambda b, z, ci: (z, cidx(b, z, ci), 0))
    return pl.pallas_call(
        functools.partial(_scan_kernel, c=c, n_heads=n_heads, per_step=per_step),
        grid=(bsz, 2, nc),
        in_specs=[dspec, dspec, dspec, dspec,
                  pl.BlockSpec((c * per_step, d), lambda b, z, ci: (cidx(b, z, ci), 0)),
                  pl.BlockSpec((None, per_step, 8, d), lambda b, z, ci: (z, cidx(b, z, ci), 0, 0)),
                  dspec,
                  pl.BlockSpec(ln8.shape, lambda b, z, ci: (0, 0))],
        out_specs=dspec,
        out_shape=jax.ShapeDtypeStruct((2, t, d), F32),
        scratch_shapes=[pltpu.VMEM((n_heads // 2, 2 * RWKV_HEAD, 2 * RWKV_HEAD), F32)],
        compiler_params=_params(("parallel", "parallel", "arbitrary")),
        name="rwkv_scan",
    )(at, rt, bt, kt, v, ed, g, ln8)


def _rwkv_out_kernel(gy_ref, gb_ref, x_ref, ga_ref, wo_ref, o_ref):
    o = gy_ref[0] + gy_ref[1] + gb_ref[...]
    o_ref[...] = x_ref[...] + ga_ref[...] * _dot(o.astype(BF16), wo_ref[...])


def _rwkv_out(gy, gb, x2, gate, w_o, bsz, tm=512):
    t, d = x2.shape
    s = t // bsz
    tm = min(tm, s)
    ns = s // tm
    row = lambda b, i: (b * ns + i, 0)
    return pl.pallas_call(
        _rwkv_out_kernel,
        grid=(bsz, ns),
        in_specs=[pl.BlockSpec((2, tm, d), lambda b, i: (0, b * ns + i, 0)),
                  pl.BlockSpec((tm, d), row), pl.BlockSpec((tm, d), row),
                  pl.BlockSpec((None, 1, d), lambda b, i: (b, 0, 0)),
                  pl.BlockSpec(w_o.shape, lambda b, i: (0, 0))],
        out_specs=pl.BlockSpec((tm, d), row),
        out_shape=jax.ShapeDtypeStruct((t, d), F32),
        compiler_params=_params(("parallel", "parallel")),
        name="rwkv_out",
    )(gy, gb, x2, gate, w_o)


def _rwkv_layer(x2, bsz, hn, gate, p):
    at, rt, bt, kt, v, ed, g, gb = _rwkv_prep(hn, bsz, p, SCAN_DT)
    gy = _rwkv_scan(at, rt, bt, kt, v, ed, g, p["ln8"], bsz)
    return _rwkv_out(gy, gb, x2, gate, p["w_o"], bsz)


def _attn_group_kernel(slope_ref, vid_ref, q_ref, k_ref, v_ref, o_ref, lse_ref, tab_ref, *,
                       tq, sub, w, half, offs, nh, dh, hg, rps):
    first = (pl.program_id(0) == 0) & (pl.program_id(1) == 0) & (pl.program_id(2) == 0)
    i = pl.program_id(2)

    @pl.when(first)
    def _():
        col = lax.broadcasted_iota(jnp.int32, (tq, w), 1)
        row = lax.broadcasted_iota(jnp.int32, (tq, w), 0)
        for v, off in enumerate(offs):
            dist = jnp.abs(col - row + off)
            distf = dist.astype(F32)
            for h in range(nh):
                tab_ref[h * len(offs) + v] = jnp.where(dist <= half, -slope_ref[h] * distf, NEG_INF)

    if w == sub:
        ws = 0
    else:
        ws = pl.multiple_of(jnp.clip(i * tq - half, 0, sub - w), 16)
    var = vid_ref[i]
    nt = (((1,), (1,)), ((), ()))
    lane = lax.broadcasted_iota(jnp.int32, (tq, 128), 1)
    for rr in range(rps):
        qrows = slice(rr * tq, (rr + 1) * tq)
        krows = pl.ds(rr * sub + ws, w)
        lse = jnp.zeros((tq, 128), F32)
        for h0 in range(0, nh, hg):
            heads = range(h0, min(h0 + hg, nh))
            hs = {h: slice(h * dh, (h + 1) * dh) for h in heads}
            sc = {h: lax.dot_general(q_ref[qrows, hs[h]], k_ref[krows, hs[h]], nt, preferred_element_type=F32)
                  + tab_ref[h * len(offs) + var] for h in heads}
            m = {h: sc[h].max(axis=-1, keepdims=True) for h in heads}
            p = {h: jnp.exp(sc[h] - m[h]) for h in heads}
            den = {h: p[h].sum(axis=-1, keepdims=True) for h in heads}
            acc = {h: _dot(p[h].astype(BF16), v_ref[krows, hs[h]]) for h in heads}
            for h in heads:
                o_ref[qrows, hs[h]] = (acc[h] / den[h]).astype(o_ref.dtype)
                lse = jnp.where(lane == h, m[h] + jnp.log(den[h]), lse)
        lse_ref[qrows, :] = lse


def _alibi_slopes(n):
    return 2.0 ** (-8.0 * jnp.arange(1, n + 1, dtype=F32) / n)


def _attention_group(qkv, bsz, gi, tq=256):
    t, width = qkv.shape
    s = t // bsz
    window, dil = ATTN_GROUPS[gi]
    half = window // (2 * dil)
    sub = s // dil
    tq = min(tq, sub)
    nq = sub // tq
    w = min(sub, tq + 2 * half)
    nh, dh = ATTN_HEADS, ATTN_HEAD_DIM
    hw = nh * dh
    offs = sorted({int(np.clip(i * tq - half, 0, sub - w)) - i * tq for i in range(nq)})
    vid = jnp.asarray([offs.index(int(np.clip(i * tq - half, 0, sub - w)) - i * tq) for i in range(nq)], jnp.int32)
    slopes = _alibi_slopes(len(ATTN_GROUPS) * nh).reshape(len(ATTN_GROUPS), nh)[gi] * dil
    smem = pl.BlockSpec(memory_space=pltpu.SMEM)
    rps = min(dil, 4) if nq == 1 else 1
    nr = dil // rps
    qrow = lambda b, r, i: ((b * nr + r) * nq + i, 0)
    return pl.pallas_call(
        functools.partial(_attn_group_kernel, tq=tq, sub=sub, w=w, half=half, offs=tuple(offs), nh=nh, dh=dh,
                          hg=nh if tq * w <= 128 * 128 else 4, rps=rps),
        grid=(bsz, nr, nq),
        in_specs=[smem, smem,
                  pl.BlockSpec((rps * tq, hw), qrow),
                  pl.BlockSpec((rps * sub, hw), lambda b, r, i: (b * nr + r, 1)),
                  pl.BlockSpec((rps * sub, hw), lambda b, r, i: (b * nr + r, 2))],
        out_specs=[pl.BlockSpec((rps * tq, hw), qrow), pl.BlockSpec((rps * tq, 128), qrow)],
        out_shape=[jax.ShapeDtypeStruct((t, hw), BF16), jax.ShapeDtypeStruct((t, 128), F32)],
        scratch_shapes=[pltpu.VMEM((nh * len(offs), tq, w), F32)],
        compiler_params=_params(("arbitrary", "arbitrary", "arbitrary")),
        name=f"dilated_attention_g{gi}",
    )(slopes, vid, qkv, qkv, qkv)


def _to_token_order(blk_ref, scr_ref, dil):
    if dil == 1:
        return blk_ref[0].astype(F32)
    n = blk_ref.shape[1]
    slabs = scr_ref.shape[0]
    for r in range(dil):
        rows = blk_ref[r].astype(F32)
        for c in range(slabs):
            scr_ref[c, pl.ds(r, n, stride=dil), :] = rows[:, c * 128:(c + 1) * 128]
    if slabs == 1:
        return scr_ref[0]
    return jnp.concatenate([scr_ref[c] for c in range(slabs)], axis=1)


def _attn_out_kernel(o1, o2, o3, l1, l2, l3, ex_ref, w_ref, x_ref, ga_ref, out_ref, so_ref, sl_ref, *, dils):
    ls = [_to_token_order(l, sl_ref, dil) for l, dil in zip((l1, l2, l3), dils)]
    m = jnp.maximum(jnp.maximum(ls[0], ls[1]), ls[2])
    es = [jnp.exp(l - m) for l in ls]
    inv = 1.0 / (es[0] + es[1] + es[2])
    merged = None
    for e, o_ref, dil in zip(es, (o1, o2, o3), dils):
        ah, al = _split2(e * inv)
        alpha = _dot(ah, ex_ref[...]) + _dot(al, ex_ref[...])
        term = alpha * _to_token_order(o_ref, so_ref, dil)
        merged = term if merged is None else merged + term
    out_ref[...] = x_ref[...] + ga_ref[...] * _dot(merged.astype(BF16), w_ref[...])


def _attn_out(os, lses, w_o, x2, gate, bsz, tm=512):
    t, d = x2.shape
    s = t // bsz
    tm = min(tm, s)
    ns = s // tm
    hw = os[0].shape[1]
    dils = tuple(dil for _, dil in ATTN_GROUPS)
    ex = np.zeros((128, hw), np.float32)
    ex[np.arange(hw) // ATTN_HEAD_DIM, np.arange(hw)] = 1.0
    ex = jnp.asarray(ex, BF16)
    row = lambda b, i: (b * ns + i, 0)

    def res_major(a, dil):
        c = a.shape[1]
        return a.reshape(bsz, dil, s // dil, c), pl.BlockSpec((None, dil, tm // dil, c), lambda b, i: (b, 0, i, 0))

    o_args, o_specs = zip(*[res_major(o, dil) for o, dil in zip(os, dils)])
    l_args, l_specs = zip(*[res_major(l, dil) for l, dil in zip(lses, dils)])
    return pl.pallas_call(
        functools.partial(_attn_out_kernel, dils=dils),
        grid=(bsz, ns),
        in_specs=list(o_specs) + list(l_specs) + [
            pl.BlockSpec(ex.shape, lambda b, i: (0, 0)),
            pl.BlockSpec(w_o.shape, lambda b, i: (0, 0)),
            pl.BlockSpec((tm, d), row),
            pl.BlockSpec((None, 1, d), lambda b, i: (b, 0, 0))],
        out_specs=pl.BlockSpec((tm, d), row),
        out_shape=jax.ShapeDtypeStruct((t, d), F32),
        scratch_shapes=[pltpu.VMEM((hw // 128, tm, 128), F32), pltpu.VMEM((1, tm, 128), F32)],
        compiler_params=_params(("parallel", "parallel")),
        name="attn_out",
    )(*o_args, *l_args, ex, w_o, x2, gate)


def _norm_attn_kernel(x_ref, g_ref, sh_ref, sc_ref, *rest, dils):
    outs, scr_ref = rest[:-1], rest[-1]
    x = x_ref[...]
    ms = jnp.mean(x * x, axis=-1, keepdims=True)
    y = x * lax.rsqrt(ms + RMS_EPS) * g_ref[...]
    y = y * (1.0 + sc_ref[...]) + sh_ref[...]
    slabs = scr_ref.shape[0]
    if any(dil > 1 for dil in dils):
        for c in range(slabs):
            scr_ref[c] = y[:, c * 128:(c + 1) * 128]
    for o, dil in zip(outs, dils):
        if dil == 1:
            o[0] = y.astype(o.dtype)
        else:
            n = o.shape[1]
            for r in range(dil):
                for c in range(slabs):
                    o[r, :, c * 128:(c + 1) * 128] = scr_ref[c, pl.ds(r, n, stride=dil), :].astype(o.dtype)


def _norm_mod_attn(x2, g, shift, scale, bsz, ts=512):
    t, d = x2.shape
    s = t // bsz
    ts = min(ts, s)
    ns = s // ts
    dils = tuple(dil for _, dil in ATTN_GROUPS)
    vec = pl.BlockSpec((None, 1, d), lambda b, i: (b, 0, 0))
    outs = pl.pallas_call(
        functools.partial(_norm_attn_kernel, dils=dils),
        grid=(bsz, ns),
        in_specs=[pl.BlockSpec((ts, d), lambda b, i: (b * ns + i, 0)),
                  pl.BlockSpec((1, d), lambda b, i: (0, 0)), vec, vec],
        out_specs=[pl.BlockSpec((None, dil, ts // dil, d), lambda b, i: (b, 0, i, 0)) for dil in dils],
        out_shape=[jax.ShapeDtypeStruct((bsz, dil, s // dil, d), BF16) for dil in dils],
        scratch_shapes=[pltpu.VMEM((d // 128, ts, 128), F32)],
        compiler_params=_params(("parallel", "parallel")),
        name="norm_mod_attn",
    )(x2, g.reshape(1, d), shift, scale)
    return [o.reshape(t, d) for o in outs]


def _pack_attn(j, w_qkv, w_o):
    d = w_qkv.shape[1]
    ng = len(ATTN_GROUPS)
    w = w_qkv[j].reshape(d, ng, 3, ATTN_HEADS * ATTN_HEAD_DIM)
    w = w.at[:, :, 0].multiply(ATTN_HEAD_DIM ** -0.5)
    return {"w_qkv": w.transpose(1, 0, 2, 3).reshape(ng, d, -1).astype(BF16), "w_o": w_o[j].astype(BF16)}


def _attn_layer(x2, bsz, hns, gate, p):
    outs = [_attention_group(_matmul(hn, p["w_qkv"][gi], BF16), bsz, gi) for gi, hn in enumerate(hns)]
    return _attn_out([o for o, _ in outs], [l for _, l in outs], p["w_o"], x2, gate, bsz)


def _router_kernel(x_ref, g_ref, sh_ref, sc_ref, wh_ref, wl_ref, b_ref, o_ref, hb_ref, seg_ref):
    x = x_ref[...]
    ms = jnp.mean(x * x, axis=-1, keepdims=True)
    hn = x * lax.rsqrt(ms + RMS_EPS) * g_ref[...]
    hn = hn * (1.0 + sc_ref[...]) + sh_ref[...]
    hb_ref[...] = hn.astype(hb_ref.dtype)
    hh, hl = _split2(hn)
    logits = _dot(hh, wh_ref[...]) + _dot(hl, wh_ref[...]) + _dot(hh, wl_ref[...]) + b_ref[...]
    lane = lax.broadcasted_iota(jnp.int32, logits.shape, 1)
    ng, ne = N_EXPERT_GROUPS, EXPERTS_PER_GROUP
    big = jnp.int32(ROUTER_LANES)
    is_g = lane < ng
    gl = jnp.where(is_g, logits, -jnp.inf)
    gmax = gl.max(axis=-1, keepdims=True)
    gsum = jnp.where(is_g, jnp.exp(logits - gmax), 0.0).sum(axis=-1, keepdims=True)
    p_group = 1.0 / gsum
    g_top = jnp.where(gl == gmax, lane, big).min(axis=-1, keepdims=True)
    lo = ng + ne * g_top
    es = jnp.where((lane >= lo) & (lane < lo + ne), logits, -jnp.inf)
    v1 = es.max(axis=-1, keepdims=True)
    i1 = jnp.where(es == v1, lane, big).min(axis=-1, keepdims=True)
    es2 = jnp.where(lane == i1, -jnp.inf, es)
    v2 = es2.max(axis=-1, keepdims=True)
    i2 = jnp.where(es2 == v2, lane, big).min(axis=-1, keepdims=True)
    e2 = jnp.exp(v2 - v1)
    w1 = p_group / (1.0 + e2)
    w2 = p_group * e2 / (1.0 + e2)
    gates = jnp.where(lane == i1, w1, 0.0) + jnp.where(lane == i2, w2, 0.0)

    tm = x.shape[0]
    onehot = jnp.where(lane == g_top, 1.0, 0.0)
    tri = jnp.where(lax.broadcasted_iota(jnp.int32, (tm, tm), 1) < lax.broadcasted_iota(jnp.int32, (tm, tm), 0),
                    1.0, 0.0).astype(BF16)
    before = _dot(tri, onehot.astype(BF16))
    counts = jnp.broadcast_to(onehot.sum(axis=0, keepdims=True), (8, ROUTER_LANES))
    upper = jnp.where(lax.broadcasted_iota(jnp.int32, (ROUTER_LANES, ROUTER_LANES), 0)
                      < lax.broadcasted_iota(jnp.int32, (ROUTER_LANES, ROUTER_LANES), 1), 1.0, 0.0).astype(BF16)
    ch, cl = _split2(counts)
    seg_lo = _dot(ch, upper) + _dot(cl, upper)
    dest = (onehot * (before + seg_lo[0:1])).sum(axis=-1, keepdims=True)
    o_ref[...] = gates + jnp.where(lane == DEST_LANE, dest, 0.0)
    row8 = lax.broadcasted_iota(jnp.int32, (8, ROUTER_LANES), 0)
    seg_ref[...] = jnp.where(row8 == 0, seg_lo, jnp.where(row8 == 1, seg_lo + counts, 0.0))


def _router(x2, g, shift, scale, wr_hi, wr_lo, bias, bsz):
    t, d = x2.shape
    s = t // bsz
    tm = min(MOE_TILE, s)
    ns = s // tm
    row = lambda b, i: (b * ns + i, 0)
    full = lambda a: pl.BlockSpec(a.shape, lambda b, i: (0, 0))
    vec = pl.BlockSpec((None, 1, d), lambda b, i: (b, 0, 0))
    g = g.reshape(1, d)
    return pl.pallas_call(
        _router_kernel,
        grid=(bsz, ns),
        in_specs=[pl.BlockSpec((tm, d), row), full(g), vec, vec, full(wr_hi), full(wr_lo), full(bias)],
        out_specs=[pl.BlockSpec((tm, ROUTER_LANES), row), pl.BlockSpec((tm, d), row),
                   pl.BlockSpec((None, 8, ROUTER_LANES), lambda b, i: (b * ns + i, 0, 0))],
        out_shape=[jax.ShapeDtypeStruct((t, ROUTER_LANES), F32), jax.ShapeDtypeStruct((t, d), BF16),
                   jax.ShapeDtypeStruct((t // tm, 8, ROUTER_LANES), F32)],
        compiler_params=_params(("parallel", "parallel")),
        name="moe_router",
    )(x2, g, shift, scale, wr_hi, wr_lo, bias)


def _experts_kernel(seg_ref, h_ref, gates_ref, wg_ref, wu_ref, wd_ref, x_ref, ga_ref, o_ref,
                    pt_ref, xs_ref, gs_ref, y_ref, *, per_step, blk):
    tile = pl.program_id(0)
    step = pl.program_id(1)
    tm = h_ref.shape[0]
    grp = step // (EXPERTS_PER_GROUP // per_step)
    tn = (((0,), (0,)), ((), ()))

    @pl.when(step == 0)
    def _():
        gates = gates_ref[...]
        lane = lax.broadcasted_iota(jnp.int32, gates.shape, 1)
        dest = jnp.where(lane == DEST_LANE, gates, 0.0).sum(axis=-1, keepdims=True)
        col = lax.broadcasted_iota(jnp.int32, (tm, tm), 1).astype(F32)
        pt = jnp.where(dest == col, 1.0, 0.0).astype(BF16)
        pt_ref[...] = pt
        xs_ref[...] = lax.dot_general(pt, h_ref[...], tn, preferred_element_type=F32).astype(BF16)
        g1 = gates.astype(BF16)
        r1 = gates - g1.astype(F32)
        g2 = r1.astype(BF16)
        g3 = (r1 - g2.astype(F32)).astype(BF16)
        gs_ref[...] = (lax.dot_general(pt, g1, tn, preferred_element_type=F32)
                       + lax.dot_general(pt, g2, tn, preferred_element_type=F32)
                       + lax.dot_general(pt, g3, tn, preferred_element_type=F32))
        y_ref[...] = jnp.zeros_like(y_ref)

    lo = seg_ref[tile * 8 + grp]
    hi = seg_ref[tile * 8 + N_EXPERT_GROUPS + grp]
    for b in range(tm // blk):
        @pl.when((lo < (b + 1) * blk) & (hi > b * blk))
        def _(b=b):
            rows = pl.ds(b * blk, blk)
            h = xs_ref[rows, :]
            gsb = gs_ref[rows, :]
            lane = lax.broadcasted_iota(jnp.int32, gsb.shape, 1)
            hids = []
            for j in range(per_step):
                e = step * per_step + j
                ge = jnp.where(lane == N_EXPERT_GROUPS + e, gsb, 0.0).sum(axis=-1, keepdims=True)
                gp = _dot(h, wg_ref[j])
                up = _dot(h, wu_ref[j])
                hids.append((gp * _sigmoid(gp) * up * ge).astype(BF16))
            y_ref[rows, :] += _dot(jnp.concatenate(hids, axis=1), wd_ref[...])

    @pl.when(step == pl.num_programs(1) - 1)
    def _():
        o_ref[...] = x_ref[...] + ga_ref[...] * _dot(pt_ref[...], y_ref[...].astype(BF16))


def _experts(hn_bf16, gates, seg, wg, wu, wd, x2, gate, bsz, per_step=4):
    t, d = x2.shape
    s = t // bsz
    tm = min(MOE_TILE, s)
    ns = s // tm
    ne, _, f = wg.shape
    blk = min(MOE_BLOCK, tm)
    row = lambda i, e, seg: (i, 0)
    wspec = pl.BlockSpec((per_step, d, f), lambda i, e, seg: (e, 0, 0))
    grid_spec = pltpu.PrefetchScalarGridSpec(
        num_scalar_prefetch=1,
        grid=(t // tm, ne // per_step),
        in_specs=[
            pl.BlockSpec((tm, d), row),
            pl.BlockSpec((tm, ROUTER_LANES), row),
            wspec, wspec,
            pl.BlockSpec((per_step * f, d), lambda i, e, seg: (e, 0)),
            pl.BlockSpec((tm, d), row),
            pl.BlockSpec((None, 1, d), lambda i, e, seg: (i // ns, 0, 0)),
        ],
        out_specs=pl.BlockSpec((tm, d), row),
        scratch_shapes=[pltpu.VMEM((tm, tm), BF16), pltpu.VMEM((tm, d), BF16),
                        pltpu.VMEM((tm, ROUTER_LANES), F32), pltpu.VMEM((tm, d), F32)],
    )
    return pl.pallas_call(
        functools.partial(_experts_kernel, per_step=per_step, blk=blk),
        grid_spec=grid_spec,
        out_shape=jax.ShapeDtypeStruct((t, d), F32),
        compiler_params=_params(("parallel", "arbitrary")),
        name="moe_experts",
    )(seg, hn_bf16, gates, wg, wu, wd, x2, gate)


def _moe_layer(x2, bsz, g, shift, scale, gate, p):
    gates, hn_bf16, seg = _router(x2, g, shift, scale, p["wr_hi"], p["wr_lo"], p["r_bias"], bsz)
    ng = N_EXPERT_GROUPS
    seg = jnp.concatenate([seg[:, 0, :ng], seg[:, 1, :ng]], axis=1).astype(jnp.int32).reshape(-1)
    return _experts(hn_bf16, gates, seg, p["wg"], p["wu"], p["wd"], x2, gate, bsz)


def _pad_rows(rows, d):
    out = jnp.zeros((8, d), F32)
    return out.at[:len(rows)].set(jnp.stack([r.reshape(d).astype(F32) for r in rows]))


def _pack_rwkv(j, mu, w_rkv, w0, w1, w2, a0, a1, a2, g1, g2, k_k, k_a, r_k, ln_w, ln_b, w_o):
    d = w_o.shape[-1]
    cat = lambda m: jnp.concatenate([m[j, 0], m[j, 1]], axis=1).astype(BF16)
    stack = lambda m: jnp.concatenate([m[j, 0], m[j, 1]], axis=0).astype(BF16)
    return {
        "mu8": _pad_rows(list(mu[j]), d),
        "w_rkv": w_rkv[j].astype(BF16),
        "w1c": cat(w1), "a1c": cat(a1), "g1c": cat(g1),
        "w2s": stack(w2), "a2s": stack(a2), "g2": g2[j].astype(BF16),
        "vec8": _pad_rows([w0[j, 0], w0[j, 1], a0[j, 0], a0[j, 1], k_k[j], k_a[j], r_k[j]], d),
        "ln8": _pad_rows([ln_w[j], ln_b[j]], d),
        "w_o": w_o[j].astype(BF16),
    }


def _pack_moe(i, router_g, router_g_b, router_e, router_e_b, w_gate, w_up, w_down):
    d = router_g.shape[1]
    ng, ne = N_EXPERT_GROUPS, EXPERTS_PER_GROUP
    wr = jnp.zeros((d, ROUTER_LANES), F32)
    wr = wr.at[:, :ng].set(router_g[i])
    wr = wr.at[:, ng:ng + ng * ne].set(router_e[i].transpose(1, 0, 2).reshape(d, ng * ne))
    bias = jnp.zeros((1, ROUTER_LANES), F32)
    bias = bias.at[0, :ng].set(router_g_b[i])
    bias = bias.at[0, ng:ng + ng * ne].set(router_e_b[i].reshape(ng * ne))
    wr_hi = wr.astype(BF16)
    wr_lo = (wr - wr_hi.astype(F32)).astype(BF16)
    f = w_gate.shape[-1]
    return {"wr_hi": wr_hi, "wr_lo": wr_lo, "r_bias": bias,
            "wg": w_gate[i].reshape(ng * ne, d, f).astype(BF16), "wu": w_up[i].reshape(ng * ne, d, f).astype(BF16),
            "wd": w_down[i].reshape(ng * ne * f, d).astype(BF16)}


def kernel(x, c, ada_w, ada_b, norm_tm_g, norm_cm_g, rw_mu, rw_w_rkv, rw_w0, rw_w1, rw_w2, rw_a0, rw_a1, rw_a2, rw_g1, rw_g2, rw_k_k, rw_k_a, rw_r_k, rw_ln_w, rw_ln_b, rw_w_o, at_w_qkv, at_w_o, moe_router_g, moe_router_g_b, moe_router_e, moe_router_e_b, moe_w_gate, moe_w_up, moe_w_down, final_g):
    bsz, s, d = x.shape
    depth = ada_w.shape[0]
    x2 = x.reshape(bsz * s, d)
    mod = _ada_mod(c, ada_w, ada_b)
    for i in range(depth):
        sh_t, sc_t, ga_t, sh_c, sc_c, ga_c = (mod[i, m] for m in range(N_MOD))
        j = i // 2
        if i % 2 == 0:
            (hn,) = _norm_mod(x2, norm_tm_g[i], sh_t, sc_t, bsz, (F32,))
            p = _pack_rwkv(j, rw_mu, rw_w_rkv, rw_w0, rw_w1, rw_w2, rw_a0, rw_a1, rw_a2, rw_g1, rw_g2,
                           rw_k_k, rw_k_a, rw_r_k, rw_ln_w, rw_ln_b, rw_w_o)
            x2 = _rwkv_layer(x2, bsz, hn, ga_t, p)
        else:
            hns = _norm_mod_attn(x2, norm_tm_g[i], sh_t, sc_t, bsz)
            p = _pack_attn(j, at_w_qkv, at_w_o)
            x2 = _attn_layer(x2, bsz, hns, ga_t, p)
        pm = _pack_moe(i, moe_router_g, moe_router_g_b, moe_router_e, moe_router_e_b,
                       moe_w_gate, moe_w_up, moe_w_down)
        x2 = _moe_layer(x2, bsz, norm_cm_g[i], sh_c, sc_c, ga_c, pm)
    (out,) = _norm_mod(x2, final_g, None, None, bsz, (F32,))
    return out.reshape(bsz, s, d)
```

```python
import functools
import math

import numpy as np
import jax
import jax.numpy as jnp
from jax import lax
from jax.experimental import pallas as pl
from jax.experimental.pallas import tpu as pltpu

F32 = jnp.float32
BF16 = jnp.bfloat16

RMS_EPS = 1e-6
N_MOD = 6
RWKV_HEAD = 64
RWKV_GN_EPS = 64e-5
ATTN_GROUPS = ((128, 1), (512, 4), (2048, 16))
ATTN_HEADS = 8
ATTN_HEAD_DIM = 128
NEG_INF = -1e30
N_EXPERT_GROUPS = 4
EXPERTS_PER_GROUP = 8
ROUTER_LANES = 128
DEST_LANE = 64
MOE_TILE = 1024
MOE_BLOCK = 128

SCAN_CHUNK = 64
SCAN_CHUNKS_PER_STEP = 4
V7X_VMEM_LIMIT = 56 * 1024 * 1024
SCAN_DT = BF16


def _params(sem):
    return pltpu.CompilerParams(dimension_semantics=sem, vmem_limit_bytes=V7X_VMEM_LIMIT)


def _dot(a, b):
    return jnp.dot(a, b, preferred_element_type=F32)


def _sigmoid(x):
    return 0.5 * jnp.tanh(0.5 * x) + 0.5


def _split2(x):
    hi = x.astype(BF16)
    lo = (x - hi.astype(F32)).astype(BF16)
    return hi, lo


def _ada_kernel(c_ref, w_ref, b_ref, o_ref):
    c = c_ref[...]
    sc = c * _sigmoid(c)
    o_ref[...] = _dot(sc.astype(BF16), w_ref[...].astype(BF16)) + b_ref[...]


def _ada_mod(c, ada_w, ada_b):
    depth, d, n = ada_w.shape
    bsz = c.shape[0]
    tn = n // 4
    out = pl.pallas_call(
        _ada_kernel,
        grid=(depth, n // tn),
        in_specs=[
            pl.BlockSpec((bsz, d), lambda i, j: (0, 0)),
            pl.BlockSpec((None, d, tn), lambda i, j: (i, 0, j)),
            pl.BlockSpec((None, 1, tn), lambda i, j: (i, 0, j)),
        ],
        out_specs=pl.BlockSpec((None, bsz, tn), lambda i, j: (i, 0, j)),
        out_shape=jax.ShapeDtypeStruct((depth, bsz, n), F32),
        compiler_params=_params(("parallel", "parallel")),
        name="ada_mod",
    )(c, ada_w, ada_b.reshape(depth, 1, n))
    return out.reshape(depth, bsz, N_MOD, 1, d).transpose(0, 2, 1, 3, 4)


def _norm_kernel(*refs, modulated, n_out):
    if modulated:
        x_ref, g_ref, sh_ref, sc_ref = refs[:4]
    else:
        x_ref, g_ref = refs[:2]
    outs = refs[-n_out:]
    x = x_ref[...]
    ms = jnp.mean(x * x, axis=-1, keepdims=True)
    y = x * lax.rsqrt(ms + RMS_EPS) * g_ref[...]
    if modulated:
        y = y * (1.0 + sc_ref[...]) + sh_ref[...]
    for o in outs:
        o[...] = y.astype(o.dtype)


def _norm_mod(x2, g, shift, scale, bsz, out_dtypes, ts=512):
    t, d = x2.shape
    s = t // bsz
    ts = min(ts, s)
    ns = s // ts
    modulated = shift is not None
    row_spec = pl.BlockSpec((ts, d), lambda b, i: (b * ns + i, 0))
    in_specs = [row_spec, pl.BlockSpec((1, d), lambda b, i: (0, 0))]
    args = [x2, g.reshape(1, d)]
    if modulated:
        vec = pl.BlockSpec((None, 1, d), lambda b, i: (b, 0, 0))
        in_specs += [vec, vec]
        args += [shift, scale]
    outs = pl.pallas_call(
        functools.partial(_norm_kernel, modulated=modulated, n_out=len(out_dtypes)),
        grid=(bsz, ns),
        in_specs=in_specs,
        out_specs=[row_spec] * len(out_dtypes),
        out_shape=[jax.ShapeDtypeStruct((t, d), dt) for dt in out_dtypes],
        compiler_params=_params(("parallel", "parallel")),
        name="norm_mod",
    )(*args)
    return outs


def _mm_kernel(a_ref, b_ref, o_ref):
    o_ref[...] = _dot(a_ref[...], b_ref[...]).astype(o_ref.dtype)


def _matmul(a, b, out_dtype, tm=1024, tn=1024):
    m, k = a.shape
    n = b.shape[1]
    tm, tn = min(tm, m), min(tn, n)
    return pl.pallas_call(
        _mm_kernel,
        grid=(n // tn, m // tm),
        in_specs=[
            pl.BlockSpec((tm, k), lambda j, i: (i, 0)),
            pl.BlockSpec((k, tn), lambda j, i: (0, j)),
        ],
        out_specs=pl.BlockSpec((tm, tn), lambda j, i: (i, j)),
        out_shape=jax.ShapeDtypeStruct((m, n), out_dtype),
        compiler_params=_params(("parallel", "parallel")),
        name="matmul",
    )(a, b)


def _seg_sum(x, seg_ref, segt_ref):
    xh, xl = _split2(x)
    s = _dot(xh, seg_ref[...]) + _dot(xl, seg_ref[...])
    sh, sl = _split2(s)
    return _dot(sh, segt_ref[...]) + _dot(sl, segt_ref[...])


def _rwkv_prep_kernel(x_ref, prev_ref, next_ref, ng_ref, sh_ref, sc_ref, mu_ref, wrkv_ref, w1_ref, a1_ref,
                      g1_ref, w2_ref, a2_ref, g2_ref, vec_ref, cm_ref, edm_ref, seg_ref, segt_ref,
                      at_ref, rt_ref, bt_ref, kt_ref, v_ref, ed_ref, g_ref, gb_ref, *, ts, ns):
    i = pl.program_id(1)

    def norm_mod(x):
        ms = jnp.mean(x * x, axis=-1, keepdims=True)
        return x * lax.rsqrt(ms + RMS_EPS) * ng_ref[...] * (1.0 + sc_ref[...]) + sh_ref[...]

    cur = norm_mod(x_ref[...])
    row = lax.broadcasted_iota(jnp.int32, (ts, 1), 0)
    prev_row = jnp.where(i > 0, norm_mod(prev_ref[7:8, :]), 0.0)
    next_row = jnp.where(i < ns - 1, norm_mod(next_ref[0:1, :]), 0.0)
    x_prev = jnp.where(row == 0, prev_row, pltpu.roll(cur, 1, 0))
    x_next = jnp.where(row == ts - 1, next_row, pltpu.roll(cur, ts - 1, 0))
    xx = 0.5 * (x_prev + x_next) - cur

    def mix(j):
        return (cur + xx * mu_ref[j:j + 1, :]).astype(BF16)

    r = _dot(mix(0), wrkv_ref[0])
    k = _dot(mix(1), wrkv_ref[1])
    v = _dot(mix(2), wrkv_ref[2])
    tw = jnp.tanh(_dot(mix(3), w1_ref[...]))
    ta = _dot(mix(4), a1_ref[...])
    tg = _sigmoid(_dot(mix(5), g1_ref[...]))
    lane = lax.broadcasted_iota(jnp.int32, (1, tw.shape[1]), 1)
    half = tw.shape[1] // 2
    hg = tg.shape[1] // 2

    k_k = vec_ref[4:5, :]
    k_a = vec_ref[5:6, :]
    r_k = vec_ref[6:7, :]
    kk = k * k_k
    ss = _seg_sum(kk * kk, seg_ref, segt_ref)
    kkn = kk * lax.rsqrt(jnp.maximum(ss, 1e-24))
    v_ref[...] = v.astype(v_ref.dtype)

    gb = jnp.zeros_like(cur)
    for z in range(2):
        sel = (lane < half) if z == 0 else (lane >= half)
        lw = _dot(jnp.where(sel, tw, 0.0).astype(BF16), w2_ref[...])
        la = _dot(jnp.where(sel, ta, 0.0).astype(BF16), a2_ref[...])
        g = _dot(tg[:, z * hg:(z + 1) * hg].astype(BF16), g2_ref[z])
        ld = -math.exp(-0.5) * _sigmoid(vec_ref[z:z + 1, :] + lw)
        a = _sigmoid(vec_ref[2 + z:3 + z, :] + la)
        kdir = k * (1.0 + (a - 1.0) * k_a)
        b = kkn * a
        parts = _split2(ld)
        cmz = cm_ref[z]
        lm = _dot(cmz, parts[0]) + _dot(cmz, parts[1])
        e_in = jnp.exp(lm)
        e_inv = jnp.exp(-lm)
        e_ex = jnp.exp(lm - ld)
        rt_ref[z] = (r * e_in).astype(rt_ref.dtype)
        at_ref[z] = (-kkn * e_ex).astype(at_ref.dtype)
        bt_ref[z] = (b * e_inv).astype(bt_ref.dtype)
        kt_ref[z] = (kdir * e_inv).astype(kt_ref.dtype)
        edz = edm_ref[z]
        ed = jnp.exp(_dot(edz, parts[0]) + _dot(edz, parts[1]))
        for ck in range(ed.shape[0] // 8):
            ed_ref[z, ck] = ed[ck * 8:(ck + 1) * 8]
        g_ref[z] = g.astype(g_ref.dtype)
        bonus = _seg_sum(r * kdir * r_k, seg_ref, segt_ref) * v
        gb = gb + g * bonus
    gb_ref[...] = gb


def _chunk_matrices(ts, c):
    t = np.arange(ts)
    same = (t[:, None] // c) == (t[None, :] // c)
    pos = t % c
    cm = np.zeros((2, ts, ts), np.float32)
    nck = ts // c
    edm = np.zeros((2, 8 * nck, ts), np.float32)
    tri_f = same & (t[None, :] <= t[:, None])
    sel_f = same & (pos[None, :] <= c // 2 - 1)
    cm[0] = tri_f.astype(np.float32) - sel_f.astype(np.float32)
    tri_b = same & (t[None, :] >= t[:, None])
    sel_b = same & (pos[None, :] >= c // 2)
    cm[1] = tri_b.astype(np.float32) - sel_b.astype(np.float32)
    for ck in range(nck):
        inck = (t // c) == ck
        edm[0, 8 * ck] = inck & (pos <= c // 2 - 1)
        edm[0, 8 * ck + 1] = inck & (pos > c // 2 - 1)
        edm[1, 8 * ck] = inck & (pos >= c // 2)
        edm[1, 8 * ck + 1] = inck & (pos < c // 2)
    return jnp.asarray(cm, BF16), jnp.asarray(edm, BF16)


def _seg_matrices(d, head):
    seg = np.zeros((d, 128), np.float32)
    seg[np.arange(d), np.arange(d) // head] = 1.0
    return jnp.asarray(seg, BF16), jnp.asarray(seg.T.copy(), BF16)


def _rwkv_prep(x2, norm_g, shift, scale, bsz, p, scan_dtype, ts=256):
    t, d = x2.shape
    s = t // bsz
    ts = min(ts, s)
    ns = s // ts
    c = SCAN_CHUNK
    assert ts % c == 0 and ts % 8 == 0
    nck = ts // c
    cm, edm = _chunk_matrices(ts, c)
    seg, segt = _seg_matrices(d, RWKV_HEAD)
    r8 = ts // 8
    nblk8 = t // 8
    row = lambda b, i: (b * ns + i, 0)
    full = lambda a: pl.BlockSpec(a.shape, lambda b, i, _n=a.ndim: (0,) * _n)
    weights = [p["mu8"], p["w_rkv"], p["w1c"], p["a1c"], p["g1c"], p["w2s"], p["a2s"], p["g2"],
               p["vec8"], cm, edm, seg, segt]
    dir_spec = pl.BlockSpec((2, ts, d), lambda b, i: (0, b * ns + i, 0))
    outs = pl.pallas_call(
        functools.partial(_rwkv_prep_kernel, ts=ts, ns=ns),
        grid=(bsz, ns),
        in_specs=[
            pl.BlockSpec((ts, d), row),
            pl.BlockSpec((8, d), lambda b, i: (jnp.maximum((b * ns + i) * r8 - 1, 0), 0)),
            pl.BlockSpec((8, d), lambda b, i: (jnp.minimum((b * ns + i + 1) * r8, nblk8 - 1), 0)),
            pl.BlockSpec((1, d), lambda b, i: (0, 0)),
            pl.BlockSpec((None, 1, d), lambda b, i: (b, 0, 0)),
            pl.BlockSpec((None, 1, d), lambda b, i: (b, 0, 0)),
        ] + [full(a) for a in weights],
        out_specs=[dir_spec, dir_spec, dir_spec, dir_spec,
                   pl.BlockSpec((ts, d), row),
                   pl.BlockSpec((2, nck, 8, d), lambda b, i: (0, b * ns + i, 0, 0)),
                   dir_spec,
                   pl.BlockSpec((ts, d), row)],
        out_shape=[jax.ShapeDtypeStruct((2, t, d), scan_dtype)] * 4 + [
            jax.ShapeDtypeStruct((t, d), scan_dtype),
            jax.ShapeDtypeStruct((2, t // c, 8, d), F32),
            jax.ShapeDtypeStruct((2, t, d), F32),
            jax.ShapeDtypeStruct((t, d), F32)],
        compiler_params=_params(("parallel", "parallel")),
        name="rwkv_prep",
    )(x2, x2, x2, norm_g.reshape(1, d), shift, scale, *weights)
    return outs


def _sdot(a, b, dims):
    return lax.dot_general(a.astype(SCAN_DT), b.astype(SCAN_DT), (dims, ((), ())),
                           preferred_element_type=F32)


def _scan_kernel(at_ref, rt_ref, bt_ref, kt_ref, v_ref, ed_ref, g_ref, ln_ref, y_ref, s_ref, *,
                 c, n_heads, per_step):
    z = pl.program_id(1)
    ci = pl.program_id(2)

    @pl.when(ci == 0)
    def _():
        s_ref[...] = jnp.zeros_like(s_ref)

    c2 = 2 * c
    pw_lanes = 2 * RWKV_HEAD
    row = lax.broadcasted_iota(jnp.int32, (c2, c2), 0)
    col = lax.broadcasted_iota(jnp.int32, (c2, c2), 1)
    same_blk = (row // c) == (col // c)
    diff = jnp.where(z == 0, row - col, col - row)
    strict = same_blk & (diff > 0)
    incl = same_blk & (diff >= 0)
    incl2 = jnp.concatenate([incl, incl], axis=1)
    lane_a = lax.broadcasted_iota(jnp.int32, (c, pw_lanes), 1) < RWKV_HEAD
    same_head = ((lax.broadcasted_iota(jnp.int32, (pw_lanes, pw_lanes), 0) // RWKV_HEAD)
                 == (lax.broadcasted_iota(jnp.int32, (pw_lanes, pw_lanes), 1) // RWKV_HEAD))
    n_steps = int(math.log2(c))
    pairs = range(n_heads // 2)
    sls = [pl.ds(hp * pw_lanes, pw_lanes) for hp in pairs]
    zero = jnp.zeros((c, pw_lanes), SCAN_DT)
    for j in range(per_step):
        jj = jnp.where(z == 0, j, per_step - 1 - j)
        rows = pl.ds(pl.multiple_of(jj * c, c), c)
        _scan_chunk(at_ref, rt_ref, bt_ref, kt_ref, v_ref, ed_ref.at[jj], g_ref, ln_ref, y_ref, s_ref, rows, pairs,
                    sls, zero, lane_a, strict, incl2, same_head, c, c2, pw_lanes, n_steps)


def _scan_chunk(at_ref, rt_ref, bt_ref, kt_ref, v_ref, ed_ref, g_ref, ln_ref, y_ref, s_ref, rows, pairs,
                sls, zero, lane_a, strict, incl2, same_head, c, c2, pw_lanes, n_steps):
    nt = (((1,), (1,)))
    nn = (((1,), (0,)))
    tn = (((0,), (0,)))
    lhs, rhs, vv, st = [], [], [], []
    for hp in pairs:
        a2 = at_ref[rows, sls[hp]]
        r2 = rt_ref[rows, sls[hp]]
        b2 = bt_ref[rows, sls[hp]]
        k2 = kt_ref[rows, sls[hp]]
        v2 = v_ref[rows, sls[hp]]
        lhs.append(jnp.concatenate([jnp.where(lane_a, a2, zero), jnp.where(lane_a, zero, a2),
                                    jnp.where(lane_a, r2, zero), jnp.where(lane_a, zero, r2)], axis=0))
        rhs.append(jnp.concatenate([b2, b2, k2, k2], axis=0))
        vv.append(jnp.concatenate([v2, v2], axis=0))
        st.append(s_ref[hp] * ed_ref[0:1, sls[hp]])
    gq = [_sdot(lhs[hp], jnp.concatenate([rhs[hp], st[hp].astype(SCAN_DT)], axis=0), nt) for hp in pairs]
    gm = [g[:, :2 * c2] for g in gq]
    q = [g[:, 2 * c2:] for g in gq]
    pw = [jnp.where(strict, gm[hp][:c2, :c2], 0.0).astype(SCAN_DT) for hp in pairs]
    m2 = [jnp.where(strict, gm[hp][:c2, c2:], 0.0).astype(SCAN_DT) for hp in pairs]
    m34 = [jnp.where(incl2, gm[hp][c2:, :], 0.0).astype(SCAN_DT) for hp in pairs]
    x = [q[hp][:c2] + _sdot(m2[hp], vv[hp], nn) for hp in pairs]
    for step in range(n_steps - 1):
        both = [_sdot(pw[hp], jnp.concatenate([x[hp].astype(SCAN_DT), pw[hp]], axis=1), nn) for hp in pairs]
        x = [x[hp] + both[hp][:, :pw_lanes] for hp in pairs]
        pw = [both[hp][:, pw_lanes:].astype(SCAN_DT) for hp in pairs]
    x = [x[hp] + _sdot(pw[hp], x[hp], nn) for hp in pairs]
    for hp in pairs:
        xs = x[hp].astype(SCAN_DT)
        uvs = jnp.concatenate([xs, vv[hp]], axis=0)
        ys = q[hp][c2:] + _sdot(m34[hp], uvs, nn)
        y = jnp.where(lane_a, ys[:c], ys[c:])
        inv_n = 1.0 / RWKV_HEAD
        sum_a = jnp.where(lane_a, y, 0.0).sum(axis=-1, keepdims=True)
        sum_b = jnp.where(lane_a, 0.0, y).sum(axis=-1, keepdims=True)
        yc = y - jnp.where(lane_a, sum_a, sum_b) * inv_n
        sq = yc * yc
        sq_a = jnp.where(lane_a, sq, 0.0).sum(axis=-1, keepdims=True)
        sq_b = jnp.where(lane_a, 0.0, sq).sum(axis=-1, keepdims=True)
        var = jnp.where(lane_a, sq_a, sq_b) * inv_n
        yn = yc * lax.rsqrt(var + RWKV_GN_EPS) * ln_ref[0:1, sls[hp]] + ln_ref[1:2, sls[hp]]
        y_ref[rows, sls[hp]] = g_ref[rows, sls[hp]] * yn
        u2 = jnp.where(lane_a, xs[:c], xs[c:])
        upd = _sdot(jnp.concatenate([u2, vv[hp][:c]], axis=0), rhs[hp][c:3 * c], tn)
        s_ref[hp] = (st[hp] + jnp.where(same_head, upd, 0.0)) * ed_ref[1:2, sls[hp]]


def _rwkv_scan(at, rt, bt, kt, v, ed, g, ln8, bsz):
    _, t, d = at.shape
    s = t // bsz
    c = SCAN_CHUNK
    per_step = min(SCAN_CHUNKS_PER_STEP, s // c)
    nc = s // (c * per_step)
    n_heads = d // RWKV_HEAD

    def cidx(b, z, ci):
        return b * nc + ci + z * (nc - 1 - 2 * ci)

    dspec = pl.BlockSpec((None, c * per_step, d), lambda b, z, ci: (z, cidx(b, z, ci), 0))
    return pl.pallas_call(
        functools.partial(_scan_kernel, c=c, n_heads=n_heads, per_step=per_step),
        grid=(bsz, 2, nc),
        in_specs=[dspec, dspec, dspec, dspec,
                  pl.BlockSpec((c * per_step, d), lambda b, z, ci: (cidx(b, z, ci), 0)),
                  pl.BlockSpec((None, per_step, 8, d), lambda b, z, ci: (z, cidx(b, z, ci), 0, 0)),
                  dspec,
                  pl.BlockSpec(ln8.shape, lambda b, z, ci: (0, 0))],
        out_specs=dspec,
        out_shape=jax.ShapeDtypeStruct((2, t, d), F32),
        scratch_shapes=[pltpu.VMEM((n_heads // 2, 2 * RWKV_HEAD, 2 * RWKV_HEAD), F32)],
        compiler_params=_params(("parallel", "parallel", "arbitrary")),
        name="rwkv_scan",
    )(at, rt, bt, kt, v, ed, g, ln8)


def _rwkv_out_kernel(gy_ref, gb_ref, x_ref, ga_ref, wo_ref, o_ref):
    o = gy_ref[0] + gy_ref[1] + gb_ref[...]
    o_ref[...] = x_ref[...] + ga_ref[...] * _dot(o.astype(BF16), wo_ref[...])


def _rwkv_out(gy, gb, x2, gate, w_o, bsz, tm=512):
    t, d = x2.shape
    s = t // bsz
    tm = min(tm, s)
    ns = s // tm
    row = lambda b, i: (b * ns + i, 0)
    return pl.pallas_call(
        _rwkv_out_kernel,
        grid=(bsz, ns),
        in_specs=[pl.BlockSpec((2, tm, d), lambda b, i: (0, b * ns + i, 0)),
                  pl.BlockSpec((tm, d), row), pl.BlockSpec((tm, d), row),
                  pl.BlockSpec((None, 1, d), lambda b, i: (b, 0, 0)),
                  pl.BlockSpec(w_o.shape, lambda b, i: (0, 0))],
        out_specs=pl.BlockSpec((tm, d), row),
        out_shape=jax.ShapeDtypeStruct((t, d), F32),
        compiler_params=_params(("parallel", "parallel")),
        name="rwkv_out",
    )(gy, gb, x2, gate, w_o)


def _rwkv_layer(x2, bsz, norm_g, shift, scale, gate, p):
    at, rt, bt, kt, v, ed, g, gb = _rwkv_prep(x2, norm_g, shift, scale, bsz, p, SCAN_DT)
    gy = _rwkv_scan(at, rt, bt, kt, v, ed, g, p["ln8"], bsz)
    return _rwkv_out(gy, gb, x2, gate, p["w_o"], bsz)


def _attn_group_kernel(slope_ref, vid_ref, q_ref, k_ref, v_ref, o_ref, lse_ref, tab_ref, *,
                       tq, sub, w, half, offs, nh, dh, hg, rps):
    first = (pl.program_id(0) == 0) & (pl.program_id(1) == 0) & (pl.program_id(2) == 0)
    i = pl.program_id(2)

    @pl.when(first)
    def _():
        col = lax.broadcasted_iota(jnp.int32, (tq, w), 1)
        row = lax.broadcasted_iota(jnp.int32, (tq, w), 0)
        for v, off in enumerate(offs):
            dist = jnp.abs(col - row + off)
            distf = dist.astype(F32)
            for h in range(nh):
                tab_ref[h * len(offs) + v] = jnp.where(dist <= half, -slope_ref[h] * distf, NEG_INF)

    if w == sub:
        ws = 0
    else:
        ws = pl.multiple_of(jnp.clip(i * tq - half, 0, sub - w), 16)
    var = vid_ref[i]
    nt = (((1,), (1,)), ((), ()))
    lane = lax.broadcasted_iota(jnp.int32, (tq, 128), 1)
    for rr in range(rps):
        qrows = slice(rr * tq, (rr + 1) * tq)
        krows = pl.ds(rr * sub + ws, w)
        lse = jnp.zeros((tq, 128), F32)
        for h0 in range(0, nh, hg):
            heads = range(h0, min(h0 + hg, nh))
            hs = {h: slice(h * dh, (h + 1) * dh) for h in heads}
            sc = {h: lax.dot_general(q_ref[qrows, hs[h]], k_ref[krows, hs[h]], nt, preferred_element_type=F32)
                  + tab_ref[h * len(offs) + var] for h in heads}
            m = {h: sc[h].max(axis=-1, keepdims=True) for h in heads}
            p = {h: jnp.exp(sc[h] - m[h]) for h in heads}
            den = {h: p[h].sum(axis=-1, keepdims=True) for h in heads}
            acc = {h: _dot(p[h].astype(BF16), v_ref[krows, hs[h]]) for h in heads}
            for h in heads:
                o_ref[qrows, hs[h]] = (acc[h] / den[h]).astype(o_ref.dtype)
                lse = jnp.where(lane == h, m[h] + jnp.log(den[h]), lse)
        lse_ref[qrows, :] = lse


def _alibi_slopes(n):
    return 2.0 ** (-8.0 * jnp.arange(1, n + 1, dtype=F32) / n)


def _attention_group(qkv, bsz, gi, tq=256):
    t, width = qkv.shape
    s = t // bsz
    window, dil = ATTN_GROUPS[gi]
    half = window // (2 * dil)
    sub = s // dil
    tq = min(tq, sub)
    nq = sub // tq
    w = min(sub, tq + 2 * half)
    nh, dh = ATTN_HEADS, ATTN_HEAD_DIM
    hw = nh * dh
    offs = sorted({int(np.clip(i * tq - half, 0, sub - w)) - i * tq for i in range(nq)})
    vid = jnp.asarray([offs.index(int(np.clip(i * tq - half, 0, sub - w)) - i * tq) for i in range(nq)], jnp.int32)
    slopes = _alibi_slopes(len(ATTN_GROUPS) * nh).reshape(len(ATTN_GROUPS), nh)[gi] * dil
    smem = pl.BlockSpec(memory_space=pltpu.SMEM)
    rps = min(dil, 4) if nq == 1 else 1
    nr = dil // rps
    qrow = lambda b, r, i: ((b * nr + r) * nq + i, 0)
    return pl.pallas_call(
        functools.partial(_attn_group_kernel, tq=tq, sub=sub, w=w, half=half, offs=tuple(offs), nh=nh, dh=dh,
                          hg=nh if tq * w <= 128 * 128 else 4, rps=rps),
        grid=(bsz, nr, nq),
        in_specs=[smem, smem,
                  pl.BlockSpec((rps * tq, hw), qrow),
                  pl.BlockSpec((rps * sub, hw), lambda b, r, i: (b * nr + r, 1)),
                  pl.BlockSpec((rps * sub, hw), lambda b, r, i: (b * nr + r, 2))],
        out_specs=[pl.BlockSpec((rps * tq, hw), qrow), pl.BlockSpec((rps * tq, 128), qrow)],
        out_shape=[jax.ShapeDtypeStruct((t, hw), BF16), jax.ShapeDtypeStruct((t, 128), F32)],
        scratch_shapes=[pltpu.VMEM((nh * len(offs), tq, w), F32)],
        compiler_params=_params(("arbitrary", "arbitrary", "arbitrary")),
        name=f"dilated_attention_g{gi}",
    )(slopes, vid, qkv, qkv, qkv)


def _to_token_order(blk_ref, scr_ref, dil):
    if dil == 1:
        return blk_ref[0].astype(F32)
    n = blk_ref.shape[1]
    slabs = scr_ref.shape[0]
    for r in range(dil):
        rows = blk_ref[r].astype(F32)
        for c in range(slabs):
            scr_ref[c, pl.ds(r, n, stride=dil), :] = rows[:, c * 128:(c + 1) * 128]
    if slabs == 1:
        return scr_ref[0]
    return jnp.concatenate([scr_ref[c] for c in range(slabs)], axis=1)


def _attn_out_kernel(o1, o2, o3, l1, l2, l3, ex_ref, w_ref, x_ref, ga_ref, out_ref, so_ref, sl_ref, *, dils):
    ls = [_to_token_order(l, sl_ref, dil) for l, dil in zip((l1, l2, l3), dils)]
    m = jnp.maximum(jnp.maximum(ls[0], ls[1]), ls[2])
    es = [jnp.exp(l - m) for l in ls]
    inv = 1.0 / (es[0] + es[1] + es[2])
    merged = None
    for e, o_ref, dil in zip(es, (o1, o2, o3), dils):
        ah, al = _split2(e * inv)
        alpha = _dot(ah, ex_ref[...]) + _dot(al, ex_ref[...])
        term = alpha * _to_token_order(o_ref, so_ref, dil)
        merged = term if merged is None else merged + term
    out_ref[...] = x_ref[...] + ga_ref[...] * _dot(merged.astype(BF16), w_ref[...])


def _attn_out(os, lses, w_o, x2, gate, bsz, tm=512):
    t, d = x2.shape
    s = t // bsz
    tm = min(tm, s)
    ns = s // tm
    hw = os[0].shape[1]
    dils = tuple(dil for _, dil in ATTN_GROUPS)
    ex = np.zeros((128, hw), np.float32)
    ex[np.arange(hw) // ATTN_HEAD_DIM, np.arange(hw)] = 1.0
    ex = jnp.asarray(ex, BF16)
    row = lambda b, i: (b * ns + i, 0)

    def res_major(a, dil):
        c = a.shape[1]
        return a.reshape(bsz, dil, s // dil, c), pl.BlockSpec((None, dil, tm // dil, c), lambda b, i: (b, 0, i, 0))

    o_args, o_specs = zip(*[res_major(o, dil) for o, dil in zip(os, dils)])
    l_args, l_specs = zip(*[res_major(l, dil) for l, dil in zip(lses, dils)])
    return pl.pallas_call(
        functools.partial(_attn_out_kernel, dils=dils),
        grid=(bsz, ns),
        in_specs=list(o_specs) + list(l_specs) + [
            pl.BlockSpec(ex.shape, lambda b, i: (0, 0)),
            pl.BlockSpec(w_o.shape, lambda b, i: (0, 0)),
            pl.BlockSpec((tm, d), row),
            pl.BlockSpec((None, 1, d), lambda b, i: (b, 0, 0))],
        out_specs=pl.BlockSpec((tm, d), row),
        out_shape=jax.ShapeDtypeStruct((t, d), F32),
        scratch_shapes=[pltpu.VMEM((hw // 128, tm, 128), F32), pltpu.VMEM((1, tm, 128), F32)],
        compiler_params=_params(("parallel", "parallel")),
        name="attn_out",
    )(*o_args, *l_args, ex, w_o, x2, gate)


def _norm_attn_kernel(x_ref, g_ref, sh_ref, sc_ref, *rest, dils):
    outs, scr_ref = rest[:-1], rest[-1]
    x = x_ref[...]
    ms = jnp.mean(x * x, axis=-1, keepdims=True)
    y = x * lax.rsqrt(ms + RMS_EPS) * g_ref[...]
    y = y * (1.0 + sc_ref[...]) + sh_ref[...]
    slabs = scr_ref.shape[0]
    if any(dil > 1 for dil in dils):
        for c in range(slabs):
            scr_ref[c] = y[:, c * 128:(c + 1) * 128]
    for o, dil in zip(outs, dils):
        if dil == 1:
            o[0] = y.astype(o.dtype)
        else:
            n = o.shape[1]
            for r in range(dil):
                for c in range(slabs):
                    o[r, :, c * 128:(c + 1) * 128] = scr_ref[c, pl.ds(r, n, stride=dil), :].astype(o.dtype)


def _norm_mod_attn(x2, g, shift, scale, bsz, ts=512):
    t, d = x2.shape
    s = t // bsz
    ts = min(ts, s)
    ns = s // ts
    dils = tuple(dil for _, dil in ATTN_GROUPS)
    vec = pl.BlockSpec((None, 1, d), lambda b, i: (b, 0, 0))
    outs = pl.pallas_call(
        functools.partial(_norm_attn_kernel, dils=dils),
        grid=(bsz, ns),
        in_specs=[pl.BlockSpec((ts, d), lambda b, i: (b * ns + i, 0)),
                  pl.BlockSpec((1, d), lambda b, i: (0, 0)), vec, vec],
        out_specs=[pl.BlockSpec((None, dil, ts // dil, d), lambda b, i: (b, 0, i, 0)) for dil in dils],
        out_shape=[jax.ShapeDtypeStruct((bsz, dil, s // dil, d), BF16) for dil in dils],
        scratch_shapes=[pltpu.VMEM((d // 128, ts, 128), F32)],
        compiler_params=_params(("parallel", "parallel")),
        name="norm_mod_attn",
    )(x2, g.reshape(1, d), shift, scale)
    return [o.reshape(t, d) for o in outs]


def _pack_attn(j, w_qkv, w_o):
    d = w_qkv.shape[1]
    ng = len(ATTN_GROUPS)
    w = w_qkv[j].reshape(d, ng, 3, ATTN_HEADS * ATTN_HEAD_DIM)
    w = w.at[:, :, 0].multiply(ATTN_HEAD_DIM ** -0.5)
    return {"w_qkv": w.transpose(1, 0, 2, 3).reshape(ng, d, -1).astype(BF16), "w_o": w_o[j].astype(BF16)}


def _attn_layer(x2, bsz, hns, gate, p):
    outs = [_attention_group(_matmul(hn, p["w_qkv"][gi], BF16), bsz, gi) for gi, hn in enumerate(hns)]
    return _attn_out([o for o, _ in outs], [l for _, l in outs], p["w_o"], x2, gate, bsz)


def _router_kernel(x_ref, g_ref, sh_ref, sc_ref, wh_ref, wl_ref, b_ref, o_ref, hb_ref, seg_ref):
    x = x_ref[...]
    ms = jnp.mean(x * x, axis=-1, keepdims=True)
    hn = x * lax.rsqrt(ms + RMS_EPS) * g_ref[...]
    hn = hn * (1.0 + sc_ref[...]) + sh_ref[...]
    hb_ref[...] = hn.astype(hb_ref.dtype)
    hh, hl = _split2(hn)
    logits = _dot(hh, wh_ref[...]) + _dot(hl, wh_ref[...]) + _dot(hh, wl_ref[...]) + b_ref[...]
    lane = lax.broadcasted_iota(jnp.int32, logits.shape, 1)
    ng, ne = N_EXPERT_GROUPS, EXPERTS_PER_GROUP
    big = jnp.int32(ROUTER_LANES)
    is_g = lane < ng
    gl = jnp.where(is_g, logits, -jnp.inf)
    gmax = gl.max(axis=-1, keepdims=True)
    gsum = jnp.where(is_g, jnp.exp(logits - gmax), 0.0).sum(axis=-1, keepdims=True)
    p_group = 1.0 / gsum
    g_top = jnp.where(gl == gmax, lane, big).min(axis=-1, keepdims=True)
    lo = ng + ne * g_top
    es = jnp.where((lane >= lo) & (lane < lo + ne), logits, -jnp.inf)
    v1 = es.max(axis=-1, keepdims=True)
    i1 = jnp.where(es == v1, lane, big).min(axis=-1, keepdims=True)
    es2 = jnp.where(lane == i1, -jnp.inf, es)
    v2 = es2.max(axis=-1, keepdims=True)
    i2 = jnp.where(es2 == v2, lane, big).min(axis=-1, keepdims=True)
    e2 = jnp.exp(v2 - v1)
    w1 = p_group / (1.0 + e2)
    w2 = p_group * e2 / (1.0 + e2)
    gates = jnp.where(lane == i1, w1, 0.0) + jnp.where(lane == i2, w2, 0.0)

    tm = x.shape[0]
    onehot = jnp.where(lane == g_top, 1.0, 0.0)
    tri = jnp.where(lax.broadcasted_iota(jnp.int32, (tm, tm), 1) < lax.broadcasted_iota(jnp.int32, (tm, tm), 0),
                    1.0, 0.0).astype(BF16)
    before = _dot(tri, onehot.astype(BF16))
    counts = jnp.broadcast_to(onehot.sum(axis=0, keepdims=True), (8, ROUTER_LANES))
    upper = jnp.where(lax.broadcasted_iota(jnp.int32, (ROUTER_LANES, ROUTER_LANES), 0)
                      < lax.broadcasted_iota(jnp.int32, (ROUTER_LANES, ROUTER_LANES), 1), 1.0, 0.0).astype(BF16)
    ch, cl = _split2(counts)
    seg_lo = _dot(ch, upper) + _dot(cl, upper)
    dest = (onehot * (before + seg_lo[0:1])).sum(axis=-1, keepdims=True)
    o_ref[...] = gates + jnp.where(lane == DEST_LANE, dest, 0.0)
    row8 = lax.broadcasted_iota(jnp.int32, (8, ROUTER_LANES), 0)
    seg_ref[...] = jnp.where(row8 == 0, seg_lo, jnp.where(row8 == 1, seg_lo + counts, 0.0))


def _router(x2, g, shift, scale, wr_hi, wr_lo, bias, bsz):
    t, d = x2.shape
    s = t // bsz
    tm = min(MOE_TILE, s)
    ns = s // tm
    row = lambda b, i: (b * ns + i, 0)
    full = lambda a: pl.BlockSpec(a.shape, lambda b, i: (0, 0))
    vec = pl.BlockSpec((None, 1, d), lambda b, i: (b, 0, 0))
    g = g.reshape(1, d)
    return pl.pallas_call(
        _router_kernel,
        grid=(bsz, ns),
        in_specs=[pl.BlockSpec((tm, d), row), full(g), vec, vec, full(wr_hi), full(wr_lo), full(bias)],
        out_specs=[pl.BlockSpec((tm, ROUTER_LANES), row), pl.BlockSpec((tm, d), row),
                   pl.BlockSpec((None, 8, ROUTER_LANES), lambda b, i: (b * ns + i, 0, 0))],
        out_shape=[jax.ShapeDtypeStruct((t, ROUTER_LANES), F32), jax.ShapeDtypeStruct((t, d), BF16),
                   jax.ShapeDtypeStruct((t // tm, 8, ROUTER_LANES), F32)],
        compiler_params=_params(("parallel", "parallel")),
        name="moe_router",
    )(x2, g, shift, scale, wr_hi, wr_lo, bias)


def _experts_kernel(seg_ref, h_ref, gates_ref, wg_ref, wu_ref, wd_ref, x_ref, ga_ref, o_ref,
                    pt_ref, xs_ref, gs_ref, y_ref, *, per_step, blk):
    tile = pl.program_id(0)
    step = pl.program_id(1)
    tm = h_ref.shape[0]
    grp = step // (EXPERTS_PER_GROUP // per_step)
    tn = (((0,), (0,)), ((), ()))

    @pl.when(step == 0)
    def _():
        gates = gates_ref[...]
        lane = lax.broadcasted_iota(jnp.int32, gates.shape, 1)
        dest = jnp.where(lane == DEST_LANE, gates, 0.0).sum(axis=-1, keepdims=True)
        col = lax.broadcasted_iota(jnp.int32, (tm, tm), 1).astype(F32)
        pt = jnp.where(dest == col, 1.0, 0.0).astype(BF16)
        pt_ref[...] = pt
        xs_ref[...] = lax.dot_general(pt, h_ref[...], tn, preferred_element_type=F32).astype(BF16)
        g1 = gates.astype(BF16)
        r1 = gates - g1.astype(F32)
        g2 = r1.astype(BF16)
        g3 = (r1 - g2.astype(F32)).astype(BF16)
        gs_ref[...] = (lax.dot_general(pt, g1, tn, preferred_element_type=F32)
                       + lax.dot_general(pt, g2, tn, preferred_element_type=F32)
                       + lax.dot_general(pt, g3, tn, preferred_element_type=F32))
        y_ref[...] = jnp.zeros_like(y_ref)

    lo = seg_ref[tile * 8 + grp]
    hi = seg_ref[tile * 8 + N_EXPERT_GROUPS + grp]
    for b in range(tm // blk):
        @pl.when((lo < (b + 1) * blk) & (hi > b * blk))
        def _(b=b):
            rows = pl.ds(b * blk, blk)
            h = xs_ref[rows, :]
            gsb = gs_ref[rows, :]
            lane = lax.broadcasted_iota(jnp.int32, gsb.shape, 1)
            hids = []
            for j in range(per_step):
                e = step * per_step + j
                ge = jnp.where(lane == N_EXPERT_GROUPS + e, gsb, 0.0).sum(axis=-1, keepdims=True)
                gp = _dot(h, wg_ref[j])
                up = _dot(h, wu_ref[j])
                hids.append((gp * _sigmoid(gp) * up * ge).astype(BF16))
            y_ref[rows, :] += _dot(jnp.concatenate(hids, axis=1), wd_ref[...])

    @pl.when(step == pl.num_programs(1) - 1)
    def _():
        o_ref[...] = x_ref[...] + ga_ref[...] * _dot(pt_ref[...], y_ref[...].astype(BF16))


def _experts(hn_bf16, gates, seg, wg, wu, wd, layer, x2, gate, bsz, per_step=4):
    t, d = x2.shape
    s = t // bsz
    tm = min(MOE_TILE, s)
    ns = s // tm
    f = wg.shape[2]
    ne = N_EXPERT_GROUPS * EXPERTS_PER_GROUP
    steps = ne // per_step
    blk = min(MOE_BLOCK, tm)
    row = lambda i, e, seg: (i, 0)
    wspec = pl.BlockSpec((per_step, d, f), lambda i, e, seg: (layer * steps + e, 0, 0))
    grid_spec = pltpu.PrefetchScalarGridSpec(
        num_scalar_prefetch=1,
        grid=(t // tm, steps),
        in_specs=[
            pl.BlockSpec((tm, d), row),
            pl.BlockSpec((tm, ROUTER_LANES), row),
            wspec, wspec,
            pl.BlockSpec((per_step * f, d), lambda i, e, seg: (layer * steps + e, 0)),
            pl.BlockSpec((tm, d), row),
            pl.BlockSpec((None, 1, d), lambda i, e, seg: (i // ns, 0, 0)),
        ],
        out_specs=pl.BlockSpec((tm, d), row),
        scratch_shapes=[pltpu.VMEM((tm, tm), BF16), pltpu.VMEM((tm, d), BF16),
                        pltpu.VMEM((tm, ROUTER_LANES), F32), pltpu.VMEM((tm, d), F32)],
    )
    return pl.pallas_call(
        functools.partial(_experts_kernel, per_step=per_step, blk=blk),
        grid_spec=grid_spec,
        out_shape=jax.ShapeDtypeStruct((t, d), F32),
        compiler_params=_params(("parallel", "arbitrary")),
        name="moe_experts",
    )(seg, hn_bf16, gates, wg, wu, wd, x2, gate)


def _moe_layer(x2, bsz, g, shift, scale, gate, p):
    gates, hn_bf16, seg = _router(x2, g, shift, scale, p["wr_hi"], p["wr_lo"], p["r_bias"], bsz)
    ng = N_EXPERT_GROUPS
    seg = jnp.concatenate([seg[:, 0, :ng], seg[:, 1, :ng]], axis=1).astype(jnp.int32).reshape(-1)
    return _experts(hn_bf16, gates, seg, p["wg"], p["wu"], p["wd"], p["layer"], x2, gate, bsz)


def _pad_rows(rows, d):
    out = jnp.zeros((8, d), F32)
    return out.at[:len(rows)].set(jnp.stack([r.reshape(d).astype(F32) for r in rows]))


def _pack_rwkv(j, mu, w_rkv, w0, w1, w2, a0, a1, a2, g1, g2, k_k, k_a, r_k, ln_w, ln_b, w_o):
    d = w_o.shape[-1]
    cat = lambda m: jnp.concatenate([m[j, 0], m[j, 1]], axis=1).astype(BF16)
    stack = lambda m: jnp.concatenate([m[j, 0], m[j, 1]], axis=0).astype(BF16)
    return {
        "mu8": _pad_rows(list(mu[j]), d),
        "w_rkv": w_rkv[j].astype(BF16),
        "w1c": cat(w1), "a1c": cat(a1), "g1c": cat(g1),
        "w2s": stack(w2), "a2s": stack(a2), "g2": g2[j].astype(BF16),
        "vec8": _pad_rows([w0[j, 0], w0[j, 1], a0[j, 0], a0[j, 1], k_k[j], k_a[j], r_k[j]], d),
        "ln8": _pad_rows([ln_w[j], ln_b[j]], d),
        "w_o": w_o[j].astype(BF16),
    }


def _pack_moe(i, router_g, router_g_b, router_e, router_e_b, w_gate, w_up, w_down):
    d = router_g.shape[1]
    ng, ne = N_EXPERT_GROUPS, EXPERTS_PER_GROUP
    wr = jnp.zeros((d, ROUTER_LANES), F32)
    wr = wr.at[:, :ng].set(router_g[i])
    wr = wr.at[:, ng:ng + ng * ne].set(router_e[i].transpose(1, 0, 2).reshape(d, ng * ne))
    bias = jnp.zeros((1, ROUTER_LANES), F32)
    bias = bias.at[0, :ng].set(router_g_b[i])
    bias = bias.at[0, ng:ng + ng * ne].set(router_e_b[i].reshape(ng * ne))
    wr_hi = wr.astype(BF16)
    wr_lo = (wr - wr_hi.astype(F32)).astype(BF16)
    f = w_gate.shape[-1]
    return {"wr_hi": wr_hi, "wr_lo": wr_lo, "r_bias": bias,
            "layer": i,
            "wg": w_gate.reshape(-1, d, f).astype(BF16), "wu": w_up.reshape(-1, d, f).astype(BF16),
            "wd": w_down.reshape(-1, d).astype(BF16)}


def kernel(x, c, ada_w, ada_b, norm_tm_g, norm_cm_g, rw_mu, rw_w_rkv, rw_w0, rw_w1, rw_w2, rw_a0, rw_a1, rw_a2, rw_g1, rw_g2, rw_k_k, rw_k_a, rw_r_k, rw_ln_w, rw_ln_b, rw_w_o, at_w_qkv, at_w_o, moe_router_g, moe_router_g_b, moe_router_e, moe_router_e_b, moe_w_gate, moe_w_up, moe_w_down, final_g):
    bsz, s, d = x.shape
    depth = ada_w.shape[0]
    x2 = x.reshape(bsz * s, d)
    mod = _ada_mod(c, ada_w, ada_b)
    for i in range(depth):
        sh_t, sc_t, ga_t, sh_c, sc_c, ga_c = (mod[i, m] for m in range(N_MOD))
        j = i // 2
        if i % 2 == 0:
            p = _pack_rwkv(j, rw_mu, rw_w_rkv, rw_w0, rw_w1, rw_w2, rw_a0, rw_a1, rw_a2, rw_g1, rw_g2,
                           rw_k_k, rw_k_a, rw_r_k, rw_ln_w, rw_ln_b, rw_w_o)
            x2 = _rwkv_layer(x2, bsz, norm_tm_g[i], sh_t, sc_t, ga_t, p)
        else:
            hns = _norm_mod_attn(x2, norm_tm_g[i], sh_t, sc_t, bsz)
            p = _pack_attn(j, at_w_qkv, at_w_o)
            x2 = _attn_layer(x2, bsz, hns, ga_t, p)
        pm = _pack_moe(i, moe_router_g, moe_router_g_b, moe_router_e, moe_router_e_b,
                       moe_w_gate, moe_w_up, moe_w_down)
        x2 = _moe_layer(x2, bsz, norm_cm_g[i], sh_c, sc_c, ga_c, pm)
    (out,) = _norm_mod(x2, final_g, None, None, bsz, (F32,))
    return out.reshape(bsz, s, d)
```

```python
import functools
import math

import numpy as np
import jax
import jax.numpy as jnp
from jax import lax
from jax.experimental import pallas as pl
from jax.experimental.pallas import tpu as pltpu

F32 = jnp.float32
BF16 = jnp.bfloat16

RMS_EPS = 1e-6
N_MOD = 6
RWKV_HEAD = 64
RWKV_GN_EPS = 64e-5
ATTN_GROUPS = ((128, 1), (512, 4), (2048, 16))
ATTN_HEADS = 8
ATTN_HEAD_DIM = 128
NEG_INF = -1e30
N_EXPERT_GROUPS = 4
EXPERTS_PER_GROUP = 8
ROUTER_LANES = 128
DEST_LANE = 64
MOE_TILE = 1024
MOE_BLOCK = 128

SCAN_CHUNK = 64
SCAN_CHUNKS_PER_STEP = 4
V7X_VMEM_LIMIT = 56 * 1024 * 1024
SCAN_DT = BF16


def _params(sem):
    return pltpu.CompilerParams(dimension_semantics=sem, vmem_limit_bytes=V7X_VMEM_LIMIT)


def _dot(a, b):
    return jnp.dot(a, b, preferred_element_type=F32)


def _sigmoid(x):
    return 0.5 * jnp.tanh(0.5 * x) + 0.5


def _split2(x):
    hi = x.astype(BF16)
    lo = (x - hi.astype(F32)).astype(BF16)
    return hi, lo


def _ada_kernel(c_ref, w_ref, b_ref, o_ref):
    c = c_ref[...]
    sc = c * _sigmoid(c)
    o_ref[...] = _dot(sc.astype(BF16), w_ref[...].astype(BF16)) + b_ref[...]


def _ada_mod(c, ada_w, ada_b):
    depth, d, n = ada_w.shape
    bsz = c.shape[0]
    tn = n // 4
    out = pl.pallas_call(
        _ada_kernel,
        grid=(depth, n // tn),
        in_specs=[
            pl.BlockSpec((bsz, d), lambda i, j: (0, 0)),
            pl.BlockSpec((None, d, tn), lambda i, j: (i, 0, j)),
            pl.BlockSpec((None, 1, tn), lambda i, j: (i, 0, j)),
        ],
        out_specs=pl.BlockSpec((None, bsz, tn), lambda i, j: (i, 0, j)),
        out_shape=jax.ShapeDtypeStruct((depth, bsz, n), F32),
        compiler_params=_params(("parallel", "parallel")),
        name="ada_mod",
    )(c, ada_w, ada_b.reshape(depth, 1, n))
    return out.reshape(depth, bsz, N_MOD, 1, d).transpose(0, 2, 1, 3, 4)


def _mm_kernel(a_ref, b_ref, o_ref):
    o_ref[...] = _dot(a_ref[...], b_ref[...]).astype(o_ref.dtype)


def _matmul(a, b, out_dtype, n=None, col0=0, tm=1024, tn=1024):
    m, k = a.shape
    n = b.shape[1] if n is None else n
    tm, tn = min(tm, m), min(tn, n)
    cb0 = col0 // tn
    return pl.pallas_call(
        _mm_kernel,
        grid=(n // tn, m // tm),
        in_specs=[
            pl.BlockSpec((tm, k), lambda j, i: (i, 0)),
            pl.BlockSpec((k, tn), lambda j, i: (0, cb0 + j)),
        ],
        out_specs=pl.BlockSpec((tm, tn), lambda j, i: (i, j)),
        out_shape=jax.ShapeDtypeStruct((m, n), out_dtype),
        compiler_params=_params(("parallel", "parallel")),
        name="matmul",
    )(a, b)


def _seg_sum(x, seg_ref, segt_ref):
    xh, xl = _split2(x)
    s = _dot(xh, seg_ref[...]) + _dot(xl, seg_ref[...])
    sh, sl = _split2(s)
    return _dot(sh, segt_ref[...]) + _dot(sl, segt_ref[...])


def _rwkv_prep_kernel(x_ref, prev_ref, next_ref, ng_ref, sh_ref, sc_ref, mu_ref, wrkv_ref, w1_ref, a1_ref,
                      g1_ref, w2_ref, a2_ref, g2_ref, vec_ref, cm_ref, edm_ref, seg_ref, segt_ref,
                      at_ref, rt_ref, bt_ref, kt_ref, v_ref, ed_ref, g_ref, gb_ref, *, ts, ns):
    i = pl.program_id(1)

    def norm_mod(x):
        ms = jnp.mean(x * x, axis=-1, keepdims=True)
        return x * lax.rsqrt(ms + RMS_EPS) * ng_ref[...] * (1.0 + sc_ref[...]) + sh_ref[...]

    cur = norm_mod(x_ref[...])
    row = lax.broadcasted_iota(jnp.int32, (ts, 1), 0)
    prev_row = jnp.where(i > 0, norm_mod(prev_ref[7:8, :]), 0.0)
    next_row = jnp.where(i < ns - 1, norm_mod(next_ref[0:1, :]), 0.0)
    x_prev = jnp.where(row == 0, prev_row, pltpu.roll(cur, 1, 0))
    x_next = jnp.where(row == ts - 1, next_row, pltpu.roll(cur, ts - 1, 0))
    xx = 0.5 * (x_prev + x_next) - cur

    def mix(j):
        return (cur + xx * mu_ref[j:j + 1, :]).astype(BF16)

    r = _dot(mix(0), wrkv_ref[0])
    k = _dot(mix(1), wrkv_ref[1])
    v = _dot(mix(2), wrkv_ref[2])
    tw = jnp.tanh(_dot(mix(3), w1_ref[...]))
    ta = _dot(mix(4), a1_ref[...])
    tg = _sigmoid(_dot(mix(5), g1_ref[...]))
    lane = lax.broadcasted_iota(jnp.int32, (1, tw.shape[1]), 1)
    half = tw.shape[1] // 2
    hg = tg.shape[1] // 2

    k_k = vec_ref[4:5, :]
    k_a = vec_ref[5:6, :]
    r_k = vec_ref[6:7, :]
    kk = k * k_k
    ss = _seg_sum(kk * kk, seg_ref, segt_ref)
    kkn = kk * lax.rsqrt(jnp.maximum(ss, 1e-24))
    v_ref[...] = v.astype(v_ref.dtype)

    gb = jnp.zeros_like(cur)
    for z in range(2):
        sel = (lane < half) if z == 0 else (lane >= half)
        lw = _dot(jnp.where(sel, tw, 0.0).astype(BF16), w2_ref[...])
        la = _dot(jnp.where(sel, ta, 0.0).astype(BF16), a2_ref[...])
        g = _dot(tg[:, z * hg:(z + 1) * hg].astype(BF16), g2_ref[z])
        ld = -math.exp(-0.5) * _sigmoid(vec_ref[z:z + 1, :] + lw)
        a = _sigmoid(vec_ref[2 + z:3 + z, :] + la)
        kdir = k * (1.0 + (a - 1.0) * k_a)
        b = kkn * a
        parts = _split2(ld)
        cmz = cm_ref[z]
        lm = _dot(cmz, parts[0]) + _dot(cmz, parts[1])
        e_in = jnp.exp(lm)
        e_inv = jnp.exp(-lm)
        e_ex = jnp.exp(lm - ld)
        rt_ref[z] = (r * e_in).astype(rt_ref.dtype)
        at_ref[z] = (-kkn * e_ex).astype(at_ref.dtype)
        bt_ref[z] = (b * e_inv).astype(bt_ref.dtype)
        kt_ref[z] = (kdir * e_inv).astype(kt_ref.dtype)
        edz = edm_ref[z]
        ed = jnp.exp(_dot(edz, parts[0]) + _dot(edz, parts[1]))
        for ck in range(ed.shape[0] // 8):
            ed_ref[z, ck] = ed[ck * 8:(ck + 1) * 8]
        g_ref[z] = g.astype(g_ref.dtype)
        bonus = _seg_sum(r * kdir * r_k, seg_ref, segt_ref) * v
        gb = gb + g * bonus
    gb_ref[...] = gb


def _chunk_matrices(ts, c):
    t = np.arange(ts)
    same = (t[:, None] // c) == (t[None, :] // c)
    pos = t % c
    cm = np.zeros((2, ts, ts), np.float32)
    nck = ts // c
    edm = np.zeros((2, 8 * nck, ts), np.float32)
    tri_f = same & (t[None, :] <= t[:, None])
    sel_f = same & (pos[None, :] <= c // 2 - 1)
    cm[0] = tri_f.astype(np.float32) - sel_f.astype(np.float32)
    tri_b = same & (t[None, :] >= t[:, None])
    sel_b = same & (pos[None, :] >= c // 2)
    cm[1] = tri_b.astype(np.float32) - sel_b.astype(np.float32)
    for ck in range(nck):
        inck = (t // c) == ck
        edm[0, 8 * ck] = inck & (pos <= c // 2 - 1)
        edm[0, 8 * ck + 1] = inck & (pos > c // 2 - 1)
        edm[1, 8 * ck] = inck & (pos >= c // 2)
        edm[1, 8 * ck + 1] = inck & (pos < c // 2)
    return jnp.asarray(cm, BF16), jnp.asarray(edm, BF16)


def _seg_matrices(d, head):
    seg = np.zeros((d, 128), np.float32)
    seg[np.arange(d), np.arange(d) // head] = 1.0
    return jnp.asarray(seg, BF16), jnp.asarray(seg.T.copy(), BF16)


def _rwkv_prep(x2, norm_g, shift, scale, bsz, p, scan_dtype, ts=256):
    t, d = x2.shape
    s = t // bsz
    ts = min(ts, s)
    ns = s // ts
    c = SCAN_CHUNK
    assert ts % c == 0 and ts % 8 == 0
    nck = ts // c
    cm, edm = _chunk_matrices(ts, c)
    seg, segt = _seg_matrices(d, RWKV_HEAD)
    r8 = ts // 8
    nblk8 = t // 8
    row = lambda b, i: (b * ns + i, 0)
    full = lambda a: pl.BlockSpec(a.shape, lambda b, i, _n=a.ndim: (0,) * _n)
    weights = [p["mu8"], p["w_rkv"], p["w1c"], p["a1c"], p["g1c"], p["w2s"], p["a2s"], p["g2"],
               p["vec8"], cm, edm, seg, segt]
    dir_spec = pl.BlockSpec((2, ts, d), lambda b, i: (0, b * ns + i, 0))
    outs = pl.pallas_call(
        functools.partial(_rwkv_prep_kernel, ts=ts, ns=ns),
        grid=(bsz, ns),
        in_specs=[
            pl.BlockSpec((ts, d), row),
            pl.BlockSpec((8, d), lambda b, i: (jnp.maximum((b * ns + i) * r8 - 1, 0), 0)),
            pl.BlockSpec((8, d), lambda b, i: (jnp.minimum((b * ns + i + 1) * r8, nblk8 - 1), 0)),
            pl.BlockSpec((1, d), lambda b, i: (0, 0)),
            pl.BlockSpec((None, 1, d), lambda b, i: (b, 0, 0)),
            pl.BlockSpec((None, 1, d), lambda b, i: (b, 0, 0)),
        ] + [full(a) for a in weights],
        out_specs=[dir_spec, dir_spec, dir_spec, dir_spec,
                   pl.BlockSpec((ts, d), row),
                   pl.BlockSpec((2, nck, 8, d), lambda b, i: (0, b * ns + i, 0, 0)),
                   dir_spec,
                   pl.BlockSpec((ts, d), row)],
        out_shape=[jax.ShapeDtypeStruct((2, t, d), scan_dtype)] * 4 + [
            jax.ShapeDtypeStruct((t, d), scan_dtype),
            jax.ShapeDtypeStruct((2, t // c, 8, d), F32),
            jax.ShapeDtypeStruct((2, t, d), F32),
            jax.ShapeDtypeStruct((t, d), F32)],
        compiler_params=_params(("parallel", "parallel")),
        name="rwkv_prep",
    )(x2, x2, x2, norm_g.reshape(1, d), shift, scale, *weights)
    return outs


def _sdot(a, b, dims):
    return lax.dot_general(a.astype(SCAN_DT), b.astype(SCAN_DT), (dims, ((), ())),
                           preferred_element_type=F32)


def _scan_kernel(at_ref, rt_ref, bt_ref, kt_ref, v_ref, ed_ref, g_ref, ln_ref, y_ref, s_ref, *,
                 c, n_heads, per_step):
    z = pl.program_id(1)
    ci = pl.program_id(2)

    @pl.when(ci == 0)
    def _():
        s_ref[...] = jnp.zeros_like(s_ref)

    c2 = 2 * c
    pw_lanes = 2 * RWKV_HEAD
    row = lax.broadcasted_iota(jnp.int32, (c2, c2), 0)
    col = lax.broadcasted_iota(jnp.int32, (c2, c2), 1)
    same_blk = (row // c) == (col // c)
    diff = jnp.where(z == 0, row - col, col - row)
    strict = same_blk & (diff > 0)
    incl = same_blk & (diff >= 0)
    incl2 = jnp.concatenate([incl, incl], axis=1)
    lane_a = lax.broadcasted_iota(jnp.int32, (c, pw_lanes), 1) < RWKV_HEAD
    same_head = ((lax.broadcasted_iota(jnp.int32, (pw_lanes, pw_lanes), 0) // RWKV_HEAD)
                 == (lax.broadcasted_iota(jnp.int32, (pw_lanes, pw_lanes), 1) // RWKV_HEAD))
    n_steps = int(math.log2(c))
    pairs = range(n_heads // 2)
    sls = [pl.ds(hp * pw_lanes, pw_lanes) for hp in pairs]
    zero = jnp.zeros((c, pw_lanes), SCAN_DT)
    for j in range(per_step):
        jj = jnp.where(z == 0, j, per_step - 1 - j)
        rows = pl.ds(pl.multiple_of(jj * c, c), c)
        _scan_chunk(at_ref, rt_ref, bt_ref, kt_ref, v_ref, ed_ref.at[jj], g_ref, ln_ref, y_ref, s_ref, rows, pairs,
                    sls, zero, lane_a, strict, incl2, same_head, c, c2, pw_lanes, n_steps)


def _scan_chunk(at_ref, rt_ref, bt_ref, kt_ref, v_ref, ed_ref, g_ref, ln_ref, y_ref, s_ref, rows, pairs,
                sls, zero, lane_a, strict, incl2, same_head, c, c2, pw_lanes, n_steps):
    nt = (((1,), (1,)))
    nn = (((1,), (0,)))
    tn = (((0,), (0,)))
    lhs, rhs, vv, st = [], [], [], []
    for hp in pairs:
        a2 = at_ref[rows, sls[hp]]
        r2 = rt_ref[rows, sls[hp]]
        b2 = bt_ref[rows, sls[hp]]
        k2 = kt_ref[rows, sls[hp]]
        v2 = v_ref[rows, sls[hp]]
        lhs.append(jnp.concatenate([jnp.where(lane_a, a2, zero), jnp.where(lane_a, zero, a2),
                                    jnp.where(lane_a, r2, zero), jnp.where(lane_a, zero, r2)], axis=0))
        rhs.append(jnp.concatenate([b2, b2, k2, k2], axis=0))
        vv.append(jnp.concatenate([v2, v2], axis=0))
        st.append(s_ref[hp] * ed_ref[0:1, sls[hp]])
    gq = [_sdot(lhs[hp], jnp.concatenate([rhs[hp], st[hp].astype(SCAN_DT)], axis=0), nt) for hp in pairs]
    gm = [g[:, :2 * c2] for g in gq]
    q = [g[:, 2 * c2:] for g in gq]
    pw = [jnp.where(strict, gm[hp][:c2, :c2], 0.0).astype(SCAN_DT) for hp in pairs]
    m2 = [jnp.where(strict, gm[hp][:c2, c2:], 0.0).astype(SCAN_DT) for hp in pairs]
    m34 = [jnp.where(incl2, gm[hp][c2:, :], 0.0).astype(SCAN_DT) for hp in pairs]
    x = [q[hp][:c2] + _sdot(m2[hp], vv[hp], nn) for hp in pairs]
    for step in range(n_steps - 1):
        both = [_sdot(pw[hp], jnp.concatenate([x[hp].astype(SCAN_DT), pw[hp]], axis=1), nn) for hp in pairs]
        x = [x[hp] + both[hp][:, :pw_lanes] for hp in pairs]
        pw = [both[hp][:, pw_lanes:].astype(SCAN_DT) for hp in pairs]
    x = [x[hp] + _sdot(pw[hp], x[hp], nn) for hp in pairs]
    for hp in pairs:
        xs = x[hp].astype(SCAN_DT)
        uvs = jnp.concatenate([xs, vv[hp]], axis=0)
        ys = q[hp][c2:] + _sdot(m34[hp], uvs, nn)
        y = jnp.where(lane_a, ys[:c], ys[c:])
        inv_n = 1.0 / RWKV_HEAD
        sum_a = jnp.where(lane_a, y, 0.0).sum(axis=-1, keepdims=True)
        sum_b = jnp.where(lane_a, 0.0, y).sum(axis=-1, keepdims=True)
        yc = y - jnp.where(lane_a, sum_a, sum_b) * inv_n
        sq = yc * yc
        sq_a = jnp.where(lane_a, sq, 0.0).sum(axis=-1, keepdims=True)
        sq_b = jnp.where(lane_a, 0.0, sq).sum(axis=-1, keepdims=True)
        var = jnp.where(lane_a, sq_a, sq_b) * inv_n
        yn = yc * lax.rsqrt(var + RWKV_GN_EPS) * ln_ref[0:1, sls[hp]] + ln_ref[1:2, sls[hp]]
        y_ref[rows, sls[hp]] = g_ref[rows, sls[hp]] * yn
        u2 = jnp.where(lane_a, xs[:c], xs[c:])
        upd = _sdot(jnp.concatenate([u2, vv[hp][:c]], axis=0), rhs[hp][c:3 * c], tn)
        s_ref[hp] = (st[hp] + jnp.where(same_head, upd, 0.0)) * ed_ref[1:2, sls[hp]]


def _rwkv_scan(at, rt, bt, kt, v, ed, g, ln8, bsz):
    _, t, d = at.shape
    s = t // bsz
    c = SCAN_CHUNK
    per_step = min(SCAN_CHUNKS_PER_STEP, s // c)
    nc = s // (c * per_step)
    n_heads = d // RWKV_HEAD

    def cidx(b, z, ci):
        return b * nc + ci + z * (nc - 1 - 2 * ci)

    dspec = pl.BlockSpec((None, c * per_step, d), lambda b, z, ci: (z, cidx(b, z, ci), 0))
    return pl.pallas_call(
        functools.partial(_scan_kernel, c=c, n_heads=n_heads, per_step=per_step),
        grid=(bsz, 2, nc),
        in_specs=[dspec, dspec, dspec, dspec,
                  pl.BlockSpec((c * per_step, d), lambda b, z, ci: (cidx(b, z, ci), 0)),
                  pl.BlockSpec((None, per_step, 8, d), lambda b, z, ci: (z, cidx(b, z, ci), 0, 0)),
                  dspec,
                  pl.BlockSpec(ln8.shape, lambda b, z, ci: (0, 0))],
        out_specs=dspec,
        out_shape=jax.ShapeDtypeStruct((2, t, d), F32),
        scratch_shapes=[pltpu.VMEM((n_heads // 2, 2 * RWKV_HEAD, 2 * RWKV_HEAD), F32)],
        compiler_params=_params(("parallel", "parallel", "arbitrary")),
        name="rwkv_scan",
    )(at, rt, bt, kt, v, ed, g, ln8)


def _rwkv_out_kernel(gy_ref, gb_ref, x_ref, ga_ref, wo_ref, o_ref):
    o = gy_ref[0] + gy_ref[1] + gb_ref[...]
    o_ref[...] = x_ref[...] + ga_ref[...] * _dot(o.astype(BF16), wo_ref[...])


def _rwkv_out(gy, gb, x2, gate, w_o, bsz, tm=512):
    t, d = x2.shape
    s = t // bsz
    tm = min(tm, s)
    ns = s // tm
    row = lambda b, i: (b * ns + i, 0)
    return pl.pallas_call(
        _rwkv_out_kernel,
        grid=(bsz, ns),
        in_specs=[pl.BlockSpec((2, tm, d), lambda b, i: (0, b * ns + i, 0)),
                  pl.BlockSpec((tm, d), row), pl.BlockSpec((tm, d), row),
                  pl.BlockSpec((None, 1, d), lambda b, i: (b, 0, 0)),
                  pl.BlockSpec(w_o.shape, lambda b, i: (0, 0))],
        out_specs=pl.BlockSpec((tm, d), row),
        out_shape=jax.ShapeDtypeStruct((t, d), F32),
        compiler_params=_params(("parallel", "parallel")),
        name="rwkv_out",
    )(gy, gb, x2, gate, w_o)


def _rwkv_layer(x2, bsz, norm_g, shift, scale, gate, p):
    at, rt, bt, kt, v, ed, g, gb = _rwkv_prep(x2, norm_g, shift, scale, bsz, p, SCAN_DT)
    gy = _rwkv_scan(at, rt, bt, kt, v, ed, g, p["ln8"], bsz)
    return _rwkv_out(gy, gb, x2, gate, p["w_o"], bsz)


def _attn_group_kernel(slope_ref, vid_ref, q_ref, k_ref, v_ref, o_ref, lse_ref, tab_ref, *,
                       tq, sub, w, half, offs, nh, dh, hg, rps):
    first = (pl.program_id(0) == 0) & (pl.program_id(1) == 0) & (pl.program_id(2) == 0)
    i = pl.program_id(2)

    @pl.when(first)
    def _():
        col = lax.broadcasted_iota(jnp.int32, (tq, w), 1)
        row = lax.broadcasted_iota(jnp.int32, (tq, w), 0)
        for v, off in enumerate(offs):
            dist = jnp.abs(col - row + off)
            distf = dist.astype(F32)
            for h in range(nh):
                tab_ref[h * len(offs) + v] = jnp.where(dist <= half, -slope_ref[h] * distf, NEG_INF)

    if w == sub:
        ws = 0
    else:
        ws = pl.multiple_of(jnp.clip(i * tq - half, 0, sub - w), 16)
    var = vid_ref[i]
    nt = (((1,), (1,)), ((), ()))
    lane = lax.broadcasted_iota(jnp.int32, (tq, 128), 1)
    for rr in range(rps):
        qrows = slice(rr * tq, (rr + 1) * tq)
        krows = pl.ds(rr * sub + ws, w)
        lse = jnp.zeros((tq, 128), F32)
        for h0 in range(0, nh, hg):
            heads = range(h0, min(h0 + hg, nh))
            hs = {h: slice(h * dh, (h + 1) * dh) for h in heads}
            sc = {h: lax.dot_general(q_ref[qrows, hs[h]], k_ref[krows, hs[h]], nt, preferred_element_type=F32)
                  + tab_ref[h * len(offs) + var] for h in heads}
            m = {h: sc[h].max(axis=-1, keepdims=True) for h in heads}
            p = {h: jnp.exp(sc[h] - m[h]) for h in heads}
            den = {h: p[h].sum(axis=-1, keepdims=True) for h in heads}
            acc = {h: _dot(p[h].astype(BF16), v_ref[krows, hs[h]]) for h in heads}
            for h in heads:
                o_ref[qrows, hs[h]] = (acc[h] / den[h]).astype(o_ref.dtype)
                lse = jnp.where(lane == h, m[h] + jnp.log(den[h]), lse)
        lse_ref[qrows, :] = lse


def _alibi_slopes(n):
    return 2.0 ** (-8.0 * jnp.arange(1, n + 1, dtype=F32) / n)


def _attention_group(qkv, bsz, gi, tq=256):
    t, width = qkv.shape
    s = t // bsz
    window, dil = ATTN_GROUPS[gi]
    half = window // (2 * dil)
    sub = s // dil
    tq = min(tq, sub)
    nq = sub // tq
    w = min(sub, tq + 2 * half)
    nh, dh = ATTN_HEADS, ATTN_HEAD_DIM
    hw = nh * dh
    offs = sorted({int(np.clip(i * tq - half, 0, sub - w)) - i * tq for i in range(nq)})
    vid = jnp.asarray([offs.index(int(np.clip(i * tq - half, 0, sub - w)) - i * tq) for i in range(nq)], jnp.int32)
    slopes = _alibi_slopes(len(ATTN_GROUPS) * nh).reshape(len(ATTN_GROUPS), nh)[gi] * dil
    smem = pl.BlockSpec(memory_space=pltpu.SMEM)
    rps = min(dil, 4) if nq == 1 else 1
    nr = dil // rps
    qrow = lambda b, r, i: ((b * nr + r) * nq + i, 0)
    return pl.pallas_call(
        functools.partial(_attn_group_kernel, tq=tq, sub=sub, w=w, half=half, offs=tuple(offs), nh=nh, dh=dh,
                          hg=nh if tq * w <= 128 * 128 else 4, rps=rps),
        grid=(bsz, nr, nq),
        in_specs=[smem, smem,
                  pl.BlockSpec((rps * tq, hw), qrow),
                  pl.BlockSpec((rps * sub, hw), lambda b, r, i: (b * nr + r, 1)),
                  pl.BlockSpec((rps * sub, hw), lambda b, r, i: (b * nr + r, 2))],
        out_specs=[pl.BlockSpec((rps * tq, hw), qrow), pl.BlockSpec((rps * tq, 128), qrow)],
        out_shape=[jax.ShapeDtypeStruct((t, hw), BF16), jax.ShapeDtypeStruct((t, 128), F32)],
        scratch_shapes=[pltpu.VMEM((nh * len(offs), tq, w), F32)],
        compiler_params=_params(("arbitrary", "arbitrary", "arbitrary")),
        name=f"dilated_attention_g{gi}",
    )(slopes, vid, qkv, qkv, qkv)


def _to_token_order(blk_ref, scr_ref, dil):
    if dil == 1:
        return blk_ref[0].astype(F32)
    n = blk_ref.shape[1]
    slabs = scr_ref.shape[0]
    for r in range(dil):
        rows = blk_ref[r].astype(F32)
        for c in range(slabs):
            scr_ref[c, pl.ds(r, n, stride=dil), :] = rows[:, c * 128:(c + 1) * 128]
    if slabs == 1:
        return scr_ref[0]
    return jnp.concatenate([scr_ref[c] for c in range(slabs)], axis=1)


def _attn_out_kernel(o1, o2, o3, l1, l2, l3, ex_ref, w_ref, x_ref, ga_ref, out_ref, so_ref, sl_ref, *, dils):
    ls = [_to_token_order(l, sl_ref, dil) for l, dil in zip((l1, l2, l3), dils)]
    m = jnp.maximum(jnp.maximum(ls[0], ls[1]), ls[2])
    es = [jnp.exp(l - m) for l in ls]
    inv = 1.0 / (es[0] + es[1] + es[2])
    merged = None
    for e, o_ref, dil in zip(es, (o1, o2, o3), dils):
        ah, al = _split2(e * inv)
        alpha = _dot(ah, ex_ref[...]) + _dot(al, ex_ref[...])
        term = alpha * _to_token_order(o_ref, so_ref, dil)
        merged = term if merged is None else merged + term
    out_ref[...] = x_ref[...] + ga_ref[...] * _dot(merged.astype(BF16), w_ref[...])


def _attn_out(os, lses, w_o, x2, gate, bsz, tm=512):
    t, d = x2.shape
    s = t // bsz
    tm = min(tm, s)
    ns = s // tm
    hw = os[0].shape[1]
    dils = tuple(dil for _, dil in ATTN_GROUPS)
    ex = np.zeros((128, hw), np.float32)
    ex[np.arange(hw) // ATTN_HEAD_DIM, np.arange(hw)] = 1.0
    ex = jnp.asarray(ex, BF16)
    row = lambda b, i: (b * ns + i, 0)

    def res_major(a, dil):
        c = a.shape[1]
        return a.reshape(bsz, dil, s // dil, c), pl.BlockSpec((None, dil, tm // dil, c), lambda b, i: (b, 0, i, 0))

    o_args, o_specs = zip(*[res_major(o, dil) for o, dil in zip(os, dils)])
    l_args, l_specs = zip(*[res_major(l, dil) for l, dil in zip(lses, dils)])
    return pl.pallas_call(
        functools.partial(_attn_out_kernel, dils=dils),
        grid=(bsz, ns),
        in_specs=list(o_specs) + list(l_specs) + [
            pl.BlockSpec(ex.shape, lambda b, i: (0, 0)),
            pl.BlockSpec(w_o.shape, lambda b, i: (0, 0)),
            pl.BlockSpec((tm, d), row),
            pl.BlockSpec((None, 1, d), lambda b, i: (b, 0, 0))],
        out_specs=pl.BlockSpec((tm, d), row),
        out_shape=jax.ShapeDtypeStruct((t, d), F32),
        scratch_shapes=[pltpu.VMEM((hw // 128, tm, 128), F32), pltpu.VMEM((1, tm, 128), F32)],
        compiler_params=_params(("parallel", "parallel")),
        name="attn_out",
    )(*o_args, *l_args, ex, w_o, x2, gate)


def _norm_attn_kernel(x_ref, g_ref, sh_ref, sc_ref, *rest, dils):
    outs, scr_ref = rest[:-1], rest[-1]
    x = x_ref[...]
    ms = jnp.mean(x * x, axis=-1, keepdims=True)
    y = x * lax.rsqrt(ms + RMS_EPS) * g_ref[...]
    y = y * (1.0 + sc_ref[...]) + sh_ref[...]
    slabs = scr_ref.shape[0]
    if any(dil > 1 for dil in dils):
        for c in range(slabs):
            scr_ref[c] = y[:, c * 128:(c + 1) * 128]
    for o, dil in zip(outs, dils):
        if dil == 1:
            o[0] = y.astype(o.dtype)
        else:
            n = o.shape[1]
            for r in range(dil):
                for c in range(slabs):
                    o[r, :, c * 128:(c + 1) * 128] = scr_ref[c, pl.ds(r, n, stride=dil), :].astype(o.dtype)


def _norm_mod_attn(x2, g, shift, scale, bsz, ts=512):
    t, d = x2.shape
    s = t // bsz
    ts = min(ts, s)
    ns = s // ts
    dils = tuple(dil for _, dil in ATTN_GROUPS)
    vec = pl.BlockSpec((None, 1, d), lambda b, i: (b, 0, 0))
    outs = pl.pallas_call(
        functools.partial(_norm_attn_kernel, dils=dils),
        grid=(bsz, ns),
        in_specs=[pl.BlockSpec((ts, d), lambda b, i: (b * ns + i, 0)),
                  pl.BlockSpec((1, d), lambda b, i: (0, 0)), vec, vec],
        out_specs=[pl.BlockSpec((None, dil, ts // dil, d), lambda b, i: (b, 0, i, 0)) for dil in dils],
        out_shape=[jax.ShapeDtypeStruct((bsz, dil, s // dil, d), BF16) for dil in dils],
        scratch_shapes=[pltpu.VMEM((d // 128, ts, 128), F32)],
        compiler_params=_params(("parallel", "parallel")),
        name="norm_mod_attn",
    )(x2, g.reshape(1, d), shift, scale)
    return [o.reshape(t, d) for o in outs]


def _pack_attn(j, w_qkv, w_o):
    d = w_qkv.shape[1]
    ng = len(ATTN_GROUPS)
    w = w_qkv[j].reshape(d, ng, 3, ATTN_HEADS * ATTN_HEAD_DIM)
    w = w.at[:, :, 0].multiply(ATTN_HEAD_DIM ** -0.5)
    return {"w_qkv": w.reshape(d, -1).astype(BF16), "w_o": w_o[j].astype(BF16)}


def _attn_layer(x2, bsz, hns, gate, p):
    gw = 3 * ATTN_HEADS * ATTN_HEAD_DIM
    outs = [_attention_group(_matmul(hn, p["w_qkv"], BF16, n=gw, col0=gi * gw), bsz, gi)
            for gi, hn in enumerate(hns)]
    return _attn_out([o for o, _ in outs], [l for _, l in outs], p["w_o"], x2, gate, bsz)


def _router_kernel(x_ref, g_ref, sh_ref, sc_ref, wh_ref, wl_ref, b_ref, o_ref, hb_ref, seg_ref):
    x = x_ref[...]
    ms = jnp.mean(x * x, axis=-1, keepdims=True)
    hn = x * lax.rsqrt(ms + RMS_EPS) * g_ref[...]
    hn = hn * (1.0 + sc_ref[...]) + sh_ref[...]
    hb_ref[...] = hn.astype(hb_ref.dtype)
    hh, hl = _split2(hn)
    logits = _dot(hh, wh_ref[...]) + _dot(hl, wh_ref[...]) + _dot(hh, wl_ref[...]) + b_ref[...]
    lane = lax.broadcasted_iota(jnp.int32, logits.shape, 1)
    ng, ne = N_EXPERT_GROUPS, EXPERTS_PER_GROUP
    big = jnp.int32(ROUTER_LANES)
    is_g = lane < ng
    gl = jnp.where(is_g, logits, -jnp.inf)
    gmax = gl.max(axis=-1, keepdims=True)
    gsum = jnp.where(is_g, jnp.exp(logits - gmax), 0.0).sum(axis=-1, keepdims=True)
    p_group = 1.0 / gsum
    g_top = jnp.where(gl == gmax, lane, big).min(axis=-1, keepdims=True)
    lo = ng + ne * g_top
    es = jnp.where((lane >= lo) & (lane < lo + ne), logits, -jnp.inf)
    v1 = es.max(axis=-1, keepdims=True)
    i1 = jnp.where(es == v1, lane, big).min(axis=-1, keepdims=True)
    es2 = jnp.where(lane == i1, -jnp.inf, es)
    v2 = es2.max(axis=-1, keepdims=True)
    i2 = jnp.where(es2 == v2, lane, big).min(axis=-1, keepdims=True)
    e2 = jnp.exp(v2 - v1)
    w1 = p_group / (1.0 + e2)
    w2 = p_group * e2 / (1.0 + e2)
    gates = jnp.where(lane == i1, w1, 0.0) + jnp.where(lane == i2, w2, 0.0)

    tm = x.shape[0]
    onehot = jnp.where(lane == g_top, 1.0, 0.0)
    tri = jnp.where(lax.broadcasted_iota(jnp.int32, (tm, tm), 1) < lax.broadcasted_iota(jnp.int32, (tm, tm), 0),
                    1.0, 0.0).astype(BF16)
    before = _dot(tri, onehot.astype(BF16))
    counts = jnp.broadcast_to(onehot.sum(axis=0, keepdims=True), (8, ROUTER_LANES))
    upper = jnp.where(lax.broadcasted_iota(jnp.int32, (ROUTER_LANES, ROUTER_LANES), 0)
                      < lax.broadcasted_iota(jnp.int32, (ROUTER_LANES, ROUTER_LANES), 1), 1.0, 0.0).astype(BF16)
    ch, cl = _split2(counts)
    seg_lo = _dot(ch, upper) + _dot(cl, upper)
    dest = (onehot * (before + seg_lo[0:1])).sum(axis=-1, keepdims=True)
    o_ref[...] = gates + jnp.where(lane == DEST_LANE, dest, 0.0)
    row8 = lax.broadcasted_iota(jnp.int32, (8, ROUTER_LANES), 0)
    seg_ref[...] = jnp.where(row8 == 0, seg_lo, jnp.where(row8 == 1, seg_lo + counts, 0.0))


def _router(x2, g, shift, scale, wr_hi, wr_lo, bias, bsz):
    t, d = x2.shape
    s = t // bsz
    tm = min(MOE_TILE, s)
    ns = s // tm
    row = lambda b, i: (b * ns + i, 0)
    full = lambda a: pl.BlockSpec(a.shape, lambda b, i: (0, 0))
    vec = pl.BlockSpec((None, 1, d), lambda b, i: (b, 0, 0))
    g = g.reshape(1, d)
    return pl.pallas_call(
        _router_kernel,
        grid=(bsz, ns),
        in_specs=[pl.BlockSpec((tm, d), row), full(g), vec, vec, full(wr_hi), full(wr_lo), full(bias)],
        out_specs=[pl.BlockSpec((tm, ROUTER_LANES), row), pl.BlockSpec((tm, d), row),
                   pl.BlockSpec((None, 8, ROUTER_LANES), lambda b, i: (b * ns + i, 0, 0))],
        out_shape=[jax.ShapeDtypeStruct((t, ROUTER_LANES), F32), jax.ShapeDtypeStruct((t, d), BF16),
                   jax.ShapeDtypeStruct((t // tm, 8, ROUTER_LANES), F32)],
        compiler_params=_params(("parallel", "parallel")),
        name="moe_router",
    )(x2, g, shift, scale, wr_hi, wr_lo, bias)


def _experts_kernel(seg_ref, h_ref, gates_ref, wg_ref, wu_ref, wd_ref, x_ref, ga_ref, fg_ref, o_ref,
                    pt_ref, xs_ref, gs_ref, y_ref, *, per_step, blk, final_norm):
    tile = pl.program_id(0)
    step = pl.program_id(1)
    tm = h_ref.shape[0]
    grp = step // (EXPERTS_PER_GROUP // per_step)
    tn = (((0,), (0,)), ((), ()))

    @pl.when(step == 0)
    def _():
        gates = gates_ref[...]
        lane = lax.broadcasted_iota(jnp.int32, gates.shape, 1)
        dest = jnp.where(lane == DEST_LANE, gates, 0.0).sum(axis=-1, keepdims=True)
        col = lax.broadcasted_iota(jnp.int32, (tm, tm), 1).astype(F32)
        pt = jnp.where(dest == col, 1.0, 0.0).astype(BF16)
        pt_ref[...] = pt
        xs_ref[...] = lax.dot_general(pt, h_ref[...], tn, preferred_element_type=F32).astype(BF16)
        g1 = gates.astype(BF16)
        r1 = gates - g1.astype(F32)
        g2 = r1.astype(BF16)
        g3 = (r1 - g2.astype(F32)).astype(BF16)
        gs_ref[...] = (lax.dot_general(pt, g1, tn, preferred_element_type=F32)
                       + lax.dot_general(pt, g2, tn, preferred_element_type=F32)
                       + lax.dot_general(pt, g3, tn, preferred_element_type=F32))
        y_ref[...] = jnp.zeros_like(y_ref)

    lo = seg_ref[tile * 8 + grp]
    hi = seg_ref[tile * 8 + N_EXPERT_GROUPS + grp]
    for b in range(tm // blk):
        @pl.when((lo < (b + 1) * blk) & (hi > b * blk))
        def _(b=b):
            rows = pl.ds(b * blk, blk)
            h = xs_ref[rows, :]
            gsb = gs_ref[rows, :]
            lane = lax.broadcasted_iota(jnp.int32, gsb.shape, 1)
            hids = []
            for j in range(per_step):
                e = step * per_step + j
                ge = jnp.where(lane == N_EXPERT_GROUPS + e, gsb, 0.0).sum(axis=-1, keepdims=True)
                gp = _dot(h, wg_ref[j])
                up = _dot(h, wu_ref[j])
                hids.append((gp * _sigmoid(gp) * up * ge).astype(BF16))
            y_ref[rows, :] += _dot(jnp.concatenate(hids, axis=1), wd_ref[...])

    @pl.when(step == pl.num_programs(1) - 1)
    def _():
        res = x_ref[...] + ga_ref[...] * _dot(pt_ref[...], y_ref[...].astype(BF16))
        if final_norm:
            ms = jnp.mean(res * res, axis=-1, keepdims=True)
            res = res * lax.rsqrt(ms + RMS_EPS) * fg_ref[...]
        o_ref[...] = res


def _experts(hn_bf16, gates, seg, wg, wu, wd, layer, x2, gate, final_g, final_norm, bsz, per_step=4):
    t, d = x2.shape
    s = t // bsz
    tm = min(MOE_TILE, s)
    ns = s // tm
    f = wg.shape[2]
    ne = N_EXPERT_GROUPS * EXPERTS_PER_GROUP
    steps = ne // per_step
    blk = min(MOE_BLOCK, tm)
    row = lambda i, e, seg: (i, 0)
    wspec = pl.BlockSpec((per_step, d, f), lambda i, e, seg: (layer * steps + e, 0, 0))
    grid_spec = pltpu.PrefetchScalarGridSpec(
        num_scalar_prefetch=1,
        grid=(t // tm, steps),
        in_specs=[
            pl.BlockSpec((tm, d), row),
            pl.BlockSpec((tm, ROUTER_LANES), row),
            wspec, wspec,
            pl.BlockSpec((per_step * f, d), lambda i, e, seg: (layer * steps + e, 0)),
            pl.BlockSpec((tm, d), row),
            pl.BlockSpec((None, 1, d), lambda i, e, seg: (i // ns, 0, 0)),
            pl.BlockSpec((1, d), lambda i, e, seg: (0, 0)),
        ],
        out_specs=pl.BlockSpec((tm, d), row),
        scratch_shapes=[pltpu.VMEM((tm, tm), BF16), pltpu.VMEM((tm, d), BF16),
                        pltpu.VMEM((tm, ROUTER_LANES), F32), pltpu.VMEM((tm, d), F32)],
    )
    return pl.pallas_call(
        functools.partial(_experts_kernel, per_step=per_step, blk=blk, final_norm=final_norm),
        grid_spec=grid_spec,
        out_shape=jax.ShapeDtypeStruct((t, d), F32),
        compiler_params=_params(("parallel", "arbitrary")),
        name="moe_experts",
    )(seg, hn_bf16, gates, wg, wu, wd, x2, gate, final_g.reshape(1, d))


def _moe_layer(x2, bsz, g, shift, scale, gate, p, final_g, final_norm):
    gates, hn_bf16, seg = _router(x2, g, shift, scale, p["wr_hi"], p["wr_lo"], p["r_bias"], bsz)
    ng = N_EXPERT_GROUPS
    seg = jnp.concatenate([seg[:, 0, :ng], seg[:, 1, :ng]], axis=1).astype(jnp.int32).reshape(-1)
    return _experts(hn_bf16, gates, seg, p["wg"], p["wu"], p["wd"], p["layer"], x2, gate,
                    final_g, final_norm, bsz)


def _pad_rows(rows, d):
    out = jnp.zeros((8, d), F32)
    return out.at[:len(rows)].set(jnp.stack([r.reshape(d).astype(F32) for r in rows]))


def _pack_rwkv(j, mu, w_rkv, w0, w1, w2, a0, a1, a2, g1, g2, k_k, k_a, r_k, ln_w, ln_b, w_o):
    d = w_o.shape[-1]
    cat = lambda m: jnp.concatenate([m[j, 0], m[j, 1]], axis=1).astype(BF16)
    stack = lambda m: jnp.concatenate([m[j, 0], m[j, 1]], axis=0).astype(BF16)
    return {
        "mu8": _pad_rows(list(mu[j]), d),
        "w_rkv": w_rkv[j].astype(BF16),
        "w1c": cat(w1), "a1c": cat(a1), "g1c": cat(g1),
        "w2s": stack(w2), "a2s": stack(a2), "g2": g2[j].astype(BF16),
        "vec8": _pad_rows([w0[j, 0], w0[j, 1], a0[j, 0], a0[j, 1], k_k[j], k_a[j], r_k[j]], d),
        "ln8": _pad_rows([ln_w[j], ln_b[j]], d),
        "w_o": w_o[j].astype(BF16),
    }


def _pack_moe(i, router_g, router_g_b, router_e, router_e_b, w_gate, w_up, w_down):
    d = router_g.shape[1]
    ng, ne = N_EXPERT_GROUPS, EXPERTS_PER_GROUP
    wr = jnp.zeros((d, ROUTER_LANES), F32)
    wr = wr.at[:, :ng].set(router_g[i])
    wr = wr.at[:, ng:ng + ng * ne].set(router_e[i].transpose(1, 0, 2).reshape(d, ng * ne))
    bias = jnp.zeros((1, ROUTER_LANES), F32)
    bias = bias.at[0, :ng].set(router_g_b[i])
    bias = bias.at[0, ng:ng + ng * ne].set(router_e_b[i].reshape(ng * ne))
    wr_hi = wr.astype(BF16)
    wr_lo = (wr - wr_hi.astype(F32)).astype(BF16)
    f = w_gate.shape[-1]
    return {"wr_hi": wr_hi, "wr_lo": wr_lo, "r_bias": bias,
            "layer": i,
            "wg": w_gate.reshape(-1, d, f).astype(BF16), "wu": w_up.reshape(-1, d, f).astype(BF16),
            "wd": w_down.reshape(-1, d).astype(BF16)}


def kernel(x, c, ada_w, ada_b, norm_tm_g, norm_cm_g, rw_mu, rw_w_rkv, rw_w0, rw_w1, rw_w2, rw_a0, rw_a1, rw_a2, rw_g1, rw_g2, rw_k_k, rw_k_a, rw_r_k, rw_ln_w, rw_ln_b, rw_w_o, at_w_qkv, at_w_o, moe_router_g, moe_router_g_b, moe_router_e, moe_router_e_b, moe_w_gate, moe_w_up, moe_w_down, final_g):
    bsz, s, d = x.shape
    depth = ada_w.shape[0]
    x2 = x.reshape(bsz * s, d)
    mod = _ada_mod(c, ada_w, ada_b)
    for i in range(depth):
        sh_t, sc_t, ga_t, sh_c, sc_c, ga_c = (mod[i, m] for m in range(N_MOD))
        j = i // 2
        if i % 2 == 0:
            p = _pack_rwkv(j, rw_mu, rw_w_rkv, rw_w0, rw_w1, rw_w2, rw_a0, rw_a1, rw_a2, rw_g1, rw_g2,
                           rw_k_k, rw_k_a, rw_r_k, rw_ln_w, rw_ln_b, rw_w_o)
            x2 = _rwkv_layer(x2, bsz, norm_tm_g[i], sh_t, sc_t, ga_t, p)
        else:
            hns = _norm_mod_attn(x2, norm_tm_g[i], sh_t, sc_t, bsz)
            p = _pack_attn(j, at_w_qkv, at_w_o)
            x2 = _attn_layer(x2, bsz, hns, ga_t, p)
        pm = _pack_moe(i, moe_router_g, moe_router_g_b, moe_router_e, moe_router_e_b,
                       moe_w_gate, moe_w_up, moe_w_down)
        x2 = _moe_layer(x2, bsz, norm_cm_g[i], sh_c, sc_c, ga_c, pm, final_g, i == depth - 1)
    return x2.reshape(bsz, s, d)
```

```python
import functools
import math

import numpy as np
import jax
import jax.numpy as jnp
from jax import lax
from jax.experimental import pallas as pl
from jax.experimental.pallas import tpu as pltpu

F32 = jnp.float32
BF16 = jnp.bfloat16

RMS_EPS = 1e-6
N_MOD = 6
RWKV_HEAD = 64
RWKV_GN_EPS = 64e-5
ATTN_GROUPS = ((128, 1), (512, 4), (2048, 16))
ATTN_HEADS = 8
ATTN_HEAD_DIM = 128
NEG_INF = -1e30
N_EXPERT_GROUPS = 4
EXPERTS_PER_GROUP = 8
ROUTER_LANES = 128
DEST_LANE = 64
MOE_TILE = 1024
MOE_BLOCK = 128

SCAN_CHUNK = 64
SCAN_CHUNKS_PER_STEP = 4
V7X_VMEM_LIMIT = 58 * 1024 * 1024
SCAN_DT = BF16


def _params(sem):
    return pltpu.CompilerParams(dimension_semantics=sem, vmem_limit_bytes=V7X_VMEM_LIMIT)


def _dot(a, b):
    return jnp.dot(a, b, preferred_element_type=F32)


def _sigmoid(x):
    return 0.5 * jnp.tanh(0.5 * x) + 0.5


def _split2(x):
    hi = x.astype(BF16)
    lo = (x - hi.astype(F32)).astype(BF16)
    return hi, lo


def _ada_kernel(c_ref, w_ref, b_ref, o_ref):
    c = c_ref[...]
    sc = c * _sigmoid(c)
    o_ref[...] = _dot(sc.astype(BF16), w_ref[...].astype(BF16)) + b_ref[...]


def _ada_mod(c, ada_w, ada_b):
    depth, d, n = ada_w.shape
    bsz = c.shape[0]
    tn = n // 4
    out = pl.pallas_call(
        _ada_kernel,
        grid=(depth, n // tn),
        in_specs=[
            pl.BlockSpec((bsz, d), lambda i, j: (0, 0)),
            pl.BlockSpec((None, d, tn), lambda i, j: (i, 0, j)),
            pl.BlockSpec((None, 1, tn), lambda i, j: (i, 0, j)),
        ],
        out_specs=pl.BlockSpec((None, bsz, tn), lambda i, j: (i, 0, j)),
        out_shape=jax.ShapeDtypeStruct((depth, bsz, n), F32),
        compiler_params=_params(("parallel", "parallel")),
        name="ada_mod",
    )(c, ada_w, ada_b.reshape(depth, 1, n))
    return out.reshape(depth, bsz, N_MOD, 1, d).transpose(0, 2, 1, 3, 4)


def _mm_kernel(a_ref, b_ref, o_ref):
    o_ref[...] = _dot(a_ref[...], b_ref[...]).astype(o_ref.dtype)


def _matmul(a, b, out_dtype, n=None, col0=0, tm=1024, tn=1024):
    m, k = a.shape
    n = b.shape[1] if n is None else n
    tm, tn = min(tm, m), min(tn, n)
    cb0 = col0 // tn
    return pl.pallas_call(
        _mm_kernel,
        grid=(n // tn, m // tm),
        in_specs=[
            pl.BlockSpec((tm, k), lambda j, i: (i, 0)),
            pl.BlockSpec((k, tn), lambda j, i: (0, cb0 + j)),
        ],
        out_specs=pl.BlockSpec((tm, tn), lambda j, i: (i, j)),
        out_shape=jax.ShapeDtypeStruct((m, n), out_dtype),
        compiler_params=_params(("parallel", "parallel")),
        name="matmul",
    )(a, b)


def _seg_sum(x, seg_ref, segt_ref):
    xh, xl = _split2(x)
    s = _dot(xh, seg_ref[...]) + _dot(xl, seg_ref[...])
    sh, sl = _split2(s)
    return _dot(sh, segt_ref[...]) + _dot(sl, segt_ref[...])


def _rwkv_prep_kernel(x_ref, prev_ref, next_ref, ng_ref, sh_ref, sc_ref, mu_ref, wrkv_ref, w1_ref, a1_ref,
                      g1_ref, w2_ref, a2_ref, g2_ref, vec_ref, cm_ref, edm_ref, seg_ref, segt_ref,
                      at_ref, rt_ref, bt_ref, kt_ref, v_ref, ed_ref, g_ref, gb_ref, *, ts, ns):
    i = pl.program_id(1)

    def norm_mod(x):
        ms = jnp.mean(x * x, axis=-1, keepdims=True)
        return x * lax.rsqrt(ms + RMS_EPS) * ng_ref[...] * (1.0 + sc_ref[...]) + sh_ref[...]

    cur = norm_mod(x_ref[...])
    row = lax.broadcasted_iota(jnp.int32, (ts, 1), 0)
    prev_row = jnp.where(i > 0, norm_mod(prev_ref[7:8, :]), 0.0)
    next_row = jnp.where(i < ns - 1, norm_mod(next_ref[0:1, :]), 0.0)
    x_prev = jnp.where(row == 0, prev_row, pltpu.roll(cur, 1, 0))
    x_next = jnp.where(row == ts - 1, next_row, pltpu.roll(cur, ts - 1, 0))
    xx = 0.5 * (x_prev + x_next) - cur

    def mix(j):
        return (cur + xx * mu_ref[j:j + 1, :]).astype(BF16)

    r = _dot(mix(0), wrkv_ref[0])
    k = _dot(mix(1), wrkv_ref[1])
    v = _dot(mix(2), wrkv_ref[2])
    tw = jnp.tanh(_dot(mix(3), w1_ref[...]))
    ta = _dot(mix(4), a1_ref[...])
    tg = _sigmoid(_dot(mix(5), g1_ref[...]))
    lane = lax.broadcasted_iota(jnp.int32, (1, tw.shape[1]), 1)
    half = tw.shape[1] // 2
    hg = tg.shape[1] // 2

    k_k = vec_ref[4:5, :]
    k_a = vec_ref[5:6, :]
    r_k = vec_ref[6:7, :]
    kk = k * k_k
    ss = _seg_sum(kk * kk, seg_ref, segt_ref)
    kkn = kk * lax.rsqrt(jnp.maximum(ss, 1e-24))
    v_ref[...] = v.astype(v_ref.dtype)

    gb = jnp.zeros_like(cur)
    for z in range(2):
        sel = (lane < half) if z == 0 else (lane >= half)
        lw = _dot(jnp.where(sel, tw, 0.0).astype(BF16), w2_ref[...])
        la = _dot(jnp.where(sel, ta, 0.0).astype(BF16), a2_ref[...])
        g = _dot(tg[:, z * hg:(z + 1) * hg].astype(BF16), g2_ref[z])
        ld = -math.exp(-0.5) * _sigmoid(vec_ref[z:z + 1, :] + lw)
        a = _sigmoid(vec_ref[2 + z:3 + z, :] + la)
        kdir = k * (1.0 + (a - 1.0) * k_a)
        b = kkn * a
        parts = _split2(ld)
        cmz = cm_ref[z]
        lm = _dot(cmz, parts[0]) + _dot(cmz, parts[1])
        e_in = jnp.exp(lm)
        e_inv = jnp.exp(-lm)
        e_ex = jnp.exp(lm - ld)
        rt_ref[z] = (r * e_in).astype(rt_ref.dtype)
        at_ref[z] = (-kkn * e_ex).astype(at_ref.dtype)
        bt_ref[z] = (b * e_inv).astype(bt_ref.dtype)
        kt_ref[z] = (kdir * e_inv).astype(kt_ref.dtype)
        edz = edm_ref[z]
        ed = jnp.exp(_dot(edz, parts[0]) + _dot(edz, parts[1]))
        for ck in range(ed.shape[0] // 8):
            ed_ref[z, ck] = ed[ck * 8:(ck + 1) * 8]
        g_ref[z] = g.astype(g_ref.dtype)
        bonus = _seg_sum(r * kdir * r_k, seg_ref, segt_ref) * v
        gb = gb + g * bonus
    gb_ref[...] = gb


def _chunk_matrices(ts, c):
    t = np.arange(ts)
    same = (t[:, None] // c) == (t[None, :] // c)
    pos = t % c
    cm = np.zeros((2, ts, ts), np.float32)
    nck = ts // c
    edm = np.zeros((2, 8 * nck, ts), np.float32)
    tri_f = same & (t[None, :] <= t[:, None])
    sel_f = same & (pos[None, :] <= c // 2 - 1)
    cm[0] = tri_f.astype(np.float32) - sel_f.astype(np.float32)
    tri_b = same & (t[None, :] >= t[:, None])
    sel_b = same & (pos[None, :] >= c // 2)
    cm[1] = tri_b.astype(np.float32) - sel_b.astype(np.float32)
    for ck in range(nck):
        inck = (t // c) == ck
        edm[0, 8 * ck] = inck & (pos <= c // 2 - 1)
        edm[0, 8 * ck + 1] = inck & (pos > c // 2 - 1)
        edm[1, 8 * ck] = inck & (pos >= c // 2)
        edm[1, 8 * ck + 1] = inck & (pos < c // 2)
    return jnp.asarray(cm, BF16), jnp.asarray(edm, BF16)


def _seg_matrices(d, head):
    seg = np.zeros((d, 128), np.float32)
    seg[np.arange(d), np.arange(d) // head] = 1.0
    return jnp.asarray(seg, BF16), jnp.asarray(seg.T.copy(), BF16)


def _rwkv_prep(x2, norm_g, shift, scale, bsz, p, scan_dtype, ts=256):
    t, d = x2.shape
    s = t // bsz
    ts = min(ts, s)
    ns = s // ts
    c = SCAN_CHUNK
    assert ts % c == 0 and ts % 8 == 0
    nck = ts // c
    cm, edm = _chunk_matrices(ts, c)
    seg, segt = _seg_matrices(d, RWKV_HEAD)
    r8 = ts // 8
    nblk8 = t // 8
    row = lambda b, i: (b * ns + i, 0)
    full = lambda a: pl.BlockSpec(a.shape, lambda b, i, _n=a.ndim: (0,) * _n)
    weights = [p["mu8"], p["w_rkv"], p["w1c"], p["a1c"], p["g1c"], p["w2s"], p["a2s"], p["g2"],
               p["vec8"], cm, edm, seg, segt]
    dir_spec = pl.BlockSpec((2, ts, d), lambda b, i: (0, b * ns + i, 0))
    outs = pl.pallas_call(
        functools.partial(_rwkv_prep_kernel, ts=ts, ns=ns),
        grid=(bsz, ns),
        in_specs=[
            pl.BlockSpec((ts, d), row),
            pl.BlockSpec((8, d), lambda b, i: (jnp.maximum((b * ns + i) * r8 - 1, 0), 0)),
            pl.BlockSpec((8, d), lambda b, i: (jnp.minimum((b * ns + i + 1) * r8, nblk8 - 1), 0)),
            pl.BlockSpec((1, d), lambda b, i: (0, 0)),
            pl.BlockSpec((None, 1, d), lambda b, i: (b, 0, 0)),
            pl.BlockSpec((None, 1, d), lambda b, i: (b, 0, 0)),
        ] + [full(a) for a in weights],
        out_specs=[dir_spec, dir_spec, dir_spec, dir_spec,
                   pl.BlockSpec((ts, d), row),
                   pl.BlockSpec((2, nck, 8, d), lambda b, i: (0, b * ns + i, 0, 0)),
                   dir_spec,
                   pl.BlockSpec((ts, d), row)],
        out_shape=[jax.ShapeDtypeStruct((2, t, d), scan_dtype)] * 4 + [
            jax.ShapeDtypeStruct((t, d), scan_dtype),
            jax.ShapeDtypeStruct((2, t // c, 8, d), F32),
            jax.ShapeDtypeStruct((2, t, d), F32),
            jax.ShapeDtypeStruct((t, d), F32)],
        compiler_params=_params(("parallel", "parallel")),
        name="rwkv_prep",
    )(x2, x2, x2, norm_g.reshape(1, d), shift, scale, *weights)
    return outs


def _sdot(a, b, dims):
    return lax.dot_general(a.astype(SCAN_DT), b.astype(SCAN_DT), (dims, ((), ())),
                           preferred_element_type=F32)


def _scan_kernel(at_ref, rt_ref, bt_ref, kt_ref, v_ref, ed_ref, g_ref, ln_ref, y_ref, s_ref, *,
                 c, n_heads, per_step):
    z = pl.program_id(1)
    ci = pl.program_id(2)

    @pl.when(ci == 0)
    def _():
        s_ref[...] = jnp.zeros_like(s_ref)

    c2 = 2 * c
    pw_lanes = 2 * RWKV_HEAD
    row = lax.broadcasted_iota(jnp.int32, (c2, c2), 0)
    col = lax.broadcasted_iota(jnp.int32, (c2, c2), 1)
    same_blk = (row // c) == (col // c)
    diff = jnp.where(z == 0, row - col, col - row)
    strict = same_blk & (diff > 0)
    incl = same_blk & (diff >= 0)
    incl2 = jnp.concatenate([incl, incl], axis=1)
    lane_a = lax.broadcasted_iota(jnp.int32, (c, pw_lanes), 1) < RWKV_HEAD
    same_head = ((lax.broadcasted_iota(jnp.int32, (pw_lanes, pw_lanes), 0) // RWKV_HEAD)
                 == (lax.broadcasted_iota(jnp.int32, (pw_lanes, pw_lanes), 1) // RWKV_HEAD))
    n_steps = int(math.log2(c))
    pairs = range(n_heads // 2)
    sls = [pl.ds(hp * pw_lanes, pw_lanes) for hp in pairs]
    zero = jnp.zeros((c, pw_lanes), SCAN_DT)
    for j in range(per_step):
        jj = jnp.where(z == 0, j, per_step - 1 - j)
        rows = pl.ds(pl.multiple_of(jj * c, c), c)
        _scan_chunk(at_ref, rt_ref, bt_ref, kt_ref, v_ref, ed_ref.at[jj], g_ref, ln_ref, y_ref, s_ref, rows, pairs,
                    sls, zero, lane_a, strict, incl2, same_head, c, c2, pw_lanes, n_steps)


def _scan_chunk(at_ref, rt_ref, bt_ref, kt_ref, v_ref, ed_ref, g_ref, ln_ref, y_ref, s_ref, rows, pairs,
                sls, zero, lane_a, strict, incl2, same_head, c, c2, pw_lanes, n_steps):
    nt = (((1,), (1,)))
    nn = (((1,), (0,)))
    tn = (((0,), (0,)))
    lhs, rhs, vv, st = [], [], [], []
    for hp in pairs:
        a2 = at_ref[rows, sls[hp]]
        r2 = rt_ref[rows, sls[hp]]
        b2 = bt_ref[rows, sls[hp]]
        k2 = kt_ref[rows, sls[hp]]
        v2 = v_ref[rows, sls[hp]]
        lhs.append(jnp.concatenate([jnp.where(lane_a, a2, zero), jnp.where(lane_a, zero, a2),
                                    jnp.where(lane_a, r2, zero), jnp.where(lane_a, zero, r2)], axis=0))
        rhs.append(jnp.concatenate([b2, b2, k2, k2], axis=0))
        vv.append(jnp.concatenate([v2, v2], axis=0))
        st.append(s_ref[hp] * ed_ref[0:1, sls[hp]])
    gq = [_sdot(lhs[hp], jnp.concatenate([rhs[hp], st[hp].astype(SCAN_DT)], axis=0), nt) for hp in pairs]
    gm = [g[:, :2 * c2] for g in gq]
    q = [g[:, 2 * c2:] for g in gq]
    pw = [jnp.where(strict, gm[hp][:c2, :c2], 0.0).astype(SCAN_DT) for hp in pairs]
    m2 = [jnp.where(strict, gm[hp][:c2, c2:], 0.0).astype(SCAN_DT) for hp in pairs]
    m34 = [jnp.where(incl2, gm[hp][c2:, :], 0.0).astype(SCAN_DT) for hp in pairs]
    x = [q[hp][:c2] + _sdot(m2[hp], vv[hp], nn) for hp in pairs]
    for step in range(n_steps - 1):
        both = [_sdot(pw[hp], jnp.concatenate([x[hp].astype(SCAN_DT), pw[hp]], axis=1), nn) for hp in pairs]
        x = [x[hp] + both[hp][:, :pw_lanes] for hp in pairs]
        pw = [both[hp][:, pw_lanes:].astype(SCAN_DT) for hp in pairs]
    x = [x[hp] + _sdot(pw[hp], x[hp], nn) for hp in pairs]
    for hp in pairs:
        xs = x[hp].astype(SCAN_DT)
        uvs = jnp.concatenate([xs, vv[hp]], axis=0)
        ys = q[hp][c2:] + _sdot(m34[hp], uvs, nn)
        y = jnp.where(lane_a, ys[:c], ys[c:])
        inv_n = 1.0 / RWKV_HEAD
        sum_a = jnp.where(lane_a, y, 0.0).sum(axis=-1, keepdims=True)
        sum_b = jnp.where(lane_a, 0.0, y).sum(axis=-1, keepdims=True)
        yc = y - jnp.where(lane_a, sum_a, sum_b) * inv_n
        sq = yc * yc
        sq_a = jnp.where(lane_a, sq, 0.0).sum(axis=-1, keepdims=True)
        sq_b = jnp.where(lane_a, 0.0, sq).sum(axis=-1, keepdims=True)
        var = jnp.where(lane_a, sq_a, sq_b) * inv_n
        yn = yc * lax.rsqrt(var + RWKV_GN_EPS) * ln_ref[0:1, sls[hp]] + ln_ref[1:2, sls[hp]]
        y_ref[rows, sls[hp]] = g_ref[rows, sls[hp]] * yn
        u2 = jnp.where(lane_a, xs[:c], xs[c:])
        upd = _sdot(jnp.concatenate([u2, vv[hp][:c]], axis=0), rhs[hp][c:3 * c], tn)
        s_ref[hp] = (st[hp] + jnp.where(same_head, upd, 0.0)) * ed_ref[1:2, sls[hp]]


def _rwkv_scan(at, rt, bt, kt, v, ed, g, ln8, bsz):
    _, t, d = at.shape
    s = t // bsz
    c = SCAN_CHUNK
    per_step = min(SCAN_CHUNKS_PER_STEP, s // c)
    nc = s // (c * per_step)
    n_heads = d // RWKV_HEAD

    def cidx(b, z, ci):
        return b * nc + ci + z * (nc - 1 - 2 * ci)

    dspec = pl.BlockSpec((None, c * per_step, d), lambda b, z, ci: (z, cidx(b, z, ci), 0))
    return pl.pallas_call(
        functools.partial(_scan_kernel, c=c, n_heads=n_heads, per_step=per_step),
        grid=(bsz, 2, nc),
        in_specs=[dspec, dspec, dspec, dspec,
                  pl.BlockSpec((c * per_step, d), lambda b, z, ci: (cidx(b, z, ci), 0)),
                  pl.BlockSpec((None, per_step, 8, d), lambda b, z, ci: (z, cidx(b, z, ci), 0, 0)),
                  dspec,
                  pl.BlockSpec(ln8.shape, lambda b, z, ci: (0, 0))],
        out_specs=dspec,
        out_shape=jax.ShapeDtypeStruct((2, t, d), F32),
        scratch_shapes=[pltpu.VMEM((n_heads // 2, 2 * RWKV_HEAD, 2 * RWKV_HEAD), F32)],
        compiler_params=_params(("parallel", "parallel", "arbitrary")),
        name="rwkv_scan",
    )(at, rt, bt, kt, v, ed, g, ln8)


def _rwkv_out_kernel(gy_ref, gb_ref, x_ref, ga_ref, wo_ref, o_ref):
    o = gy_ref[0] + gy_ref[1] + gb_ref[...]
    o_ref[...] = x_ref[...] + ga_ref[...] * _dot(o.astype(BF16), wo_ref[...])


def _rwkv_out(gy, gb, x2, gate, w_o, bsz, tm=512):
    t, d = x2.shape
    s = t // bsz
    tm = min(tm, s)
    ns = s // tm
    row = lambda b, i: (b * ns + i, 0)
    return pl.pallas_call(
        _rwkv_out_kernel,
        grid=(bsz, ns),
        in_specs=[pl.BlockSpec((2, tm, d), lambda b, i: (0, b * ns + i, 0)),
                  pl.BlockSpec((tm, d), row), pl.BlockSpec((tm, d), row),
                  pl.BlockSpec((None, 1, d), lambda b, i: (b, 0, 0)),
                  pl.BlockSpec(w_o.shape, lambda b, i: (0, 0))],
        out_specs=pl.BlockSpec((tm, d), row),
        out_shape=jax.ShapeDtypeStruct((t, d), F32),
        compiler_params=_params(("parallel", "parallel")),
        name="rwkv_out",
    )(gy, gb, x2, gate, w_o)


def _rwkv_layer(x2, bsz, norm_g, shift, scale, gate, p):
    at, rt, bt, kt, v, ed, g, gb = _rwkv_prep(x2, norm_g, shift, scale, bsz, p, SCAN_DT)
    gy = _rwkv_scan(at, rt, bt, kt, v, ed, g, p["ln8"], bsz)
    return _rwkv_out(gy, gb, x2, gate, p["w_o"], bsz)


def _attn_group_kernel(slope_ref, vid_ref, q_ref, k_ref, v_ref, o_ref, lse_ref, tab_ref, *,
                       tq, sub, w, half, offs, nh, dh, hg, rps):
    first = (pl.program_id(0) == 0) & (pl.program_id(1) == 0) & (pl.program_id(2) == 0)
    i = pl.program_id(2)

    @pl.when(first)
    def _():
        col = lax.broadcasted_iota(jnp.int32, (tq, w), 1)
        row = lax.broadcasted_iota(jnp.int32, (tq, w), 0)
        for v, off in enumerate(offs):
            dist = jnp.abs(col - row + off)
            distf = dist.astype(F32)
            for h in range(nh):
                tab_ref[h * len(offs) + v] = jnp.where(dist <= half, -slope_ref[h] * distf, NEG_INF)

    if w == sub:
        ws = 0
    else:
        ws = pl.multiple_of(jnp.clip(i * tq - half, 0, sub - w), 16)
    var = vid_ref[i]
    nt = (((1,), (1,)), ((), ()))
    lane = lax.broadcasted_iota(jnp.int32, (tq, 128), 1)
    for rr in range(rps):
        qrows = slice(rr * tq, (rr + 1) * tq)
        krows = pl.ds(rr * sub + ws, w)
        lse = jnp.zeros((tq, 128), F32)
        for h0 in range(0, nh, hg):
            heads = range(h0, min(h0 + hg, nh))
            hs = {h: slice(h * dh, (h + 1) * dh) for h in heads}
            sc = {h: lax.dot_general(q_ref[qrows, hs[h]], k_ref[krows, hs[h]], nt, preferred_element_type=F32)
                  + tab_ref[h * len(offs) + var] for h in heads}
            m = {h: sc[h].max(axis=-1, keepdims=True) for h in heads}
            p = {h: jnp.exp(sc[h] - m[h]) for h in heads}
            den = {h: p[h].sum(axis=-1, keepdims=True) for h in heads}
            acc = {h: _dot(p[h].astype(BF16), v_ref[krows, hs[h]]) for h in heads}
            for h in heads:
                o_ref[qrows, hs[h]] = (acc[h] / den[h]).astype(o_ref.dtype)
                lse = jnp.where(lane == h, m[h] + jnp.log(den[h]), lse)
        lse_ref[qrows, :] = lse


def _alibi_slopes(n):
    return 2.0 ** (-8.0 * jnp.arange(1, n + 1, dtype=F32) / n)


def _attention_group(qkv, bsz, gi, tq=256):
    t, width = qkv.shape
    s = t // bsz
    window, dil = ATTN_GROUPS[gi]
    half = window // (2 * dil)
    sub = s // dil
    tq = min(tq, sub)
    nq = sub // tq
    w = min(sub, tq + 2 * half)
    nh, dh = ATTN_HEADS, ATTN_HEAD_DIM
    hw = nh * dh
    offs = sorted({int(np.clip(i * tq - half, 0, sub - w)) - i * tq for i in range(nq)})
    vid = jnp.asarray([offs.index(int(np.clip(i * tq - half, 0, sub - w)) - i * tq) for i in range(nq)], jnp.int32)
    slopes = _alibi_slopes(len(ATTN_GROUPS) * nh).reshape(len(ATTN_GROUPS), nh)[gi] * dil
    smem = pl.BlockSpec(memory_space=pltpu.SMEM)
    rps = min(dil, 4) if nq == 1 else 1
    nr = dil // rps
    qrow = lambda b, r, i: ((b * nr + r) * nq + i, 0)
    return pl.pallas_call(
        functools.partial(_attn_group_kernel, tq=tq, sub=sub, w=w, half=half, offs=tuple(offs), nh=nh, dh=dh,
                          hg=nh if tq * w <= 128 * 128 else 4, rps=rps),
        grid=(bsz, nr, nq),
        in_specs=[smem, smem,
                  pl.BlockSpec((rps * tq, hw), qrow),
                  pl.BlockSpec((rps * sub, hw), lambda b, r, i: (b * nr + r, 1)),
                  pl.BlockSpec((rps * sub, hw), lambda b, r, i: (b * nr + r, 2))],
        out_specs=[pl.BlockSpec((rps * tq, hw), qrow), pl.BlockSpec((rps * tq, 128), qrow)],
        out_shape=[jax.ShapeDtypeStruct((t, hw), BF16), jax.ShapeDtypeStruct((t, 128), F32)],
        scratch_shapes=[pltpu.VMEM((nh * len(offs), tq, w), F32)],
        compiler_params=_params(("arbitrary", "arbitrary", "arbitrary")),
        name=f"dilated_attention_g{gi}",
    )(slopes, vid, qkv, qkv, qkv)


def _to_token_order(blk_ref, scr_ref, dil):
    if dil == 1:
        return blk_ref[0].astype(F32)
    n = blk_ref.shape[1]
    slabs = scr_ref.shape[0]
    for r in range(dil):
        rows = blk_ref[r].astype(F32)
        for c in range(slabs):
            scr_ref[c, pl.ds(r, n, stride=dil), :] = rows[:, c * 128:(c + 1) * 128]
    if slabs == 1:
        return scr_ref[0]
    return jnp.concatenate([scr_ref[c] for c in range(slabs)], axis=1)


def _attn_out_kernel(o1, o2, o3, l1, l2, l3, ex_ref, w_ref, x_ref, ga_ref, out_ref, so_ref, sl_ref, *, dils):
    ls = [_to_token_order(l, sl_ref, dil) for l, dil in zip((l1, l2, l3), dils)]
    m = jnp.maximum(jnp.maximum(ls[0], ls[1]), ls[2])
    es = [jnp.exp(l - m) for l in ls]
    inv = 1.0 / (es[0] + es[1] + es[2])
    merged = None
    for e, o_ref, dil in zip(es, (o1, o2, o3), dils):
        ah, al = _split2(e * inv)
        alpha = _dot(ah, ex_ref[...]) + _dot(al, ex_ref[...])
        term = alpha * _to_token_order(o_ref, so_ref, dil)
        merged = term if merged is None else merged + term
    out_ref[...] = x_ref[...] + ga_ref[...] * _dot(merged.astype(BF16), w_ref[...])


def _attn_out(os, lses, w_o, x2, gate, bsz, tm=512):
    t, d = x2.shape
    s = t // bsz
    tm = min(tm, s)
    ns = s // tm
    hw = os[0].shape[1]
    dils = tuple(dil for _, dil in ATTN_GROUPS)
    ex = np.zeros((128, hw), np.float32)
    ex[np.arange(hw) // ATTN_HEAD_DIM, np.arange(hw)] = 1.0
    ex = jnp.asarray(ex, BF16)
    row = lambda b, i: (b * ns + i, 0)

    def res_major(a, dil):
        c = a.shape[1]
        return a.reshape(bsz, dil, s // dil, c), pl.BlockSpec((None, dil, tm // dil, c), lambda b, i: (b, 0, i, 0))

    o_args, o_specs = zip(*[res_major(o, dil) for o, dil in zip(os, dils)])
    l_args, l_specs = zip(*[res_major(l, dil) for l, dil in zip(lses, dils)])
    return pl.pallas_call(
        functools.partial(_attn_out_kernel, dils=dils),
        grid=(bsz, ns),
        in_specs=list(o_specs) + list(l_specs) + [
            pl.BlockSpec(ex.shape, lambda b, i: (0, 0)),
            pl.BlockSpec(w_o.shape, lambda b, i: (0, 0)),
            pl.BlockSpec((tm, d), row),
            pl.BlockSpec((None, 1, d), lambda b, i: (b, 0, 0))],
        out_specs=pl.BlockSpec((tm, d), row),
        out_shape=jax.ShapeDtypeStruct((t, d), F32),
        scratch_shapes=[pltpu.VMEM((hw // 128, tm, 128), F32), pltpu.VMEM((1, tm, 128), F32)],
        compiler_params=_params(("parallel", "parallel")),
        name="attn_out",
    )(*o_args, *l_args, ex, w_o, x2, gate)


def _norm_attn_kernel(x_ref, g_ref, sh_ref, sc_ref, *rest, dils):
    outs, scr_ref = rest[:-1], rest[-1]
    x = x_ref[...]
    ms = jnp.mean(x * x, axis=-1, keepdims=True)
    y = x * lax.rsqrt(ms + RMS_EPS) * g_ref[...]
    y = y * (1.0 + sc_ref[...]) + sh_ref[...]
    slabs = scr_ref.shape[0]
    if any(dil > 1 for dil in dils):
        for c in range(slabs):
            scr_ref[c] = y[:, c * 128:(c + 1) * 128]
    for o, dil in zip(outs, dils):
        if dil == 1:
            o[0] = y.astype(o.dtype)
        else:
            n = o.shape[1]
            for r in range(dil):
                for c in range(slabs):
                    o[r, :, c * 128:(c + 1) * 128] = scr_ref[c, pl.ds(r, n, stride=dil), :].astype(o.dtype)


def _norm_mod_attn(x2, g, shift, scale, bsz, ts=512):
    t, d = x2.shape
    s = t // bsz
    ts = min(ts, s)
    ns = s // ts
    dils = tuple(dil for _, dil in ATTN_GROUPS)
    vec = pl.BlockSpec((None, 1, d), lambda b, i: (b, 0, 0))
    outs = pl.pallas_call(
        functools.partial(_norm_attn_kernel, dils=dils),
        grid=(bsz, ns),
        in_specs=[pl.BlockSpec((ts, d), lambda b, i: (b * ns + i, 0)),
                  pl.BlockSpec((1, d), lambda b, i: (0, 0)), vec, vec],
        out_specs=[pl.BlockSpec((None, dil, ts // dil, d), lambda b, i: (b, 0, i, 0)) for dil in dils],
        out_shape=[jax.ShapeDtypeStruct((bsz, dil, s // dil, d), BF16) for dil in dils],
        scratch_shapes=[pltpu.VMEM((d // 128, ts, 128), F32)],
        compiler_params=_params(("parallel", "parallel")),
        name="norm_mod_attn",
    )(x2, g.reshape(1, d), shift, scale)
    return [o.reshape(t, d) for o in outs]


def _pack_attn(j, w_qkv, w_o):
    d = w_qkv.shape[1]
    ng = len(ATTN_GROUPS)
    w = w_qkv[j].reshape(d, ng, 3, ATTN_HEADS * ATTN_HEAD_DIM)
    w = w.at[:, :, 0].multiply(ATTN_HEAD_DIM ** -0.5)
    return {"w_qkv": w.reshape(d, -1).astype(BF16), "w_o": w_o[j].astype(BF16)}


def _attn_layer(x2, bsz, hns, gate, p):
    gw = 3 * ATTN_HEADS * ATTN_HEAD_DIM
    outs = [_attention_group(_matmul(hn, p["w_qkv"], BF16, n=gw, col0=gi * gw), bsz, gi)
            for gi, hn in enumerate(hns)]
    return _attn_out([o for o, _ in outs], [l for _, l in outs], p["w_o"], x2, gate, bsz)


def _router_kernel(x_ref, g_ref, sh_ref, sc_ref, wh_ref, wl_ref, b_ref, o_ref, hb_ref, seg_ref):
    x = x_ref[...]
    ms = jnp.mean(x * x, axis=-1, keepdims=True)
    hn = x * lax.rsqrt(ms + RMS_EPS) * g_ref[...]
    hn = hn * (1.0 + sc_ref[...]) + sh_ref[...]
    hb_ref[...] = hn.astype(hb_ref.dtype)
    hh, hl = _split2(hn)
    logits = _dot(hh, wh_ref[...]) + _dot(hl, wh_ref[...]) + _dot(hh, wl_ref[...]) + b_ref[...]
    lane = lax.broadcasted_iota(jnp.int32, logits.shape, 1)
    ng, ne = N_EXPERT_GROUPS, EXPERTS_PER_GROUP
    big = jnp.int32(ROUTER_LANES)
    is_g = lane < ng
    gl = jnp.where(is_g, logits, -jnp.inf)
    gmax = gl.max(axis=-1, keepdims=True)
    gsum = jnp.where(is_g, jnp.exp(logits - gmax), 0.0).sum(axis=-1, keepdims=True)
    p_group = 1.0 / gsum
    g_top = jnp.where(gl == gmax, lane, big).min(axis=-1, keepdims=True)
    lo = ng + ne * g_top
    es = jnp.where((lane >= lo) & (lane < lo + ne), logits, -jnp.inf)
    v1 = es.max(axis=-1, keepdims=True)
    i1 = jnp.where(es == v1, lane, big).min(axis=-1, keepdims=True)
    es2 = jnp.where(lane == i1, -jnp.inf, es)
    v2 = es2.max(axis=-1, keepdims=True)
    i2 = jnp.where(es2 == v2, lane, big).min(axis=-1, keepdims=True)
    e2 = jnp.exp(v2 - v1)
    w1 = p_group / (1.0 + e2)
    w2 = p_group * e2 / (1.0 + e2)
    gates = jnp.where(lane == i1, w1, 0.0) + jnp.where(lane == i2, w2, 0.0)

    tm = x.shape[0]
    onehot = jnp.where(lane == g_top, 1.0, 0.0)
    tri = jnp.where(lax.broadcasted_iota(jnp.int32, (tm, tm), 1) < lax.broadcasted_iota(jnp.int32, (tm, tm), 0),
                    1.0, 0.0).astype(BF16)
    before = _dot(tri, onehot.astype(BF16))
    counts = jnp.broadcast_to(onehot.sum(axis=0, keepdims=True), (8, ROUTER_LANES))
    upper = jnp.where(lax.broadcasted_iota(jnp.int32, (ROUTER_LANES, ROUTER_LANES), 0)
                      < lax.broadcasted_iota(jnp.int32, (ROUTER_LANES, ROUTER_LANES), 1), 1.0, 0.0).astype(BF16)
    ch, cl = _split2(counts)
    seg_lo = _dot(ch, upper) + _dot(cl, upper)
    dest = (onehot * (before + seg_lo[0:1])).sum(axis=-1, keepdims=True)
    o_ref[...] = gates + jnp.where(lane == DEST_LANE, dest, 0.0)
    row8 = lax.broadcasted_iota(jnp.int32, (8, ROUTER_LANES), 0)
    seg_ref[...] = jnp.where(row8 == 0, seg_lo, jnp.where(row8 == 1, seg_lo + counts, 0.0))


def _router(x2, g, shift, scale, wr_hi, wr_lo, bias, bsz):
    t, d = x2.shape
    s = t // bsz
    tm = min(MOE_TILE, s)
    ns = s // tm
    row = lambda b, i: (b * ns + i, 0)
    full = lambda a: pl.BlockSpec(a.shape, lambda b, i: (0, 0))
    vec = pl.BlockSpec((None, 1, d), lambda b, i: (b, 0, 0))
    g = g.reshape(1, d)
    return pl.pallas_call(
        _router_kernel,
        grid=(bsz, ns),
        in_specs=[pl.BlockSpec((tm, d), row), full(g), vec, vec, full(wr_hi), full(wr_lo), full(bias)],
        out_specs=[pl.BlockSpec((tm, ROUTER_LANES), row), pl.BlockSpec((tm, d), row),
                   pl.BlockSpec((None, 8, ROUTER_LANES), lambda b, i: (b * ns + i, 0, 0))],
        out_shape=[jax.ShapeDtypeStruct((t, ROUTER_LANES), F32), jax.ShapeDtypeStruct((t, d), BF16),
                   jax.ShapeDtypeStruct((t // tm, 8, ROUTER_LANES), F32)],
        compiler_params=_params(("parallel", "parallel")),
        name="moe_router",
    )(x2, g, shift, scale, wr_hi, wr_lo, bias)


def _experts_kernel(seg_ref, h_ref, gates_ref, wg_ref, wu_ref, wd_ref, x_ref, ga_ref, fg_ref, o_ref,
                    pt_ref, xs_ref, gs_ref, y_ref, *, per_step, blk, final_norm):
    tile = pl.program_id(0)
    step = pl.program_id(1)
    tm = h_ref.shape[0]
    grp = step // (EXPERTS_PER_GROUP // per_step)
    tn = (((0,), (0,)), ((), ()))

    @pl.when(step == 0)
    def _():
        gates = gates_ref[...]
        lane = lax.broadcasted_iota(jnp.int32, gates.shape, 1)
        dest = jnp.where(lane == DEST_LANE, gates, 0.0).sum(axis=-1, keepdims=True)
        col = lax.broadcasted_iota(jnp.int32, (tm, tm), 1).astype(F32)
        pt = jnp.where(dest == col, 1.0, 0.0).astype(BF16)
        pt_ref[...] = pt
        xs_ref[...] = lax.dot_general(pt, h_ref[...], tn, preferred_element_type=F32).astype(BF16)
        g1 = gates.astype(BF16)
        r1 = gates - g1.astype(F32)
        g2 = r1.astype(BF16)
        g3 = (r1 - g2.astype(F32)).astype(BF16)
        gs_ref[...] = (lax.dot_general(pt, g1, tn, preferred_element_type=F32)
                       + lax.dot_general(pt, g2, tn, preferred_element_type=F32)
                       + lax.dot_general(pt, g3, tn, preferred_element_type=F32))
        y_ref[...] = jnp.zeros_like(y_ref)

    lo = seg_ref[tile * 8 + grp]
    hi = seg_ref[tile * 8 + N_EXPERT_GROUPS + grp]
    for b in range(tm // blk):
        @pl.when((lo < (b + 1) * blk) & (hi > b * blk))
        def _(b=b):
            rows = pl.ds(b * blk, blk)
            h = xs_ref[rows, :]
            gsb = gs_ref[rows, :]
            lane = lax.broadcasted_iota(jnp.int32, gsb.shape, 1)
            hids = []
            for j in range(per_step):
                e = step * per_step + j
                ge = jnp.where(lane == N_EXPERT_GROUPS + e, gsb, 0.0).sum(axis=-1, keepdims=True)
                gp = _dot(h, wg_ref[j])
                up = _dot(h, wu_ref[j])
                hids.append((gp * _sigmoid(gp) * up * ge).astype(BF16))
            y_ref[rows, :] += _dot(jnp.concatenate(hids, axis=1), wd_ref[...])

    @pl.when(step == pl.num_programs(1) - 1)
    def _():
        res = x_ref[...] + ga_ref[...] * _dot(pt_ref[...], y_ref[...].astype(BF16))
        if final_norm:
            ms = jnp.mean(res * res, axis=-1, keepdims=True)
            res = res * lax.rsqrt(ms + RMS_EPS) * fg_ref[...]
        o_ref[...] = res


def _experts(hn_bf16, gates, seg, wg, wu, wd, layer, x2, gate, final_g, final_norm, bsz, per_step=8):
    t, d = x2.shape
    s = t // bsz
    tm = min(MOE_TILE, s)
    ns = s // tm
    f = wg.shape[2]
    ne = N_EXPERT_GROUPS * EXPERTS_PER_GROUP
    steps = ne // per_step
    blk = min(MOE_BLOCK, tm)
    row = lambda i, e, seg: (i, 0)
    wspec = pl.BlockSpec((per_step, d, f), lambda i, e, seg: (layer * steps + e, 0, 0))
    grid_spec = pltpu.PrefetchScalarGridSpec(
        num_scalar_prefetch=1,
        grid=(t // tm, steps),
        in_specs=[
            pl.BlockSpec((tm, d), row),
            pl.BlockSpec((tm, ROUTER_LANES), row),
            wspec, wspec,
            pl.BlockSpec((per_step * f, d), lambda i, e, seg: (layer * steps + e, 0)),
            pl.BlockSpec((tm, d), row),
            pl.BlockSpec((None, 1, d), lambda i, e, seg: (i // ns, 0, 0)),
            pl.BlockSpec((1, d), lambda i, e, seg: (0, 0)),
        ],
        out_specs=pl.BlockSpec((tm, d), row),
        scratch_shapes=[pltpu.VMEM((tm, tm), BF16), pltpu.VMEM((tm, d), BF16),
                        pltpu.VMEM((tm, ROUTER_LANES), F32), pltpu.VMEM((tm, d), F32)],
    )
    return pl.pallas_call(
        functools.partial(_experts_kernel, per_step=per_step, blk=blk, final_norm=final_norm),
        grid_spec=grid_spec,
        out_shape=jax.ShapeDtypeStruct((t, d), F32),
        compiler_params=_params(("parallel", "arbitrary")),
        name="moe_experts",
    )(seg, hn_bf16, gates, wg, wu, wd, x2, gate, final_g.reshape(1, d))


def _moe_layer(x2, bsz, g, shift, scale, gate, p, final_g, final_norm):
    gates, hn_bf16, seg = _router(x2, g, shift, scale, p["wr_hi"], p["wr_lo"], p["r_bias"], bsz)
    ng = N_EXPERT_GROUPS
    seg = jnp.concatenate([seg[:, 0, :ng], seg[:, 1, :ng]], axis=1).astype(jnp.int32).reshape(-1)
    return _experts(hn_bf16, gates, seg, p["wg"], p["wu"], p["wd"], p["layer"], x2, gate,
                    final_g, final_norm, bsz)


def _pad_rows(rows, d):
    out = jnp.zeros((8, d), F32)
    return out.at[:len(rows)].set(jnp.stack([r.reshape(d).astype(F32) for r in rows]))


def _pack_rwkv(j, mu, w_rkv, w0, w1, w2, a0, a1, a2, g1, g2, k_k, k_a, r_k, ln_w, ln_b, w_o):
    d = w_o.shape[-1]
    cat = lambda m: jnp.concatenate([m[j, 0], m[j, 1]], axis=1).astype(BF16)
    stack = lambda m: jnp.concatenate([m[j, 0], m[j, 1]], axis=0).astype(BF16)
    return {
        "mu8": _pad_rows(list(mu[j]), d),
        "w_rkv": w_rkv[j].astype(BF16),
        "w1c": cat(w1), "a1c": cat(a1), "g1c": cat(g1),
        "w2s": stack(w2), "a2s": stack(a2), "g2": g2[j].astype(BF16),
        "vec8": _pad_rows([w0[j, 0], w0[j, 1], a0[j, 0], a0[j, 1], k_k[j], k_a[j], r_k[j]], d),
        "ln8": _pad_rows([ln_w[j], ln_b[j]], d),
        "w_o": w_o[j].astype(BF16),
    }


def _pack_moe(i, router_g, router_g_b, router_e, router_e_b, w_gate, w_up, w_down):
    d = router_g.shape[1]
    ng, ne = N_EXPERT_GROUPS, EXPERTS_PER_GROUP
    wr = jnp.zeros((d, ROUTER_LANES), F32)
    wr = wr.at[:, :ng].set(router_g[i])
    wr = wr.at[:, ng:ng + ng * ne].set(router_e[i].transpose(1, 0, 2).reshape(d, ng * ne))
    bias = jnp.zeros((1, ROUTER_LANES), F32)
    bias = bias.at[0, :ng].set(router_g_b[i])
    bias = bias.at[0, ng:ng + ng * ne].set(router_e_b[i].reshape(ng * ne))
    wr_hi = wr.astype(BF16)
    wr_lo = (wr - wr_hi.astype(F32)).astype(BF16)
    f = w_gate.shape[-1]
    return {"wr_hi": wr_hi, "wr_lo": wr_lo, "r_bias": bias,
            "layer": i,
            "wg": w_gate.reshape(-1, d, f).astype(BF16), "wu": w_up.reshape(-1, d, f).astype(BF16),
            "wd": w_down.reshape(-1, d).astype(BF16)}


def kernel(x, c, ada_w, ada_b, norm_tm_g, norm_cm_g, rw_mu, rw_w_rkv, rw_w0, rw_w1, rw_w2, rw_a0, rw_a1, rw_a2, rw_g1, rw_g2, rw_k_k, rw_k_a, rw_r_k, rw_ln_w, rw_ln_b, rw_w_o, at_w_qkv, at_w_o, moe_router_g, moe_router_g_b, moe_router_e, moe_router_e_b, moe_w_gate, moe_w_up, moe_w_down, final_g):
    bsz, s, d = x.shape
    depth = ada_w.shape[0]
    x2 = x.reshape(bsz * s, d)
    mod = _ada_mod(c, ada_w, ada_b)
    for i in range(depth):
        sh_t, sc_t, ga_t, sh_c, sc_c, ga_c = (mod[i, m] for m in range(N_MOD))
        j = i // 2
        if i % 2 == 0:
            p = _pack_rwkv(j, rw_mu, rw_w_rkv, rw_w0, rw_w1, rw_w2, rw_a0, rw_a1, rw_a2, rw_g1, rw_g2,
                           rw_k_k, rw_k_a, rw_r_k, rw_ln_w, rw_ln_b, rw_w_o)
            x2 = _rwkv_layer(x2, bsz, norm_tm_g[i], sh_t, sc_t, ga_t, p)
        else:
            hns = _norm_mod_attn(x2, norm_tm_g[i], sh_t, sc_t, bsz)
            p = _pack_attn(j, at_w_qkv, at_w_o)
            x2 = _attn_layer(x2, bsz, hns, ga_t, p)
        pm = _pack_moe(i, moe_router_g, moe_router_g_b, moe_router_e, moe_router_e_b,
                       moe_w_gate, moe_w_up, moe_w_down)
        x2 = _moe_layer(x2, bsz, norm_cm_g[i], sh_c, sc_c, ga_c, pm, final_g, i == depth - 1)
    return x2.reshape(bsz, s, d)
```

```python
import functools
import math

import numpy as np
import jax
import jax.numpy as jnp
from jax import lax
from jax.experimental import pallas as pl
from jax.experimental.pallas import tpu as pltpu

F32 = jnp.float32
BF16 = jnp.bfloat16

RMS_EPS = 1e-6
N_MOD = 6
RWKV_HEAD = 64
RWKV_GN_EPS = 64e-5
ATTN_GROUPS = ((128, 1), (512, 4), (2048, 16))
ATTN_HEADS = 8
ATTN_HEAD_DIM = 128
NEG_INF = -1e30
N_EXPERT_GROUPS = 4
EXPERTS_PER_GROUP = 8
ROUTER_LANES = 128
DEST_LANE = 64
MOE_TILE = 1024
MOE_BLOCK = 128

SCAN_CHUNK = 64
SCAN_CHUNKS_PER_STEP = 4
V7X_VMEM_LIMIT = 58 * 1024 * 1024
SCAN_DT = BF16


def _params(sem):
    return pltpu.CompilerParams(dimension_semantics=sem, vmem_limit_bytes=V7X_VMEM_LIMIT)


def _dot(a, b):
    return jnp.dot(a, b, preferred_element_type=F32)


def _sigmoid(x):
    return 0.5 * jnp.tanh(0.5 * x) + 0.5


def _split2(x):
    hi = x.astype(BF16)
    lo = (x - hi.astype(F32)).astype(BF16)
    return hi, lo


def _ada_kernel(c_ref, w_ref, b_ref, o_ref):
    c = c_ref[...]
    sc = c * _sigmoid(c)
    o_ref[...] = _dot(sc.astype(BF16), w_ref[...].astype(BF16)) + b_ref[...]


def _ada_mod(c, ada_w, ada_b):
    depth, d, n = ada_w.shape
    bsz = c.shape[0]
    tn = n // 4
    out = pl.pallas_call(
        _ada_kernel,
        grid=(depth, n // tn),
        in_specs=[
            pl.BlockSpec((bsz, d), lambda i, j: (0, 0)),
            pl.BlockSpec((None, d, tn), lambda i, j: (i, 0, j)),
            pl.BlockSpec((None, 1, tn), lambda i, j: (i, 0, j)),
        ],
        out_specs=pl.BlockSpec((None, bsz, tn), lambda i, j: (i, 0, j)),
        out_shape=jax.ShapeDtypeStruct((depth, bsz, n), F32),
        compiler_params=_params(("parallel", "parallel")),
        name="ada_mod",
    )(c, ada_w, ada_b.reshape(depth, 1, n))
    return out.reshape(depth, bsz, N_MOD, 1, d).transpose(0, 2, 1, 3, 4)


def _mm_kernel(a_ref, b_ref, o_ref):
    o_ref[...] = _dot(a_ref[...], b_ref[...]).astype(o_ref.dtype)


def _matmul(a, b, out_dtype, n=None, col0=0, tm=1024, tn=1024):
    m, k = a.shape
    n = b.shape[1] if n is None else n
    tm, tn = min(tm, m), min(tn, n)
    cb0 = col0 // tn
    return pl.pallas_call(
        _mm_kernel,
        grid=(n // tn, m // tm),
        in_specs=[
            pl.BlockSpec((tm, k), lambda j, i: (i, 0)),
            pl.BlockSpec((k, tn), lambda j, i: (0, cb0 + j)),
        ],
        out_specs=pl.BlockSpec((tm, tn), lambda j, i: (i, j)),
        out_shape=jax.ShapeDtypeStruct((m, n), out_dtype),
        compiler_params=_params(("parallel", "parallel")),
        name="matmul",
    )(a, b)


def _seg_sum(x, seg_ref, segt_ref):
    xh, xl = _split2(x)
    s = _dot(xh, seg_ref[...]) + _dot(xl, seg_ref[...])
    sh, sl = _split2(s)
    return _dot(sh, segt_ref[...]) + _dot(sl, segt_ref[...])


def _rwkv_prep_kernel(x_ref, prev_ref, next_ref, ng_ref, sh_ref, sc_ref, mu_ref, wrkv_ref, w1_ref, a1_ref,
                      g1_ref, w2_ref, a2_ref, g2_ref, vec_ref, cm_ref, edm_ref, seg_ref, segt_ref,
                      at_ref, rt_ref, bt_ref, kt_ref, v_ref, ed_ref, g_ref, gb_ref, *, ts, ns):
    i = pl.program_id(1)

    def norm_mod(x):
        ms = jnp.mean(x * x, axis=-1, keepdims=True)
        return x * lax.rsqrt(ms + RMS_EPS) * ng_ref[...] * (1.0 + sc_ref[...]) + sh_ref[...]

    cur = norm_mod(x_ref[...])
    row = lax.broadcasted_iota(jnp.int32, (ts, 1), 0)
    prev_row = jnp.where(i > 0, norm_mod(prev_ref[7:8, :]), 0.0)
    next_row = jnp.where(i < ns - 1, norm_mod(next_ref[0:1, :]), 0.0)
    x_prev = jnp.where(row == 0, prev_row, pltpu.roll(cur, 1, 0))
    x_next = jnp.where(row == ts - 1, next_row, pltpu.roll(cur, ts - 1, 0))
    xx = 0.5 * (x_prev + x_next) - cur

    def mix(j):
        return (cur + xx * mu_ref[j:j + 1, :]).astype(BF16)

    r = _dot(mix(0), wrkv_ref[0])
    k = _dot(mix(1), wrkv_ref[1])
    v = _dot(mix(2), wrkv_ref[2])
    tw = jnp.tanh(_dot(mix(3), w1_ref[...]))
    ta = _dot(mix(4), a1_ref[...])
    tg = _sigmoid(_dot(mix(5), g1_ref[...]))
    lane = lax.broadcasted_iota(jnp.int32, (1, tw.shape[1]), 1)
    half = tw.shape[1] // 2
    hg = tg.shape[1] // 2

    k_k = vec_ref[4:5, :]
    k_a = vec_ref[5:6, :]
    r_k = vec_ref[6:7, :]
    kk = k * k_k
    ss = _seg_sum(kk * kk, seg_ref, segt_ref)
    kkn = kk * lax.rsqrt(jnp.maximum(ss, 1e-24))
    v_ref[...] = v.astype(v_ref.dtype)

    gb = jnp.zeros_like(cur)
    for z in range(2):
        sel = (lane < half) if z == 0 else (lane >= half)
        lw = _dot(jnp.where(sel, tw, 0.0).astype(BF16), w2_ref[...])
        la = _dot(jnp.where(sel, ta, 0.0).astype(BF16), a2_ref[...])
        g = _dot(tg[:, z * hg:(z + 1) * hg].astype(BF16), g2_ref[z])
        ld = -math.exp(-0.5) * _sigmoid(vec_ref[z:z + 1, :] + lw)
        a = _sigmoid(vec_ref[2 + z:3 + z, :] + la)
        kdir = k * (1.0 + (a - 1.0) * k_a)
        b = kkn * a
        parts = _split2(ld)
        cmz = cm_ref[z]
        lm = _dot(cmz, parts[0]) + _dot(cmz, parts[1])
        e_in = jnp.exp(lm)
        e_inv = jnp.exp(-lm)
        e_ex = jnp.exp(lm - ld)
        rt_ref[z] = (r * e_in).astype(rt_ref.dtype)
        at_ref[z] = (-kkn * e_ex).astype(at_ref.dtype)
        bt_ref[z] = (b * e_inv).astype(bt_ref.dtype)
        kt_ref[z] = (kdir * e_inv).astype(kt_ref.dtype)
        edz = edm_ref[z]
        ed = jnp.exp(_dot(edz, parts[0]) + _dot(edz, parts[1]))
        for ck in range(ed.shape[0] // 8):
            ed_ref[z, ck] = ed[ck * 8:(ck + 1) * 8]
        g_ref[z] = g.astype(g_ref.dtype)
        bonus = _seg_sum(r * kdir * r_k, seg_ref, segt_ref) * v
        gb = gb + g * bonus
    gb_ref[...] = gb


def _chunk_matrices(ts, c):
    t = np.arange(ts)
    same = (t[:, None] // c) == (t[None, :] // c)
    pos = t % c
    cm = np.zeros((2, ts, ts), np.float32)
    nck = ts // c
    edm = np.zeros((2, 8 * nck, ts), np.float32)
    tri_f = same & (t[None, :] <= t[:, None])
    sel_f = same & (pos[None, :] <= c // 2 - 1)
    cm[0] = tri_f.astype(np.float32) - sel_f.astype(np.float32)
    tri_b = same & (t[None, :] >= t[:, None])
    sel_b = same & (pos[None, :] >= c // 2)
    cm[1] = tri_b.astype(np.float32) - sel_b.astype(np.float32)
    for ck in range(nck):
        inck = (t // c) == ck
        edm[0, 8 * ck] = inck & (pos <= c // 2 - 1)
        edm[0, 8 * ck + 1] = inck & (pos > c // 2 - 1)
        edm[1, 8 * ck] = inck & (pos >= c // 2)
        edm[1, 8 * ck + 1] = inck & (pos < c // 2)
    return jnp.asarray(cm, BF16), jnp.asarray(edm, BF16)


def _seg_matrices(d, head):
    seg = np.zeros((d, 128), np.float32)
    seg[np.arange(d), np.arange(d) // head] = 1.0
    return jnp.asarray(seg, BF16), jnp.asarray(seg.T.copy(), BF16)


def _rwkv_prep(x2, norm_g, shift, scale, bsz, p, scan_dtype, ts=256):
    t, d = x2.shape
    s = t // bsz
    ts = min(ts, s)
    ns = s // ts
    c = SCAN_CHUNK
    assert ts % c == 0 and ts % 8 == 0
    nck = ts // c
    cm, edm = _chunk_matrices(ts, c)
    seg, segt = _seg_matrices(d, RWKV_HEAD)
    r8 = ts // 8
    nblk8 = t // 8
    row = lambda b, i: (b * ns + i, 0)
    full = lambda a: pl.BlockSpec(a.shape, lambda b, i, _n=a.ndim: (0,) * _n)
    weights = [p["mu8"], p["w_rkv"], p["w1c"], p["a1c"], p["g1c"], p["w2s"], p["a2s"], p["g2"],
               p["vec8"], cm, edm, seg, segt]
    dir_spec = pl.BlockSpec((2, ts, d), lambda b, i: (0, b * ns + i, 0))
    outs = pl.pallas_call(
        functools.partial(_rwkv_prep_kernel, ts=ts, ns=ns),
        grid=(bsz, ns),
        in_specs=[
            pl.BlockSpec((ts, d), row),
            pl.BlockSpec((8, d), lambda b, i: (jnp.maximum((b * ns + i) * r8 - 1, 0), 0)),
            pl.BlockSpec((8, d), lambda b, i: (jnp.minimum((b * ns + i + 1) * r8, nblk8 - 1), 0)),
            pl.BlockSpec((1, d), lambda b, i: (0, 0)),
            pl.BlockSpec((None, 1, d), lambda b, i: (b, 0, 0)),
            pl.BlockSpec((None, 1, d), lambda b, i: (b, 0, 0)),
        ] + [full(a) for a in weights],
        out_specs=[dir_spec, dir_spec, dir_spec, dir_spec,
                   pl.BlockSpec((ts, d), row),
                   pl.BlockSpec((2, nck, 8, d), lambda b, i: (0, b * ns + i, 0, 0)),
                   dir_spec,
                   pl.BlockSpec((ts, d), row)],
        out_shape=[jax.ShapeDtypeStruct((2, t, d), scan_dtype)] * 4 + [
            jax.ShapeDtypeStruct((t, d), scan_dtype),
            jax.ShapeDtypeStruct((2, t // c, 8, d), F32),
            jax.ShapeDtypeStruct((2, t, d), F32),
            jax.ShapeDtypeStruct((t, d), F32)],
        compiler_params=_params(("parallel", "parallel")),
        name="rwkv_prep",
    )(x2, x2, x2, norm_g.reshape(1, d), shift, scale, *weights)
    return outs


def _sdot(a, b, dims):
    return lax.dot_general(a.astype(SCAN_DT), b.astype(SCAN_DT), (dims, ((), ())),
                           preferred_element_type=F32)


def _scan_kernel(at_ref, rt_ref, bt_ref, kt_ref, v_ref, ed_ref, g_ref, ln_ref, y_ref, s_ref, *,
                 c, n_heads, per_step):
    z = pl.program_id(1)
    ci = pl.program_id(2)

    @pl.when(ci == 0)
    def _():
        s_ref[...] = jnp.zeros_like(s_ref)

    c2 = 2 * c
    pw_lanes = 2 * RWKV_HEAD
    row = lax.broadcasted_iota(jnp.int32, (c2, c2), 0)
    col = lax.broadcasted_iota(jnp.int32, (c2, c2), 1)
    same_blk = (row // c) == (col // c)
    diff = jnp.where(z == 0, row - col, col - row)
    strict = same_blk & (diff > 0)
    incl = same_blk & (diff >= 0)
    incl2 = jnp.concatenate([incl, incl], axis=1)
    lane_a = lax.broadcasted_iota(jnp.int32, (c, pw_lanes), 1) < RWKV_HEAD
    same_head = ((lax.broadcasted_iota(jnp.int32, (pw_lanes, pw_lanes), 0) // RWKV_HEAD)
                 == (lax.broadcasted_iota(jnp.int32, (pw_lanes, pw_lanes), 1) // RWKV_HEAD))
    n_steps = int(math.log2(c))
    pairs = range(n_heads // 2)
    sls = [pl.ds(hp * pw_lanes, pw_lanes) for hp in pairs]
    zero = jnp.zeros((c, pw_lanes), SCAN_DT)
    for j in range(per_step):
        jj = jnp.where(z == 0, j, per_step - 1 - j)
        rows = pl.ds(pl.multiple_of(jj * c, c), c)
        _scan_chunk(at_ref, rt_ref, bt_ref, kt_ref, v_ref, ed_ref.at[jj], g_ref, ln_ref, y_ref, s_ref, rows, pairs,
                    sls, zero, lane_a, strict, incl2, same_head, c, c2, pw_lanes, n_steps)


def _scan_chunk(at_ref, rt_ref, bt_ref, kt_ref, v_ref, ed_ref, g_ref, ln_ref, y_ref, s_ref, rows, pairs,
                sls, zero, lane_a, strict, incl2, same_head, c, c2, pw_lanes, n_steps):
    nt = (((1,), (1,)))
    nn = (((1,), (0,)))
    tn = (((0,), (0,)))
    lhs, rhs, vv, st = [], [], [], []
    for hp in pairs:
        a2 = at_ref[rows, sls[hp]]
        r2 = rt_ref[rows, sls[hp]]
        b2 = bt_ref[rows, sls[hp]]
        k2 = kt_ref[rows, sls[hp]]
        v2 = v_ref[rows, sls[hp]]
        lhs.append(jnp.concatenate([jnp.where(lane_a, a2, zero), jnp.where(lane_a, zero, a2),
                                    jnp.where(lane_a, r2, zero), jnp.where(lane_a, zero, r2)], axis=0))
        rhs.append(jnp.concatenate([b2, b2, k2, k2], axis=0))
        vv.append(jnp.concatenate([v2, v2], axis=0))
        st.append(s_ref[hp] * ed_ref[0:1, sls[hp]])
    gq = [_sdot(lhs[hp], jnp.concatenate([rhs[hp], st[hp].astype(SCAN_DT)], axis=0), nt) for hp in pairs]
    gm = [g[:, :2 * c2] for g in gq]
    q = [g[:, 2 * c2:] for g in gq]
    pw = [jnp.where(strict, gm[hp][:c2, :c2], 0.0).astype(SCAN_DT) for hp in pairs]
    m2 = [jnp.where(strict, gm[hp][:c2, c2:], 0.0).astype(SCAN_DT) for hp in pairs]
    m34 = [jnp.where(incl2, gm[hp][c2:, :], 0.0).astype(SCAN_DT) for hp in pairs]
    x = [q[hp][:c2] + _sdot(m2[hp], vv[hp], nn) for hp in pairs]
    for step in range(n_steps - 1):
        both = [_sdot(pw[hp], jnp.concatenate([x[hp].astype(SCAN_DT), pw[hp]], axis=1), nn) for hp in pairs]
        x = [x[hp] + both[hp][:, :pw_lanes] for hp in pairs]
        pw = [both[hp][:, pw_lanes:].astype(SCAN_DT) for hp in pairs]
    x = [x[hp] + _sdot(pw[hp], x[hp], nn) for hp in pairs]
    for hp in pairs:
        xs = x[hp].astype(SCAN_DT)
        uvs = jnp.concatenate([xs, vv[hp]], axis=0)
        ys = q[hp][c2:] + _sdot(m34[hp], uvs, nn)
        y = jnp.where(lane_a, ys[:c], ys[c:])
        inv_n = 1.0 / RWKV_HEAD
        sum_a = jnp.where(lane_a, y, 0.0).sum(axis=-1, keepdims=True)
        sum_b = jnp.where(lane_a, 0.0, y).sum(axis=-1, keepdims=True)
        yc = y - jnp.where(lane_a, sum_a, sum_b) * inv_n
        sq = yc * yc
        sq_a = jnp.where(lane_a, sq, 0.0).sum(axis=-1, keepdims=True)
        sq_b = jnp.where(lane_a, 0.0, sq).sum(axis=-1, keepdims=True)
        var = jnp.where(lane_a, sq_a, sq_b) * inv_n
        yn = yc * lax.rsqrt(var + RWKV_GN_EPS) * ln_ref[0:1, sls[hp]] + ln_ref[1:2, sls[hp]]
        y_ref[rows, sls[hp]] = g_ref[rows, sls[hp]] * yn
        u2 = jnp.where(lane_a, xs[:c], xs[c:])
        upd = _sdot(jnp.concatenate([u2, vv[hp][:c]], axis=0), rhs[hp][c:3 * c], tn)
        s_ref[hp] = (st[hp] + jnp.where(same_head, upd, 0.0)) * ed_ref[1:2, sls[hp]]


def _rwkv_scan(at, rt, bt, kt, v, ed, g, ln8, bsz):
    _, t, d = at.shape
    s = t // bsz
    c = SCAN_CHUNK
    per_step = min(SCAN_CHUNKS_PER_STEP, s // c)
    nc = s // (c * per_step)
    n_heads = d // RWKV_HEAD

    def cidx(b, z, ci):
        return b * nc + ci + z * (nc - 1 - 2 * ci)

    dspec = pl.BlockSpec((None, c * per_step, d), lambda b, z, ci: (z, cidx(b, z, ci), 0))
    return pl.pallas_call(
        functools.partial(_scan_kernel, c=c, n_heads=n_heads, per_step=per_step),
        grid=(bsz, 2, nc),
        in_specs=[dspec, dspec, dspec, dspec,
                  pl.BlockSpec((c * per_step, d), lambda b, z, ci: (cidx(b, z, ci), 0)),
                  pl.BlockSpec((None, per_step, 8, d), lambda b, z, ci: (z, cidx(b, z, ci), 0, 0)),
                  dspec,
                  pl.BlockSpec(ln8.shape, lambda b, z, ci: (0, 0))],
        out_specs=dspec,
        out_shape=jax.ShapeDtypeStruct((2, t, d), F32),
        scratch_shapes=[pltpu.VMEM((n_heads // 2, 2 * RWKV_HEAD, 2 * RWKV_HEAD), F32)],
        compiler_params=_params(("parallel", "parallel", "arbitrary")),
        name="rwkv_scan",
    )(at, rt, bt, kt, v, ed, g, ln8)


def _rwkv_out_kernel(gy_ref, gb_ref, x_ref, ga_ref, wo_ref, o_ref):
    o = gy_ref[0] + gy_ref[1] + gb_ref[...]
    o_ref[...] = x_ref[...] + ga_ref[...] * _dot(o.astype(BF16), wo_ref[...])


def _rwkv_out(gy, gb, x2, gate, w_o, bsz, tm=512):
    t, d = x2.shape
    s = t // bsz
    tm = min(tm, s)
    ns = s // tm
    row = lambda b, i: (b * ns + i, 0)
    return pl.pallas_call(
        _rwkv_out_kernel,
        grid=(bsz, ns),
        in_specs=[pl.BlockSpec((2, tm, d), lambda b, i: (0, b * ns + i, 0)),
                  pl.BlockSpec((tm, d), row), pl.BlockSpec((tm, d), row),
                  pl.BlockSpec((None, 1, d), lambda b, i: (b, 0, 0)),
                  pl.BlockSpec(w_o.shape, lambda b, i: (0, 0))],
        out_specs=pl.BlockSpec((tm, d), row),
        out_shape=jax.ShapeDtypeStruct((t, d), F32),
        compiler_params=_params(("parallel", "parallel")),
        name="rwkv_out",
    )(gy, gb, x2, gate, w_o)


def _rwkv_layer(x2, bsz, norm_g, shift, scale, gate, p):
    at, rt, bt, kt, v, ed, g, gb = _rwkv_prep(x2, norm_g, shift, scale, bsz, p, SCAN_DT)
    gy = _rwkv_scan(at, rt, bt, kt, v, ed, g, p["ln8"], bsz)
    return _rwkv_out(gy, gb, x2, gate, p["w_o"], bsz)


def _attn_group_kernel(slope_ref, vid_ref, q_ref, k_ref, v_ref, o_ref, lse_ref, tab_ref, *,
                       tq, sub, w, half, offs, nh, dh, hg, rps):
    first = (pl.program_id(0) == 0) & (pl.program_id(1) == 0) & (pl.program_id(2) == 0)
    i = pl.program_id(2)

    @pl.when(first)
    def _():
        col = lax.broadcasted_iota(jnp.int32, (tq, w), 1)
        row = lax.broadcasted_iota(jnp.int32, (tq, w), 0)
        for v, off in enumerate(offs):
            dist = jnp.abs(col - row + off)
            distf = dist.astype(F32)
            for h in range(nh):
                tab_ref[h * len(offs) + v] = jnp.where(dist <= half, -slope_ref[h] * distf, NEG_INF)

    if w == sub:
        ws = 0
    else:
        ws = pl.multiple_of(jnp.clip(i * tq - half, 0, sub - w), 16)
    var = vid_ref[i]
    nt = (((1,), (1,)), ((), ()))
    lane = lax.broadcasted_iota(jnp.int32, (tq, 128), 1)
    for rr in range(rps):
        qrows = slice(rr * tq, (rr + 1) * tq)
        krows = pl.ds(rr * sub + ws, w)
        lse = jnp.zeros((tq, 128), F32)
        for h0 in range(0, nh, hg):
            heads = range(h0, min(h0 + hg, nh))
            hs = {h: slice(h * dh, (h + 1) * dh) for h in heads}
            sc = {h: lax.dot_general(q_ref[qrows, hs[h]], k_ref[krows, hs[h]], nt, preferred_element_type=F32)
                  + tab_ref[h * len(offs) + var] for h in heads}
            m = {h: sc[h].max(axis=-1, keepdims=True) for h in heads}
            p = {h: jnp.exp(sc[h] - m[h]) for h in heads}
            den = {h: p[h].sum(axis=-1, keepdims=True) for h in heads}
            acc = {h: _dot(p[h].astype(BF16), v_ref[krows, hs[h]]) for h in heads}
            for h in heads:
                o_ref[qrows, hs[h]] = (acc[h] / den[h]).astype(o_ref.dtype)
                lse = jnp.where(lane == h, m[h] + jnp.log(den[h]), lse)
        lse_ref[qrows, :] = lse


def _alibi_slopes(n):
    return 2.0 ** (-8.0 * jnp.arange(1, n + 1, dtype=F32) / n)


def _attention_group(qkv, bsz, gi, tq=256):
    t, width = qkv.shape
    s = t // bsz
    window, dil = ATTN_GROUPS[gi]
    half = window // (2 * dil)
    sub = s // dil
    tq = min(tq, sub)
    nq = sub // tq
    w = min(sub, tq + 2 * half)
    nh, dh = ATTN_HEADS, ATTN_HEAD_DIM
    hw = nh * dh
    offs = sorted({int(np.clip(i * tq - half, 0, sub - w)) - i * tq for i in range(nq)})
    vid = jnp.asarray([offs.index(int(np.clip(i * tq - half, 0, sub - w)) - i * tq) for i in range(nq)], jnp.int32)
    slopes = _alibi_slopes(len(ATTN_GROUPS) * nh).reshape(len(ATTN_GROUPS), nh)[gi] * dil
    smem = pl.BlockSpec(memory_space=pltpu.SMEM)
    rps = min(dil, 4) if nq == 1 else 1
    nr = dil // rps
    qrow = lambda b, r, i: ((b * nr + r) * nq + i, 0)
    return pl.pallas_call(
        functools.partial(_attn_group_kernel, tq=tq, sub=sub, w=w, half=half, offs=tuple(offs), nh=nh, dh=dh,
                          hg=nh if tq * w <= 128 * 128 else 4, rps=rps),
        grid=(bsz, nr, nq),
        in_specs=[smem, smem,
                  pl.BlockSpec((rps * tq, hw), qrow),
                  pl.BlockSpec((rps * sub, hw), lambda b, r, i: (b * nr + r, 1)),
                  pl.BlockSpec((rps * sub, hw), lambda b, r, i: (b * nr + r, 2))],
        out_specs=[pl.BlockSpec((rps * tq, hw), qrow), pl.BlockSpec((rps * tq, 128), qrow)],
        out_shape=[jax.ShapeDtypeStruct((t, hw), BF16), jax.ShapeDtypeStruct((t, 128), F32)],
        scratch_shapes=[pltpu.VMEM((nh * len(offs), tq, w), F32)],
        compiler_params=_params(("arbitrary", "arbitrary", "arbitrary")),
        name=f"dilated_attention_g{gi}",
    )(slopes, vid, qkv, qkv, qkv)


def _to_token_order(blk_ref, scr_ref, dil):
    if dil == 1:
        return blk_ref[0].astype(F32)
    n = blk_ref.shape[1]
    slabs = scr_ref.shape[0]
    for r in range(dil):
        rows = blk_ref[r].astype(F32)
        for c in range(slabs):
            scr_ref[c, pl.ds(r, n, stride=dil), :] = rows[:, c * 128:(c + 1) * 128]
    if slabs == 1:
        return scr_ref[0]
    return jnp.concatenate([scr_ref[c] for c in range(slabs)], axis=1)


def _attn_out_kernel(o1, o2, o3, l1, l2, l3, ex_ref, w_ref, x_ref, ga_ref, out_ref, so_ref, sl_ref, *, dils):
    ls = [_to_token_order(l, sl_ref, dil) for l, dil in zip((l1, l2, l3), dils)]
    m = jnp.maximum(jnp.maximum(ls[0], ls[1]), ls[2])
    es = [jnp.exp(l - m) for l in ls]
    inv = 1.0 / (es[0] + es[1] + es[2])
    merged = None
    for e, o_ref, dil in zip(es, (o1, o2, o3), dils):
        ah, al = _split2(e * inv)
        alpha = _dot(ah, ex_ref[...]) + _dot(al, ex_ref[...])
        term = alpha * _to_token_order(o_ref, so_ref, dil)
        merged = term if merged is None else merged + term
    out_ref[...] = x_ref[...] + ga_ref[...] * _dot(merged.astype(BF16), w_ref[...])


def _attn_out(os, lses, w_o, x2, gate, bsz, tm=512):
    t, d = x2.shape
    s = t // bsz
    tm = min(tm, s)
    ns = s // tm
    hw = os[0].shape[1]
    dils = tuple(dil for _, dil in ATTN_GROUPS)
    ex = np.zeros((128, hw), np.float32)
    ex[np.arange(hw) // ATTN_HEAD_DIM, np.arange(hw)] = 1.0
    ex = jnp.asarray(ex, BF16)
    row = lambda b, i: (b * ns + i, 0)

    def res_major(a, dil):
        c = a.shape[1]
        return a.reshape(bsz, dil, s // dil, c), pl.BlockSpec((None, dil, tm // dil, c), lambda b, i: (b, 0, i, 0))

    o_args, o_specs = zip(*[res_major(o, dil) for o, dil in zip(os, dils)])
    l_args, l_specs = zip(*[res_major(l, dil) for l, dil in zip(lses, dils)])
    return pl.pallas_call(
        functools.partial(_attn_out_kernel, dils=dils),
        grid=(bsz, ns),
        in_specs=list(o_specs) + list(l_specs) + [
            pl.BlockSpec(ex.shape, lambda b, i: (0, 0)),
            pl.BlockSpec(w_o.shape, lambda b, i: (0, 0)),
            pl.BlockSpec((tm, d), row),
            pl.BlockSpec((None, 1, d), lambda b, i: (b, 0, 0))],
        out_specs=pl.BlockSpec((tm, d), row),
        out_shape=jax.ShapeDtypeStruct((t, d), F32),
        scratch_shapes=[pltpu.VMEM((hw // 128, tm, 128), F32), pltpu.VMEM((1, tm, 128), F32)],
        compiler_params=_params(("parallel", "parallel")),
        name="attn_out",
    )(*o_args, *l_args, ex, w_o, x2, gate)


def _norm_attn_kernel(x_ref, g_ref, sh_ref, sc_ref, *rest, dils):
    outs, scr_ref = rest[:-1], rest[-1]
    x = x_ref[...]
    ms = jnp.mean(x * x, axis=-1, keepdims=True)
    y = x * lax.rsqrt(ms + RMS_EPS) * g_ref[...]
    y = y * (1.0 + sc_ref[...]) + sh_ref[...]
    slabs = scr_ref.shape[0]
    if any(dil > 1 for dil in dils):
        for c in range(slabs):
            scr_ref[c] = y[:, c * 128:(c + 1) * 128]
    for o, dil in zip(outs, dils):
        if dil == 1:
            o[0] = y.astype(o.dtype)
        else:
            n = o.shape[1]
            for r in range(dil):
                for c in range(slabs):
                    o[r, :, c * 128:(c + 1) * 128] = scr_ref[c, pl.ds(r, n, stride=dil), :].astype(o.dtype)


def _norm_mod_attn(x2, g, shift, scale, bsz, ts=512):
    t, d = x2.shape
    s = t // bsz
    ts = min(ts, s)
    ns = s // ts
    dils = tuple(dil for _, dil in ATTN_GROUPS)
    vec = pl.BlockSpec((None, 1, d), lambda b, i: (b, 0, 0))
    outs = pl.pallas_call(
        functools.partial(_norm_attn_kernel, dils=dils),
        grid=(bsz, ns),
        in_specs=[pl.BlockSpec((ts, d), lambda b, i: (b * ns + i, 0)),
                  pl.BlockSpec((1, d), lambda b, i: (0, 0)), vec, vec],
        out_specs=[pl.BlockSpec((None, dil, ts // dil, d), lambda b, i: (b, 0, i, 0)) for dil in dils],
        out_shape=[jax.ShapeDtypeStruct((bsz, dil, s // dil, d), BF16) for dil in dils],
        scratch_shapes=[pltpu.VMEM((d // 128, ts, 128), F32)],
        compiler_params=_params(("parallel", "parallel")),
        name="norm_mod_attn",
    )(x2, g.reshape(1, d), shift, scale)
    return [o.reshape(t, d) for o in outs]


def _pack_attn(j, w_qkv, w_o):
    d = w_qkv.shape[1]
    ng = len(ATTN_GROUPS)
    w = w_qkv[j].reshape(d, ng, 3, ATTN_HEADS * ATTN_HEAD_DIM)
    w = w.at[:, :, 0].multiply(ATTN_HEAD_DIM ** -0.5)
    return {"w_qkv": w.reshape(d, -1).astype(BF16), "w_o": w_o[j].astype(BF16)}


def _attn_layer(x2, bsz, hns, gate, p):
    gw = 3 * ATTN_HEADS * ATTN_HEAD_DIM
    outs = [_attention_group(_matmul(hn, p["w_qkv"], BF16, n=gw, col0=gi * gw), bsz, gi)
            for gi, hn in enumerate(hns)]
    return _attn_out([o for o, _ in outs], [l for _, l in outs], p["w_o"], x2, gate, bsz)


def _router_kernel(x_ref, g_ref, sh_ref, sc_ref, wh_ref, wl_ref, b_ref, o_ref, hb_ref, seg_ref):
    x = x_ref[...]
    ms = jnp.mean(x * x, axis=-1, keepdims=True)
    hn = x * lax.rsqrt(ms + RMS_EPS) * g_ref[...]
    hn = hn * (1.0 + sc_ref[...]) + sh_ref[...]
    hb_ref[...] = hn.astype(hb_ref.dtype)
    hh, hl = _split2(hn)
    logits = _dot(hh, wh_ref[...]) + _dot(hl, wh_ref[...]) + _dot(hh, wl_ref[...]) + b_ref[...]
    lane = lax.broadcasted_iota(jnp.int32, logits.shape, 1)
    ng, ne = N_EXPERT_GROUPS, EXPERTS_PER_GROUP
    big = jnp.int32(ROUTER_LANES)
    is_g = lane < ng
    gl = jnp.where(is_g, logits, -jnp.inf)
    gmax = gl.max(axis=-1, keepdims=True)
    gsum = jnp.where(is_g, jnp.exp(logits - gmax), 0.0).sum(axis=-1, keepdims=True)
    p_group = 1.0 / gsum
    g_top = jnp.where(gl == gmax, lane, big).min(axis=-1, keepdims=True)
    lo = ng + ne * g_top
    es = jnp.where((lane >= lo) & (lane < lo + ne), logits, -jnp.inf)
    v1 = es.max(axis=-1, keepdims=True)
    i1 = jnp.where(es == v1, lane, big).min(axis=-1, keepdims=True)
    es2 = jnp.where(lane == i1, -jnp.inf, es)
    v2 = es2.max(axis=-1, keepdims=True)
    i2 = jnp.where(es2 == v2, lane, big).min(axis=-1, keepdims=True)
    e2 = jnp.exp(v2 - v1)
    w1 = p_group / (1.0 + e2)
    w2 = p_group * e2 / (1.0 + e2)
    gates = jnp.where(lane == i1, w1, 0.0) + jnp.where(lane == i2, w2, 0.0)

    tm = x.shape[0]
    onehot = jnp.where(lane == g_top, 1.0, 0.0)
    sub = min(tm, ROUTER_LANES)
    tri = jnp.where(lax.broadcasted_iota(jnp.int32, (sub, sub), 1) < lax.broadcasted_iota(jnp.int32, (sub, sub), 0),
                    1.0, 0.0).astype(BF16)
    oh_b = onehot.astype(BF16)
    run = jnp.zeros((1, ROUTER_LANES), F32)
    parts = []
    for k in range(tm // sub):
        parts.append(_dot(tri, oh_b[k * sub:(k + 1) * sub]) + run)
        run = run + onehot[k * sub:(k + 1) * sub].sum(axis=0, keepdims=True)
    before = jnp.concatenate(parts, axis=0)
    counts = jnp.broadcast_to(run, (8, ROUTER_LANES))
    upper = jnp.where(lax.broadcasted_iota(jnp.int32, (ROUTER_LANES, ROUTER_LANES), 0)
                      < lax.broadcasted_iota(jnp.int32, (ROUTER_LANES, ROUTER_LANES), 1), 1.0, 0.0).astype(BF16)
    ch, cl = _split2(counts)
    seg_lo = _dot(ch, upper) + _dot(cl, upper)
    dest = (onehot * (before + seg_lo[0:1])).sum(axis=-1, keepdims=True)
    o_ref[...] = gates + jnp.where(lane == DEST_LANE, dest, 0.0)
    row8 = lax.broadcasted_iota(jnp.int32, (8, ROUTER_LANES), 0)
    seg_ref[...] = jnp.where(row8 == 0, seg_lo, jnp.where(row8 == 1, seg_lo + counts, 0.0))


def _router(x2, g, shift, scale, wr_hi, wr_lo, bias, bsz):
    t, d = x2.shape
    s = t // bsz
    tm = min(MOE_TILE, s)
    ns = s // tm
    row = lambda b, i: (b * ns + i, 0)
    full = lambda a: pl.BlockSpec(a.shape, lambda b, i: (0, 0))
    vec = pl.BlockSpec((None, 1, d), lambda b, i: (b, 0, 0))
    g = g.reshape(1, d)
    return pl.pallas_call(
        _router_kernel,
        grid=(bsz, ns),
        in_specs=[pl.BlockSpec((tm, d), row), full(g), vec, vec, full(wr_hi), full(wr_lo), full(bias)],
        out_specs=[pl.BlockSpec((tm, ROUTER_LANES), row), pl.BlockSpec((tm, d), row),
                   pl.BlockSpec((None, 8, ROUTER_LANES), lambda b, i: (b * ns + i, 0, 0))],
        out_shape=[jax.ShapeDtypeStruct((t, ROUTER_LANES), F32), jax.ShapeDtypeStruct((t, d), BF16),
                   jax.ShapeDtypeStruct((t // tm, 8, ROUTER_LANES), F32)],
        compiler_params=_params(("parallel", "parallel")),
        name="moe_router",
    )(x2, g, shift, scale, wr_hi, wr_lo, bias)


def _experts_kernel(seg_ref, h_ref, gates_ref, wg_ref, wu_ref, wd_ref, x_ref, ga_ref, fg_ref, o_ref,
                    pt_ref, xs_ref, gs_ref, y_ref, *, per_step, blk, final_norm):
    tile = pl.program_id(0)
    step = pl.program_id(1)
    tm = h_ref.shape[0]
    grp = step // (EXPERTS_PER_GROUP // per_step)
    tn = (((0,), (0,)), ((), ()))

    @pl.when(step == 0)
    def _():
        gates = gates_ref[...]
        lane = lax.broadcasted_iota(jnp.int32, gates.shape, 1)
        dest = jnp.where(lane == DEST_LANE, gates, 0.0).sum(axis=-1, keepdims=True)
        col = lax.broadcasted_iota(jnp.int32, (tm, tm), 1).astype(F32)
        pt = jnp.where(dest == col, 1.0, 0.0).astype(BF16)
        pt_ref[...] = pt
        xs_ref[...] = lax.dot_general(pt, h_ref[...], tn, preferred_element_type=F32).astype(BF16)
        g1 = gates.astype(BF16)
        r1 = gates - g1.astype(F32)
        g2 = r1.astype(BF16)
        g3 = (r1 - g2.astype(F32)).astype(BF16)
        gs_ref[...] = (lax.dot_general(pt, g1, tn, preferred_element_type=F32)
                       + lax.dot_general(pt, g2, tn, preferred_element_type=F32)
                       + lax.dot_general(pt, g3, tn, preferred_element_type=F32))
        y_ref[...] = jnp.zeros_like(y_ref)

    lo = seg_ref[tile * 8 + grp]
    hi = seg_ref[tile * 8 + N_EXPERT_GROUPS + grp]
    for b in range(tm // blk):
        @pl.when((lo < (b + 1) * blk) & (hi > b * blk))
        def _(b=b):
            rows = pl.ds(b * blk, blk)
            h = xs_ref[rows, :]
            gsb = gs_ref[rows, :]
            lane = lax.broadcasted_iota(jnp.int32, gsb.shape, 1)
            hids = []
            for j in range(per_step):
                e = step * per_step + j
                ge = jnp.where(lane == N_EXPERT_GROUPS + e, gsb, 0.0).sum(axis=-1, keepdims=True)
                gp = _dot(h, wg_ref[j])
                up = _dot(h, wu_ref[j])
                hids.append((gp * _sigmoid(gp) * up * ge).astype(BF16))
            y_ref[rows, :] += _dot(jnp.concatenate(hids, axis=1), wd_ref[...])

    @pl.when(step == pl.num_programs(1) - 1)
    def _():
        res = x_ref[...] + ga_ref[...] * _dot(pt_ref[...], y_ref[...].astype(BF16))
        if final_norm:
            ms = jnp.mean(res * res, axis=-1, keepdims=True)
            res = res * lax.rsqrt(ms + RMS_EPS) * fg_ref[...]
        o_ref[...] = res


def _experts(hn_bf16, gates, seg, wg, wu, wd, layer, x2, gate, final_g, final_norm, bsz, per_step=8):
    t, d = x2.shape
    s = t // bsz
    tm = min(MOE_TILE, s)
    ns = s // tm
    f = wg.shape[2]
    ne = N_EXPERT_GROUPS * EXPERTS_PER_GROUP
    steps = ne // per_step
    blk = min(MOE_BLOCK, tm)
    row = lambda i, e, seg: (i, 0)
    wspec = pl.BlockSpec((per_step, d, f), lambda i, e, seg: (layer * steps + e, 0, 0))
    grid_spec = pltpu.PrefetchScalarGridSpec(
        num_scalar_prefetch=1,
        grid=(t // tm, steps),
        in_specs=[
            pl.BlockSpec((tm, d), row),
            pl.BlockSpec((tm, ROUTER_LANES), row),
            wspec, wspec,
            pl.BlockSpec((per_step * f, d), lambda i, e, seg: (layer * steps + e, 0)),
            pl.BlockSpec((tm, d), row),
            pl.BlockSpec((None, 1, d), lambda i, e, seg: (i // ns, 0, 0)),
            pl.BlockSpec((1, d), lambda i, e, seg: (0, 0)),
        ],
        out_specs=pl.BlockSpec((tm, d), row),
        scratch_shapes=[pltpu.VMEM((tm, tm), BF16), pltpu.VMEM((tm, d), BF16),
                        pltpu.VMEM((tm, ROUTER_LANES), F32), pltpu.VMEM((tm, d), F32)],
    )
    return pl.pallas_call(
        functools.partial(_experts_kernel, per_step=per_step, blk=blk, final_norm=final_norm),
        grid_spec=grid_spec,
        out_shape=jax.ShapeDtypeStruct((t, d), F32),
        compiler_params=_params(("parallel", "arbitrary")),
        name="moe_experts",
    )(seg, hn_bf16, gates, wg, wu, wd, x2, gate, final_g.reshape(1, d))


def _moe_layer(x2, bsz, g, shift, scale, gate, p, final_g, final_norm):
    gates, hn_bf16, seg = _router(x2, g, shift, scale, p["wr_hi"], p["wr_lo"], p["r_bias"], bsz)
    ng = N_EXPERT_GROUPS
    seg = jnp.concatenate([seg[:, 0, :ng], seg[:, 1, :ng]], axis=1).astype(jnp.int32).reshape(-1)
    return _experts(hn_bf16, gates, seg, p["wg"], p["wu"], p["wd"], p["layer"], x2, gate,
                    final_g, final_norm, bsz)


def _pad_rows(rows, d):
    out = jnp.zeros((8, d), F32)
    return out.at[:len(rows)].set(jnp.stack([r.reshape(d).astype(F32) for r in rows]))


def _pack_rwkv(j, mu, w_rkv, w0, w1, w2, a0, a1, a2, g1, g2, k_k, k_a, r_k, ln_w, ln_b, w_o):
    d = w_o.shape[-1]
    cat = lambda m: jnp.concatenate([m[j, 0], m[j, 1]], axis=1).astype(BF16)
    stack = lambda m: jnp.concatenate([m[j, 0], m[j, 1]], axis=0).astype(BF16)
    return {
        "mu8": _pad_rows(list(mu[j]), d),
        "w_rkv": w_rkv[j].astype(BF16),
        "w1c": cat(w1), "a1c": cat(a1), "g1c": cat(g1),
        "w2s": stack(w2), "a2s": stack(a2), "g2": g2[j].astype(BF16),
        "vec8": _pad_rows([w0[j, 0], w0[j, 1], a0[j, 0], a0[j, 1], k_k[j], k_a[j], r_k[j]], d),
        "ln8": _pad_rows([ln_w[j], ln_b[j]], d),
        "w_o": w_o[j].astype(BF16),
    }


def _pack_moe(i, router_g, router_g_b, router_e, router_e_b, w_gate, w_up, w_down):
    d = router_g.shape[1]
    ng, ne = N_EXPERT_GROUPS, EXPERTS_PER_GROUP
    wr = jnp.zeros((d, ROUTER_LANES), F32)
    wr = wr.at[:, :ng].set(router_g[i])
    wr = wr.at[:, ng:ng + ng * ne].set(router_e[i].transpose(1, 0, 2).reshape(d, ng * ne))
    bias = jnp.zeros((1, ROUTER_LANES), F32)
    bias = bias.at[0, :ng].set(router_g_b[i])
    bias = bias.at[0, ng:ng + ng * ne].set(router_e_b[i].reshape(ng * ne))
    wr_hi = wr.astype(BF16)
    wr_lo = (wr - wr_hi.astype(F32)).astype(BF16)
    f = w_gate.shape[-1]
    return {"wr_hi": wr_hi, "wr_lo": wr_lo, "r_bias": bias,
            "layer": i,
            "wg": w_gate.reshape(-1, d, f).astype(BF16), "wu": w_up.reshape(-1, d, f).astype(BF16),
            "wd": w_down.reshape(-1, d).astype(BF16)}


def kernel(x, c, ada_w, ada_b, norm_tm_g, norm_cm_g, rw_mu, rw_w_rkv, rw_w0, rw_w1, rw_w2, rw_a0, rw_a1, rw_a2, rw_g1, rw_g2, rw_k_k, rw_k_a, rw_r_k, rw_ln_w, rw_ln_b, rw_w_o, at_w_qkv, at_w_o, moe_router_g, moe_router_g_b, moe_router_e, moe_router_e_b, moe_w_gate, moe_w_up, moe_w_down, final_g):
    bsz, s, d = x.shape
    depth = ada_w.shape[0]
    x2 = x.reshape(bsz * s, d)
    mod = _ada_mod(c, ada_w, ada_b)
    for i in range(depth):
        sh_t, sc_t, ga_t, sh_c, sc_c, ga_c = (mod[i, m] for m in range(N_MOD))
        j = i // 2
        if i % 2 == 0:
            p = _pack_rwkv(j, rw_mu, rw_w_rkv, rw_w0, rw_w1, rw_w2, rw_a0, rw_a1, rw_a2, rw_g1, rw_g2,
                           rw_k_k, rw_k_a, rw_r_k, rw_ln_w, rw_ln_b, rw_w_o)
            x2 = _rwkv_layer(x2, bsz, norm_tm_g[i], sh_t, sc_t, ga_t, p)
        else:
            hns = _norm_mod_attn(x2, norm_tm_g[i], sh_t, sc_t, bsz)
            p = _pack_attn(j, at_w_qkv, at_w_o)
            x2 = _attn_layer(x2, bsz, hns, ga_t, p)
        pm = _pack_moe(i, moe_router_g, moe_router_g_b, moe_router_e, moe_router_e_b,
                       moe_w_gate, moe_w_up, moe_w_down)
        x2 = _moe_layer(x2, bsz, norm_cm_g[i], sh_c, sc_c, ga_c, pm, final_g, i == depth - 1)
    return x2.reshape(bsz, s, d)
```
